```python
import jax, jax.numpy as jnp
from jax import lax
import numpy as np


D_MODEL = 2048
BATCH = 2
SEQ = 4096
DEPTH = 1

HEAD_DIM = 128
MIX_WIDTH = D_MODEL
NSA_HEADS = MIX_WIDTH // 2 // HEAD_DIM
NSA_KV_GROUPS = 2
NSA_HPG = NSA_HEADS // NSA_KV_GROUPS
MOBA_HEADS = MIX_WIDTH // 2 // HEAD_DIM
CMP_LEN = 32
CMP_STRIDE = 16
CMP_HIDDEN = HEAD_DIM
SEL_LEN = 64
SEL_TOPK = 16
WINDOW = 512
MOBA_BLOCK = 256
MOBA_TOPK = 3
D_FF = 4 * D_MODEL
ROPE_THETA = 10000.0
EPS = 1e-6
NSA_Q_CHUNK = 64
MOBA_Q_CHUNK = 16

NSA_Q_W = NSA_HEADS * HEAD_DIM
NSA_KV_W = NSA_KV_GROUPS * HEAD_DIM
NSA_GATE_W = 3 * NSA_HEADS
MOBA_W = MOBA_HEADS * HEAD_DIM
IN_WIDTHS = (NSA_Q_W,) + (NSA_KV_W,) * 6 + (NSA_GATE_W, MOBA_W, MOBA_W, MOBA_W)
IN_COLS = sum(IN_WIDTHS)

kernel_name = 'hymba_nsa_moba_sandwich_adaln_block'


def rms_norm(x, g):
    x32 = x.astype(jnp.float32)
    y = x32 * lax.rsqrt(jnp.mean(x32 * x32, axis=-1, keepdims=True) + EPS)
    return (y * g.astype(jnp.float32)).astype(x.dtype)


def modulate(h, shift, scale):
    return h * (1.0 + scale[:, None, :]) + shift[:, None, :]


def rope_tables(T):
    pos = jnp.arange(T, dtype=jnp.float32)
    inv = ROPE_THETA ** (-jnp.arange(0, HEAD_DIM, 2, dtype=jnp.float32) / HEAD_DIM)
    ang = pos[:, None] * inv[None, :]
    return jnp.cos(ang), jnp.sin(ang)


def apply_rope(x, cos, sin):
    shp = (x.shape[1],) + (1,) * (x.ndim - 3) + (HEAD_DIM // 2,)
    c_ = cos.reshape(shp)
    s_ = sin.reshape(shp)
    x32 = x.astype(jnp.float32)
    x1, x2 = jnp.split(x32, 2, axis=-1)
    return jnp.concatenate([x1 * c_ - x2 * s_, x2 * c_ + x1 * s_], axis=-1).astype(x.dtype)


def masked_softmax(s, mask):
    s = jnp.where(mask, s.astype(jnp.float32), -jnp.inf)
    m = jnp.max(s, axis=-1, keepdims=True)
    m = jnp.where(jnp.isfinite(m), m, 0.0)
    e = jnp.exp(s - m)
    return e / jnp.maximum(jnp.sum(e, axis=-1, keepdims=True), jnp.float32(1e-30))


def compress(x, pos, w1, w2):
    B, T, G, dh = x.shape
    n_cmp = (T - CMP_LEN) // CMP_STRIDE + 1
    idx = np.arange(n_cmp)[:, None] * CMP_STRIDE + np.arange(CMP_LEN)[None, :]
    blocks = x[:, idx] + pos[:, None, :].astype(x.dtype)
    blocks = blocks.transpose(0, 1, 3, 2, 4).reshape(B, n_cmp, G, CMP_LEN * dh)
    return jax.nn.silu(blocks @ w1) @ w2


def nsa_mixer(q, k_cmp, v_cmp, k_sel, v_sel, k_win, v_win, gate_logits,
              cmp_k_pos, cmp_k_w1, cmp_k_w2, cmp_v_pos, cmp_v_w1, cmp_v_w2, cos, sin):
    B, T, _ = q.shape
    G, Hg, dh = NSA_KV_GROUPS, NSA_HPG, HEAD_DIM
    Qc = NSA_Q_CHUNK
    q = apply_rope(q.reshape(B, T, G, Hg, dh), cos, sin) * (dh ** -0.5)
    kv = lambda a: a.reshape(B, T, G, dh)
    k_cmp = apply_rope(kv(k_cmp), cos, sin)
    k_sel = apply_rope(kv(k_sel), cos, sin)
    k_win = apply_rope(kv(k_win), cos, sin)
    v_cmp, v_sel, v_win = kv(v_cmp), kv(v_sel), kv(v_win)
    gates = jax.nn.sigmoid(gate_logits.astype(jnp.float32)).reshape(B, T, G, Hg, 3)

    kc = compress(k_cmp, cmp_k_pos, cmp_k_w1, cmp_k_w2)
    vc = compress(v_cmp, cmp_v_pos, cmp_v_w1, cmp_v_w2)
    n_cmp = kc.shape[1]
    cmp_start = np.arange(n_cmp) * CMP_STRIDE
    cmp_end_np = cmp_start + CMP_LEN
    cmp_end = jnp.asarray(cmp_end_np - 1)

    n_sel = T // SEL_LEN
    k_top = min(SEL_TOPK, n_sel)
    sel_start = np.arange(n_sel) * SEL_LEN
    overlap = np.clip(np.minimum(cmp_end_np[:, None], sel_start[None, :] + SEL_LEN)
                      - np.maximum(cmp_start[:, None], sel_start[None, :]), 0, None)
    sel_map = jnp.asarray((overlap / CMP_LEN).astype(np.float32))
    ks_blk = k_sel.reshape(B, n_sel, SEL_LEN, G, dh).transpose(0, 3, 1, 2, 4)
    vs_blk = v_sel.reshape(B, n_sel, SEL_LEN, G, dh).transpose(0, 3, 1, 2, 4)
    bi = jnp.arange(B)[:, None, None, None]
    gi = jnp.arange(G)[None, :, None, None]

    pad = ((0, 0), (WINDOW, 0), (0, 0), (0, 0))
    kw_pad = jnp.pad(k_win, pad)
    vw_pad = jnp.pad(v_win, pad)

    def chunk(i):
        s0 = i * Qc
        qc = lax.dynamic_slice_in_dim(q, s0, Qc, axis=1)
        gc = lax.dynamic_slice_in_dim(gates, s0, Qc, axis=1)
        tq = s0 + jnp.arange(Qc)
        sc = jnp.einsum('bqghd,bcgd->bghqc', qc, kc)
        p_c = masked_softmax(sc, cmp_end[None, :] <= tq[:, None])
        o_c = jnp.einsum('bghqc,bcgd->bqghd', p_c.astype(vc.dtype), vc)
        imp = jnp.einsum('bghqc,cs->bgqs', p_c, sel_map)
        cb = tq // SEL_LEN
        blk = jnp.arange(n_sel)[None, :]
        causal = blk <= cb[:, None]
        forced = (blk == 0) | (blk == cb[:, None]) | (blk == cb[:, None] - 1)
        score = jnp.where(causal, jnp.where(forced, jnp.inf, imp), -jnp.inf)
        _, idx = lax.top_k(score, k_top)
        ksg = ks_blk[bi, gi, idx]
        vsg = vs_blk[bi, gi, idx]
        kpos = idx[..., None] * SEL_LEN + jnp.arange(SEL_LEN)
        mask_s = (kpos <= tq[:, None, None]).reshape(B, G, 1, Qc, k_top * SEL_LEN)
        ss = jnp.einsum('bqghd,bgqkld->bghqkl', qc, ksg).reshape(B, G, Hg, Qc, k_top * SEL_LEN)
        p_s = masked_softmax(ss, mask_s)
        o_s = jnp.einsum('bghqn,bgqnd->bqghd', p_s.astype(vsg.dtype),
                         vsg.reshape(B, G, Qc, k_top * SEL_LEN, dh))
        kw = lax.dynamic_slice_in_dim(kw_pad, s0, Qc + WINDOW, axis=1)
        vw = lax.dynamic_slice_in_dim(vw_pad, s0, Qc + WINDOW, axis=1)
        kp = s0 - WINDOW + jnp.arange(Qc + WINDOW)
        dist = tq[:, None] - kp[None, :]
        mask_w = (dist >= 0) & (dist < WINDOW) & (kp[None, :] >= 0)
        sw = jnp.einsum('bqghd,bkgd->bghqk', qc, kw)
        p_w = masked_softmax(sw, mask_w)
        o_w = jnp.einsum('bghqk,bkgd->bqghd', p_w.astype(vw.dtype), vw)
        g = gc.astype(o_c.dtype)
        o = g[..., 0:1] * o_c + g[..., 1:2] * o_s + g[..., 2:3] * o_w
        return o.reshape(B, Qc, NSA_HEADS * dh)

    out = lax.map(chunk, jnp.arange(T // Qc))
    return jnp.moveaxis(out, 0, 1).reshape(B, T, NSA_HEADS * dh)


def moba_mixer(q, k, v, cos, sin):
    B, T, _ = q.shape
    H, dh, BLK, Qc = MOBA_HEADS, HEAD_DIM, MOBA_BLOCK, MOBA_Q_CHUNK
    q = apply_rope(q.reshape(B, T, H, dh), cos, sin) * (dh ** -0.5)
    k = apply_rope(k.reshape(B, T, H, dh), cos, sin)
    v = v.reshape(B, T, H, dh)
    nb = -(-T // BLK)
    pad = ((0, 0), (0, nb * BLK - T), (0, 0), (0, 0))
    k_pad = jnp.pad(k, pad)
    v_pad = jnp.pad(v, pad)
    k_blk = k_pad.reshape(B, nb, BLK, H, dh).transpose(0, 3, 1, 2, 4)
    v_blk = v_pad.reshape(B, nb, BLK, H, dh).transpose(0, 3, 1, 2, 4)
    k_mean = jnp.mean(k_blk.astype(jnp.float32), axis=3).astype(k.dtype)
    k_past = min(MOBA_TOPK, nb - 1)
    bi = jnp.arange(B)[:, None, None, None]
    hi = jnp.arange(H)[None, :, None, None]

    def chunk(i):
        s0 = i * Qc
        qc = lax.dynamic_slice_in_dim(q, s0, Qc, axis=1)
        tq = s0 + jnp.arange(Qc)
        cur = s0 // BLK
        ko = lax.dynamic_slice_in_dim(k_pad, cur * BLK, BLK, axis=1)
        vo = lax.dynamic_slice_in_dim(v_pad, cur * BLK, BLK, axis=1)
        opos = cur * BLK + jnp.arange(BLK)
        s_own = jnp.einsum('bqhd,blhd->bhql', qc, ko)
        m_own = jnp.broadcast_to(opos[None, :] <= tq[:, None], (B, H, Qc, BLK))
        if k_past > 0:
            gsc = jnp.einsum('bqhd,bhnd->bhqn', qc, k_mean).astype(jnp.float32)
            gsc = jnp.where(jnp.arange(nb) < cur, gsc, -jnp.inf)
            _, idx = lax.top_k(gsc, k_past)
            kg = k_blk[bi, hi, idx]
            vg = v_blk[bi, hi, idx]
            s_past = jnp.einsum('bqhd,bhqkld->bhqkl', qc, kg).reshape(B, H, Qc, k_past * BLK)
            m_past = jnp.broadcast_to((idx < cur)[..., None],
                                      (B, H, Qc, k_past, BLK)).reshape(B, H, Qc, k_past * BLK)
            p = masked_softmax(jnp.concatenate([s_past, s_own], axis=-1),
                               jnp.concatenate([m_past, m_own], axis=-1))
            p = p.astype(v.dtype)
            o = (jnp.einsum('bhqn,bhqnd->bqhd', p[..., :k_past * BLK],
                            vg.reshape(B, H, Qc, k_past * BLK, dh))
                 + jnp.einsum('bhql,blhd->bqhd', p[..., k_past * BLK:], vo))
        else:
            p = masked_softmax(s_own, m_own).astype(v.dtype)
            o = jnp.einsum('bhql,blhd->bqhd', p, vo)
        return o.reshape(B, Qc, H * dh)

    out = lax.map(chunk, jnp.arange(T // Qc))
    return jnp.moveaxis(out, 0, 1).reshape(B, T, H * dh)


def setup_inputs(seed: int = 0) -> dict:
    key = jax.random.key(seed)
    ks = jax.random.split(key, 18)
    nrm = lambda k, shape, s: jax.random.normal(k, shape, jnp.float32) * s
    L = DEPTH
    return {
        'x': nrm(ks[0], (BATCH, SEQ, D_MODEL), 1.0),
        'c': nrm(ks[1], (BATCH, D_MODEL), 1.0),
        'w_ada': nrm(ks[2], (L, D_MODEL, 6 * D_MODEL), 0.5 * D_MODEL ** -0.5),
        'b_ada': nrm(ks[3], (L, 6 * D_MODEL), 0.01),
        'pre_norm_mix': 1.0 + nrm(ks[4], (L, D_MODEL), 0.05),
        'post_norm_mix': 1.0 + nrm(ks[5], (L, D_MODEL), 0.05),
        'w_in': nrm(ks[6], (L, D_MODEL, IN_COLS), D_MODEL ** -0.5),
        'cmp_k_pos': nrm(ks[7], (L, CMP_LEN, HEAD_DIM), 0.1),
        'cmp_k_w1': nrm(ks[8], (L, CMP_LEN * HEAD_DIM, CMP_HIDDEN), (CMP_LEN * HEAD_DIM) ** -0.5),
        'cmp_k_w2': nrm(ks[9], (L, CMP_HIDDEN, HEAD_DIM), CMP_HIDDEN ** -0.5),
        'cmp_v_pos': nrm(ks[10], (L, CMP_LEN, HEAD_DIM), 0.1),
        'cmp_v_w1': nrm(ks[11], (L, CMP_LEN * HEAD_DIM, CMP_HIDDEN), (CMP_LEN * HEAD_DIM) ** -0.5),
        'cmp_v_w2': nrm(ks[12], (L, CMP_HIDDEN, HEAD_DIM), CMP_HIDDEN ** -0.5),
        'w_o': nrm(ks[13], (L, MIX_WIDTH, D_MODEL), MIX_WIDTH ** -0.5),
        'pre_norm_ffn': 1.0 + nrm(ks[14], (L, D_MODEL), 0.05),
        'post_norm_ffn': 1.0 + nrm(ks[15], (L, D_MODEL), 0.05),
        'w_up': nrm(ks[16], (L, D_MODEL, D_FF), D_MODEL ** -0.5),
        'w_down': nrm(ks[17], (L, D_FF, D_MODEL), D_FF ** -0.5),
    }


def reference(x, c, w_ada, b_ada, pre_norm_mix, post_norm_mix, w_in,
              cmp_k_pos, cmp_k_w1, cmp_k_w2, cmp_v_pos, cmp_v_w1, cmp_v_w2,
              w_o, pre_norm_ffn, post_norm_ffn, w_up, w_down):
    T = x.shape[1]
    cos, sin = rope_tables(T)
    split_at = np.cumsum(IN_WIDTHS)[:-1].tolist()
    for l in range(DEPTH):
        mod = jax.nn.silu(c) @ w_ada[l] + b_ada[l]
        shift_m, scale_m, gate_m, shift_f, scale_f, gate_f = jnp.split(mod, 6, axis=-1)
        h = modulate(rms_norm(x, pre_norm_mix[l]), shift_m, scale_m)
        (q_n, k_c, v_c, k_s, v_s, k_w, v_w, g_n,
         q_m, k_m, v_m) = jnp.split(h @ w_in[l], split_at, axis=-1)
        o_n = nsa_mixer(q_n, k_c, v_c, k_s, v_s, k_w, v_w, g_n,
                        cmp_k_pos[l], cmp_k_w1[l], cmp_k_w2[l],
                        cmp_v_pos[l], cmp_v_w1[l], cmp_v_w2[l], cos, sin)
        o_m = moba_mixer(q_m, k_m, v_m, cos, sin)
        o = jnp.concatenate([o_n, o_m], axis=-1) @ w_o[l]
        x = x + gate_m[:, None, :] * rms_norm(o, post_norm_mix[l])
        h = modulate(rms_norm(x, pre_norm_ffn[l]), shift_f, scale_f)
        f = jnp.square(jax.nn.relu(h @ w_up[l])) @ w_down[l]
        x = x + gate_f[:, None, :] * rms_norm(f, post_norm_ffn[l])
    return x
```

```python
import functools

import numpy as np
import jax
import jax.numpy as jnp
from jax import lax
from jax.experimental import pallas as pl
from jax.experimental.pallas import tpu as pltpu

F32 = jnp.float32
BF16 = jnp.bfloat16
I32 = jnp.int32

HEAD_DIM = 128
NSA_HEADS = 8
NSA_KV_GROUPS = 2
NSA_HPG = NSA_HEADS // NSA_KV_GROUPS
MOBA_HEADS = 8
CMP_LEN = 32
CMP_STRIDE = 16
SEL_LEN = 64
SEL_TOPK = 16
WINDOW = 512
MOBA_BLOCK = 256
MOBA_TOPK = 3
ROPE_THETA = 10000.0
EPS = 1e-6

LANES = 128
NEG_BIG = -1e30
VMEM_LIMIT = 56 * 1024 * 1024

QN0 = 0
QM0 = 8
KM0 = 16
KC0 = 24
KS0 = 26
KW0 = 28
VM0 = 32
VC0 = 40
VS0 = 42
VW0 = 44
GT0 = 46
N_BLOCKS = 48
SEC_BLOCKS = 16


def _nt_dot(a, b):
    return lax.dot_general(a, b, (((1,), (1,)), ((), ())), preferred_element_type=F32)


def _rms_norm(x, g):
    ms = jnp.mean(x * x, axis=-1, keepdims=True)
    return x * lax.rsqrt(ms + EPS) * g


def _split_bf16(a):
    hi = a.astype(BF16)
    lo = (a - hi.astype(F32)).astype(BF16)
    return hi, lo


def _adaln_kernel(c_ref, w_ref, b_ref, o_ref):
    c = c_ref[...]
    a = c * (1.0 / (1.0 + jnp.exp(-c)))
    o_ref[...] = jnp.dot(a, w_ref[...], preferred_element_type=F32,
                         precision=lax.Precision.HIGHEST) + b_ref[...]


def _adaln(c, w, b):
    bsz, d = c.shape
    n = w.shape[1]
    rows = 8
    tn = 1024
    c8 = jnp.zeros((rows, d), F32).at[:bsz].set(c)
    out = pl.pallas_call(
        _adaln_kernel,
        grid=(n // tn,),
        in_specs=[pl.BlockSpec((rows, d), lambda j: (0, 0)),
                  pl.BlockSpec((d, tn), lambda j: (0, j)),
                  pl.BlockSpec((1, tn), lambda j: (0, j))],
        out_specs=pl.BlockSpec((rows, tn), lambda j: (0, j)),
        out_shape=jax.ShapeDtypeStruct((rows, n), F32),
        compiler_params=pltpu.CompilerParams(dimension_semantics=("parallel",),
                                             vmem_limit_bytes=VMEM_LIMIT),
        name="adaln",
    )(c8, w, b.reshape(1, n))
    return out[:bsz]


def _inproj_kernel(x_ref, g_ref, sc_ref, sh_ref, w_ref, cos_ref, sin_ref,
                   o_ref, kvc_ref, h_ref, *, tn):
    j = pl.program_id(1)
    tiles_per_sec = SEC_BLOCKS * LANES // tn

    @pl.when(j == 0)
    def _():
        h = _rms_norm(x_ref[...], g_ref[...]) * (1.0 + sc_ref[...]) + sh_ref[...]
        h_ref[...] = h.astype(BF16)

    acc = jnp.dot(h_ref[...], w_ref[...], preferred_element_type=F32)

    def rope(a, scale):
        cos = cos_ref[...]
        sin = sin_ref[...]
        outs = []
        for c in range(tn // LANES):
            ch = a[:, c * LANES:(c + 1) * LANES]
            r = ch * cos + pltpu.roll(ch, HEAD_DIM // 2, 1) * sin
            if scale is not None:
                r = r * scale
            outs.append(r.astype(BF16))
        return jnp.concatenate(outs, axis=1)

    @pl.when(j < tiles_per_sec)
    def _():
        o_ref[...] = rope(acc, HEAD_DIM ** -0.5)

    @pl.when(jnp.logical_and(j >= tiles_per_sec, j < 2 * tiles_per_sec))
    def _():
        o_ref[...] = rope(acc, None)

    @pl.when(j >= 2 * tiles_per_sec)
    def _():
        o_ref[...] = acc.astype(BF16)

    jk, ok = divmod(KC0 * LANES, tn)
    jv, ov = divmod(VC0 * LANES, tn)

    @pl.when(j == jk)
    def _():
        r = rope(acc, None)
        for g in range(NSA_KV_GROUPS):
            kvc_ref[g] = r[:, ok + g * LANES: ok + (g + 1) * LANES]

    @pl.when(j == jv)
    def _():
        for g in range(NSA_KV_GROUPS):
            kvc_ref[NSA_KV_GROUPS + g] = acc[:, ov + g * LANES: ov + (g + 1) * LANES].astype(BF16)


def _inproj(x2, g, scale, shift, w_perm, cos2, sin2, seq):
    m, d = x2.shape
    ncols = w_perm.shape[1]
    tm = min(512, seq)
    tn = 1024
    tiles_per_b = seq // tm
    return pl.pallas_call(
        functools.partial(_inproj_kernel, tn=tn),
        grid=(m // tm, ncols // tn),
        in_specs=[pl.BlockSpec((tm, d), lambda i, j: (i, 0)),
                  pl.BlockSpec((1, d), lambda i, j: (0, 0)),
                  pl.BlockSpec((None, 1, d), lambda i, j: (i // tiles_per_b, 0, 0)),
                  pl.BlockSpec((None, 1, d), lambda i, j: (i // tiles_per_b, 0, 0)),
                  pl.BlockSpec((d, tn), lambda i, j: (0, j)),
                  pl.BlockSpec((tm, LANES), lambda i, j: (i % tiles_per_b, 0)),
                  pl.BlockSpec((tm, LANES), lambda i, j: (i % tiles_per_b, 0))],
        out_specs=[pl.BlockSpec((tm, tn), lambda i, j: (i, j)),
                   pl.BlockSpec((2 * NSA_KV_GROUPS, tm, LANES), lambda i, j: (0, i, 0))],
        out_shape=[jax.ShapeDtypeStruct((m, ncols), BF16),
                   jax.ShapeDtypeStruct((2 * NSA_KV_GROUPS, m, LANES), BF16)],
        scratch_shapes=[pltpu.VMEM((tm, d), BF16)],
        compiler_params=pltpu.CompilerParams(dimension_semantics=("parallel", "arbitrary"),
                                             vmem_limit_bytes=VMEM_LIMIT),
        name="inproj",
    )(x2, g, scale, shift, w_perm, cos2, sin2)


def _compress_kernel(x_ref, pos_ref, w1_ref, w2_ref, o_ref):
    half = CMP_STRIDE * HEAD_DIM
    x = x_ref[...].astype(F32)
    pos = pos_ref[...]
    w1 = w1_ref[...].astype(BF16)
    a = jnp.dot((x + pos[:, :half]).astype(BF16), w1[:half], preferred_element_type=F32)
    b = jnp.dot((x + pos[:, half:]).astype(BF16), w1[half:], preferred_element_type=F32)
    n_rows = x.shape[0]
    h1 = a + pltpu.roll(b, n_rows - 1, 0)
    h = h1 * (1.0 / (1.0 + jnp.exp(-h1)))
    o_ref[...] = jnp.dot(h.astype(BF16), w2_ref[...].astype(BF16),
                         preferred_element_type=F32).astype(BF16)


def _compress(kvc4, pos2, w1s, w2s):
    ns, bsz, n_rows, width = kvc4.shape
    return pl.pallas_call(
        _compress_kernel,
        grid=(ns, bsz),
        in_specs=[pl.BlockSpec((None, None, n_rows, width), lambda s, b: (s, b, 0, 0)),
                  pl.BlockSpec((None, 1, 2 * width), lambda s, b: (s // NSA_KV_GROUPS, 0, 0)),
                  pl.BlockSpec((None, 2 * width, HEAD_DIM), lambda s, b: (s // NSA_KV_GROUPS, 0, 0)),
                  pl.BlockSpec((None, HEAD_DIM, HEAD_DIM), lambda s, b: (s // NSA_KV_GROUPS, 0, 0))],
        out_specs=pl.BlockSpec((None, None, n_rows, HEAD_DIM), lambda s, b: (s, b, 0, 0)),
        out_shape=jax.ShapeDtypeStruct((ns, bsz, n_rows, HEAD_DIM), BF16),
        compiler_params=pltpu.CompilerParams(dimension_semantics=("parallel", "parallel"),
                                             vmem_limit_bytes=VMEM_LIMIT),
        name="compress",
    )(kvc4, pos2, w1s, w2s)


def _topk_mask(score, blk, n_cand, k):
    rank = jnp.zeros(score.shape, I32)
    for m in range(n_cand):
        col = score[:, m:m + 1]
        ge = jnp.where(col >= score, 1, 0)
        gt = jnp.where(col > score, 1, 0)
        rank = rank + jnp.where(blk > m, ge, gt)
    return rank < k


def _flash_step(qa, ka, v, m, l, acc, mask):
    s = _nt_dot(qa, ka)
    if mask is not None:
        s = jnp.where(mask, s, NEG_BIG)
    m_new = jnp.maximum(m, jnp.max(s, axis=1, keepdims=True))
    alpha = jnp.exp(m - m_new)
    p = jnp.exp(s - m_new)
    l = alpha * l + jnp.sum(p, axis=1, keepdims=True)
    acc = alpha * acc + jnp.dot(p.astype(BF16), v, preferred_element_type=F32)
    return m_new, l, acc


def _nsa_kernel(q_ref, kc_ref, vc_ref, ks_ref, vs_ref, kw_ref, vw_ref, gl_ref,
                selmap_ref, et_ref, o_ref, *, tq, seq, ck):
    qi = pl.program_id(2)
    t0 = qi * tq
    rows = NSA_HPG * tq
    n_cmp_pad = kc_ref.shape[0]
    n_sel = seq // SEL_LEN

    q = q_ref[...]
    qs = jnp.concatenate([q[:, h * LANES:(h + 1) * LANES] for h in range(NSA_HPG)], axis=0)

    s = _nt_dot(qs, kc_ref[...])
    tt = t0 + (lax.broadcasted_iota(I32, (rows, n_cmp_pad), 0) & (tq - 1))
    cend = lax.broadcasted_iota(I32, (rows, n_cmp_pad), 1) * CMP_STRIDE + (CMP_LEN - 1)
    s = jnp.where(cend <= tt, s, -jnp.inf)
    mx = jnp.max(s, axis=1, keepdims=True)
    mx = jnp.where(mx == -jnp.inf, 0.0, mx)
    e = jnp.exp(s - mx)
    den = jnp.maximum(jnp.sum(e, axis=1, keepdims=True), 1e-30)
    p_c = e * (1.0 / den)
    o_c = jnp.dot(p_c.astype(BF16), vc_ref[...], preferred_element_type=F32)

    p_sum = p_c[0:tq]
    for h in range(1, NSA_HPG):
        p_sum = p_sum + p_c[h * tq:(h + 1) * tq]
    hi, lo = _split_bf16(p_sum)
    smap = selmap_ref[...]
    imp = (jnp.dot(hi, smap, preferred_element_type=F32)
           + jnp.dot(lo, smap, preferred_element_type=F32))
    blk = lax.broadcasted_iota(I32, (tq, LANES), 1)
    cb = (t0 + lax.broadcasted_iota(I32, (tq, LANES), 0)) // SEL_LEN
    causal = blk <= cb
    forced = (blk == 0) | (blk == cb) | (blk == cb - 1)
    score = jnp.where(causal, jnp.where(forced, jnp.inf, imp), -jnp.inf)
    sel = _topk_mask(score, blk, n_sel, min(SEL_TOPK, n_sel)) & causal
    selb = jnp.where(sel, 0.0, NEG_BIG).astype(BF16)
    qa = jnp.concatenate([qs, jnp.concatenate([selb] * NSA_HPG, axis=0)], axis=1)

    def kv_chunk(j):
        start = pl.multiple_of(j * ck, ck)
        ka = jnp.concatenate([ks_ref[pl.ds(start, ck), :], et_ref[pl.ds(start, ck), :]], axis=1)
        return ka, vs_ref[pl.ds(start, ck), :]

    j_last = t0 // ck
    trow = t0 + (lax.broadcasted_iota(I32, (rows, ck), 0) & (tq - 1))
    kpos = j_last * ck + lax.broadcasted_iota(I32, (rows, ck), 1)
    m0 = jnp.full((rows, 1), NEG_BIG, F32)
    l0 = jnp.zeros((rows, 1), F32)
    a0 = jnp.zeros((rows, HEAD_DIM), F32)
    ka, v = kv_chunk(j_last)
    carry = _flash_step(qa, ka, v, m0, l0, a0, kpos <= trow)

    def body(j, c):
        ka_j, v_j = kv_chunk(j)
        return _flash_step(qa, ka_j, v_j, *c, None)

    m_s, l_s, a_s = lax.fori_loop(0, j_last, body, carry)
    o_s = a_s * (1.0 / l_s)

    lw = min(WINDOW + tq, seq)
    wstart = pl.multiple_of(jnp.maximum(t0 + tq - lw, 0), tq)
    sw = _nt_dot(qs, kw_ref[pl.ds(wstart, lw), :])
    trow_w = t0 + (lax.broadcasted_iota(I32, (rows, lw), 0) & (tq - 1))
    dist = trow_w - (wstart + lax.broadcasted_iota(I32, (rows, lw), 1))
    sw = jnp.where((dist >= 0) & (dist < WINDOW), sw, NEG_BIG)
    mw = jnp.max(sw, axis=1, keepdims=True)
    ew = jnp.exp(sw - mw)
    lw_sum = jnp.sum(ew, axis=1, keepdims=True)
    o_w = jnp.dot(ew.astype(BF16), vw_ref[pl.ds(wstart, lw), :],
                  preferred_element_type=F32) * (1.0 / lw_sum)

    gl = gl_ref[...].astype(F32)
    sig = 1.0 / (1.0 + jnp.exp(-gl))
    for h in range(NSA_HPG):
        r = slice(h * tq, (h + 1) * tq)
        o = (sig[:, 3 * h:3 * h + 1] * o_c[r] + sig[:, 3 * h + 1:3 * h + 2] * o_s[r]
             + sig[:, 3 * h + 2:3 * h + 3] * o_w[r])
        o_ref[:, h * LANES:(h + 1) * LANES] = o.astype(BF16)


def _nsa(qkv, kcv, selmap, et_sel, bsz, seq):
    m = qkv.shape[0]
    tq = min(128, seq)
    ck = min(512, seq)
    nq = seq // tq
    n_cmp_pad = kcv.shape[2]
    gw = NSA_HPG * LANES
    col = lambda c0: (lambda b, g, i: (b, c0 + g))
    full = lambda c0: pl.BlockSpec((seq, LANES), col(c0))
    return pl.pallas_call(
        functools.partial(_nsa_kernel, tq=tq, seq=seq, ck=ck),
        grid=(bsz, NSA_KV_GROUPS, nq),
        in_specs=[pl.BlockSpec((tq, gw), lambda b, g, i: (b * nq + i, g)),
                  pl.BlockSpec((None, None, n_cmp_pad, LANES), lambda b, g, i: (g, b, 0, 0)),
                  pl.BlockSpec((None, None, n_cmp_pad, LANES),
                               lambda b, g, i: (NSA_KV_GROUPS + g, b, 0, 0)),
                  full(KS0), full(VS0), full(KW0), full(VW0),
                  pl.BlockSpec((tq, LANES), lambda b, g, i: (b * nq + i, GT0 + g)),
                  pl.BlockSpec(selmap.shape, lambda b, g, i: (0, 0)),
                  pl.BlockSpec(et_sel.shape, lambda b, g, i: (0, 0))],
        out_specs=pl.BlockSpec((tq, gw), lambda b, g, i: (b * nq + i, g)),
        out_shape=jax.ShapeDtypeStruct((m, NSA_HEADS * HEAD_DIM), BF16),
        compiler_params=pltpu.CompilerParams(
            dimension_semantics=("parallel", "parallel", "arbitrary"),
            vmem_limit_bytes=VMEM_LIMIT),
        name="nsa_attn",
    )(qkv, kcv, kcv, qkv, qkv, qkv, qkv, qkv, selmap, et_sel)


def _moba_kernel(q_ref, k_ref, v_ref, avg_ref, et_ref, o_ref, kmh_ref, kml_ref, *, seq):
    qi = pl.program_id(2)
    blk_len = MOBA_BLOCK
    nb = seq // blk_len

    @pl.when(qi == 0)
    def _():
        km = jnp.dot(avg_ref[...], k_ref[...], preferred_element_type=F32)
        hi, lo = _split_bf16(km)
        kmh_ref[...] = hi
        kml_ref[...] = lo

    q = q_ref[...]
    gate = _nt_dot(q, kmh_ref[...]) + _nt_dot(q, kml_ref[...])
    blk = lax.broadcasted_iota(I32, (blk_len, LANES), 1)
    past = blk < qi
    score = jnp.where(past, gate, -jnp.inf)
    sel = (_topk_mask(score, blk, nb, min(MOBA_TOPK, nb - 1)) & past) | (blk == qi)
    selb = jnp.where(sel, 0.0, NEG_BIG).astype(BF16)
    qa = jnp.concatenate([q, selb], axis=1)

    def kv_chunk(j):
        start = pl.multiple_of(j * blk_len, blk_len)
        ka = jnp.concatenate([k_ref[pl.ds(start, blk_len), :], et_ref[pl.ds(start, blk_len), :]], axis=1)
        return ka, v_ref[pl.ds(start, blk_len), :]

    row = lax.broadcasted_iota(I32, (blk_len, blk_len), 0)
    colk = lax.broadcasted_iota(I32, (blk_len, blk_len), 1)
    m0 = jnp.full((blk_len, 1), NEG_BIG, F32)
    l0 = jnp.zeros((blk_len, 1), F32)
    a0 = jnp.zeros((blk_len, HEAD_DIM), F32)
    ka, v = kv_chunk(qi)
    carry = _flash_step(qa, ka, v, m0, l0, a0, colk <= row)

    def body(j, c):
        ka_j, v_j = kv_chunk(j)
        return _flash_step(qa, ka_j, v_j, *c, None)

    m_f, l_f, a_f = lax.fori_loop(0, qi, body, carry)
    o_ref[...] = (a_f * (1.0 / l_f)).astype(BF16)


def _moba(qkv, avg, et_blk, bsz, seq):
    m = qkv.shape[0]
    nq = seq // MOBA_BLOCK
    return pl.pallas_call(
        functools.partial(_moba_kernel, seq=seq),
        grid=(bsz, MOBA_HEADS, nq),
        in_specs=[pl.BlockSpec((MOBA_BLOCK, LANES), lambda b, h, i: (b * nq + i, QM0 + h)),
                  pl.BlockSpec((seq, LANES), lambda b, h, i: (b, KM0 + h)),
                  pl.BlockSpec((seq, LANES), lambda b, h, i: (b, VM0 + h)),
                  pl.BlockSpec(avg.shape, lambda b, h, i: (0, 0)),
                  pl.BlockSpec(et_blk.shape, lambda b, h, i: (0, 0))],
        out_specs=pl.BlockSpec((MOBA_BLOCK, LANES), lambda b, h, i: (b * nq + i, h)),
        out_shape=jax.ShapeDtypeStruct((m, MOBA_HEADS * HEAD_DIM), BF16),
        scratch_shapes=[pltpu.VMEM((LANES, HEAD_DIM), BF16), pltpu.VMEM((LANES, HEAD_DIM), BF16)],
        compiler_params=pltpu.CompilerParams(
            dimension_semantics=("parallel", "parallel", "arbitrary"),
            vmem_limit_bytes=VMEM_LIMIT),
        name="moba_attn",
    )(qkv, qkv, qkv, avg, et_blk)


def _outproj_kernel(on_ref, om_ref, w_ref, x_ref, g_ref, gate_ref, o_ref):
    kn = on_ref.shape[1]
    acc = (jnp.dot(on_ref[...], w_ref[:kn, :], preferred_element_type=F32)
           + jnp.dot(om_ref[...], w_ref[kn:, :], preferred_element_type=F32))
    o_ref[...] = x_ref[...] + gate_ref[...] * _rms_norm(acc, g_ref[...])


def _outproj(o_n, o_m, w_o, x2, g_post, gate, seq):
    m, d = x2.shape
    tm = min(512, seq)
    tiles_per_b = seq // tm
    return pl.pallas_call(
        _outproj_kernel,
        grid=(m // tm,),
        in_specs=[pl.BlockSpec((tm, o_n.shape[1]), lambda i: (i, 0)),
                  pl.BlockSpec((tm, o_m.shape[1]), lambda i: (i, 0)),
                  pl.BlockSpec(w_o.shape, lambda i: (0, 0)),
                  pl.BlockSpec((tm, d), lambda i: (i, 0)),
                  pl.BlockSpec((1, d), lambda i: (0, 0)),
                  pl.BlockSpec((None, 1, d), lambda i: (i // tiles_per_b, 0, 0))],
        out_specs=pl.BlockSpec((tm, d), lambda i: (i, 0)),
        out_shape=jax.ShapeDtypeStruct((m, d), F32),
        compiler_params=pltpu.CompilerParams(dimension_semantics=("parallel",),
                                             vmem_limit_bytes=VMEM_LIMIT),
        name="outproj",
    )(o_n, o_m, w_o, x2, g_post, gate)


def _ffn_kernel(x_ref, gpre_ref, sc_ref, sh_ref, wu_ref, wd_ref, gpost_ref, gate_ref,
                o_ref, h_ref, acc_ref):
    k = pl.program_id(1)

    @pl.when(k == 0)
    def _():
        h = _rms_norm(x_ref[...], gpre_ref[...]) * (1.0 + sc_ref[...]) + sh_ref[...]
        h_ref[...] = h.astype(BF16)
        acc_ref[...] = jnp.zeros_like(acc_ref)

    u = jnp.dot(h_ref[...], wu_ref[...], preferred_element_type=F32)
    u = jnp.square(jnp.maximum(u, 0.0)).astype(BF16)
    acc_ref[...] += jnp.dot(u, wd_ref[...], preferred_element_type=F32)

    @pl.when(k == pl.num_programs(1) - 1)
    def _():
        o_ref[...] = x_ref[...] + gate_ref[...] * _rms_norm(acc_ref[...], gpost_ref[...])


def _ffn(x2, g_pre, scale, shift, w_up, w_down, g_post, gate, seq):
    m, d = x2.shape
    dff = w_up.shape[1]
    tm = min(512, seq)
    ck = 512
    tiles_per_b = seq // tm
    vec = pl.BlockSpec((1, d), lambda i, k: (0, 0))
    per_b = pl.BlockSpec((None, 1, d), lambda i, k: (i // tiles_per_b, 0, 0))
    return pl.pallas_call(
        _ffn_kernel,
        grid=(m // tm, dff // ck),
        in_specs=[pl.BlockSpec((tm, d), lambda i, k: (i, 0)), vec, per_b, per_b,
                  pl.BlockSpec((d, ck), lambda i, k: (0, k)),
                  pl.BlockSpec((ck, d), lambda i, k: (k, 0)),
                  vec, per_b],
        out_specs=pl.BlockSpec((tm, d), lambda i, k: (i, 0)),
        out_shape=jax.ShapeDtypeStruct((m, d), F32),
        scratch_shapes=[pltpu.VMEM((tm, d), BF16), pltpu.VMEM((tm, d), F32)],
        compiler_params=pltpu.CompilerParams(dimension_semantics=("parallel", "arbitrary"),
                                             vmem_limit_bytes=VMEM_LIMIT),
        name="ffn",
    )(x2, g_pre, scale, shift, w_up, w_down, g_post, gate)


def _permute_w_in(w_in):
    d = w_in.shape[0]
    widths = [NSA_HEADS * HEAD_DIM] + [NSA_KV_GROUPS * HEAD_DIM] * 6 + \
             [3 * NSA_HEADS, MOBA_HEADS * HEAD_DIM, MOBA_HEADS * HEAD_DIM, MOBA_HEADS * HEAD_DIM]
    offs = np.cumsum([0] + widths)
    q_n, k_c, v_c, k_s, v_s, k_w, v_w, g_n, q_m, k_m, v_m = [
        w_in[:, offs[i]:offs[i + 1]] for i in range(11)]
    pad = lambda n: jnp.zeros((d, n), w_in.dtype)
    gates = []
    per_group = 3 * NSA_HPG
    for g in range(NSA_KV_GROUPS):
        gates += [g_n[:, g * per_group:(g + 1) * per_group], pad(LANES - per_group)]
    cols = [q_n, q_m, k_m, k_c, k_s, k_w, pad(2 * LANES), v_m, v_c, v_s, v_w] + gates
    return jnp.concatenate(cols, axis=1).astype(BF16)


def _constants(seq):
    n_cmp_pad = seq // CMP_STRIDE
    n_cmp = (seq - CMP_LEN) // CMP_STRIDE + 1
    n_sel = seq // SEL_LEN
    cs = np.arange(n_cmp) * CMP_STRIDE
    ss = np.arange(n_sel) * SEL_LEN
    overlap = np.clip(np.minimum(cs[:, None] + CMP_LEN, ss[None, :] + SEL_LEN)
                      - np.maximum(cs[:, None], ss[None, :]), 0, None)
    selmap = np.zeros((n_cmp_pad, LANES), np.float32)
    selmap[:n_cmp, :n_sel] = overlap / CMP_LEN
    keys = np.arange(seq)
    et_sel = (keys[:, None] // SEL_LEN == np.arange(LANES)[None, :]).astype(np.float32)
    et_blk = (keys[:, None] // MOBA_BLOCK == np.arange(LANES)[None, :]).astype(np.float32)
    avg = et_blk.T / MOBA_BLOCK
    to = lambda a: jnp.asarray(a, BF16)
    return to(selmap), to(et_sel), to(et_blk), to(avg)


def _rope_tables(seq):
    pos = jnp.arange(seq, dtype=F32)
    inv = ROPE_THETA ** (-jnp.arange(0, HEAD_DIM, 2, dtype=F32) / HEAD_DIM)
    ang = pos[:, None] * inv[None, :]
    cos, sin = jnp.cos(ang), jnp.sin(ang)
    return jnp.concatenate([cos, cos], axis=1), jnp.concatenate([-sin, sin], axis=1)


def kernel(x, c, w_ada, b_ada, pre_norm_mix, post_norm_mix, w_in, cmp_k_pos, cmp_k_w1, cmp_k_w2,
           cmp_v_pos, cmp_v_w1, cmp_v_w2, w_o, pre_norm_ffn, post_norm_ffn, w_up, w_down):
    bsz, seq, d = x.shape
    depth = w_ada.shape[0]
    cos2, sin2 = _rope_tables(seq)
    selmap, et_sel, et_blk, avg = _constants(seq)
    x2 = x.reshape(bsz * seq, d)
    for l in range(depth):
        mod = _adaln(c, w_ada[l], b_ada[l])
        shift_m, scale_m, gate_m, shift_f, scale_f, gate_f = [
            a.reshape(bsz, 1, d) for a in jnp.split(mod, 6, axis=-1)]
        row = lambda a: a.reshape(1, d)

        qkv, kvc = _inproj(x2, row(pre_norm_mix[l]), scale_m, shift_m, _permute_w_in(w_in[l]),
                           cos2, sin2, seq)
        kvc4 = kvc.reshape(2 * NSA_KV_GROUPS, bsz, seq // CMP_STRIDE, CMP_STRIDE * HEAD_DIM)
        pos2 = jnp.stack([cmp_k_pos[l].reshape(1, -1), cmp_v_pos[l].reshape(1, -1)])
        kcv = _compress(kvc4, pos2, jnp.stack([cmp_k_w1[l], cmp_v_w1[l]]),
                        jnp.stack([cmp_k_w2[l], cmp_v_w2[l]]))
        o_n = _nsa(qkv, kcv, selmap, et_sel, bsz, seq)
        o_m = _moba(qkv, avg, et_blk, bsz, seq)
        x2 = _outproj(o_n, o_m, w_o[l].astype(BF16), x2, row(post_norm_mix[l]), gate_m, seq)
        x2 = _ffn(x2, row(pre_norm_ffn[l]), scale_f, shift_f, w_up[l].astype(BF16),
                  w_down[l].astype(BF16), row(post_norm_ffn[l]), gate_f, seq)
    return x2.reshape(bsz, seq, d)
```

```python
import functools

import numpy as np
import jax
import jax.numpy as jnp
from jax import lax
from jax.experimental import pallas as pl
from jax.experimental.pallas import tpu as pltpu

F32 = jnp.float32
BF16 = jnp.bfloat16
I32 = jnp.int32

HEAD_DIM = 128
NSA_HEADS = 8
NSA_KV_GROUPS = 2
NSA_HPG = NSA_HEADS // NSA_KV_GROUPS
MOBA_HEADS = 8
CMP_LEN = 32
CMP_STRIDE = 16
SEL_LEN = 64
SEL_TOPK = 16
WINDOW = 512
MOBA_BLOCK = 256
MOBA_TOPK = 3
ROPE_THETA = 10000.0
EPS = 1e-6

LANES = 128
BF16_SUBLANES = 16
NSA_TQ = 256
MOBA_HEADS_PER_STEP = 8
NEG_BIG = -1e30
VMEM_LIMIT = 56 * 1024 * 1024

QN0 = 0
QM0 = 8
KM0 = 16
KC0 = 24
KS0 = 26
KW0 = 28
VM0 = 32
VC0 = 40
VS0 = 42
VW0 = 44
GT0 = 46
N_BLOCKS = 48
SEC_BLOCKS = 16


def _nt_dot(a, b):
    return lax.dot_general(a, b, (((1,), (1,)), ((), ())), preferred_element_type=F32)


def _rms_norm(x, g):
    ms = jnp.mean(x * x, axis=-1, keepdims=True)
    return x * lax.rsqrt(ms + EPS) * g


def _split_bf16(a):
    hi = a.astype(BF16)
    lo = (a - hi.astype(F32)).astype(BF16)
    return hi, lo


def _adaln_kernel(c_ref, w_ref, b_ref, o_ref):
    c = c_ref[...]
    a = c * (1.0 / (1.0 + jnp.exp(-c)))
    o_ref[...] = jnp.dot(a, w_ref[...], preferred_element_type=F32,
                         precision=lax.Precision.HIGHEST) + b_ref[...]


def _adaln(c, w, b):
    bsz, d = c.shape
    n = w.shape[1]
    rows = 8
    tn = 1024
    c8 = jnp.zeros((rows, d), F32).at[:bsz].set(c)
    out = pl.pallas_call(
        _adaln_kernel,
        grid=(n // tn,),
        in_specs=[pl.BlockSpec((rows, d), lambda j: (0, 0)),
                  pl.BlockSpec((d, tn), lambda j: (0, j)),
                  pl.BlockSpec((1, tn), lambda j: (0, j))],
        out_specs=pl.BlockSpec((rows, tn), lambda j: (0, j)),
        out_shape=jax.ShapeDtypeStruct((rows, n), F32),
        compiler_params=pltpu.CompilerParams(dimension_semantics=("parallel",),
                                             vmem_limit_bytes=VMEM_LIMIT),
        name="adaln",
    )(c8, w, b.reshape(1, n))
    return out[:bsz]


def _inproj_kernel(x_ref, g_ref, sc_ref, sh_ref, w_ref, cos_ref, sin_ref,
                   o_ref, kvc_ref, h_ref, *, tn):
    j = pl.program_id(1)
    tiles_per_sec = SEC_BLOCKS * LANES // tn

    @pl.when(j == 0)
    def _():
        h = _rms_norm(x_ref[...], g_ref[...]) * (1.0 + sc_ref[...]) + sh_ref[...]
        h_ref[...] = h.astype(BF16)

    acc = jnp.dot(h_ref[...], w_ref[...], preferred_element_type=F32)

    def rope(a, scale):
        cos = cos_ref[...]
        sin = sin_ref[...]
        outs = []
        for c in range(tn // LANES):
            ch = a[:, c * LANES:(c + 1) * LANES]
            r = ch * cos + pltpu.roll(ch, HEAD_DIM // 2, 1) * sin
            if scale is not None:
                r = r * scale
            outs.append(r.astype(BF16))
        return jnp.concatenate(outs, axis=1)

    @pl.when(j < tiles_per_sec)
    def _():
        o_ref[...] = rope(acc, HEAD_DIM ** -0.5)

    @pl.when(jnp.logical_and(j >= tiles_per_sec, j < 2 * tiles_per_sec))
    def _():
        o_ref[...] = rope(acc, None)

    @pl.when(j >= 2 * tiles_per_sec)
    def _():
        o_ref[...] = acc.astype(BF16)

    jk, ok = divmod(KC0 * LANES, tn)
    jv, ov = divmod(VC0 * LANES, tn)

    @pl.when(j == jk)
    def _():
        r = rope(acc, None)
        for g in range(NSA_KV_GROUPS):
            kvc_ref[g] = r[:, ok + g * LANES: ok + (g + 1) * LANES]

    @pl.when(j == jv)
    def _():
        for g in range(NSA_KV_GROUPS):
            kvc_ref[NSA_KV_GROUPS + g] = acc[:, ov + g * LANES: ov + (g + 1) * LANES].astype(BF16)


def _inproj(x2, g, scale, shift, w_perm, cos2, sin2, seq):
    m, d = x2.shape
    ncols = w_perm.shape[1]
    tm = min(512, seq)
    tn = 1024
    tiles_per_b = seq // tm
    return pl.pallas_call(
        functools.partial(_inproj_kernel, tn=tn),
        grid=(m // tm, ncols // tn),
        in_specs=[pl.BlockSpec((tm, d), lambda i, j: (i, 0)),
                  pl.BlockSpec((1, d), lambda i, j: (0, 0)),
                  pl.BlockSpec((None, 1, d), lambda i, j: (i // tiles_per_b, 0, 0)),
                  pl.BlockSpec((None, 1, d), lambda i, j: (i // tiles_per_b, 0, 0)),
                  pl.BlockSpec((d, tn), lambda i, j: (0, j)),
                  pl.BlockSpec((tm, LANES), lambda i, j: (i % tiles_per_b, 0)),
                  pl.BlockSpec((tm, LANES), lambda i, j: (i % tiles_per_b, 0))],
        out_specs=[pl.BlockSpec((tm, tn), lambda i, j: (i, j)),
                   pl.BlockSpec((2 * NSA_KV_GROUPS, tm, LANES), lambda i, j: (0, i, 0))],
        out_shape=[jax.ShapeDtypeStruct((m, ncols), BF16),
                   jax.ShapeDtypeStruct((2 * NSA_KV_GROUPS, m, LANES), BF16)],
        scratch_shapes=[pltpu.VMEM((tm, d), BF16)],
        compiler_params=pltpu.CompilerParams(dimension_semantics=("parallel", "arbitrary"),
                                             vmem_limit_bytes=VMEM_LIMIT),
        name="inproj",
    )(x2, g, scale, shift, w_perm, cos2, sin2)


def _compress_kernel(x_ref, pos_ref, w1_ref, w2_ref, o_ref):
    half = CMP_STRIDE * HEAD_DIM
    x = x_ref[...].astype(F32)
    pos = pos_ref[...]
    w1 = w1_ref[...].astype(BF16)
    a = jnp.dot((x + pos[:, :half]).astype(BF16), w1[:half], preferred_element_type=F32)
    b = jnp.dot((x + pos[:, half:]).astype(BF16), w1[half:], preferred_element_type=F32)
    n_rows = x.shape[0]
    h1 = a + pltpu.roll(b, n_rows - 1, 0)
    h = h1 * (1.0 / (1.0 + jnp.exp(-h1)))
    o_ref[...] = jnp.dot(h.astype(BF16), w2_ref[...].astype(BF16),
                         preferred_element_type=F32).astype(BF16)


def _compress(kvc4, pos2, w1s, w2s):
    ns, bsz, n_rows, width = kvc4.shape
    return pl.pallas_call(
        _compress_kernel,
        grid=(ns, bsz),
        in_specs=[pl.BlockSpec((None, None, n_rows, width), lambda s, b: (s, b, 0, 0)),
                  pl.BlockSpec((None, 1, 2 * width), lambda s, b: (s // NSA_KV_GROUPS, 0, 0)),
                  pl.BlockSpec((None, 2 * width, HEAD_DIM), lambda s, b: (s // NSA_KV_GROUPS, 0, 0)),
                  pl.BlockSpec((None, HEAD_DIM, HEAD_DIM), lambda s, b: (s // NSA_KV_GROUPS, 0, 0))],
        out_specs=pl.BlockSpec((None, None, n_rows, HEAD_DIM), lambda s, b: (s, b, 0, 0)),
        out_shape=jax.ShapeDtypeStruct((ns, bsz, n_rows, HEAD_DIM), BF16),
        compiler_params=pltpu.CompilerParams(dimension_semantics=("parallel", "parallel"),
                                             vmem_limit_bytes=VMEM_LIMIT),
        name="compress",
    )(kvc4, pos2, w1s, w2s)


def _topk_rows(score, k):
    idx = lax.broadcasted_iota(I32, score.shape, 0).astype(F32)
    work = score
    sel = jnp.zeros(score.shape, F32)
    for _ in range(k):
        mx = jnp.max(work, axis=0, keepdims=True)
        first = jnp.min(jnp.where(work == mx, idx, float(score.shape[0])), axis=0, keepdims=True)
        hit = idx == first
        sel = jnp.where(hit, 1.0, sel)
        work = jnp.where(hit, -jnp.inf, work)
    return sel


def _bias_from_keep(keep_t):
    n, q = keep_t.shape
    padded = jnp.concatenate([keep_t, jnp.zeros((LANES - n, q), F32)], axis=0) if n < LANES else keep_t
    return jnp.where(padded.T > 0.5, 0.0, NEG_BIG).astype(BF16)


def _with_ones(v):
    return jnp.concatenate([v, jnp.ones(v.shape, v.dtype)], axis=1)


def _flash_step(qa, ka, va, m, acc, mask):
    s = _nt_dot(qa, ka)
    if mask is not None:
        s = jnp.where(mask, s, NEG_BIG)
    m_new = jnp.maximum(m, jnp.max(s, axis=1, keepdims=True))
    alpha = jnp.exp(m - m_new)
    p = jnp.exp(s - m_new).astype(BF16)
    acc = alpha * acc + jnp.dot(p, va, preferred_element_type=F32)
    return m_new, acc


def _flash_out(acc):
    return acc[:, :HEAD_DIM] * (1.0 / acc[:, HEAD_DIM:])


def _nsa_kernel(q_ref, kc_ref, vc_ref, ks_ref, vs_ref, kw_ref, vw_ref, gl_ref,
                selmap_ref, et_ref, o_ref, *, tq, seq):
    qi = pl.program_id(2)
    t0 = qi * tq
    rows = NSA_HPG * tq
    n_cmp_pad = kc_ref.shape[0]
    n_sel = seq // SEL_LEN

    q = q_ref[...]
    qs = jnp.concatenate([q[:, h * LANES:(h + 1) * LANES] for h in range(NSA_HPG)], axis=0)

    s = _nt_dot(qs, kc_ref[...])
    tt = t0 + (lax.broadcasted_iota(I32, (rows, n_cmp_pad), 0) & (tq - 1))
    cend = lax.broadcasted_iota(I32, (rows, n_cmp_pad), 1) * CMP_STRIDE + (CMP_LEN - 1)
    s = jnp.where(cend <= tt, s, -jnp.inf)
    mx = jnp.max(s, axis=1, keepdims=True)
    mx = jnp.where(mx == -jnp.inf, 0.0, mx)
    e = jnp.exp(s - mx)
    den = jnp.maximum(jnp.sum(e, axis=1, keepdims=True), 1e-30)
    p_c = e * (1.0 / den)
    o_c = jnp.dot(p_c.astype(BF16), vc_ref[...], preferred_element_type=F32)

    p_sum = p_c[0:tq]
    for h in range(1, NSA_HPG):
        p_sum = p_sum + p_c[h * tq:(h + 1) * tq]
    hi, lo = _split_bf16(p_sum)
    smap_t = selmap_ref[...]
    imp_t = _nt_dot(smap_t, hi) + _nt_dot(smap_t, lo)
    blk = lax.broadcasted_iota(I32, (n_sel, tq), 0)
    cb = (t0 + lax.broadcasted_iota(I32, (n_sel, tq), 1)) // SEL_LEN
    causal = blk <= cb
    forced = (blk == 0) | (blk == cb) | (blk == cb - 1)
    n_forced = 3
    free = jnp.where(causal, jnp.where(forced, 0.0, 1.0), 0.0)
    picked = _topk_rows(jnp.where(free > 0.5, imp_t, -jnp.inf), max(min(SEL_TOPK, n_sel) - n_forced, 0))
    keep_t = jnp.where(forced, 1.0, picked * free)
    selb = _bias_from_keep(keep_t)
    qa = jnp.concatenate([qs, jnp.concatenate([selb] * NSA_HPG, axis=0)], axis=1)

    def kv_chunk(j):
        start = pl.multiple_of(j * tq, tq)
        ka = jnp.concatenate([ks_ref[pl.ds(start, tq), :], et_ref[pl.ds(start, tq), :]], axis=1)
        return ka, _with_ones(vs_ref[pl.ds(start, tq), :])

    qrow = lax.broadcasted_iota(I32, (rows, tq), 0) & (tq - 1)
    kcol = lax.broadcasted_iota(I32, (rows, tq), 1)
    m0 = jnp.full((rows, 1), NEG_BIG, F32)
    a0 = jnp.zeros((rows, 2 * HEAD_DIM), F32)
    ka, va = kv_chunk(qi)
    carry = _flash_step(qa, ka, va, m0, a0, kcol <= qrow)

    def body(j, c):
        ka_j, va_j = kv_chunk(j)
        return _flash_step(qa, ka_j, va_j, *c, None)

    _, a_s = lax.fori_loop(0, qi, body, carry)
    o_s = _flash_out(a_s)

    n_back = WINDOW // tq
    scores, values = [], []
    for back in range(n_back + 1):
        start = pl.multiple_of(jnp.maximum(qi - back, 0) * tq, tq)
        s_b = _nt_dot(qs, kw_ref[pl.ds(start, tq), :])
        if back == 0:
            s_b = jnp.where(kcol <= qrow, s_b, NEG_BIG)
        else:
            if back == n_back:
                s_b = jnp.where(kcol > qrow, s_b, NEG_BIG)
            s_b = s_b + jnp.where(qi >= back, 0.0, NEG_BIG)
        scores.append(s_b)
        values.append(_with_ones(vw_ref[pl.ds(start, tq), :]))
    mw = scores[0].max(axis=1, keepdims=True)
    for s_b in scores[1:]:
        mw = jnp.maximum(mw, s_b.max(axis=1, keepdims=True))
    a_w = None
    for s_b, v_b in zip(scores, values):
        t = jnp.dot(jnp.exp(s_b - mw).astype(BF16), v_b, preferred_element_type=F32)
        a_w = t if a_w is None else a_w + t
    o_w = _flash_out(a_w)

    gl = gl_ref[...].astype(F32)
    sig = 1.0 / (1.0 + jnp.exp(-gl))
    for h in range(NSA_HPG):
        r = slice(h * tq, (h + 1) * tq)
        o = (sig[:, 3 * h:3 * h + 1] * o_c[r] + sig[:, 3 * h + 1:3 * h + 2] * o_s[r]
             + sig[:, 3 * h + 2:3 * h + 3] * o_w[r])
        o_ref[:, h * LANES:(h + 1) * LANES] = o.astype(BF16)


def _nsa(qkv, kcv, selmap, et_sel, bsz, seq):
    m = qkv.shape[0]
    tq = NSA_TQ
    assert seq % tq == 0 and WINDOW % tq == 0 and tq % SEL_LEN == 0
    nq = seq // tq
    n_cmp_pad = kcv.shape[2]
    gw = NSA_HPG * LANES
    col = lambda c0: (lambda b, g, i: (b, c0 + g))
    full = lambda c0: pl.BlockSpec((seq, LANES), col(c0))
    return pl.pallas_call(
        functools.partial(_nsa_kernel, tq=tq, seq=seq),
        grid=(bsz, NSA_KV_GROUPS, nq),
        in_specs=[pl.BlockSpec((tq, gw), lambda b, g, i: (b * nq + i, g)),
                  pl.BlockSpec((None, None, n_cmp_pad, LANES), lambda b, g, i: (g, b, 0, 0)),
                  pl.BlockSpec((None, None, n_cmp_pad, LANES),
                               lambda b, g, i: (NSA_KV_GROUPS + g, b, 0, 0)),
                  full(KS0), full(VS0), full(KW0), full(VW0),
                  pl.BlockSpec((tq, LANES), lambda b, g, i: (b * nq + i, GT0 + g)),
                  pl.BlockSpec(selmap.shape, lambda b, g, i: (0, 0)),
                  pl.BlockSpec(et_sel.shape, lambda b, g, i: (0, 0))],
        out_specs=pl.BlockSpec((tq, gw), lambda b, g, i: (b * nq + i, g)),
        out_shape=jax.ShapeDtypeStruct((m, NSA_HEADS * HEAD_DIM), BF16),
        compiler_params=pltpu.CompilerParams(
            dimension_semantics=("parallel", "parallel", "arbitrary"),
            vmem_limit_bytes=VMEM_LIMIT),
        name="nsa_attn",
    )(qkv, kcv, kcv, qkv, qkv, qkv, qkv, qkv, selmap, et_sel)


def _moba_kernel(q_ref, k_ref, v_ref, avg_ref, et_ref, o_ref, kmh_ref, kml_ref, *, seq, hb):
    qi = pl.program_id(2)
    blk_len = MOBA_BLOCK
    nb = seq // blk_len
    nb_rows = avg_ref.shape[0]
    head = lambda h: slice(h * LANES, (h + 1) * LANES)

    @pl.when(qi == 0)
    def _():
        for h in range(hb):
            km = jnp.dot(avg_ref[...], k_ref[:, head(h)], preferred_element_type=F32)
            kmh_ref[h], kml_ref[h] = _split_bf16(km)

    blk = lax.broadcasted_iota(I32, (nb_rows, blk_len), 0)
    past = jnp.where(blk < qi, 1.0, 0.0)
    own = jnp.where(blk == qi, 1.0, 0.0)
    qas = []
    for h in range(hb):
        q = q_ref[:, head(h)]
        gate_t = _nt_dot(kmh_ref[h], q) + _nt_dot(kml_ref[h], q)
        picked = _topk_rows(jnp.where(past > 0.5, gate_t, -jnp.inf), min(MOBA_TOPK, nb - 1))
        qas.append(jnp.concatenate([q, _bias_from_keep(picked * past + own)], axis=1))

    def kv_chunk(j, h):
        rows = pl.ds(pl.multiple_of(j * blk_len, blk_len), blk_len)
        ka = jnp.concatenate([k_ref[rows, head(h)], et_ref[rows, :]], axis=1)
        return ka, _with_ones(v_ref[rows, head(h)])

    row = lax.broadcasted_iota(I32, (blk_len, blk_len), 0)
    colk = lax.broadcasted_iota(I32, (blk_len, blk_len), 1)
    m0 = jnp.full((blk_len, 1), NEG_BIG, F32)
    a0 = jnp.zeros((blk_len, 2 * HEAD_DIM), F32)
    carry = []
    for h in range(hb):
        carry += _flash_step(qas[h], *kv_chunk(qi, h), m0, a0, colk <= row)

    def body(j, c):
        new = []
        for h in range(hb):
            new += _flash_step(qas[h], *kv_chunk(j, h), c[2 * h], c[2 * h + 1], None)
        return tuple(new)

    final = lax.fori_loop(0, qi, body, tuple(carry))
    for h in range(hb):
        o_ref[:, head(h)] = _flash_out(final[2 * h + 1]).astype(BF16)


def _moba(qkv, avg, et_blk, bsz, seq):
    m = qkv.shape[0]
    nq = seq // MOBA_BLOCK
    hb = MOBA_HEADS_PER_STEP
    assert MOBA_HEADS % hb == 0 and QM0 % hb == 0 and KM0 % hb == 0 and VM0 % hb == 0
    hw = hb * LANES
    return pl.pallas_call(
        functools.partial(_moba_kernel, seq=seq, hb=hb),
        grid=(bsz, MOBA_HEADS // hb, nq),
        in_specs=[pl.BlockSpec((MOBA_BLOCK, hw), lambda b, h, i: (b * nq + i, QM0 // hb + h)),
                  pl.BlockSpec((seq, hw), lambda b, h, i: (b, KM0 // hb + h)),
                  pl.BlockSpec((seq, hw), lambda b, h, i: (b, VM0 // hb + h)),
                  pl.BlockSpec(avg.shape, lambda b, h, i: (0, 0)),
                  pl.BlockSpec(et_blk.shape, lambda b, h, i: (0, 0))],
        out_specs=pl.BlockSpec((MOBA_BLOCK, hw), lambda b, h, i: (b * nq + i, h)),
        out_shape=jax.ShapeDtypeStruct((m, MOBA_HEADS * HEAD_DIM), BF16),
        scratch_shapes=[pltpu.VMEM((hb, avg.shape[0], HEAD_DIM), BF16),
                        pltpu.VMEM((hb, avg.shape[0], HEAD_DIM), BF16)],
        compiler_params=pltpu.CompilerParams(
            dimension_semantics=("parallel", "parallel", "arbitrary"),
            vmem_limit_bytes=VMEM_LIMIT),
        name="moba_attn",
    )(qkv, qkv, qkv, avg, et_blk)


def _outproj_kernel(on_ref, om_ref, w_ref, x_ref, g_ref, gate_ref, o_ref):
    kn = on_ref.shape[1]
    acc = (jnp.dot(on_ref[...], w_ref[:kn, :], preferred_element_type=F32)
           + jnp.dot(om_ref[...], w_ref[kn:, :], preferred_element_type=F32))
    o_ref[...] = x_ref[...] + gate_ref[...] * _rms_norm(acc, g_ref[...])


def _outproj(o_n, o_m, w_o, x2, g_post, gate, seq):
    m, d = x2.shape
    tm = min(512, seq)
    tiles_per_b = seq // tm
    return pl.pallas_call(
        _outproj_kernel,
        grid=(m // tm,),
        in_specs=[pl.BlockSpec((tm, o_n.shape[1]), lambda i: (i, 0)),
                  pl.BlockSpec((tm, o_m.shape[1]), lambda i: (i, 0)),
                  pl.BlockSpec(w_o.shape, lambda i: (0, 0)),
                  pl.BlockSpec((tm, d), lambda i: (i, 0)),
                  pl.BlockSpec((1, d), lambda i: (0, 0)),
                  pl.BlockSpec((None, 1, d), lambda i: (i // tiles_per_b, 0, 0))],
        out_specs=pl.BlockSpec((tm, d), lambda i: (i, 0)),
        out_shape=jax.ShapeDtypeStruct((m, d), F32),
        compiler_params=pltpu.CompilerParams(dimension_semantics=("parallel",),
                                             vmem_limit_bytes=VMEM_LIMIT),
        name="outproj",
    )(o_n, o_m, w_o, x2, g_post, gate)


def _ffn_kernel(x_ref, gpre_ref, sc_ref, sh_ref, wu_ref, wd_ref, gpost_ref, gate_ref,
                o_ref, h_ref, acc_ref):
    k = pl.program_id(1)

    @pl.when(k == 0)
    def _():
        h = _rms_norm(x_ref[...], gpre_ref[...]) * (1.0 + sc_ref[...]) + sh_ref[...]
        h_ref[...] = h.astype(BF16)
        acc_ref[...] = jnp.zeros_like(acc_ref)

    u = jnp.dot(h_ref[...], wu_ref[...], preferred_element_type=F32)
    u = jnp.square(jnp.maximum(u, 0.0)).astype(BF16)
    acc_ref[...] += jnp.dot(u, wd_ref[...], preferred_element_type=F32)

    @pl.when(k == pl.num_programs(1) - 1)
    def _():
        o_ref[...] = x_ref[...] + gate_ref[...] * _rms_norm(acc_ref[...], gpost_ref[...])


def _ffn(x2, g_pre, scale, shift, w_up, w_down, g_post, gate, seq):
    m, d = x2.shape
    dff = w_up.shape[1]
    tm = min(512, seq)
    ck = 512
    tiles_per_b = seq // tm
    vec = pl.BlockSpec((1, d), lambda i, k: (0, 0))
    per_b = pl.BlockSpec((None, 1, d), lambda i, k: (i // tiles_per_b, 0, 0))
    return pl.pallas_call(
        _ffn_kernel,
        grid=(m // tm, dff // ck),
        in_specs=[pl.BlockSpec((tm, d), lambda i, k: (i, 0)), vec, per_b, per_b,
                  pl.BlockSpec((d, ck), lambda i, k: (0, k)),
                  pl.BlockSpec((ck, d), lambda i, k: (k, 0)),
                  vec, per_b],
        out_specs=pl.BlockSpec((tm, d), lambda i, k: (i, 0)),
        out_shape=jax.ShapeDtypeStruct((m, d), F32),
        scratch_shapes=[pltpu.VMEM((tm, d), BF16), pltpu.VMEM((tm, d), F32)],
        compiler_params=pltpu.CompilerParams(dimension_semantics=("parallel", "arbitrary"),
                                             vmem_limit_bytes=VMEM_LIMIT),
        name="ffn",
    )(x2, g_pre, scale, shift, w_up, w_down, g_post, gate)


def _permute_w_in(w_in):
    d = w_in.shape[0]
    widths = [NSA_HEADS * HEAD_DIM] + [NSA_KV_GROUPS * HEAD_DIM] * 6 + \
             [3 * NSA_HEADS, MOBA_HEADS * HEAD_DIM, MOBA_HEADS * HEAD_DIM, MOBA_HEADS * HEAD_DIM]
    offs = np.cumsum([0] + widths)
    q_n, k_c, v_c, k_s, v_s, k_w, v_w, g_n, q_m, k_m, v_m = [
        w_in[:, offs[i]:offs[i + 1]] for i in range(11)]
    pad = lambda n: jnp.zeros((d, n), w_in.dtype)
    gates = []
    per_group = 3 * NSA_HPG
    for g in range(NSA_KV_GROUPS):
        gates += [g_n[:, g * per_group:(g + 1) * per_group], pad(LANES - per_group)]
    cols = [q_n, q_m, k_m, k_c, k_s, k_w, pad(2 * LANES), v_m, v_c, v_s, v_w] + gates
    return jnp.concatenate(cols, axis=1).astype(BF16)


def _constants(seq):
    n_cmp_pad = seq // CMP_STRIDE
    n_cmp = (seq - CMP_LEN) // CMP_STRIDE + 1
    n_sel = seq // SEL_LEN
    cs = np.arange(n_cmp) * CMP_STRIDE
    ss = np.arange(n_sel) * SEL_LEN
    overlap = np.clip(np.minimum(cs[:, None] + CMP_LEN, ss[None, :] + SEL_LEN)
                      - np.maximum(cs[:, None], ss[None, :]), 0, None)
    selmap_t = np.zeros((n_sel, n_cmp_pad), np.float32)
    selmap_t[:, :n_cmp] = (overlap / CMP_LEN).T
    keys = np.arange(seq)
    et_sel = (keys[:, None] // SEL_LEN == np.arange(LANES)[None, :]).astype(np.float32)
    et_blk = (keys[:, None] // MOBA_BLOCK == np.arange(LANES)[None, :]).astype(np.float32)
    nb_rows = -(-(seq // MOBA_BLOCK) // BF16_SUBLANES) * BF16_SUBLANES
    avg = et_blk.T[:nb_rows] / MOBA_BLOCK
    to = lambda a: jnp.asarray(a, BF16)
    return to(selmap_t), to(et_sel), to(et_blk), to(avg)


def _rope_tables(seq):
    pos = jnp.arange(seq, dtype=F32)
    inv = ROPE_THETA ** (-jnp.arange(0, HEAD_DIM, 2, dtype=F32) / HEAD_DIM)
    ang = pos[:, None] * inv[None, :]
    cos, sin = jnp.cos(ang), jnp.sin(ang)
    return jnp.concatenate([cos, cos], axis=1), jnp.concatenate([-sin, sin], axis=1)


def kernel(x, c, w_ada, b_ada, pre_norm_mix, post_norm_mix, w_in, cmp_k_pos, cmp_k_w1, cmp_k_w2,
           cmp_v_pos, cmp_v_w1, cmp_v_w2, w_o, pre_norm_ffn, post_norm_ffn, w_up, w_down):
    bsz, seq, d = x.shape
    depth = w_ada.shape[0]
    cos2, sin2 = _rope_tables(seq)
    selmap, et_sel, et_blk, avg = _constants(seq)
    x2 = x.reshape(bsz * seq, d)
    for l in range(depth):
        mod = _adaln(c, w_ada[l], b_ada[l])
        shift_m, scale_m, gate_m, shift_f, scale_f, gate_f = [
            a.reshape(bsz, 1, d) for a in jnp.split(mod, 6, axis=-1)]
        row = lambda a: a.reshape(1, d)

        qkv, kvc = _inproj(x2, row(pre_norm_mix[l]), scale_m, shift_m, _permute_w_in(w_in[l]),
                           cos2, sin2, seq)
        kvc4 = kvc.reshape(2 * NSA_KV_GROUPS, bsz, seq // CMP_STRIDE, CMP_STRIDE * HEAD_DIM)
        pos2 = jnp.stack([cmp_k_pos[l].reshape(1, -1), cmp_v_pos[l].reshape(1, -1)])
        kcv = _compress(kvc4, pos2, jnp.stack([cmp_k_w1[l], cmp_v_w1[l]]),
                        jnp.stack([cmp_k_w2[l], cmp_v_w2[l]]))
        o_n = _nsa(qkv, kcv, selmap, et_sel, bsz, seq)
        o_m = _moba(qkv, avg, et_blk, bsz, seq)
        x2 = _outproj(o_n, o_m, w_o[l].astype(BF16), x2, row(post_norm_mix[l]), gate_m, seq)
        x2 = _ffn(x2, row(pre_norm_ffn[l]), scale_f, shift_f, w_up[l].astype(BF16),
                  w_down[l].astype(BF16), row(post_norm_ffn[l]), gate_f, seq)
    return x2.reshape(bsz, seq, d)
```

```python
import functools

import numpy as np
import jax
import jax.numpy as jnp
from jax import lax
from jax.experimental import pallas as pl
from jax.experimental.pallas import tpu as pltpu

F32 = jnp.float32
BF16 = jnp.bfloat16
I32 = jnp.int32

HEAD_DIM = 128
NSA_HEADS = 8
NSA_KV_GROUPS = 2
NSA_HPG = NSA_HEADS // NSA_KV_GROUPS
MOBA_HEADS = 8
CMP_LEN = 32
CMP_STRIDE = 16
SEL_LEN = 64
SEL_TOPK = 16
WINDOW = 512
MOBA_BLOCK = 256
MOBA_TOPK = 3
ROPE_THETA = 10000.0
EPS = 1e-6

LANES = 128
BF16_SUBLANES = 16
NSA_TQ = 256
MOBA_HEADS_PER_STEP = 8
NEG_BIG = -1e30
VMEM_LIMIT = 56 * 1024 * 1024

QN0 = 0
QM0 = 8
KM0 = 16
KC0 = 24
KS0 = 26
KW0 = 28
VM0 = 32
VC0 = 40
VS0 = 42
VW0 = 44
GT0 = 46
N_BLOCKS = 48
SEC_BLOCKS = 16


def _nt_dot(a, b):
    return lax.dot_general(a, b, (((1,), (1,)), ((), ())), preferred_element_type=F32)


def _rms_norm(x, g):
    ms = jnp.mean(x * x, axis=-1, keepdims=True)
    return x * lax.rsqrt(ms + EPS) * g


def _split_bf16(a):
    hi = a.astype(BF16)
    lo = (a - hi.astype(F32)).astype(BF16)
    return hi, lo


def _adaln_kernel(ct_ref, w_ref, b_ref, o_ref):
    ct = ct_ref[...]
    a = ct * (1.0 / (1.0 + jnp.exp(-ct)))
    w = w_ref[...]
    for b in range(o_ref.shape[0]):
        o_ref[b] = jnp.sum(w * a[:, b:b + 1], axis=0, keepdims=True) + b_ref[...]


def _adaln(c, w, b):
    bsz, d = c.shape
    n = w.shape[1]
    tn = 1024
    assert bsz <= LANES
    ct = jnp.zeros((d, LANES), F32).at[:, :bsz].set(c.T)
    out = pl.pallas_call(
        _adaln_kernel,
        grid=(n // tn,),
        in_specs=[pl.BlockSpec((d, LANES), lambda j: (0, 0)),
                  pl.BlockSpec((d, tn), lambda j: (0, j)),
                  pl.BlockSpec((1, tn), lambda j: (0, j))],
        out_specs=pl.BlockSpec((bsz, 1, tn), lambda j: (0, 0, j)),
        out_shape=jax.ShapeDtypeStruct((bsz, 1, n), F32),
        compiler_params=pltpu.CompilerParams(dimension_semantics=("parallel",),
                                             vmem_limit_bytes=VMEM_LIMIT),
        name="adaln",
    )(ct, w, b.reshape(1, n))
    return out.reshape(bsz, n)


def _inproj_kernel(x_ref, g_ref, sc_ref, sh_ref, w_ref, cos_ref, sin_ref,
                   o_ref, kvc_ref, h_ref, *, tn):
    j = pl.program_id(1)
    tiles_per_sec = SEC_BLOCKS * LANES // tn

    @pl.when(j == 0)
    def _():
        h = _rms_norm(x_ref[...], g_ref[...]) * (1.0 + sc_ref[...]) + sh_ref[...]
        h_ref[...] = h.astype(BF16)

    acc = jnp.dot(h_ref[...], w_ref[...], preferred_element_type=F32)

    def rope(a, scale):
        cos = cos_ref[...]
        sin = sin_ref[...]
        outs = []
        for c in range(tn // LANES):
            ch = a[:, c * LANES:(c + 1) * LANES]
            r = ch * cos + pltpu.roll(ch, HEAD_DIM // 2, 1) * sin
            if scale is not None:
                r = r * scale
            outs.append(r.astype(BF16))
        return jnp.concatenate(outs, axis=1)

    @pl.when(j < tiles_per_sec)
    def _():
        o_ref[...] = rope(acc, HEAD_DIM ** -0.5)

    @pl.when(jnp.logical_and(j >= tiles_per_sec, j < 2 * tiles_per_sec))
    def _():
        o_ref[...] = rope(acc, None)

    @pl.when(j >= 2 * tiles_per_sec)
    def _():
        o_ref[...] = acc.astype(BF16)

    jk, ok = divmod(KC0 * LANES, tn)
    jv, ov = divmod(VC0 * LANES, tn)

    @pl.when(j == jk)
    def _():
        r = rope(acc, None)
        for g in range(NSA_KV_GROUPS):
            kvc_ref[g] = r[:, ok + g * LANES: ok + (g + 1) * LANES]

    @pl.when(j == jv)
    def _():
        for g in range(NSA_KV_GROUPS):
            kvc_ref[NSA_KV_GROUPS + g] = acc[:, ov + g * LANES: ov + (g + 1) * LANES].astype(BF16)


def _inproj(x2, g, scale, shift, w_perm, cos2, sin2, seq):
    m, d = x2.shape
    ncols = w_perm.shape[1]
    tm = min(1024, seq)
    tn = 1024
    tiles_per_b = seq // tm
    return pl.pallas_call(
        functools.partial(_inproj_kernel, tn=tn),
        grid=(m // tm, ncols // tn),
        in_specs=[pl.BlockSpec((tm, d), lambda i, j: (i, 0)),
                  pl.BlockSpec((1, d), lambda i, j: (0, 0)),
                  pl.BlockSpec((None, 1, d), lambda i, j: (i // tiles_per_b, 0, 0)),
                  pl.BlockSpec((None, 1, d), lambda i, j: (i // tiles_per_b, 0, 0)),
                  pl.BlockSpec((d, tn), lambda i, j: (0, j)),
                  pl.BlockSpec((tm, LANES), lambda i, j: (i % tiles_per_b, 0)),
                  pl.BlockSpec((tm, LANES), lambda i, j: (i % tiles_per_b, 0))],
        out_specs=[pl.BlockSpec((tm, tn), lambda i, j: (i, j)),
                   pl.BlockSpec((2 * NSA_KV_GROUPS, tm, LANES), lambda i, j: (0, i, 0))],
        out_shape=[jax.ShapeDtypeStruct((m, ncols), BF16),
                   jax.ShapeDtypeStruct((2 * NSA_KV_GROUPS, m, LANES), BF16)],
        scratch_shapes=[pltpu.VMEM((tm, d), BF16)],
        compiler_params=pltpu.CompilerParams(dimension_semantics=("parallel", "arbitrary"),
                                             vmem_limit_bytes=VMEM_LIMIT),
        name="inproj",
    )(x2, g, scale, shift, w_perm, cos2, sin2)


def _compress_kernel(x_ref, pos_ref, w1_ref, w2_ref, o_ref):
    half = CMP_STRIDE * HEAD_DIM
    x = x_ref[...].astype(F32)
    pos = pos_ref[...]
    w1 = w1_ref[...].astype(BF16)
    a = jnp.dot((x + pos[:, :half]).astype(BF16), w1[:half], preferred_element_type=F32)
    b = jnp.dot((x + pos[:, half:]).astype(BF16), w1[half:], preferred_element_type=F32)
    n_rows = x.shape[0]
    h1 = a + pltpu.roll(b, n_rows - 1, 0)
    h = h1 * (1.0 / (1.0 + jnp.exp(-h1)))
    o_ref[...] = jnp.dot(h.astype(BF16), w2_ref[...].astype(BF16),
                         preferred_element_type=F32).astype(BF16)


def _compress(kvc4, pos2, w1s, w2s):
    ns, bsz, n_rows, width = kvc4.shape
    return pl.pallas_call(
        _compress_kernel,
        grid=(ns, bsz),
        in_specs=[pl.BlockSpec((None, None, n_rows, width), lambda s, b: (s, b, 0, 0)),
                  pl.BlockSpec((None, 1, 2 * width), lambda s, b: (s // NSA_KV_GROUPS, 0, 0)),
                  pl.BlockSpec((None, 2 * width, HEAD_DIM), lambda s, b: (s // NSA_KV_GROUPS, 0, 0)),
                  pl.BlockSpec((None, HEAD_DIM, HEAD_DIM), lambda s, b: (s // NSA_KV_GROUPS, 0, 0))],
        out_specs=pl.BlockSpec((None, None, n_rows, HEAD_DIM), lambda s, b: (s, b, 0, 0)),
        out_shape=jax.ShapeDtypeStruct((ns, bsz, n_rows, HEAD_DIM), BF16),
        compiler_params=pltpu.CompilerParams(dimension_semantics=("parallel", "parallel"),
                                             vmem_limit_bytes=VMEM_LIMIT),
        name="compress",
    )(kvc4, pos2, w1s, w2s)


def _topk_rows(score, k):
    idx = lax.broadcasted_iota(I32, score.shape, 0).astype(F32)
    work = score
    sel = jnp.zeros(score.shape, F32)
    for _ in range(k):
        mx = jnp.max(work, axis=0, keepdims=True)
        first = jnp.min(jnp.where(work == mx, idx, float(score.shape[0])), axis=0, keepdims=True)
        hit = idx == first
        sel = jnp.where(hit, 1.0, sel)
        work = jnp.where(hit, -jnp.inf, work)
    return sel


def _bias_from_keep(keep_t):
    n, q = keep_t.shape
    padded = jnp.concatenate([keep_t, jnp.zeros((LANES - n, q), F32)], axis=0) if n < LANES else keep_t
    return jnp.where(padded.T > 0.5, 0.0, NEG_BIG).astype(BF16)


def _with_ones(v):
    return jnp.concatenate([v, jnp.ones(v.shape, v.dtype)], axis=1)


def _flash_step(qa, ka, va, m, acc, mask):
    s = _nt_dot(qa, ka)
    if mask is not None:
        s = jnp.where(mask, s, NEG_BIG)
    m_new = jnp.maximum(m, jnp.max(s, axis=1, keepdims=True))
    alpha = jnp.exp(m - m_new)
    p = jnp.exp(s - m_new).astype(BF16)
    acc = alpha * acc + jnp.dot(p, va, preferred_element_type=F32)
    return m_new, acc


def _flash_out(acc):
    return acc[:, :HEAD_DIM] * (1.0 / acc[:, HEAD_DIM:])


def _nsa_kernel(q_ref, kc_ref, vc_ref, ks_ref, vs_ref, kw_ref, vw_ref, gl_ref,
                selmap_ref, et_ref, o_ref, *, tq, seq):
    qi = pl.program_id(2)
    t0 = qi * tq
    rows = NSA_HPG * tq
    n_cmp_pad = kc_ref.shape[0]
    n_sel = seq // SEL_LEN

    q = q_ref[...]
    qs = jnp.concatenate([q[:, h * LANES:(h + 1) * LANES] for h in range(NSA_HPG)], axis=0)

    s = _nt_dot(qs, kc_ref[...])
    tt = t0 + (lax.broadcasted_iota(I32, (rows, n_cmp_pad), 0) & (tq - 1))
    cend = lax.broadcasted_iota(I32, (rows, n_cmp_pad), 1) * CMP_STRIDE + (CMP_LEN - 1)
    s = jnp.where(cend <= tt, s, -jnp.inf)
    mx = jnp.max(s, axis=1, keepdims=True)
    mx = jnp.where(mx == -jnp.inf, 0.0, mx)
    e = jnp.exp(s - mx)
    den = jnp.maximum(jnp.sum(e, axis=1, keepdims=True), 1e-30)
    p_c = e * (1.0 / den)
    o_c = jnp.dot(p_c.astype(BF16), vc_ref[...], preferred_element_type=F32)

    p_sum = p_c[0:tq]
    for h in range(1, NSA_HPG):
        p_sum = p_sum + p_c[h * tq:(h + 1) * tq]
    hi, lo = _split_bf16(p_sum)
    smap_t = selmap_ref[...]
    imp_t = _nt_dot(smap_t, hi) + _nt_dot(smap_t, lo)
    blk = lax.broadcasted_iota(I32, (n_sel, tq), 0)
    cb = (t0 + lax.broadcasted_iota(I32, (n_sel, tq), 1)) // SEL_LEN
    causal = blk <= cb
    forced = (blk == 0) | (blk == cb) | (blk == cb - 1)
    n_forced = 3
    free = jnp.where(causal, jnp.where(forced, 0.0, 1.0), 0.0)
    picked = _topk_rows(jnp.where(free > 0.5, imp_t, -jnp.inf), max(min(SEL_TOPK, n_sel) - n_forced, 0))
    keep_t = jnp.where(forced, 1.0, picked * free)
    selb = _bias_from_keep(keep_t)
    qa = jnp.concatenate([qs, jnp.concatenate([selb] * NSA_HPG, axis=0)], axis=1)

    def kv_chunk(j):
        start = pl.multiple_of(j * tq, tq)
        ka = jnp.concatenate([ks_ref[pl.ds(start, tq), :], et_ref[pl.ds(start, tq), :]], axis=1)
        return ka, _with_ones(vs_ref[pl.ds(start, tq), :])

    qrow = lax.broadcasted_iota(I32, (rows, tq), 0) & (tq - 1)
    kcol = lax.broadcasted_iota(I32, (rows, tq), 1)
    m0 = jnp.full((rows, 1), NEG_BIG, F32)
    a0 = jnp.zeros((rows, 2 * HEAD_DIM), F32)
    ka, va = kv_chunk(qi)
    carry = _flash_step(qa, ka, va, m0, a0, kcol <= qrow)

    def body(j, c):
        ka_j, va_j = kv_chunk(j)
        return _flash_step(qa, ka_j, va_j, *c, None)

    _, a_s = lax.fori_loop(0, qi, body, carry)
    o_s = _flash_out(a_s)

    n_back = WINDOW // tq
    scores, values = [], []
    for back in range(n_back + 1):
        start = pl.multiple_of(jnp.maximum(qi - back, 0) * tq, tq)
        s_b = _nt_dot(qs, kw_ref[pl.ds(start, tq), :])
        if back == 0:
            s_b = jnp.where(kcol <= qrow, s_b, NEG_BIG)
        else:
            if back == n_back:
                s_b = jnp.where(kcol > qrow, s_b, NEG_BIG)
            s_b = s_b + jnp.where(qi >= back, 0.0, NEG_BIG)
        scores.append(s_b)
        values.append(_with_ones(vw_ref[pl.ds(start, tq), :]))
    mw = scores[0].max(axis=1, keepdims=True)
    for s_b in scores[1:]:
        mw = jnp.maximum(mw, s_b.max(axis=1, keepdims=True))
    a_w = None
    for s_b, v_b in zip(scores, values):
        t = jnp.dot(jnp.exp(s_b - mw).astype(BF16), v_b, preferred_element_type=F32)
        a_w = t if a_w is None else a_w + t
    o_w = _flash_out(a_w)

    gl = gl_ref[...].astype(F32)
    sig = 1.0 / (1.0 + jnp.exp(-gl))
    for h in range(NSA_HPG):
        r = slice(h * tq, (h + 1) * tq)
        o = (sig[:, 3 * h:3 * h + 1] * o_c[r] + sig[:, 3 * h + 1:3 * h + 2] * o_s[r]
             + sig[:, 3 * h + 2:3 * h + 3] * o_w[r])
        o_ref[:, h * LANES:(h + 1) * LANES] = o.astype(BF16)


def _nsa(qkv, kcv, selmap, et_sel, bsz, seq):
    m = qkv.shape[0]
    tq = NSA_TQ
    assert seq % tq == 0 and WINDOW % tq == 0 and tq % SEL_LEN == 0
    nq = seq // tq
    n_cmp_pad = kcv.shape[2]
    gw = NSA_HPG * LANES
    col = lambda c0: (lambda b, g, i: (b, c0 + g))
    full = lambda c0: pl.BlockSpec((seq, LANES), col(c0))
    return pl.pallas_call(
        functools.partial(_nsa_kernel, tq=tq, seq=seq),
        grid=(bsz, NSA_KV_GROUPS, nq),
        in_specs=[pl.BlockSpec((tq, gw), lambda b, g, i: (b * nq + i, g)),
                  pl.BlockSpec((None, None, n_cmp_pad, LANES), lambda b, g, i: (g, b, 0, 0)),
                  pl.BlockSpec((None, None, n_cmp_pad, LANES),
                               lambda b, g, i: (NSA_KV_GROUPS + g, b, 0, 0)),
                  full(KS0), full(VS0), full(KW0), full(VW0),
                  pl.BlockSpec((tq, LANES), lambda b, g, i: (b * nq + i, GT0 + g)),
                  pl.BlockSpec(selmap.shape, lambda b, g, i: (0, 0)),
                  pl.BlockSpec(et_sel.shape, lambda b, g, i: (0, 0))],
        out_specs=pl.BlockSpec((tq, gw), lambda b, g, i: (b * nq + i, g)),
        out_shape=jax.ShapeDtypeStruct((m, NSA_HEADS * HEAD_DIM), BF16),
        compiler_params=pltpu.CompilerParams(
            dimension_semantics=("parallel", "parallel", "arbitrary"),
            vmem_limit_bytes=VMEM_LIMIT),
        name="nsa_attn",
    )(qkv, kcv, kcv, qkv, qkv, qkv, qkv, qkv, selmap, et_sel)


def _moba_kernel(q_ref, k_ref, v_ref, avg_ref, et_ref, o_ref, kmh_ref, kml_ref, *, seq, hb):
    qi = pl.program_id(2)
    blk_len = MOBA_BLOCK
    nb = seq // blk_len
    nb_rows = avg_ref.shape[0]
    head = lambda h: slice(h * LANES, (h + 1) * LANES)

    @pl.when(qi == 0)
    def _():
        for h in range(hb):
            km = jnp.dot(avg_ref[...], k_ref[:, head(h)], preferred_element_type=F32)
            kmh_ref[h], kml_ref[h] = _split_bf16(km)

    blk = lax.broadcasted_iota(I32, (nb_rows, blk_len), 0)
    past = jnp.where(blk < qi, 1.0, 0.0)
    own = jnp.where(blk == qi, 1.0, 0.0)
    qas = []
    for h in range(hb):
        q = q_ref[:, head(h)]
        gate_t = _nt_dot(kmh_ref[h], q) + _nt_dot(kml_ref[h], q)
        picked = _topk_rows(jnp.where(past > 0.5, gate_t, -jnp.inf), min(MOBA_TOPK, nb - 1))
        qas.append(jnp.concatenate([q, _bias_from_keep(picked * past + own)], axis=1))

    def kv_chunk(j, h):
        rows = pl.ds(pl.multiple_of(j * blk_len, blk_len), blk_len)
        ka = jnp.concatenate([k_ref[rows, head(h)], et_ref[rows, :]], axis=1)
        return ka, _with_ones(v_ref[rows, head(h)])

    row = lax.broadcasted_iota(I32, (blk_len, blk_len), 0)
    colk = lax.broadcasted_iota(I32, (blk_len, blk_len), 1)
    m0 = jnp.full((blk_len, 1), NEG_BIG, F32)
    a0 = jnp.zeros((blk_len, 2 * HEAD_DIM), F32)
    carry = []
    for h in range(hb):
        carry += _flash_step(qas[h], *kv_chunk(qi, h), m0, a0, colk <= row)

    def body(j, c):
        new = []
        for h in range(hb):
            new += _flash_step(qas[h], *kv_chunk(j, h), c[2 * h], c[2 * h + 1], None)
        return tuple(new)

    final = lax.fori_loop(0, qi, body, tuple(carry))
    for h in range(hb):
        o_ref[:, head(h)] = _flash_out(final[2 * h + 1]).astype(BF16)


def _moba(qkv, avg, et_blk, bsz, seq):
    m = qkv.shape[0]
    nq = seq // MOBA_BLOCK
    hb = MOBA_HEADS_PER_STEP
    assert MOBA_HEADS % hb == 0 and QM0 % hb == 0 and KM0 % hb == 0 and VM0 % hb == 0
    hw = hb * LANES
    return pl.pallas_call(
        functools.partial(_moba_kernel, seq=seq, hb=hb),
        grid=(bsz, MOBA_HEADS // hb, nq),
        in_specs=[pl.BlockSpec((MOBA_BLOCK, hw), lambda b, h, i: (b * nq + i, QM0 // hb + h)),
                  pl.BlockSpec((seq, hw), lambda b, h, i: (b, KM0 // hb + h)),
                  pl.BlockSpec((seq, hw), lambda b, h, i: (b, VM0 // hb + h)),
                  pl.BlockSpec(avg.shape, lambda b, h, i: (0, 0)),
                  pl.BlockSpec(et_blk.shape, lambda b, h, i: (0, 0))],
        out_specs=pl.BlockSpec((MOBA_BLOCK, hw), lambda b, h, i: (b * nq + i, h)),
        out_shape=jax.ShapeDtypeStruct((m, MOBA_HEADS * HEAD_DIM), BF16),
        scratch_shapes=[pltpu.VMEM((hb, avg.shape[0], HEAD_DIM), BF16),
                        pltpu.VMEM((hb, avg.shape[0], HEAD_DIM), BF16)],
        compiler_params=pltpu.CompilerParams(
            dimension_semantics=("parallel", "parallel", "arbitrary"),
            vmem_limit_bytes=VMEM_LIMIT),
        name="moba_attn",
    )(qkv, qkv, qkv, avg, et_blk)


def _outproj_kernel(on_ref, om_ref, w_ref, x_ref, g_ref, gate_ref, o_ref):
    kn = on_ref.shape[1]
    acc = (jnp.dot(on_ref[...], w_ref[:kn, :], preferred_element_type=F32)
           + jnp.dot(om_ref[...], w_ref[kn:, :], preferred_element_type=F32))
    o_ref[...] = x_ref[...] + gate_ref[...] * _rms_norm(acc, g_ref[...])


def _outproj(o_n, o_m, w_o, x2, g_post, gate, seq):
    m, d = x2.shape
    tm = min(512, seq)
    tiles_per_b = seq // tm
    return pl.pallas_call(
        _outproj_kernel,
        grid=(m // tm,),
        in_specs=[pl.BlockSpec((tm, o_n.shape[1]), lambda i: (i, 0)),
                  pl.BlockSpec((tm, o_m.shape[1]), lambda i: (i, 0)),
                  pl.BlockSpec(w_o.shape, lambda i: (0, 0)),
                  pl.BlockSpec((tm, d), lambda i: (i, 0)),
                  pl.BlockSpec((1, d), lambda i: (0, 0)),
                  pl.BlockSpec((None, 1, d), lambda i: (i // tiles_per_b, 0, 0))],
        out_specs=pl.BlockSpec((tm, d), lambda i: (i, 0)),
        out_shape=jax.ShapeDtypeStruct((m, d), F32),
        compiler_params=pltpu.CompilerParams(dimension_semantics=("parallel",),
                                             vmem_limit_bytes=VMEM_LIMIT),
        name="outproj",
    )(o_n, o_m, w_o, x2, g_post, gate)


def _ffn_kernel(x_ref, gpre_ref, sc_ref, sh_ref, wu_ref, wd_ref, gpost_ref, gate_ref,
                o_ref, h_ref):
    k = pl.program_id(1)

    @pl.when(k == 0)
    def _():
        h = _rms_norm(x_ref[...], gpre_ref[...]) * (1.0 + sc_ref[...]) + sh_ref[...]
        h_ref[...] = h.astype(BF16)
        o_ref[...] = jnp.zeros_like(o_ref)

    u = jnp.dot(h_ref[...], wu_ref[...], preferred_element_type=F32)
    u = jnp.square(jnp.maximum(u, 0.0)).astype(BF16)
    o_ref[...] += jnp.dot(u, wd_ref[...], preferred_element_type=F32)

    @pl.when(k == pl.num_programs(1) - 1)
    def _():
        o_ref[...] = x_ref[...] + gate_ref[...] * _rms_norm(o_ref[...], gpost_ref[...])


def _ffn(x2, g_pre, scale, shift, w_up, w_down, g_post, gate, seq):
    m, d = x2.shape
    dff = w_up.shape[1]
    tm = min(1024, seq)
    ck = 512
    tiles_per_b = seq // tm
    vec = pl.BlockSpec((1, d), lambda i, k: (0, 0))
    per_b = pl.BlockSpec((None, 1, d), lambda i, k: (i // tiles_per_b, 0, 0))
    return pl.pallas_call(
        _ffn_kernel,
        grid=(m // tm, dff // ck),
        in_specs=[pl.BlockSpec((tm, d), lambda i, k: (i, 0)), vec, per_b, per_b,
                  pl.BlockSpec((d, ck), lambda i, k: (0, k)),
                  pl.BlockSpec((ck, d), lambda i, k: (k, 0)),
                  vec, per_b],
        out_specs=pl.BlockSpec((tm, d), lambda i, k: (i, 0)),
        out_shape=jax.ShapeDtypeStruct((m, d), F32),
        scratch_shapes=[pltpu.VMEM((tm, d), BF16)],
        compiler_params=pltpu.CompilerParams(dimension_semantics=("parallel", "arbitrary"),
                                             vmem_limit_bytes=VMEM_LIMIT),
        name="ffn",
    )(x2, g_pre, scale, shift, w_up, w_down, g_post, gate)


def _permute_w_in(w_in):
    d = w_in.shape[0]
    widths = [NSA_HEADS * HEAD_DIM] + [NSA_KV_GROUPS * HEAD_DIM] * 6 + \
             [3 * NSA_HEADS, MOBA_HEADS * HEAD_DIM, MOBA_HEADS * HEAD_DIM, MOBA_HEADS * HEAD_DIM]
    offs = np.cumsum([0] + widths)
    q_n, k_c, v_c, k_s, v_s, k_w, v_w, g_n, q_m, k_m, v_m = [
        w_in[:, offs[i]:offs[i + 1]] for i in range(11)]
    pad = lambda n: jnp.zeros((d, n), w_in.dtype)
    gates = []
    per_group = 3 * NSA_HPG
    for g in range(NSA_KV_GROUPS):
        gates += [g_n[:, g * per_group:(g + 1) * per_group], pad(LANES - per_group)]
    cols = [q_n, q_m, k_m, k_c, k_s, k_w, pad(2 * LANES), v_m, v_c, v_s, v_w] + gates
    return jnp.concatenate(cols, axis=1).astype(BF16)


def _constants(seq):
    n_cmp_pad = seq // CMP_STRIDE
    n_cmp = (seq - CMP_LEN) // CMP_STRIDE + 1
    n_sel = seq // SEL_LEN
    cs = np.arange(n_cmp) * CMP_STRIDE
    ss = np.arange(n_sel) * SEL_LEN
    overlap = np.clip(np.minimum(cs[:, None] + CMP_LEN, ss[None, :] + SEL_LEN)
                      - np.maximum(cs[:, None], ss[None, :]), 0, None)
    selmap_t = np.zeros((n_sel, n_cmp_pad), np.float32)
    selmap_t[:, :n_cmp] = (overlap / CMP_LEN).T
    keys = np.arange(seq)
    et_sel = (keys[:, None] // SEL_LEN == np.arange(LANES)[None, :]).astype(np.float32)
    et_blk = (keys[:, None] // MOBA_BLOCK == np.arange(LANES)[None, :]).astype(np.float32)
    nb_rows = -(-(seq // MOBA_BLOCK) // BF16_SUBLANES) * BF16_SUBLANES
    avg = et_blk.T[:nb_rows] / MOBA_BLOCK
    to = lambda a: jnp.asarray(a, BF16)
    return to(selmap_t), to(et_sel), to(et_blk), to(avg)


def _rope_tables(seq):
    pos = jnp.arange(seq, dtype=F32)
    inv = ROPE_THETA ** (-jnp.arange(0, HEAD_DIM, 2, dtype=F32) / HEAD_DIM)
    ang = pos[:, None] * inv[None, :]
    cos, sin = jnp.cos(ang), jnp.sin(ang)
    return jnp.concatenate([cos, cos], axis=1), jnp.concatenate([-sin, sin], axis=1)


def kernel(x, c, w_ada, b_ada, pre_norm_mix, post_norm_mix, w_in, cmp_k_pos, cmp_k_w1, cmp_k_w2,
           cmp_v_pos, cmp_v_w1, cmp_v_w2, w_o, pre_norm_ffn, post_norm_ffn, w_up, w_down):
    bsz, seq, d = x.shape
    depth = w_ada.shape[0]
    cos2, sin2 = _rope_tables(seq)
    selmap, et_sel, et_blk, avg = _constants(seq)
    x2 = x.reshape(bsz * seq, d)
    for l in range(depth):
        mod = _adaln(c, w_ada[l], b_ada[l])
        shift_m, scale_m, gate_m, shift_f, scale_f, gate_f = [
            a.reshape(bsz, 1, d) for a in jnp.split(mod, 6, axis=-1)]
        row = lambda a: a.reshape(1, d)

        qkv, kvc = _inproj(x2, row(pre_norm_mix[l]), scale_m, shift_m, _permute_w_in(w_in[l]),
                           cos2, sin2, seq)
        kvc4 = kvc.reshape(2 * NSA_KV_GROUPS, bsz, seq // CMP_STRIDE, CMP_STRIDE * HEAD_DIM)
        pos2 = jnp.stack([cmp_k_pos[l].reshape(1, -1), cmp_v_pos[l].reshape(1, -1)])
        kcv = _compress(kvc4, pos2, jnp.stack([cmp_k_w1[l], cmp_v_w1[l]]),
                        jnp.stack([cmp_k_w2[l], cmp_v_w2[l]]))
        o_n = _nsa(qkv, kcv, selmap, et_sel, bsz, seq)
        o_m = _moba(qkv, avg, et_blk, bsz, seq)
        x2 = _outproj(o_n, o_m, w_o[l].astype(BF16), x2, row(post_norm_mix[l]), gate_m, seq)
        x2 = _ffn(x2, row(pre_norm_ffn[l]), scale_f, shift_f, w_up[l].astype(BF16),
                  w_down[l].astype(BF16), row(post_norm_ffn[l]), gate_f, seq)
    return x2.reshape(bsz, seq, d)
```

```python
import functools

import numpy as np
import jax
import jax.numpy as jnp
from jax import lax
from jax.experimental import pallas as pl
from jax.experimental.pallas import tpu as pltpu

F32 = jnp.float32
BF16 = jnp.bfloat16
I32 = jnp.int32

HEAD_DIM = 128
NSA_HEADS = 8
NSA_KV_GROUPS = 2
NSA_HPG = NSA_HEADS // NSA_KV_GROUPS
MOBA_HEADS = 8
CMP_LEN = 32
CMP_STRIDE = 16
SEL_LEN = 64
SEL_TOPK = 16
WINDOW = 512
MOBA_BLOCK = 256
MOBA_TOPK = 3
ROPE_THETA = 10000.0
EPS = 1e-6

LANES = 128
BF16_SUBLANES = 16
NSA_TQ = 256
MOBA_HEADS_PER_STEP = 8
NEG_BIG = -1e30
VMEM_LIMIT = 56 * 1024 * 1024

QN0 = 0
QM0 = 8
KM0 = 16
KC0 = 24
KS0 = 26
KW0 = 28
VM0 = 32
VC0 = 40
VS0 = 42
VW0 = 44
GT0 = 46
N_BLOCKS = 48
SEC_BLOCKS = 16


def _nt_dot(a, b):
    return lax.dot_general(a, b, (((1,), (1,)), ((), ())), preferred_element_type=F32)


def _rms_norm(x, g):
    ms = jnp.mean(x * x, axis=-1, keepdims=True)
    return x * lax.rsqrt(ms + EPS) * g


def _split_bf16(a):
    hi = a.astype(BF16)
    lo = (a - hi.astype(F32)).astype(BF16)
    return hi, lo


def _adaln_kernel(ct_ref, w_ref, b_ref, o_ref):
    ct = ct_ref[...]
    a = ct * (1.0 / (1.0 + jnp.exp(-ct)))
    w = w_ref[...]
    for b in range(o_ref.shape[0]):
        o_ref[b] = jnp.sum(w * a[:, b:b + 1], axis=0, keepdims=True) + b_ref[...]


def _adaln(c, w, b):
    bsz, d = c.shape
    n = w.shape[1]
    tn = 1024
    assert bsz <= LANES
    ct = jnp.zeros((d, LANES), F32).at[:, :bsz].set(c.T)
    out = pl.pallas_call(
        _adaln_kernel,
        grid=(n // tn,),
        in_specs=[pl.BlockSpec((d, LANES), lambda j: (0, 0)),
                  pl.BlockSpec((d, tn), lambda j: (0, j)),
                  pl.BlockSpec((1, tn), lambda j: (0, j))],
        out_specs=pl.BlockSpec((bsz, 1, tn), lambda j: (0, 0, j)),
        out_shape=jax.ShapeDtypeStruct((bsz, 1, n), F32),
        compiler_params=pltpu.CompilerParams(dimension_semantics=("parallel",),
                                             vmem_limit_bytes=VMEM_LIMIT),
        name="adaln",
    )(ct, w, b.reshape(1, n))
    return out.reshape(bsz, n)


def _inproj_kernel(x_ref, g_ref, sc_ref, sh_ref, w_ref, cos_ref, sin_ref,
                   o_ref, kvc_ref, h_ref, *, tn):
    j = pl.program_id(1)
    tiles_per_sec = SEC_BLOCKS * LANES // tn

    @pl.when(j == 0)
    def _():
        h = _rms_norm(x_ref[...], g_ref[...]) * (1.0 + sc_ref[...]) + sh_ref[...]
        h_ref[...] = h.astype(BF16)

    acc = jnp.dot(h_ref[...], w_ref[...], preferred_element_type=F32)

    def rope(a, scale):
        cos = cos_ref[...]
        sin = sin_ref[...]
        outs = []
        for c in range(tn // LANES):
            ch = a[:, c * LANES:(c + 1) * LANES]
            r = ch * cos + pltpu.roll(ch, HEAD_DIM // 2, 1) * sin
            if scale is not None:
                r = r * scale
            outs.append(r.astype(BF16))
        return jnp.concatenate(outs, axis=1)

    @pl.when(j < tiles_per_sec)
    def _():
        o_ref[...] = rope(acc, HEAD_DIM ** -0.5)

    @pl.when(jnp.logical_and(j >= tiles_per_sec, j < 2 * tiles_per_sec))
    def _():
        o_ref[...] = rope(acc, None)

    @pl.when(j >= 2 * tiles_per_sec)
    def _():
        o_ref[...] = acc.astype(BF16)

    jk, ok = divmod(KC0 * LANES, tn)
    jv, ov = divmod(VC0 * LANES, tn)

    @pl.when(j == jk)
    def _():
        r = rope(acc, None)
        for g in range(NSA_KV_GROUPS):
            kvc_ref[g] = r[:, ok + g * LANES: ok + (g + 1) * LANES]

    @pl.when(j == jv)
    def _():
        for g in range(NSA_KV_GROUPS):
            kvc_ref[NSA_KV_GROUPS + g] = acc[:, ov + g * LANES: ov + (g + 1) * LANES].astype(BF16)


def _inproj(x2, g, scale, shift, w_perm, cos2, sin2, seq):
    m, d = x2.shape
    ncols = w_perm.shape[1]
    tm = min(1024, seq)
    tn = 1024
    tiles_per_b = seq // tm
    return pl.pallas_call(
        functools.partial(_inproj_kernel, tn=tn),
        grid=(m // tm, ncols // tn),
        in_specs=[pl.BlockSpec((tm, d), lambda i, j: (i, 0)),
                  pl.BlockSpec((1, d), lambda i, j: (0, 0)),
                  pl.BlockSpec((None, 1, d), lambda i, j: (i // tiles_per_b, 0, 0)),
                  pl.BlockSpec((None, 1, d), lambda i, j: (i // tiles_per_b, 0, 0)),
                  pl.BlockSpec((d, tn), lambda i, j: (0, j)),
                  pl.BlockSpec((tm, LANES), lambda i, j: (i % tiles_per_b, 0)),
                  pl.BlockSpec((tm, LANES), lambda i, j: (i % tiles_per_b, 0))],
        out_specs=[pl.BlockSpec((tm, tn), lambda i, j: (i, j)),
                   pl.BlockSpec((2 * NSA_KV_GROUPS, tm, LANES), lambda i, j: (0, i, 0))],
        out_shape=[jax.ShapeDtypeStruct((m, ncols), BF16),
                   jax.ShapeDtypeStruct((2 * NSA_KV_GROUPS, m, LANES), BF16)],
        scratch_shapes=[pltpu.VMEM((tm, d), BF16)],
        compiler_params=pltpu.CompilerParams(dimension_semantics=("parallel", "arbitrary"),
                                             vmem_limit_bytes=VMEM_LIMIT),
        name="inproj",
    )(x2, g, scale, shift, w_perm, cos2, sin2)


def _compress_kernel(x_ref, pos_ref, w1_ref, w2_ref, o_ref):
    half = CMP_STRIDE * HEAD_DIM
    x = x_ref[...].astype(F32)
    pos = pos_ref[...]
    w1 = w1_ref[...].astype(BF16)
    a = jnp.dot((x + pos[:, :half]).astype(BF16), w1[:half], preferred_element_type=F32)
    b = jnp.dot((x + pos[:, half:]).astype(BF16), w1[half:], preferred_element_type=F32)
    n_rows = x.shape[0]
    h1 = a + pltpu.roll(b, n_rows - 1, 0)
    h = h1 * (1.0 / (1.0 + jnp.exp(-h1)))
    o_ref[...] = jnp.dot(h.astype(BF16), w2_ref[...].astype(BF16),
                         preferred_element_type=F32).astype(BF16)


def _compress(kvc4, pos2, w1s, w2s):
    ns, bsz, n_rows, width = kvc4.shape
    return pl.pallas_call(
        _compress_kernel,
        grid=(ns, bsz),
        in_specs=[pl.BlockSpec((None, None, n_rows, width), lambda s, b: (s, b, 0, 0)),
                  pl.BlockSpec((None, 1, 2 * width), lambda s, b: (s // NSA_KV_GROUPS, 0, 0)),
                  pl.BlockSpec((None, 2 * width, HEAD_DIM), lambda s, b: (s // NSA_KV_GROUPS, 0, 0)),
                  pl.BlockSpec((None, HEAD_DIM, HEAD_DIM), lambda s, b: (s // NSA_KV_GROUPS, 0, 0))],
        out_specs=pl.BlockSpec((None, None, n_rows, HEAD_DIM), lambda s, b: (s, b, 0, 0)),
        out_shape=jax.ShapeDtypeStruct((ns, bsz, n_rows, HEAD_DIM), BF16),
        compiler_params=pltpu.CompilerParams(dimension_semantics=("parallel", "parallel"),
                                             vmem_limit_bytes=VMEM_LIMIT),
        name="compress",
    )(kvc4, pos2, w1s, w2s)


def _topk_rows(score, k):
    idx = lax.broadcasted_iota(I32, score.shape, 0).astype(F32)
    work = score
    sel = jnp.zeros(score.shape, F32)
    for _ in range(k):
        mx = jnp.max(work, axis=0, keepdims=True)
        first = jnp.min(jnp.where(work == mx, idx, float(score.shape[0])), axis=0, keepdims=True)
        hit = idx == first
        sel = jnp.where(hit, 1.0, sel)
        work = jnp.where(hit, -jnp.inf, work)
    return sel


def _bias_from_keep(keep_t):
    n, q = keep_t.shape
    padded = jnp.concatenate([keep_t, jnp.zeros((LANES - n, q), F32)], axis=0) if n < LANES else keep_t
    return jnp.where(padded.T > 0.5, 0.0, NEG_BIG).astype(BF16)


def _with_ones(v):
    return jnp.concatenate([v, jnp.ones(v.shape, v.dtype)], axis=1)


def _flash_step(qa, ka, va, m, acc, mask, penalty=None):
    s = _nt_dot(qa, ka)
    if mask is not None:
        s = jnp.where(mask, s, NEG_BIG)
    if penalty is not None:
        s = s + penalty
    m_new = jnp.maximum(m, jnp.max(s, axis=1, keepdims=True))
    alpha = jnp.exp(m - m_new)
    p = jnp.exp(s - m_new).astype(BF16)
    acc = alpha * acc + jnp.dot(p, va, preferred_element_type=F32)
    return m_new, acc


def _flash_out(acc):
    return acc[:, :HEAD_DIM] * (1.0 / acc[:, HEAD_DIM:])


def _nsa_kernel(q_ref, kc_ref, vc_ref, ks_ref, vs_ref, kw_ref, vw_ref, gl_ref,
                selmap_ref, et_ref, o_ref, *, tq, seq):
    qi = pl.program_id(2)
    t0 = qi * tq
    rows = NSA_HPG * tq
    n_cmp_pad = kc_ref.shape[0]
    n_sel = seq // SEL_LEN

    q = q_ref[...]
    qs = jnp.concatenate([q[:, h * LANES:(h + 1) * LANES] for h in range(NSA_HPG)], axis=0)

    s = _nt_dot(qs, kc_ref[...])
    tt = t0 + (lax.broadcasted_iota(I32, (rows, n_cmp_pad), 0) & (tq - 1))
    cend = lax.broadcasted_iota(I32, (rows, n_cmp_pad), 1) * CMP_STRIDE + (CMP_LEN - 1)
    s = jnp.where(cend <= tt, s, -jnp.inf)
    mx = jnp.max(s, axis=1, keepdims=True)
    mx = jnp.where(mx == -jnp.inf, 0.0, mx)
    e = jnp.exp(s - mx)
    den = jnp.maximum(jnp.sum(e, axis=1, keepdims=True), 1e-30)
    p_c = e * (1.0 / den)
    o_c = jnp.dot(p_c.astype(BF16), vc_ref[...], preferred_element_type=F32)

    p_sum = p_c[0:tq]
    for h in range(1, NSA_HPG):
        p_sum = p_sum + p_c[h * tq:(h + 1) * tq]
    hi, lo = _split_bf16(p_sum)
    smap_t = selmap_ref[...]
    imp_t = _nt_dot(smap_t, hi) + _nt_dot(smap_t, lo)
    blk = lax.broadcasted_iota(I32, (n_sel, tq), 0)
    cb = (t0 + lax.broadcasted_iota(I32, (n_sel, tq), 1)) // SEL_LEN
    causal = blk <= cb
    forced = (blk == 0) | (blk == cb) | (blk == cb - 1)
    n_forced = 3
    free = jnp.where(causal, jnp.where(forced, 0.0, 1.0), 0.0)
    picked = _topk_rows(jnp.where(free > 0.5, imp_t, -jnp.inf), max(min(SEL_TOPK, n_sel) - n_forced, 0))
    keep_t = jnp.where(forced, 1.0, picked * free)
    selb = _bias_from_keep(keep_t)
    qa = jnp.concatenate([qs, jnp.concatenate([selb] * NSA_HPG, axis=0)], axis=1)

    def kv_chunk(j):
        start = pl.multiple_of(j * tq, tq)
        ka = jnp.concatenate([ks_ref[pl.ds(start, tq), :], et_ref[pl.ds(start, tq), :]], axis=1)
        return ka, _with_ones(vs_ref[pl.ds(start, tq), :])

    qrow = lax.broadcasted_iota(I32, (rows, tq), 0) & (tq - 1)
    kcol = lax.broadcasted_iota(I32, (rows, tq), 1)
    m0 = jnp.full((rows, 1), NEG_BIG, F32)
    a0 = jnp.zeros((rows, 2 * HEAD_DIM), F32)
    carry = _flash_step(qa, *kv_chunk(qi), m0, a0, kcol <= qrow)
    odd = jnp.where(qi % 2 == 1, 0.0, NEG_BIG)
    carry = _flash_step(qa, *kv_chunk(jnp.maximum(qi - 1, 0)), *carry, None, odd)

    def body(i, c):
        c = _flash_step(qa, *kv_chunk(2 * i), *c, None)
        return _flash_step(qa, *kv_chunk(2 * i + 1), *c, None)

    _, a_s = lax.fori_loop(0, qi // 2, body, carry)
    o_s = _flash_out(a_s)

    n_back = WINDOW // tq
    scores, values = [], []
    for back in range(n_back + 1):
        start = pl.multiple_of(jnp.maximum(qi - back, 0) * tq, tq)
        s_b = _nt_dot(qs, kw_ref[pl.ds(start, tq), :])
        if back == 0:
            s_b = jnp.where(kcol <= qrow, s_b, NEG_BIG)
        else:
            if back == n_back:
                s_b = jnp.where(kcol > qrow, s_b, NEG_BIG)
            s_b = s_b + jnp.where(qi >= back, 0.0, NEG_BIG)
        scores.append(s_b)
        values.append(_with_ones(vw_ref[pl.ds(start, tq), :]))
    mw = scores[0].max(axis=1, keepdims=True)
    for s_b in scores[1:]:
        mw = jnp.maximum(mw, s_b.max(axis=1, keepdims=True))
    a_w = None
    for s_b, v_b in zip(scores, values):
        t = jnp.dot(jnp.exp(s_b - mw).astype(BF16), v_b, preferred_element_type=F32)
        a_w = t if a_w is None else a_w + t
    o_w = _flash_out(a_w)

    gl = gl_ref[...].astype(F32)
    sig = 1.0 / (1.0 + jnp.exp(-gl))
    for h in range(NSA_HPG):
        r = slice(h * tq, (h + 1) * tq)
        o = (sig[:, 3 * h:3 * h + 1] * o_c[r] + sig[:, 3 * h + 1:3 * h + 2] * o_s[r]
             + sig[:, 3 * h + 2:3 * h + 3] * o_w[r])
        o_ref[:, h * LANES:(h + 1) * LANES] = o.astype(BF16)


def _nsa(qkv, kcv, selmap, et_sel, bsz, seq):
    m = qkv.shape[0]
    tq = NSA_TQ
    assert seq % tq == 0 and WINDOW % tq == 0 and tq % SEL_LEN == 0
    nq = seq // tq
    n_cmp_pad = kcv.shape[2]
    gw = NSA_HPG * LANES
    col = lambda c0: (lambda b, g, i: (b, c0 + g))
    full = lambda c0: pl.BlockSpec((seq, LANES), col(c0))
    return pl.pallas_call(
        functools.partial(_nsa_kernel, tq=tq, seq=seq),
        grid=(bsz, NSA_KV_GROUPS, nq),
        in_specs=[pl.BlockSpec((tq, gw), lambda b, g, i: (b * nq + i, g)),
                  pl.BlockSpec((None, None, n_cmp_pad, LANES), lambda b, g, i: (g, b, 0, 0)),
                  pl.BlockSpec((None, None, n_cmp_pad, LANES),
                               lambda b, g, i: (NSA_KV_GROUPS + g, b, 0, 0)),
                  full(KS0), full(VS0), full(KW0), full(VW0),
                  pl.BlockSpec((tq, LANES), lambda b, g, i: (b * nq + i, GT0 + g)),
                  pl.BlockSpec(selmap.shape, lambda b, g, i: (0, 0)),
                  pl.BlockSpec(et_sel.shape, lambda b, g, i: (0, 0))],
        out_specs=pl.BlockSpec((tq, gw), lambda b, g, i: (b * nq + i, g)),
        out_shape=jax.ShapeDtypeStruct((m, NSA_HEADS * HEAD_DIM), BF16),
        compiler_params=pltpu.CompilerParams(
            dimension_semantics=("parallel", "parallel", "arbitrary"),
            vmem_limit_bytes=VMEM_LIMIT),
        name="nsa_attn",
    )(qkv, kcv, kcv, qkv, qkv, qkv, qkv, qkv, selmap, et_sel)


def _moba_kernel(q_ref, k_ref, v_ref, avg_ref, et_ref, o_ref, kmh_ref, kml_ref, *, seq, hb):
    qi = pl.program_id(2)
    blk_len = MOBA_BLOCK
    nb = seq // blk_len
    nb_rows = avg_ref.shape[0]
    head = lambda h: slice(h * LANES, (h + 1) * LANES)

    @pl.when(qi == 0)
    def _():
        for h in range(hb):
            km = jnp.dot(avg_ref[...], k_ref[:, head(h)], preferred_element_type=F32)
            kmh_ref[h], kml_ref[h] = _split_bf16(km)

    blk = lax.broadcasted_iota(I32, (nb_rows, blk_len), 0)
    past = jnp.where(blk < qi, 1.0, 0.0)
    own = jnp.where(blk == qi, 1.0, 0.0)
    qas = []
    for h in range(hb):
        q = q_ref[:, head(h)]
        gate_t = _nt_dot(kmh_ref[h], q) + _nt_dot(kml_ref[h], q)
        picked = _topk_rows(jnp.where(past > 0.5, gate_t, -jnp.inf), min(MOBA_TOPK, nb - 1))
        qas.append(jnp.concatenate([q, _bias_from_keep(picked * past + own)], axis=1))

    def kv_chunk(j, h):
        rows = pl.ds(pl.multiple_of(j * blk_len, blk_len), blk_len)
        ka = jnp.concatenate([k_ref[rows, head(h)], et_ref[rows, :]], axis=1)
        return ka, _with_ones(v_ref[rows, head(h)])

    row = lax.broadcasted_iota(I32, (blk_len, blk_len), 0)
    colk = lax.broadcasted_iota(I32, (blk_len, blk_len), 1)
    m0 = jnp.full((blk_len, 1), NEG_BIG, F32)
    a0 = jnp.zeros((blk_len, 2 * HEAD_DIM), F32)
    def all_heads(j, c, mask=None, penalty=None):
        new = []
        for h in range(hb):
            new += _flash_step(qas[h], *kv_chunk(j, h), c[2 * h], c[2 * h + 1], mask, penalty)
        return tuple(new)

    carry = all_heads(qi, (m0, a0) * hb, colk <= row)
    carry = all_heads(jnp.maximum(qi - 1, 0), carry, None, jnp.where(qi % 2 == 1, 0.0, NEG_BIG))
    final = lax.fori_loop(0, qi // 2, lambda i, c: all_heads(2 * i + 1, all_heads(2 * i, c)), carry)
    for h in range(hb):
        o_ref[:, head(h)] = _flash_out(final[2 * h + 1]).astype(BF16)


def _moba(qkv, avg, et_blk, bsz, seq):
    m = qkv.shape[0]
    nq = seq // MOBA_BLOCK
    hb = MOBA_HEADS_PER_STEP
    assert MOBA_HEADS % hb == 0 and QM0 % hb == 0 and KM0 % hb == 0 and VM0 % hb == 0
    hw = hb * LANES
    return pl.pallas_call(
        functools.partial(_moba_kernel, seq=seq, hb=hb),
        grid=(bsz, MOBA_HEADS // hb, nq),
        in_specs=[pl.BlockSpec((MOBA_BLOCK, hw), lambda b, h, i: (b * nq + i, QM0 // hb + h)),
                  pl.BlockSpec((seq, hw), lambda b, h, i: (b, KM0 // hb + h)),
                  pl.BlockSpec((seq, hw), lambda b, h, i: (b, VM0 // hb + h)),
                  pl.BlockSpec(avg.shape, lambda b, h, i: (0, 0)),
                  pl.BlockSpec(et_blk.shape, lambda b, h, i: (0, 0))],
        out_specs=pl.BlockSpec((MOBA_BLOCK, hw), lambda b, h, i: (b * nq + i, h)),
        out_shape=jax.ShapeDtypeStruct((m, MOBA_HEADS * HEAD_DIM), BF16),
        scratch_shapes=[pltpu.VMEM((hb, avg.shape[0], HEAD_DIM), BF16),
                        pltpu.VMEM((hb, avg.shape[0], HEAD_DIM), BF16)],
        compiler_params=pltpu.CompilerParams(
            dimension_semantics=("parallel", "parallel", "arbitrary"),
            vmem_limit_bytes=VMEM_LIMIT),
        name="moba_attn",
    )(qkv, qkv, qkv, avg, et_blk)


def _outproj_kernel(on_ref, om_ref, w_ref, x_ref, g_ref, gate_ref, o_ref):
    kn = on_ref.shape[1]
    acc = (jnp.dot(on_ref[...], w_ref[:kn, :], preferred_element_type=F32)
           + jnp.dot(om_ref[...], w_ref[kn:, :], preferred_element_type=F32))
    o_ref[...] = x_ref[...] + gate_ref[...] * _rms_norm(acc, g_ref[...])


def _outproj(o_n, o_m, w_o, x2, g_post, gate, seq):
    m, d = x2.shape
    tm = min(512, seq)
    tiles_per_b = seq // tm
    return pl.pallas_call(
        _outproj_kernel,
        grid=(m // tm,),
        in_specs=[pl.BlockSpec((tm, o_n.shape[1]), lambda i: (i, 0)),
                  pl.BlockSpec((tm, o_m.shape[1]), lambda i: (i, 0)),
                  pl.BlockSpec(w_o.shape, lambda i: (0, 0)),
                  pl.BlockSpec((tm, d), lambda i: (i, 0)),
                  pl.BlockSpec((1, d), lambda i: (0, 0)),
                  pl.BlockSpec((None, 1, d), lambda i: (i // tiles_per_b, 0, 0))],
        out_specs=pl.BlockSpec((tm, d), lambda i: (i, 0)),
        out_shape=jax.ShapeDtypeStruct((m, d), F32),
        compiler_params=pltpu.CompilerParams(dimension_semantics=("parallel",),
                                             vmem_limit_bytes=VMEM_LIMIT),
        name="outproj",
    )(o_n, o_m, w_o, x2, g_post, gate)


def _ffn_kernel(x_ref, gpre_ref, sc_ref, sh_ref, wu_ref, wd_ref, gpost_ref, gate_ref,
                o_ref, h_ref):
    k = pl.program_id(1)

    @pl.when(k == 0)
    def _():
        h = _rms_norm(x_ref[...], gpre_ref[...]) * (1.0 + sc_ref[...]) + sh_ref[...]
        h_ref[...] = h.astype(BF16)
        o_ref[...] = jnp.zeros_like(o_ref)

    u = jnp.dot(h_ref[...], wu_ref[...], preferred_element_type=F32)
    u = jnp.square(jnp.maximum(u, 0.0)).astype(BF16)
    o_ref[...] += jnp.dot(u, wd_ref[...], preferred_element_type=F32)

    @pl.when(k == pl.num_programs(1) - 1)
    def _():
        o_ref[...] = x_ref[...] + gate_ref[...] * _rms_norm(o_ref[...], gpost_ref[...])


def _ffn(x2, g_pre, scale, shift, w_up, w_down, g_post, gate, seq):
    m, d = x2.shape
    dff = w_up.shape[1]
    tm = min(1024, seq)
    ck = 512
    tiles_per_b = seq // tm
    vec = pl.BlockSpec((1, d), lambda i, k: (0, 0))
    per_b = pl.BlockSpec((None, 1, d), lambda i, k: (i // tiles_per_b, 0, 0))
    return pl.pallas_call(
        _ffn_kernel,
        grid=(m // tm, dff // ck),
        in_specs=[pl.BlockSpec((tm, d), lambda i, k: (i, 0)), vec, per_b, per_b,
                  pl.BlockSpec((d, ck), lambda i, k: (0, k)),
                  pl.BlockSpec((ck, d), lambda i, k: (k, 0)),
                  vec, per_b],
        out_specs=pl.BlockSpec((tm, d), lambda i, k: (i, 0)),
        out_shape=jax.ShapeDtypeStruct((m, d), F32),
        scratch_shapes=[pltpu.VMEM((tm, d), BF16)],
        compiler_params=pltpu.CompilerParams(dimension_semantics=("parallel", "arbitrary"),
                                             vmem_limit_bytes=VMEM_LIMIT),
        name="ffn",
    )(x2, g_pre, scale, shift, w_up, w_down, g_post, gate)


def _permute_w_in(w_in):
    d = w_in.shape[0]
    widths = [NSA_HEADS * HEAD_DIM] + [NSA_KV_GROUPS * HEAD_DIM] * 6 + \
             [3 * NSA_HEADS, MOBA_HEADS * HEAD_DIM, MOBA_HEADS * HEAD_DIM, MOBA_HEADS * HEAD_DIM]
    offs = np.cumsum([0] + widths)
    q_n, k_c, v_c, k_s, v_s, k_w, v_w, g_n, q_m, k_m, v_m = [
        w_in[:, offs[i]:offs[i + 1]] for i in range(11)]
    pad = lambda n: jnp.zeros((d, n), w_in.dtype)
    gates = []
    per_group = 3 * NSA_HPG
    for g in range(NSA_KV_GROUPS):
        gates += [g_n[:, g * per_group:(g + 1) * per_group], pad(LANES - per_group)]
    cols = [q_n, q_m, k_m, k_c, k_s, k_w, pad(2 * LANES), v_m, v_c, v_s, v_w] + gates
    return jnp.concatenate(cols, axis=1).astype(BF16)


def _constants(seq):
    n_cmp_pad = seq // CMP_STRIDE
    n_cmp = (seq - CMP_LEN) // CMP_STRIDE + 1
    n_sel = seq // SEL_LEN
    cs = np.arange(n_cmp) * CMP_STRIDE
    ss = np.arange(n_sel) * SEL_LEN
    overlap = np.clip(np.minimum(cs[:, None] + CMP_LEN, ss[None, :] + SEL_LEN)
                      - np.maximum(cs[:, None], ss[None, :]), 0, None)
    selmap_t = np.zeros((n_sel, n_cmp_pad), np.float32)
    selmap_t[:, :n_cmp] = (overlap / CMP_LEN).T
    keys = np.arange(seq)
    et_sel = (keys[:, None] // SEL_LEN == np.arange(LANES)[None, :]).astype(np.float32)
    et_blk = (keys[:, None] // MOBA_BLOCK == np.arange(LANES)[None, :]).astype(np.float32)
    nb_rows = -(-(seq // MOBA_BLOCK) // BF16_SUBLANES) * BF16_SUBLANES
    avg = et_blk.T[:nb_rows] / MOBA_BLOCK
    to = lambda a: jnp.asarray(a, BF16)
    return to(selmap_t), to(et_sel), to(et_blk), to(avg)


def _rope_tables(seq):
    pos = jnp.arange(seq, dtype=F32)
    inv = ROPE_THETA ** (-jnp.arange(0, HEAD_DIM, 2, dtype=F32) / HEAD_DIM)
    ang = pos[:, None] * inv[None, :]
    cos, sin = jnp.cos(ang), jnp.sin(ang)
    return jnp.concatenate([cos, cos], axis=1), jnp.concatenate([-sin, sin], axis=1)


def kernel(x, c, w_ada, b_ada, pre_norm_mix, post_norm_mix, w_in, cmp_k_pos, cmp_k_w1, cmp_k_w2,
           cmp_v_pos, cmp_v_w1, cmp_v_w2, w_o, pre_norm_ffn, post_norm_ffn, w_up, w_down):
    bsz, seq, d = x.shape
    depth = w_ada.shape[0]
    cos2, sin2 = _rope_tables(seq)
    selmap, et_sel, et_blk, avg = _constants(seq)
    x2 = x.reshape(bsz * seq, d)
    for l in range(depth):
        mod = _adaln(c, w_ada[l], b_ada[l])
        shift_m, scale_m, gate_m, shift_f, scale_f, gate_f = [
            a.reshape(bsz, 1, d) for a in jnp.split(mod, 6, axis=-1)]
        row = lambda a: a.reshape(1, d)

        qkv, kvc = _inproj(x2, row(pre_norm_mix[l]), scale_m, shift_m, _permute_w_in(w_in[l]),
                           cos2, sin2, seq)
        kvc4 = kvc.reshape(2 * NSA_KV_GROUPS, bsz, seq // CMP_STRIDE, CMP_STRIDE * HEAD_DIM)
        pos2 = jnp.stack([cmp_k_pos[l].reshape(1, -1), cmp_v_pos[l].reshape(1, -1)])
        kcv = _compress(kvc4, pos2, jnp.stack([cmp_k_w1[l], cmp_v_w1[l]]),
                        jnp.stack([cmp_k_w2[l], cmp_v_w2[l]]))
        o_n = _nsa(qkv, kcv, selmap, et_sel, bsz, seq)
        o_m = _moba(qkv, avg, et_blk, bsz, seq)
        x2 = _outproj(o_n, o_m, w_o[l].astype(BF16), x2, row(post_norm_mix[l]), gate_m, seq)
        x2 = _ffn(x2, row(pre_norm_ffn[l]), scale_f, shift_f, w_up[l].astype(BF16),
                  w_down[l].astype(BF16), row(post_norm_ffn[l]), gate_f, seq)
    return x2.reshape(bsz, seq, d)
```

```python
import functools

import numpy as np
import jax
import jax.numpy as jnp
from jax import lax
from jax.experimental import pallas as pl
from jax.experimental.pallas import tpu as pltpu

F32 = jnp.float32
BF16 = jnp.bfloat16
I32 = jnp.int32

HEAD_DIM = 128
NSA_HEADS = 8
NSA_KV_GROUPS = 2
NSA_HPG = NSA_HEADS // NSA_KV_GROUPS
MOBA_HEADS = 8
CMP_LEN = 32
CMP_STRIDE = 16
SEL_LEN = 64
SEL_TOPK = 16
WINDOW = 512
MOBA_BLOCK = 256
MOBA_TOPK = 3
ROPE_THETA = 10000.0
EPS = 1e-6

LANES = 128
BF16_SUBLANES = 16
MXU_COLS = 256
NSA_TQ = 256
MOBA_HEADS_PER_STEP = 8
NEG_BIG = -1e30
VMEM_LIMIT = 56 * 1024 * 1024

QN0 = 0
QM0 = 8
KM0 = 16
VM0 = 24
KC0 = 32
VC0 = 34
KS0 = 36
VS0 = 38
KW0 = 40
VW0 = 42
GT0 = 44
N_BLOCKS = 48
INPROJ_TN = 1024
HEAD_COLS = 2560
GATE_COLS = 3 * NSA_HEADS
ROPE_Q, ROPE_K, PLAIN, GATES, ZERO = range(5)
_INPROJ_TILES = (
    ("head", 0, (ROPE_Q,) * 4),
    ("tail", 0, (ROPE_Q,) * 4),
    ("tail", 1, (ROPE_K,) * 4),
    ("tail", 2, (PLAIN,) * 4),
    ("head", 1, (ROPE_K, PLAIN, ROPE_K, PLAIN)),
    ("head", 2, (ROPE_K, PLAIN, GATES, ZERO)),
)


def _nt_dot(a, b):
    return lax.dot_general(a, b, (((1,), (1,)), ((), ())), preferred_element_type=F32)


def _rms_norm(x, g):
    ms = jnp.mean(x * x, axis=-1, keepdims=True)
    return x * lax.rsqrt(ms + EPS) * g


def _split_bf16(a):
    hi = a.astype(BF16)
    lo = (a - hi.astype(F32)).astype(BF16)
    return hi, lo


def _adaln_kernel(ct_ref, w_ref, b_ref, o_ref):
    ct = ct_ref[...]
    a = ct * (1.0 / (1.0 + jnp.exp(-ct)))
    w = w_ref[...]
    for b in range(o_ref.shape[0]):
        o_ref[b] = jnp.sum(w * a[:, b:b + 1], axis=0, keepdims=True) + b_ref[...]


def _adaln(c, w, b):
    bsz, d = c.shape
    n = w.shape[1]
    tn = 1024
    assert bsz <= LANES
    ct = jnp.zeros((d, LANES), F32).at[:, :bsz].set(c.T)
    out = pl.pallas_call(
        _adaln_kernel,
        grid=(n // tn,),
        in_specs=[pl.BlockSpec((d, LANES), lambda j: (0, 0)),
                  pl.BlockSpec((d, tn), lambda j: (0, j)),
                  pl.BlockSpec((1, tn), lambda j: (0, j))],
        out_specs=pl.BlockSpec((bsz, 1, tn), lambda j: (0, 0, j)),
        out_shape=jax.ShapeDtypeStruct((bsz, 1, n), F32),
        compiler_params=pltpu.CompilerParams(dimension_semantics=("parallel",),
                                             vmem_limit_bytes=VMEM_LIMIT),
        name="adaln",
    )(ct, w, b.reshape(1, n))
    return out.reshape(bsz, n)


def _inproj_kernel(x_ref, g_ref, sc_ref, sh_ref, wh_ref, wt_ref, cos_ref, sin_ref,
                   o_ref, kvc_ref, h_ref):
    j = pl.program_id(1)

    @pl.when(j == 0)
    def _():
        h = _rms_norm(x_ref[...], g_ref[...]) * (1.0 + sc_ref[...]) + sh_ref[...]
        h_ref[...] = h.astype(BF16)

    def tile(w_ref, kinds):
        for ci, kind in enumerate(kinds):
            c0 = ci * MXU_COLS
            if kind == ZERO:
                o_ref[:, c0:c0 + MXU_COLS] = jnp.zeros((o_ref.shape[0], MXU_COLS), BF16)
                continue
            acc = jnp.dot(h_ref[...], w_ref[:, c0:c0 + MXU_COLS], preferred_element_type=F32)
            for c in range(c0, c0 + MXU_COLS, LANES):
                ch = acc[:, c - c0:c - c0 + LANES]
                if kind in (ROPE_Q, ROPE_K):
                    ch = ch * cos_ref[...] + pltpu.roll(ch, HEAD_DIM // 2, 1) * sin_ref[...]
                if kind == ROPE_Q:
                    ch = ch * HEAD_DIM ** -0.5
                o_ref[:, c:c + LANES] = ch.astype(BF16)

    for jj, (src, _, kinds) in enumerate(_INPROJ_TILES):
        pl.when(j == jj)(functools.partial(tile, wh_ref if src == "head" else wt_ref, kinds))

    jkv, ok = divmod(KC0 * LANES, INPROJ_TN)
    ov = VC0 * LANES - jkv * INPROJ_TN
    assert 0 < ov < INPROJ_TN

    @pl.when(j == jkv)
    def _():
        for g in range(NSA_KV_GROUPS):
            kvc_ref[g] = o_ref[:, ok + g * LANES: ok + (g + 1) * LANES]
            kvc_ref[NSA_KV_GROUPS + g] = o_ref[:, ov + g * LANES: ov + (g + 1) * LANES]


def _step_lookup(j, table):
    out = jnp.int32(table[0])
    for k in range(1, len(table)):
        out = jnp.where(j >= k, table[k], out)
    return out


def _inproj(x2, g, scale, shift, w_head, w_tail, cos2, sin2, seq):
    m, d = x2.shape
    tm = min(1024, seq)
    tn = INPROJ_TN
    tiles_per_b = seq // tm
    blocks = {"head": [], "tail": []}
    for src, blk, _ in _INPROJ_TILES:
        for name, lst in blocks.items():
            lst.append(blk if src == name else (lst[-1] if lst else None))
    for lst in blocks.values():
        first = next(b for b in lst if b is not None)
        lst[:] = [first if b is None else b for b in lst]
    return pl.pallas_call(
        _inproj_kernel,
        grid=(m // tm, len(_INPROJ_TILES)),
        in_specs=[pl.BlockSpec((tm, d), lambda i, j: (i, 0)),
                  pl.BlockSpec((1, d), lambda i, j: (0, 0)),
                  pl.BlockSpec((None, 1, d), lambda i, j: (i // tiles_per_b, 0, 0)),
                  pl.BlockSpec((None, 1, d), lambda i, j: (i // tiles_per_b, 0, 0)),
                  pl.BlockSpec((d, tn), lambda i, j: (0, _step_lookup(j, blocks["head"]))),
                  pl.BlockSpec((d, tn), lambda i, j: (0, _step_lookup(j, blocks["tail"]))),
                  pl.BlockSpec((tm, LANES), lambda i, j: (i % tiles_per_b, 0)),
                  pl.BlockSpec((tm, LANES), lambda i, j: (i % tiles_per_b, 0))],
        out_specs=[pl.BlockSpec((tm, tn), lambda i, j: (i, j)),
                   pl.BlockSpec((2 * NSA_KV_GROUPS, tm, LANES), lambda i, j: (0, i, 0))],
        out_shape=[jax.ShapeDtypeStruct((m, N_BLOCKS * LANES), BF16),
                   jax.ShapeDtypeStruct((2 * NSA_KV_GROUPS, m, LANES), BF16)],
        scratch_shapes=[pltpu.VMEM((tm, d), BF16)],
        compiler_params=pltpu.CompilerParams(dimension_semantics=("parallel", "arbitrary"),
                                             vmem_limit_bytes=VMEM_LIMIT),
        name="inproj",
    )(x2, g, scale, shift, w_head, w_tail, cos2, sin2)


def _compress_kernel(x_ref, pos_ref, w1_ref, w2_ref, o_ref):
    half = CMP_STRIDE * HEAD_DIM
    x = x_ref[...].astype(F32)
    pos = pos_ref[...]
    w1 = w1_ref[...].astype(BF16)
    a = jnp.dot((x + pos[:, :half]).astype(BF16), w1[:half], preferred_element_type=F32)
    b = jnp.dot((x + pos[:, half:]).astype(BF16), w1[half:], preferred_element_type=F32)
    n_rows = x.shape[0]
    h1 = a + pltpu.roll(b, n_rows - 1, 0)
    h = h1 * (1.0 / (1.0 + jnp.exp(-h1)))
    o_ref[...] = jnp.dot(h.astype(BF16), w2_ref[...].astype(BF16),
                         preferred_element_type=F32).astype(BF16)


def _compress(kvc4, pos2, w1s, w2s):
    ns, bsz, n_rows, width = kvc4.shape
    return pl.pallas_call(
        _compress_kernel,
        grid=(ns, bsz),
        in_specs=[pl.BlockSpec((None, None, n_rows, width), lambda s, b: (s, b, 0, 0)),
                  pl.BlockSpec((None, 1, 2 * width), lambda s, b: (s // NSA_KV_GROUPS, 0, 0)),
                  pl.BlockSpec((None, 2 * width, HEAD_DIM), lambda s, b: (s // NSA_KV_GROUPS, 0, 0)),
                  pl.BlockSpec((None, HEAD_DIM, HEAD_DIM), lambda s, b: (s // NSA_KV_GROUPS, 0, 0))],
        out_specs=pl.BlockSpec((None, None, n_rows, HEAD_DIM), lambda s, b: (s, b, 0, 0)),
        out_shape=jax.ShapeDtypeStruct((ns, bsz, n_rows, HEAD_DIM), BF16),
        compiler_params=pltpu.CompilerParams(dimension_semantics=("parallel", "parallel"),
                                             vmem_limit_bytes=VMEM_LIMIT),
        name="compress",
    )(kvc4, pos2, w1s, w2s)


def _topk_rows(score, k):
    idx = lax.broadcasted_iota(I32, score.shape, 0).astype(F32)
    work = score
    sel = jnp.zeros(score.shape, F32)
    for _ in range(k):
        mx = jnp.max(work, axis=0, keepdims=True)
        first = jnp.min(jnp.where(work == mx, idx, float(score.shape[0])), axis=0, keepdims=True)
        hit = idx == first
        sel = jnp.where(hit, 1.0, sel)
        work = jnp.where(hit, -jnp.inf, work)
    return sel


def _bias_from_keep(keep_t):
    n, q = keep_t.shape
    padded = jnp.concatenate([keep_t, jnp.zeros((LANES - n, q), F32)], axis=0) if n < LANES else keep_t
    return jnp.where(padded.T > 0.5, 0.0, NEG_BIG).astype(BF16)


def _with_ones(v):
    return jnp.concatenate([v, jnp.ones(v.shape, v.dtype)], axis=1)


def _flash_step(qa, ka, va, m, acc, mask, penalty=None):
    s = _nt_dot(qa, ka)
    if mask is not None:
        s = jnp.where(mask, s, NEG_BIG)
    if penalty is not None:
        s = s + penalty
    m_new = jnp.maximum(m, jnp.max(s, axis=1, keepdims=True))
    alpha = jnp.exp(m - m_new)
    p = jnp.exp(s - m_new).astype(BF16)
    acc = alpha * acc + jnp.dot(p, va, preferred_element_type=F32)
    return m_new, acc


def _flash_out(acc):
    return acc[:, :HEAD_DIM] * (1.0 / acc[:, HEAD_DIM:])


def _nsa_kernel(q_ref, kc_ref, vc_ref, ks_ref, vs_ref, kw_ref, vw_ref, gl_ref,
                selmap_ref, et_ref, o_ref, *, tq, seq):
    qi = pl.program_id(2)
    t0 = qi * tq
    rows = NSA_HPG * tq
    n_cmp_pad = kc_ref.shape[0]
    n_sel = seq // SEL_LEN

    q = q_ref[...]
    qs = jnp.concatenate([q[:, h * LANES:(h + 1) * LANES] for h in range(NSA_HPG)], axis=0)

    s = _nt_dot(qs, kc_ref[...])
    tt = t0 + (lax.broadcasted_iota(I32, (rows, n_cmp_pad), 0) & (tq - 1))
    cend = lax.broadcasted_iota(I32, (rows, n_cmp_pad), 1) * CMP_STRIDE + (CMP_LEN - 1)
    s = jnp.where(cend <= tt, s, -jnp.inf)
    mx = jnp.max(s, axis=1, keepdims=True)
    mx = jnp.where(mx == -jnp.inf, 0.0, mx)
    e = jnp.exp(s - mx)
    den = jnp.maximum(jnp.sum(e, axis=1, keepdims=True), 1e-30)
    p_c = e * (1.0 / den)
    o_c = jnp.dot(p_c.astype(BF16), vc_ref[...], preferred_element_type=F32)

    p_sum = p_c[0:tq]
    for h in range(1, NSA_HPG):
        p_sum = p_sum + p_c[h * tq:(h + 1) * tq]
    hi, lo = _split_bf16(p_sum)
    smap_t = selmap_ref[...]
    imp_t = _nt_dot(smap_t, hi) + _nt_dot(smap_t, lo)
    blk = lax.broadcasted_iota(I32, (n_sel, tq), 0)
    cb = (t0 + lax.broadcasted_iota(I32, (n_sel, tq), 1)) // SEL_LEN
    causal = blk <= cb
    forced = (blk == 0) | (blk == cb) | (blk == cb - 1)
    n_forced = 3
    free = jnp.where(causal, jnp.where(forced, 0.0, 1.0), 0.0)
    picked = _topk_rows(jnp.where(free > 0.5, imp_t, -jnp.inf), max(min(SEL_TOPK, n_sel) - n_forced, 0))
    keep_t = jnp.where(forced, 1.0, picked * free)
    selb = _bias_from_keep(keep_t)
    qa = jnp.concatenate([qs, jnp.concatenate([selb] * NSA_HPG, axis=0)], axis=1)

    def kv_chunk(j):
        start = pl.multiple_of(j * tq, tq)
        ka = jnp.concatenate([ks_ref[pl.ds(start, tq), :], et_ref[pl.ds(start, tq), :]], axis=1)
        return ka, _with_ones(vs_ref[pl.ds(start, tq), :])

    qrow = lax.broadcasted_iota(I32, (rows, tq), 0) & (tq - 1)
    kcol = lax.broadcasted_iota(I32, (rows, tq), 1)
    m0 = jnp.full((rows, 1), NEG_BIG, F32)
    a0 = jnp.zeros((rows, 2 * HEAD_DIM), F32)
    carry = _flash_step(qa, *kv_chunk(qi), m0, a0, kcol <= qrow)
    odd = jnp.where(qi % 2 == 1, 0.0, NEG_BIG)
    carry = _flash_step(qa, *kv_chunk(jnp.maximum(qi - 1, 0)), *carry, None, odd)

    def body(i, c):
        c = _flash_step(qa, *kv_chunk(2 * i), *c, None)
        return _flash_step(qa, *kv_chunk(2 * i + 1), *c, None)

    _, a_s = lax.fori_loop(0, qi // 2, body, carry)
    o_s = _flash_out(a_s)

    n_back = WINDOW // tq
    scores, values = [], []
    for back in range(n_back + 1):
        start = pl.multiple_of(jnp.maximum(qi - back, 0) * tq, tq)
        s_b = _nt_dot(qs, kw_ref[pl.ds(start, tq), :])
        if back == 0:
            s_b = jnp.where(kcol <= qrow, s_b, NEG_BIG)
        else:
            if back == n_back:
                s_b = jnp.where(kcol > qrow, s_b, NEG_BIG)
            s_b = s_b + jnp.where(qi >= back, 0.0, NEG_BIG)
        scores.append(s_b)
        values.append(_with_ones(vw_ref[pl.ds(start, tq), :]))
    mw = scores[0].max(axis=1, keepdims=True)
    for s_b in scores[1:]:
        mw = jnp.maximum(mw, s_b.max(axis=1, keepdims=True))
    a_w = None
    for s_b, v_b in zip(scores, values):
        t = jnp.dot(jnp.exp(s_b - mw).astype(BF16), v_b, preferred_element_type=F32)
        a_w = t if a_w is None else a_w + t
    o_w = _flash_out(a_w)

    gl = gl_ref[...].astype(F32)
    sig = 1.0 / (1.0 + jnp.exp(-gl))
    for h in range(NSA_HPG):
        r = slice(h * tq, (h + 1) * tq)
        o = (sig[:, 3 * h:3 * h + 1] * o_c[r] + sig[:, 3 * h + 1:3 * h + 2] * o_s[r]
             + sig[:, 3 * h + 2:3 * h + 3] * o_w[r])
        o_ref[:, h * LANES:(h + 1) * LANES] = o.astype(BF16)


def _nsa(qkv, kcv, selmap, et_sel, bsz, seq):
    m = qkv.shape[0]
    tq = NSA_TQ
    assert seq % tq == 0 and WINDOW % tq == 0 and tq % SEL_LEN == 0
    nq = seq // tq
    n_cmp_pad = kcv.shape[2]
    gw = NSA_HPG * LANES
    col = lambda c0: (lambda b, g, i: (b, c0 + g))
    full = lambda c0: pl.BlockSpec((seq, LANES), col(c0))
    return pl.pallas_call(
        functools.partial(_nsa_kernel, tq=tq, seq=seq),
        grid=(bsz, NSA_KV_GROUPS, nq),
        in_specs=[pl.BlockSpec((tq, gw), lambda b, g, i: (b * nq + i, g)),
                  pl.BlockSpec((None, None, n_cmp_pad, LANES), lambda b, g, i: (g, b, 0, 0)),
                  pl.BlockSpec((None, None, n_cmp_pad, LANES),
                               lambda b, g, i: (NSA_KV_GROUPS + g, b, 0, 0)),
                  full(KS0), full(VS0), full(KW0), full(VW0),
                  pl.BlockSpec((tq, LANES), lambda b, g, i: (b * nq + i, GT0 + g)),
                  pl.BlockSpec(selmap.shape, lambda b, g, i: (0, 0)),
                  pl.BlockSpec(et_sel.shape, lambda b, g, i: (0, 0))],
        out_specs=pl.BlockSpec((tq, gw), lambda b, g, i: (b * nq + i, g)),
        out_shape=jax.ShapeDtypeStruct((m, NSA_HEADS * HEAD_DIM), BF16),
        compiler_params=pltpu.CompilerParams(
            dimension_semantics=("parallel", "parallel", "arbitrary"),
            vmem_limit_bytes=VMEM_LIMIT),
        name="nsa_attn",
    )(qkv, kcv, kcv, qkv, qkv, qkv, qkv, qkv, selmap, et_sel)


def _moba_kernel(q_ref, k_ref, v_ref, avg_ref, et_ref, o_ref, kmh_ref, kml_ref, *, seq, hb):
    qi = pl.program_id(2)
    blk_len = MOBA_BLOCK
    nb = seq // blk_len
    nb_rows = avg_ref.shape[0]
    head = lambda h: slice(h * LANES, (h + 1) * LANES)

    @pl.when(qi == 0)
    def _():
        for h in range(hb):
            km = jnp.dot(avg_ref[...], k_ref[:, head(h)], preferred_element_type=F32)
            kmh_ref[h], kml_ref[h] = _split_bf16(km)

    blk = lax.broadcasted_iota(I32, (nb_rows, blk_len), 0)
    past = jnp.where(blk < qi, 1.0, 0.0)
    own = jnp.where(blk == qi, 1.0, 0.0)
    qas = []
    for h in range(hb):
        q = q_ref[:, head(h)]
        gate_t = _nt_dot(kmh_ref[h], q) + _nt_dot(kml_ref[h], q)
        picked = _topk_rows(jnp.where(past > 0.5, gate_t, -jnp.inf), min(MOBA_TOPK, nb - 1))
        qas.append(jnp.concatenate([q, _bias_from_keep(picked * past + own)], axis=1))

    def kv_chunk(j, h):
        rows = pl.ds(pl.multiple_of(j * blk_len, blk_len), blk_len)
        ka = jnp.concatenate([k_ref[rows, head(h)], et_ref[rows, :]], axis=1)
        return ka, _with_ones(v_ref[rows, head(h)])

    row = lax.broadcasted_iota(I32, (blk_len, blk_len), 0)
    colk = lax.broadcasted_iota(I32, (blk_len, blk_len), 1)
    m0 = jnp.full((blk_len, 1), NEG_BIG, F32)
    a0 = jnp.zeros((blk_len, 2 * HEAD_DIM), F32)
    def all_heads(j, c, mask=None, penalty=None):
        new = []
        for h in range(hb):
            new += _flash_step(qas[h], *kv_chunk(j, h), c[2 * h], c[2 * h + 1], mask, penalty)
        return tuple(new)

    carry = all_heads(qi, (m0, a0) * hb, colk <= row)
    carry = all_heads(jnp.maximum(qi - 1, 0), carry, None, jnp.where(qi % 2 == 1, 0.0, NEG_BIG))
    final = lax.fori_loop(0, qi // 2, lambda i, c: all_heads(2 * i + 1, all_heads(2 * i, c)), carry)
    for h in range(hb):
        o_ref[:, head(h)] = _flash_out(final[2 * h + 1]).astype(BF16)


def _moba(qkv, avg, et_blk, bsz, seq):
    m = qkv.shape[0]
    nq = seq // MOBA_BLOCK
    hb = MOBA_HEADS_PER_STEP
    assert MOBA_HEADS % hb == 0 and QM0 % hb == 0 and KM0 % hb == 0 and VM0 % hb == 0
    hw = hb * LANES
    return pl.pallas_call(
        functools.partial(_moba_kernel, seq=seq, hb=hb),
        grid=(bsz, MOBA_HEADS // hb, nq),
        in_specs=[pl.BlockSpec((MOBA_BLOCK, hw), lambda b, h, i: (b * nq + i, QM0 // hb + h)),
                  pl.BlockSpec((seq, hw), lambda b, h, i: (b, KM0 // hb + h)),
                  pl.BlockSpec((seq, hw), lambda b, h, i: (b, VM0 // hb + h)),
                  pl.BlockSpec(avg.shape, lambda b, h, i: (0, 0)),
                  pl.BlockSpec(et_blk.shape, lambda b, h, i: (0, 0))],
        out_specs=pl.BlockSpec((MOBA_BLOCK, hw), lambda b, h, i: (b * nq + i, h)),
        out_shape=jax.ShapeDtypeStruct((m, MOBA_HEADS * HEAD_DIM), BF16),
        scratch_shapes=[pltpu.VMEM((hb, avg.shape[0], HEAD_DIM), BF16),
                        pltpu.VMEM((hb, avg.shape[0], HEAD_DIM), BF16)],
        compiler_params=pltpu.CompilerParams(
            dimension_semantics=("parallel", "parallel", "arbitrary"),
            vmem_limit_bytes=VMEM_LIMIT),
        name="moba_attn",
    )(qkv, qkv, qkv, avg, et_blk)


def _outproj_kernel(on_ref, om_ref, w_ref, x_ref, g_ref, gate_ref, o_ref):
    kn = on_ref.shape[1]
    acc = (jnp.dot(on_ref[...], w_ref[:kn, :], preferred_element_type=F32)
           + jnp.dot(om_ref[...], w_ref[kn:, :], preferred_element_type=F32))
    o_ref[...] = x_ref[...] + gate_ref[...] * _rms_norm(acc, g_ref[...])


def _outproj(o_n, o_m, w_o, x2, g_post, gate, seq):
    m, d = x2.shape
    tm = min(512, seq)
    tiles_per_b = seq // tm
    return pl.pallas_call(
        _outproj_kernel,
        grid=(m // tm,),
        in_specs=[pl.BlockSpec((tm, o_n.shape[1]), lambda i: (i, 0)),
                  pl.BlockSpec((tm, o_m.shape[1]), lambda i: (i, 0)),
                  pl.BlockSpec(w_o.shape, lambda i: (0, 0)),
                  pl.BlockSpec((tm, d), lambda i: (i, 0)),
                  pl.BlockSpec((1, d), lambda i: (0, 0)),
                  pl.BlockSpec((None, 1, d), lambda i: (i // tiles_per_b, 0, 0))],
        out_specs=pl.BlockSpec((tm, d), lambda i: (i, 0)),
        out_shape=jax.ShapeDtypeStruct((m, d), F32),
        compiler_params=pltpu.CompilerParams(dimension_semantics=("parallel",),
                                             vmem_limit_bytes=VMEM_LIMIT),
        name="outproj",
    )(o_n, o_m, w_o, x2, g_post, gate)


def _ffn_kernel(x_ref, gpre_ref, sc_ref, sh_ref, wu_ref, wd_ref, gpost_ref, gate_ref,
                o_ref, h_ref):
    k = pl.program_id(1)

    @pl.when(k == 0)
    def _():
        h = _rms_norm(x_ref[...], gpre_ref[...]) * (1.0 + sc_ref[...]) + sh_ref[...]
        h_ref[...] = h.astype(BF16)
        o_ref[...] = jnp.zeros_like(o_ref)

    u = jnp.dot(h_ref[...], wu_ref[...], preferred_element_type=F32)
    u = jnp.square(jnp.maximum(u, 0.0)).astype(BF16)
    o_ref[...] += jnp.dot(u, wd_ref[...], preferred_element_type=F32)

    @pl.when(k == pl.num_programs(1) - 1)
    def _():
        o_ref[...] = x_ref[...] + gate_ref[...] * _rms_norm(o_ref[...], gpost_ref[...])


def _ffn(x2, g_pre, scale, shift, w_up, w_down, g_post, gate, seq):
    m, d = x2.shape
    dff = w_up.shape[1]
    tm = min(1024, seq)
    ck = 512
    tiles_per_b = seq // tm
    vec = pl.BlockSpec((1, d), lambda i, k: (0, 0))
    per_b = pl.BlockSpec((None, 1, d), lambda i, k: (i // tiles_per_b, 0, 0))
    return pl.pallas_call(
        _ffn_kernel,
        grid=(m // tm, dff // ck),
        in_specs=[pl.BlockSpec((tm, d), lambda i, k: (i, 0)), vec, per_b, per_b,
                  pl.BlockSpec((d, ck), lambda i, k: (0, k)),
                  pl.BlockSpec((ck, d), lambda i, k: (k, 0)),
                  vec, per_b],
        out_specs=pl.BlockSpec((tm, d), lambda i, k: (i, 0)),
        out_shape=jax.ShapeDtypeStruct((m, d), F32),
        scratch_shapes=[pltpu.VMEM((tm, d), BF16)],
        compiler_params=pltpu.CompilerParams(dimension_semantics=("parallel", "arbitrary"),
                                             vmem_limit_bytes=VMEM_LIMIT),
        name="ffn",
    )(x2, g_pre, scale, shift, w_up, w_down, g_post, gate)


def _split_w_in(w_in):
    d = w_in.shape[0]
    n_head = sum(1 for src, _, _ in _INPROJ_TILES if src == "head") * INPROJ_TN
    pad = lambda n: jnp.zeros((d, n), BF16)
    per_group = 3 * NSA_HPG
    pieces = [w_in[:, :HEAD_COLS].astype(BF16)]
    for g in range(NSA_KV_GROUPS):
        lo = HEAD_COLS + g * per_group
        pieces += [w_in[:, lo:lo + per_group].astype(BF16), pad(LANES - per_group)]
    pieces.append(pad(n_head - HEAD_COLS - NSA_KV_GROUPS * LANES))
    return jnp.concatenate(pieces, axis=1), w_in[:, HEAD_COLS + GATE_COLS:].astype(BF16)


def _constants(seq):
    n_cmp_pad = seq // CMP_STRIDE
    n_cmp = (seq - CMP_LEN) // CMP_STRIDE + 1
    n_sel = seq // SEL_LEN
    cs = np.arange(n_cmp) * CMP_STRIDE
    ss = np.arange(n_sel) * SEL_LEN
    overlap = np.clip(np.minimum(cs[:, None] + CMP_LEN, ss[None, :] + SEL_LEN)
                      - np.maximum(cs[:, None], ss[None, :]), 0, None)
    selmap_t = np.zeros((n_sel, n_cmp_pad), np.float32)
    selmap_t[:, :n_cmp] = (overlap / CMP_LEN).T
    keys = np.arange(seq)
    et_sel = (keys[:, None] // SEL_LEN == np.arange(LANES)[None, :]).astype(np.float32)
    et_blk = (keys[:, None] // MOBA_BLOCK == np.arange(LANES)[None, :]).astype(np.float32)
    nb_rows = -(-(seq // MOBA_BLOCK) // BF16_SUBLANES) * BF16_SUBLANES
    avg = et_blk.T[:nb_rows] / MOBA_BLOCK
    to = lambda a: jnp.asarray(a, BF16)
    return to(selmap_t), to(et_sel), to(et_blk), to(avg)


def _rope_tables(seq):
    pos = jnp.arange(seq, dtype=F32)
    inv = ROPE_THETA ** (-jnp.arange(0, HEAD_DIM, 2, dtype=F32) / HEAD_DIM)
    ang = pos[:, None] * inv[None, :]
    cos, sin = jnp.cos(ang), jnp.sin(ang)
    return jnp.concatenate([cos, cos], axis=1), jnp.concatenate([-sin, sin], axis=1)


def kernel(x, c, w_ada, b_ada, pre_norm_mix, post_norm_mix, w_in, cmp_k_pos, cmp_k_w1, cmp_k_w2,
           cmp_v_pos, cmp_v_w1, cmp_v_w2, w_o, pre_norm_ffn, post_norm_ffn, w_up, w_down):
    bsz, seq, d = x.shape
    depth = w_ada.shape[0]
    cos2, sin2 = _rope_tables(seq)
    selmap, et_sel, et_blk, avg = _constants(seq)
    x2 = x.reshape(bsz * seq, d)
    for l in range(depth):
        mod = _adaln(c, w_ada[l], b_ada[l])
        shift_m, scale_m, gate_m, shift_f, scale_f, gate_f = [
            a.reshape(bsz, 1, d) for a in jnp.split(mod, 6, axis=-1)]
        row = lambda a: a.reshape(1, d)

        qkv, kvc = _inproj(x2, row(pre_norm_mix[l]), scale_m, shift_m, *_split_w_in(w_in[l]),
                           cos2, sin2, seq)
        kvc4 = kvc.reshape(2 * NSA_KV_GROUPS, bsz, seq // CMP_STRIDE, CMP_STRIDE * HEAD_DIM)
        pos2 = jnp.stack([cmp_k_pos[l].reshape(1, -1), cmp_v_pos[l].reshape(1, -1)])
        kcv = _compress(kvc4, pos2, jnp.stack([cmp_k_w1[l], cmp_v_w1[l]]),
                        jnp.stack([cmp_k_w2[l], cmp_v_w2[l]]))
        o_n = _nsa(qkv, kcv, selmap, et_sel, bsz, seq)
        o_m = _moba(qkv, avg, et_blk, bsz, seq)
        x2 = _outproj(o_n, o_m, w_o[l].astype(BF16), x2, row(post_norm_mix[l]), gate_m, seq)
        x2 = _ffn(x2, row(pre_norm_ffn[l]), scale_f, shift_f, w_up[l].astype(BF16),
                  w_down[l].astype(BF16), row(post_norm_ffn[l]), gate_f, seq)
    return x2.reshape(bsz, seq, d)
```

```python
import functools

import numpy as np
import jax
import jax.numpy as jnp
from jax import lax
from jax.experimental import pallas as pl
from jax.experimental.pallas import tpu as pltpu

F32 = jnp.float32
BF16 = jnp.bfloat16
I32 = jnp.int32

HEAD_DIM = 128
NSA_HEADS = 8
NSA_KV_GROUPS = 2
NSA_HPG = NSA_HEADS // NSA_KV_GROUPS
MOBA_HEADS = 8
CMP_LEN = 32
CMP_STRIDE = 16
SEL_LEN = 64
SEL_TOPK = 16
WINDOW = 512
MOBA_BLOCK = 256
MOBA_TOPK = 3
ROPE_THETA = 10000.0
EPS = 1e-6

LANES = 128
BF16_SUBLANES = 16
MXU_COLS = 256
FFN_ROW_BLOCK = 256
NSA_TQ = 256
MOBA_HEADS_PER_STEP = 8
NEG_BIG = -1e30
VMEM_LIMIT = 56 * 1024 * 1024

QN0 = 0
QM0 = 8
KM0 = 16
VM0 = 24
KC0 = 32
VC0 = 34
KS0 = 36
VS0 = 38
KW0 = 40
VW0 = 42
GT0 = 44
N_BLOCKS = 48
INPROJ_TN = 1024
HEAD_COLS = 2560
GATE_COLS = 3 * NSA_HEADS
ROPE_Q, ROPE_K, PLAIN, GATES, ZERO = range(5)
_INPROJ_TILES = (
    ("head", 0, (ROPE_Q,) * 4),
    ("tail", 0, (ROPE_Q,) * 4),
    ("tail", 1, (ROPE_K,) * 4),
    ("tail", 2, (PLAIN,) * 4),
    ("head", 1, (ROPE_K, PLAIN, ROPE_K, PLAIN)),
    ("head", 2, (ROPE_K, PLAIN, GATES, ZERO)),
)


def _nt_dot(a, b):
    return lax.dot_general(a, b, (((1,), (1,)), ((), ())), preferred_element_type=F32)


def _rms_norm(x, g):
    ms = jnp.mean(x * x, axis=-1, keepdims=True)
    return x * lax.rsqrt(ms + EPS) * g


def _split_bf16(a):
    hi = a.astype(BF16)
    lo = (a - hi.astype(F32)).astype(BF16)
    return hi, lo


def _adaln_kernel(ct_ref, w_ref, b_ref, o_ref):
    ct = ct_ref[...]
    a = ct * (1.0 / (1.0 + jnp.exp(-ct)))
    w = w_ref[...]
    for b in range(o_ref.shape[0]):
        o_ref[b] = jnp.sum(w * a[:, b:b + 1], axis=0, keepdims=True) + b_ref[...]


def _adaln(c, w, b):
    bsz, d = c.shape
    n = w.shape[1]
    tn = 1024
    assert bsz <= LANES
    ct = jnp.zeros((d, LANES), F32).at[:, :bsz].set(c.T)
    out = pl.pallas_call(
        _adaln_kernel,
        grid=(n // tn,),
        in_specs=[pl.BlockSpec((d, LANES), lambda j: (0, 0)),
                  pl.BlockSpec((d, tn), lambda j: (0, j)),
                  pl.BlockSpec((1, tn), lambda j: (0, j))],
        out_specs=pl.BlockSpec((bsz, 1, tn), lambda j: (0, 0, j)),
        out_shape=jax.ShapeDtypeStruct((bsz, 1, n), F32),
        compiler_params=pltpu.CompilerParams(dimension_semantics=("parallel",),
                                             vmem_limit_bytes=VMEM_LIMIT),
        name="adaln",
    )(ct, w, b.reshape(1, n))
    return out.reshape(bsz, n)


def _inproj_kernel(x_ref, g_ref, sc_ref, sh_ref, wh_ref, wt_ref, cos_ref, sin_ref,
                   o_ref, kvc_ref, h_ref):
    j = pl.program_id(1)

    @pl.when(j == 0)
    def _():
        h = _rms_norm(x_ref[...], g_ref[...]) * (1.0 + sc_ref[...]) + sh_ref[...]
        h_ref[...] = h.astype(BF16)

    def tile(w_ref, kinds):
        for ci, kind in enumerate(kinds):
            c0 = ci * MXU_COLS
            if kind == ZERO:
                o_ref[:, c0:c0 + MXU_COLS] = jnp.zeros((o_ref.shape[0], MXU_COLS), BF16)
                continue
            acc = jnp.dot(h_ref[...], w_ref[:, c0:c0 + MXU_COLS], preferred_element_type=F32)
            for c in range(c0, c0 + MXU_COLS, LANES):
                ch = acc[:, c - c0:c - c0 + LANES]
                if kind in (ROPE_Q, ROPE_K):
                    ch = ch * cos_ref[...] + pltpu.roll(ch, HEAD_DIM // 2, 1) * sin_ref[...]
                if kind == ROPE_Q:
                    ch = ch * HEAD_DIM ** -0.5
                o_ref[:, c:c + LANES] = ch.astype(BF16)

    for jj, (src, _, kinds) in enumerate(_INPROJ_TILES):
        pl.when(j == jj)(functools.partial(tile, wh_ref if src == "head" else wt_ref, kinds))

    jkv, ok = divmod(KC0 * LANES, INPROJ_TN)
    ov = VC0 * LANES - jkv * INPROJ_TN
    assert 0 < ov < INPROJ_TN

    @pl.when(j == jkv)
    def _():
        for g in range(NSA_KV_GROUPS):
            kvc_ref[g] = o_ref[:, ok + g * LANES: ok + (g + 1) * LANES]
            kvc_ref[NSA_KV_GROUPS + g] = o_ref[:, ov + g * LANES: ov + (g + 1) * LANES]


def _step_lookup(j, table):
    out = jnp.int32(table[0])
    for k in range(1, len(table)):
        out = jnp.where(j >= k, table[k], out)
    return out


def _inproj(x2, g, scale, shift, w_head, w_tail, cos2, sin2, seq):
    m, d = x2.shape
    tm = min(1024, seq)
    tn = INPROJ_TN
    tiles_per_b = seq // tm
    blocks = {"head": [], "tail": []}
    for src, blk, _ in _INPROJ_TILES:
        for name, lst in blocks.items():
            lst.append(blk if src == name else (lst[-1] if lst else None))
    for lst in blocks.values():
        first = next(b for b in lst if b is not None)
        lst[:] = [first if b is None else b for b in lst]
    return pl.pallas_call(
        _inproj_kernel,
        grid=(m // tm, len(_INPROJ_TILES)),
        in_specs=[pl.BlockSpec((tm, d), lambda i, j: (i, 0)),
                  pl.BlockSpec((1, d), lambda i, j: (0, 0)),
                  pl.BlockSpec((None, 1, d), lambda i, j: (i // tiles_per_b, 0, 0)),
                  pl.BlockSpec((None, 1, d), lambda i, j: (i // tiles_per_b, 0, 0)),
                  pl.BlockSpec((d, tn), lambda i, j: (0, _step_lookup(j, blocks["head"]))),
                  pl.BlockSpec((d, tn), lambda i, j: (0, _step_lookup(j, blocks["tail"]))),
                  pl.BlockSpec((tm, LANES), lambda i, j: (i % tiles_per_b, 0)),
                  pl.BlockSpec((tm, LANES), lambda i, j: (i % tiles_per_b, 0))],
        out_specs=[pl.BlockSpec((tm, tn), lambda i, j: (i, j)),
                   pl.BlockSpec((2 * NSA_KV_GROUPS, tm, LANES), lambda i, j: (0, i, 0))],
        out_shape=[jax.ShapeDtypeStruct((m, N_BLOCKS * LANES), BF16),
                   jax.ShapeDtypeStruct((2 * NSA_KV_GROUPS, m, LANES), BF16)],
        scratch_shapes=[pltpu.VMEM((tm, d), BF16)],
        compiler_params=pltpu.CompilerParams(dimension_semantics=("parallel", "arbitrary"),
                                             vmem_limit_bytes=VMEM_LIMIT),
        name="inproj",
    )(x2, g, scale, shift, w_head, w_tail, cos2, sin2)


def _compress_kernel(x_ref, pos_ref, w1_ref, w2_ref, o_ref):
    half = CMP_STRIDE * HEAD_DIM
    x = x_ref[...].astype(F32)
    pos = pos_ref[...]
    w1 = w1_ref[...].astype(BF16)
    a = jnp.dot((x + pos[:, :half]).astype(BF16), w1[:half], preferred_element_type=F32)
    b = jnp.dot((x + pos[:, half:]).astype(BF16), w1[half:], preferred_element_type=F32)
    n_rows = x.shape[0]
    h1 = a + pltpu.roll(b, n_rows - 1, 0)
    h = h1 * (1.0 / (1.0 + jnp.exp(-h1)))
    o_ref[...] = jnp.dot(h.astype(BF16), w2_ref[...].astype(BF16),
                         preferred_element_type=F32).astype(BF16)


def _compress(kvc4, pos2, w1s, w2s):
    ns, bsz, n_rows, width = kvc4.shape
    return pl.pallas_call(
        _compress_kernel,
        grid=(ns, bsz),
        in_specs=[pl.BlockSpec((None, None, n_rows, width), lambda s, b: (s, b, 0, 0)),
                  pl.BlockSpec((None, 1, 2 * width), lambda s, b: (s // NSA_KV_GROUPS, 0, 0)),
                  pl.BlockSpec((None, 2 * width, HEAD_DIM), lambda s, b: (s // NSA_KV_GROUPS, 0, 0)),
                  pl.BlockSpec((None, HEAD_DIM, HEAD_DIM), lambda s, b: (s // NSA_KV_GROUPS, 0, 0))],
        out_specs=pl.BlockSpec((None, None, n_rows, HEAD_DIM), lambda s, b: (s, b, 0, 0)),
        out_shape=jax.ShapeDtypeStruct((ns, bsz, n_rows, HEAD_DIM), BF16),
        compiler_params=pltpu.CompilerParams(dimension_semantics=("parallel", "parallel"),
                                             vmem_limit_bytes=VMEM_LIMIT),
        name="compress",
    )(kvc4, pos2, w1s, w2s)


def _topk_rows(score, k):
    idx = lax.broadcasted_iota(I32, score.shape, 0).astype(F32)
    work = score
    sel = jnp.zeros(score.shape, F32)
    for _ in range(k):
        mx = jnp.max(work, axis=0, keepdims=True)
        first = jnp.min(jnp.where(work == mx, idx, float(score.shape[0])), axis=0, keepdims=True)
        hit = idx == first
        sel = jnp.where(hit, 1.0, sel)
        work = jnp.where(hit, -jnp.inf, work)
    return sel


def _bias_from_keep(keep_t):
    n, q = keep_t.shape
    padded = jnp.concatenate([keep_t, jnp.zeros((LANES - n, q), F32)], axis=0) if n < LANES else keep_t
    return jnp.where(padded.T > 0.5, 0.0, NEG_BIG).astype(BF16)


def _with_ones(v):
    return jnp.concatenate([v, jnp.ones(v.shape, v.dtype)], axis=1)


def _flash_step(qa, ka, va, m, acc, mask, penalty=None):
    s = _nt_dot(qa, ka)
    if mask is not None:
        s = jnp.where(mask, s, NEG_BIG)
    if penalty is not None:
        s = s + penalty
    m_new = jnp.maximum(m, jnp.max(s, axis=1, keepdims=True))
    alpha = jnp.exp(m - m_new)
    p = jnp.exp(s - m_new).astype(BF16)
    acc = alpha * acc + jnp.dot(p, va, preferred_element_type=F32)
    return m_new, acc


def _flash_out(acc):
    return acc[:, :HEAD_DIM] * (1.0 / acc[:, HEAD_DIM:])


def _nsa_kernel(q_ref, kc_ref, vc_ref, ks_ref, vs_ref, kw_ref, vw_ref, gl_ref,
                selmap_ref, et_ref, o_ref, *, tq, seq):
    qi = pl.program_id(2)
    t0 = qi * tq
    rows = NSA_HPG * tq
    n_cmp_pad = kc_ref.shape[0]
    n_sel = seq // SEL_LEN

    q = q_ref[...]
    qs = jnp.concatenate([q[:, h * LANES:(h + 1) * LANES] for h in range(NSA_HPG)], axis=0)

    s = _nt_dot(qs, kc_ref[...])
    tt = t0 + (lax.broadcasted_iota(I32, (rows, n_cmp_pad), 0) & (tq - 1))
    cend = lax.broadcasted_iota(I32, (rows, n_cmp_pad), 1) * CMP_STRIDE + (CMP_LEN - 1)
    s = jnp.where(cend <= tt, s, -jnp.inf)
    mx = jnp.max(s, axis=1, keepdims=True)
    mx = jnp.where(mx == -jnp.inf, 0.0, mx)
    e = jnp.exp(s - mx)
    den = jnp.maximum(jnp.sum(e, axis=1, keepdims=True), 1e-30)
    p_c = e * (1.0 / den)
    o_c = jnp.dot(p_c.astype(BF16), vc_ref[...], preferred_element_type=F32)

    p_sum = p_c[0:tq]
    for h in range(1, NSA_HPG):
        p_sum = p_sum + p_c[h * tq:(h + 1) * tq]
    hi, lo = _split_bf16(p_sum)
    smap_t = selmap_ref[...]
    imp_t = _nt_dot(smap_t, hi) + _nt_dot(smap_t, lo)
    blk = lax.broadcasted_iota(I32, (n_sel, tq), 0)
    cb = (t0 + lax.broadcasted_iota(I32, (n_sel, tq), 1)) // SEL_LEN
    causal = blk <= cb
    forced = (blk == 0) | (blk == cb) | (blk == cb - 1)
    n_forced = 3
    free = jnp.where(causal, jnp.where(forced, 0.0, 1.0), 0.0)
    picked = _topk_rows(jnp.where(free > 0.5, imp_t, -jnp.inf), max(min(SEL_TOPK, n_sel) - n_forced, 0))
    keep_t = jnp.where(forced, 1.0, picked * free)
    selb = _bias_from_keep(keep_t)
    qa = jnp.concatenate([qs, jnp.concatenate([selb] * NSA_HPG, axis=0)], axis=1)

    def kv_chunk(j):
        start = pl.multiple_of(j * tq, tq)
        ka = jnp.concatenate([ks_ref[pl.ds(start, tq), :], et_ref[pl.ds(start, tq), :]], axis=1)
        return ka, _with_ones(vs_ref[pl.ds(start, tq), :])

    qrow = lax.broadcasted_iota(I32, (rows, tq), 0) & (tq - 1)
    kcol = lax.broadcasted_iota(I32, (rows, tq), 1)
    m0 = jnp.full((rows, 1), NEG_BIG, F32)
    a0 = jnp.zeros((rows, 2 * HEAD_DIM), F32)
    carry = _flash_step(qa, *kv_chunk(qi), m0, a0, kcol <= qrow)
    odd = jnp.where(qi % 2 == 1, 0.0, NEG_BIG)
    carry = _flash_step(qa, *kv_chunk(jnp.maximum(qi - 1, 0)), *carry, None, odd)

    def body(i, c):
        c = _flash_step(qa, *kv_chunk(2 * i), *c, None)
        return _flash_step(qa, *kv_chunk(2 * i + 1), *c, None)

    _, a_s = lax.fori_loop(0, qi // 2, body, carry)
    o_s = _flash_out(a_s)

    n_back = WINDOW // tq
    scores, values = [], []
    for back in range(n_back + 1):
        start = pl.multiple_of(jnp.maximum(qi - back, 0) * tq, tq)
        s_b = _nt_dot(qs, kw_ref[pl.ds(start, tq), :])
        if back == 0:
            s_b = jnp.where(kcol <= qrow, s_b, NEG_BIG)
        else:
            if back == n_back:
                s_b = jnp.where(kcol > qrow, s_b, NEG_BIG)
            s_b = s_b + jnp.where(qi >= back, 0.0, NEG_BIG)
        scores.append(s_b)
        values.append(_with_ones(vw_ref[pl.ds(start, tq), :]))
    mw = scores[0].max(axis=1, keepdims=True)
    for s_b in scores[1:]:
        mw = jnp.maximum(mw, s_b.max(axis=1, keepdims=True))
    a_w = None
    for s_b, v_b in zip(scores, values):
        t = jnp.dot(jnp.exp(s_b - mw).astype(BF16), v_b, preferred_element_type=F32)
        a_w = t if a_w is None else a_w + t
    o_w = _flash_out(a_w)

    gl = gl_ref[...].astype(F32)
    sig = 1.0 / (1.0 + jnp.exp(-gl))
    for h in range(NSA_HPG):
        r = slice(h * tq, (h + 1) * tq)
        o = (sig[:, 3 * h:3 * h + 1] * o_c[r] + sig[:, 3 * h + 1:3 * h + 2] * o_s[r]
             + sig[:, 3 * h + 2:3 * h + 3] * o_w[r])
        o_ref[:, h * LANES:(h + 1) * LANES] = o.astype(BF16)


def _nsa(qkv, kcv, selmap, et_sel, bsz, seq):
    m = qkv.shape[0]
    tq = NSA_TQ
    assert seq % tq == 0 and WINDOW % tq == 0 and tq % SEL_LEN == 0
    nq = seq // tq
    n_cmp_pad = kcv.shape[2]
    gw = NSA_HPG * LANES
    col = lambda c0: (lambda b, g, i: (b, c0 + g))
    full = lambda c0: pl.BlockSpec((seq, LANES), col(c0))
    return pl.pallas_call(
        functools.partial(_nsa_kernel, tq=tq, seq=seq),
        grid=(bsz, NSA_KV_GROUPS, nq),
        in_specs=[pl.BlockSpec((tq, gw), lambda b, g, i: (b * nq + i, g)),
                  pl.BlockSpec((None, None, n_cmp_pad, LANES), lambda b, g, i: (g, b, 0, 0)),
                  pl.BlockSpec((None, None, n_cmp_pad, LANES),
                               lambda b, g, i: (NSA_KV_GROUPS + g, b, 0, 0)),
                  full(KS0), full(VS0), full(KW0), full(VW0),
                  pl.BlockSpec((tq, LANES), lambda b, g, i: (b * nq + i, GT0 + g)),
                  pl.BlockSpec(selmap.shape, lambda b, g, i: (0, 0)),
                  pl.BlockSpec(et_sel.shape, lambda b, g, i: (0, 0))],
        out_specs=pl.BlockSpec((tq, gw), lambda b, g, i: (b * nq + i, g)),
        out_shape=jax.ShapeDtypeStruct((m, NSA_HEADS * HEAD_DIM), BF16),
        compiler_params=pltpu.CompilerParams(
            dimension_semantics=("parallel", "parallel", "arbitrary"),
            vmem_limit_bytes=VMEM_LIMIT),
        name="nsa_attn",
    )(qkv, kcv, kcv, qkv, qkv, qkv, qkv, qkv, selmap, et_sel)


def _moba_kernel(q_ref, k_ref, v_ref, avg_ref, et_ref, o_ref, kmh_ref, kml_ref, *, seq, hb):
    qi = pl.program_id(2)
    blk_len = MOBA_BLOCK
    nb = seq // blk_len
    nb_rows = avg_ref.shape[0]
    head = lambda h: slice(h * LANES, (h + 1) * LANES)

    @pl.when(qi == 0)
    def _():
        for h in range(hb):
            km = jnp.dot(avg_ref[...], k_ref[:, head(h)], preferred_element_type=F32)
            kmh_ref[h], kml_ref[h] = _split_bf16(km)

    blk = lax.broadcasted_iota(I32, (nb_rows, blk_len), 0)
    past = jnp.where(blk < qi, 1.0, 0.0)
    own = jnp.where(blk == qi, 1.0, 0.0)
    qas = []
    for h in range(hb):
        q = q_ref[:, head(h)]
        gate_t = _nt_dot(kmh_ref[h], q) + _nt_dot(kml_ref[h], q)
        picked = _topk_rows(jnp.where(past > 0.5, gate_t, -jnp.inf), min(MOBA_TOPK, nb - 1))
        qas.append(jnp.concatenate([q, _bias_from_keep(picked * past + own)], axis=1))

    def kv_chunk(j, h):
        rows = pl.ds(pl.multiple_of(j * blk_len, blk_len), blk_len)
        ka = jnp.concatenate([k_ref[rows, head(h)], et_ref[rows, :]], axis=1)
        return ka, _with_ones(v_ref[rows, head(h)])

    row = lax.broadcasted_iota(I32, (blk_len, blk_len), 0)
    colk = lax.broadcasted_iota(I32, (blk_len, blk_len), 1)
    m0 = jnp.full((blk_len, 1), NEG_BIG, F32)
    a0 = jnp.zeros((blk_len, 2 * HEAD_DIM), F32)
    def all_heads(j, c, mask=None, penalty=None):
        new = []
        for h in range(hb):
            new += _flash_step(qas[h], *kv_chunk(j, h), c[2 * h], c[2 * h + 1], mask, penalty)
        return tuple(new)

    carry = all_heads(qi, (m0, a0) * hb, colk <= row)
    carry = all_heads(jnp.maximum(qi - 1, 0), carry, None, jnp.where(qi % 2 == 1, 0.0, NEG_BIG))
    final = lax.fori_loop(0, qi // 2, lambda i, c: all_heads(2 * i + 1, all_heads(2 * i, c)), carry)
    for h in range(hb):
        o_ref[:, head(h)] = _flash_out(final[2 * h + 1]).astype(BF16)


def _moba(qkv, avg, et_blk, bsz, seq):
    m = qkv.shape[0]
    nq = seq // MOBA_BLOCK
    hb = MOBA_HEADS_PER_STEP
    assert MOBA_HEADS % hb == 0 and QM0 % hb == 0 and KM0 % hb == 0 and VM0 % hb == 0
    hw = hb * LANES
    return pl.pallas_call(
        functools.partial(_moba_kernel, seq=seq, hb=hb),
        grid=(bsz, MOBA_HEADS // hb, nq),
        in_specs=[pl.BlockSpec((MOBA_BLOCK, hw), lambda b, h, i: (b * nq + i, QM0 // hb + h)),
                  pl.BlockSpec((seq, hw), lambda b, h, i: (b, KM0 // hb + h)),
                  pl.BlockSpec((seq, hw), lambda b, h, i: (b, VM0 // hb + h)),
                  pl.BlockSpec(avg.shape, lambda b, h, i: (0, 0)),
                  pl.BlockSpec(et_blk.shape, lambda b, h, i: (0, 0))],
        out_specs=pl.BlockSpec((MOBA_BLOCK, hw), lambda b, h, i: (b * nq + i, h)),
        out_shape=jax.ShapeDtypeStruct((m, MOBA_HEADS * HEAD_DIM), BF16),
        scratch_shapes=[pltpu.VMEM((hb, avg.shape[0], HEAD_DIM), BF16),
                        pltpu.VMEM((hb, avg.shape[0], HEAD_DIM), BF16)],
        compiler_params=pltpu.CompilerParams(
            dimension_semantics=("parallel", "parallel", "arbitrary"),
            vmem_limit_bytes=VMEM_LIMIT),
        name="moba_attn",
    )(qkv, qkv, qkv, avg, et_blk)


def _outproj_kernel(on_ref, om_ref, w_ref, x_ref, g_ref, gate_ref, o_ref):
    kn = on_ref.shape[1]
    acc = (jnp.dot(on_ref[...], w_ref[:kn, :], preferred_element_type=F32)
           + jnp.dot(om_ref[...], w_ref[kn:, :], preferred_element_type=F32))
    o_ref[...] = x_ref[...] + gate_ref[...] * _rms_norm(acc, g_ref[...])


def _outproj(o_n, o_m, w_o, x2, g_post, gate, seq):
    m, d = x2.shape
    tm = min(512, seq)
    tiles_per_b = seq // tm
    return pl.pallas_call(
        _outproj_kernel,
        grid=(m // tm,),
        in_specs=[pl.BlockSpec((tm, o_n.shape[1]), lambda i: (i, 0)),
                  pl.BlockSpec((tm, o_m.shape[1]), lambda i: (i, 0)),
                  pl.BlockSpec(w_o.shape, lambda i: (0, 0)),
                  pl.BlockSpec((tm, d), lambda i: (i, 0)),
                  pl.BlockSpec((1, d), lambda i: (0, 0)),
                  pl.BlockSpec((None, 1, d), lambda i: (i // tiles_per_b, 0, 0))],
        out_specs=pl.BlockSpec((tm, d), lambda i: (i, 0)),
        out_shape=jax.ShapeDtypeStruct((m, d), F32),
        compiler_params=pltpu.CompilerParams(dimension_semantics=("parallel",),
                                             vmem_limit_bytes=VMEM_LIMIT),
        name="outproj",
    )(o_n, o_m, w_o, x2, g_post, gate)


def _ffn_kernel(x_ref, gpre_ref, sc_ref, sh_ref, wu_ref, wd_ref, gpost_ref, gate_ref,
                o_ref, h_ref):
    k = pl.program_id(1)
    last = pl.num_programs(1) - 1
    row_blocks = [slice(r, r + FFN_ROW_BLOCK) for r in range(0, x_ref.shape[0], FFN_ROW_BLOCK)]

    def mlp(h):
        u = jnp.dot(h, wu_ref[...], preferred_element_type=F32)
        u = jnp.square(jnp.maximum(u, 0.0)).astype(BF16)
        return jnp.dot(u, wd_ref[...], preferred_element_type=F32)

    @pl.when(k == 0)
    def _():
        for rows in row_blocks:
            h = _rms_norm(x_ref[rows], gpre_ref[...]) * (1.0 + sc_ref[...]) + sh_ref[...]
            h_ref[rows] = h.astype(BF16)
            o_ref[rows] = mlp(h_ref[rows])

    @pl.when(jnp.logical_and(k > 0, k < last))
    def _():
        o_ref[...] += mlp(h_ref[...])

    @pl.when(k == last)
    def _():
        for rows in row_blocks:
            f = o_ref[rows] + mlp(h_ref[rows])
            o_ref[rows] = x_ref[rows] + gate_ref[...] * _rms_norm(f, gpost_ref[...])


def _ffn(x2, g_pre, scale, shift, w_up, w_down, g_post, gate, seq):
    m, d = x2.shape
    dff = w_up.shape[1]
    tm = min(1024, seq)
    ck = 512
    assert dff // ck >= 2 and tm % FFN_ROW_BLOCK == 0
    tiles_per_b = seq // tm
    vec = pl.BlockSpec((1, d), lambda i, k: (0, 0))
    per_b = pl.BlockSpec((None, 1, d), lambda i, k: (i // tiles_per_b, 0, 0))
    return pl.pallas_call(
        _ffn_kernel,
        grid=(m // tm, dff // ck),
        in_specs=[pl.BlockSpec((tm, d), lambda i, k: (i, 0)), vec, per_b, per_b,
                  pl.BlockSpec((d, ck), lambda i, k: (0, k)),
                  pl.BlockSpec((ck, d), lambda i, k: (k, 0)),
                  vec, per_b],
        out_specs=pl.BlockSpec((tm, d), lambda i, k: (i, 0)),
        out_shape=jax.ShapeDtypeStruct((m, d), F32),
        scratch_shapes=[pltpu.VMEM((tm, d), BF16)],
        compiler_params=pltpu.CompilerParams(dimension_semantics=("parallel", "arbitrary"),
                                             vmem_limit_bytes=VMEM_LIMIT),
        name="ffn",
    )(x2, g_pre, scale, shift, w_up, w_down, g_post, gate)


def _split_w_in(w_in):
    d = w_in.shape[0]
    n_head = sum(1 for src, _, _ in _INPROJ_TILES if src == "head") * INPROJ_TN
    pad = lambda n: jnp.zeros((d, n), BF16)
    per_group = 3 * NSA_HPG
    pieces = [w_in[:, :HEAD_COLS].astype(BF16)]
    for g in range(NSA_KV_GROUPS):
        lo = HEAD_COLS + g * per_group
        pieces += [w_in[:, lo:lo + per_group].astype(BF16), pad(LANES - per_group)]
    pieces.append(pad(n_head - HEAD_COLS - NSA_KV_GROUPS * LANES))
    return jnp.concatenate(pieces, axis=1), w_in[:, HEAD_COLS + GATE_COLS:].astype(BF16)


def _constants(seq):
    n_cmp_pad = seq // CMP_STRIDE
    n_cmp = (seq - CMP_LEN) // CMP_STRIDE + 1
    n_sel = seq // SEL_LEN
    cs = np.arange(n_cmp) * CMP_STRIDE
    ss = np.arange(n_sel) * SEL_LEN
    overlap = np.clip(np.minimum(cs[:, None] + CMP_LEN, ss[None, :] + SEL_LEN)
                      - np.maximum(cs[:, None], ss[None, :]), 0, None)
    selmap_t = np.zeros((n_sel, n_cmp_pad), np.float32)
    selmap_t[:, :n_cmp] = (overlap / CMP_LEN).T
    keys = np.arange(seq)
    et_sel = (keys[:, None] // SEL_LEN == np.arange(LANES)[None, :]).astype(np.float32)
    et_blk = (keys[:, None] // MOBA_BLOCK == np.arange(LANES)[None, :]).astype(np.float32)
    nb_rows = -(-(seq // MOBA_BLOCK) // BF16_SUBLANES) * BF16_SUBLANES
    avg = et_blk.T[:nb_rows] / MOBA_BLOCK
    to = lambda a: jnp.asarray(a, BF16)
    return to(selmap_t), to(et_sel), to(et_blk), to(avg)


def _rope_tables(seq):
    pos = jnp.arange(seq, dtype=F32)
    inv = ROPE_THETA ** (-jnp.arange(0, HEAD_DIM, 2, dtype=F32) / HEAD_DIM)
    ang = pos[:, None] * inv[None, :]
    cos, sin = jnp.cos(ang), jnp.sin(ang)
    return jnp.concatenate([cos, cos], axis=1), jnp.concatenate([-sin, sin], axis=1)


def kernel(x, c, w_ada, b_ada, pre_norm_mix, post_norm_mix, w_in, cmp_k_pos, cmp_k_w1, cmp_k_w2,
           cmp_v_pos, cmp_v_w1, cmp_v_w2, w_o, pre_norm_ffn, post_norm_ffn, w_up, w_down):
    bsz, seq, d = x.shape
    depth = w_ada.shape[0]
    cos2, sin2 = _rope_tables(seq)
    selmap, et_sel, et_blk, avg = _constants(seq)
    x2 = x.reshape(bsz * seq, d)
    for l in range(depth):
        mod = _adaln(c, w_ada[l], b_ada[l])
        shift_m, scale_m, gate_m, shift_f, scale_f, gate_f = [
            a.reshape(bsz, 1, d) for a in jnp.split(mod, 6, axis=-1)]
        row = lambda a: a.reshape(1, d)

        qkv, kvc = _inproj(x2, row(pre_norm_mix[l]), scale_m, shift_m, *_split_w_in(w_in[l]),
                           cos2, sin2, seq)
        kvc4 = kvc.reshape(2 * NSA_KV_GROUPS, bsz, seq // CMP_STRIDE, CMP_STRIDE * HEAD_DIM)
        pos2 = jnp.stack([cmp_k_pos[l].reshape(1, -1), cmp_v_pos[l].reshape(1, -1)])
        kcv = _compress(kvc4, pos2, jnp.stack([cmp_k_w1[l], cmp_v_w1[l]]),
                        jnp.stack([cmp_k_w2[l], cmp_v_w2[l]]))
        o_n = _nsa(qkv, kcv, selmap, et_sel, bsz, seq)
        o_m = _moba(qkv, avg, et_blk, bsz, seq)
        x2 = _outproj(o_n, o_m, w_o[l].astype(BF16), x2, row(post_norm_mix[l]), gate_m, seq)
        x2 = _ffn(x2, row(pre_norm_ffn[l]), scale_f, shift_f, w_up[l].astype(BF16),
                  w_down[l].astype(BF16), row(post_norm_ffn[l]), gate_f, seq)
    return x2.reshape(bsz, seq, d)
```

```python
import functools

import numpy as np
import jax
import jax.numpy as jnp
from jax import lax
from jax.experimental import pallas as pl
from jax.experimental.pallas import tpu as pltpu

F32 = jnp.float32
BF16 = jnp.bfloat16
I32 = jnp.int32

HEAD_DIM = 128
NSA_HEADS = 8
NSA_KV_GROUPS = 2
NSA_HPG = NSA_HEADS // NSA_KV_GROUPS
MOBA_HEADS = 8
CMP_LEN = 32
CMP_STRIDE = 16
SEL_LEN = 64
SEL_TOPK = 16
WINDOW = 512
MOBA_BLOCK = 256
MOBA_TOPK = 3
ROPE_THETA = 10000.0
EPS = 1e-6

LANES = 128
BF16_SUBLANES = 16
MXU_COLS = 256
FFN_ROW_BLOCK = 256
NSA_TQ = 256
MOBA_HEADS_PER_STEP = 8
NEG_BIG = -1e30
VMEM_LIMIT = 56 * 1024 * 1024

QN0 = 0
QM0 = 8
KM0 = 16
VM0 = 24
KC0 = 32
VC0 = 34
KS0 = 36
VS0 = 38
KW0 = 40
VW0 = 42
GT0 = 44
N_BLOCKS = 48
INPROJ_TN = 1024
HEAD_COLS = 2560
GATE_COLS = 3 * NSA_HEADS
ROPE_Q, ROPE_K, PLAIN, GATES, ZERO = range(5)
_INPROJ_TILES = (
    ("head", 0, (ROPE_Q,) * 4),
    ("tail", 0, (ROPE_Q,) * 4),
    ("tail", 1, (ROPE_K,) * 4),
    ("tail", 2, (PLAIN,) * 4),
    ("head", 1, (ROPE_K, PLAIN, ROPE_K, PLAIN)),
    ("head", 2, (ROPE_K, PLAIN, GATES, ZERO)),
)


def _nt_dot(a, b):
    return lax.dot_general(a, b, (((1,), (1,)), ((), ())), preferred_element_type=F32)


def _rms_norm(x, g):
    ms = jnp.mean(x * x, axis=-1, keepdims=True)
    return x * lax.rsqrt(ms + EPS) * g


def _split_bf16(a):
    hi = a.astype(BF16)
    lo = (a - hi.astype(F32)).astype(BF16)
    return hi, lo


def _adaln_kernel(ct_ref, w_ref, b_ref, o_ref):
    ct = ct_ref[...]
    a = ct * (1.0 / (1.0 + jnp.exp(-ct)))
    w = w_ref[...]
    for b in range(o_ref.shape[0]):
        o_ref[b] = jnp.sum(w * a[:, b:b + 1], axis=0, keepdims=True) + b_ref[...]


def _adaln(c, w, b):
    bsz, d = c.shape
    n = w.shape[1]
    tn = 1024
    assert bsz <= LANES
    ct = jnp.zeros((d, LANES), F32).at[:, :bsz].set(c.T)
    out = pl.pallas_call(
        _adaln_kernel,
        grid=(n // tn,),
        in_specs=[pl.BlockSpec((d, LANES), lambda j: (0, 0)),
                  pl.BlockSpec((d, tn), lambda j: (0, j)),
                  pl.BlockSpec((1, tn), lambda j: (0, j))],
        out_specs=pl.BlockSpec((bsz, 1, tn), lambda j: (0, 0, j)),
        out_shape=jax.ShapeDtypeStruct((bsz, 1, n), F32),
        compiler_params=pltpu.CompilerParams(dimension_semantics=("parallel",),
                                             vmem_limit_bytes=VMEM_LIMIT),
        name="adaln",
    )(ct, w, b.reshape(1, n))
    return out.reshape(bsz, n)


def _inproj_kernel(x_ref, g_ref, sc_ref, sh_ref, wh_ref, wt_ref, cos_ref, sin_ref,
                   o_ref, kvc_ref, h_ref):
    j = pl.program_id(1)

    @pl.when(j == 0)
    def _():
        h = _rms_norm(x_ref[...], g_ref[...]) * (1.0 + sc_ref[...]) + sh_ref[...]
        h_ref[...] = h.astype(BF16)

    def tile(w_ref, kinds):
        for ci, kind in enumerate(kinds):
            c0 = ci * MXU_COLS
            if kind == ZERO:
                o_ref[:, c0:c0 + MXU_COLS] = jnp.zeros((o_ref.shape[0], MXU_COLS), BF16)
                continue
            acc = jnp.dot(h_ref[...], w_ref[:, c0:c0 + MXU_COLS], preferred_element_type=F32)
            for c in range(c0, c0 + MXU_COLS, LANES):
                ch = acc[:, c - c0:c - c0 + LANES]
                if kind in (ROPE_Q, ROPE_K):
                    ch = ch * cos_ref[...] + pltpu.roll(ch, HEAD_DIM // 2, 1) * sin_ref[...]
                if kind == ROPE_Q:
                    ch = ch * HEAD_DIM ** -0.5
                o_ref[:, c:c + LANES] = ch.astype(BF16)

    for jj, (src, _, kinds) in enumerate(_INPROJ_TILES):
        pl.when(j == jj)(functools.partial(tile, wh_ref if src == "head" else wt_ref, kinds))

    jkv, ok = divmod(KC0 * LANES, INPROJ_TN)
    ov = VC0 * LANES - jkv * INPROJ_TN
    assert 0 < ov < INPROJ_TN

    @pl.when(j == jkv)
    def _():
        for g in range(NSA_KV_GROUPS):
            kvc_ref[g] = o_ref[:, ok + g * LANES: ok + (g + 1) * LANES]
            kvc_ref[NSA_KV_GROUPS + g] = o_ref[:, ov + g * LANES: ov + (g + 1) * LANES]


def _step_lookup(j, table):
    out = jnp.int32(table[0])
    for k in range(1, len(table)):
        out = jnp.where(j >= k, table[k], out)
    return out


def _inproj(x2, g, scale, shift, w_head, w_tail, cos2, sin2, seq):
    m, d = x2.shape
    tm = min(1024, seq)
    tn = INPROJ_TN
    tiles_per_b = seq // tm
    blocks = {"head": [], "tail": []}
    for src, blk, _ in _INPROJ_TILES:
        for name, lst in blocks.items():
            lst.append(blk if src == name else (lst[-1] if lst else None))
    for lst in blocks.values():
        first = next(b for b in lst if b is not None)
        lst[:] = [first if b is None else b for b in lst]
    return pl.pallas_call(
        _inproj_kernel,
        grid=(m // tm, len(_INPROJ_TILES)),
        in_specs=[pl.BlockSpec((tm, d), lambda i, j: (i, 0)),
                  pl.BlockSpec((1, d), lambda i, j: (0, 0)),
                  pl.BlockSpec((None, 1, d), lambda i, j: (i // tiles_per_b, 0, 0)),
                  pl.BlockSpec((None, 1, d), lambda i, j: (i // tiles_per_b, 0, 0)),
                  pl.BlockSpec((d, tn), lambda i, j: (0, _step_lookup(j, blocks["head"]))),
                  pl.BlockSpec((d, tn), lambda i, j: (0, _step_lookup(j, blocks["tail"]))),
                  pl.BlockSpec((tm, LANES), lambda i, j: (i % tiles_per_b, 0)),
                  pl.BlockSpec((tm, LANES), lambda i, j: (i % tiles_per_b, 0))],
        out_specs=[pl.BlockSpec((tm, tn), lambda i, j: (i, j)),
                   pl.BlockSpec((2 * NSA_KV_GROUPS, tm, LANES), lambda i, j: (0, i, 0))],
        out_shape=[jax.ShapeDtypeStruct((m, N_BLOCKS * LANES), BF16),
                   jax.ShapeDtypeStruct((2 * NSA_KV_GROUPS, m, LANES), BF16)],
        scratch_shapes=[pltpu.VMEM((tm, d), BF16)],
        compiler_params=pltpu.CompilerParams(dimension_semantics=("parallel", "arbitrary"),
                                             vmem_limit_bytes=VMEM_LIMIT),
        name="inproj",
    )(x2, g, scale, shift, w_head, w_tail, cos2, sin2)


def _compress_kernel(x_ref, pos_ref, w1_ref, w2_ref, o_ref):
    half = CMP_STRIDE * HEAD_DIM
    x = x_ref[...].astype(F32)
    pos = pos_ref[...]
    w1 = w1_ref[...].astype(BF16)
    a = jnp.dot((x + pos[:, :half]).astype(BF16), w1[:half], preferred_element_type=F32)
    b = jnp.dot((x + pos[:, half:]).astype(BF16), w1[half:], preferred_element_type=F32)
    n_rows = x.shape[0]
    h1 = a + pltpu.roll(b, n_rows - 1, 0)
    h = h1 * (1.0 / (1.0 + jnp.exp(-h1)))
    o_ref[...] = jnp.dot(h.astype(BF16), w2_ref[...].astype(BF16),
                         preferred_element_type=F32).astype(BF16)


def _compress(kvc4, pos2, w1s, w2s):
    ns, bsz, n_rows, width = kvc4.shape
    return pl.pallas_call(
        _compress_kernel,
        grid=(ns, bsz),
        in_specs=[pl.BlockSpec((None, None, n_rows, width), lambda s, b: (s, b, 0, 0)),
                  pl.BlockSpec((None, 1, 2 * width), lambda s, b: (s // NSA_KV_GROUPS, 0, 0)),
                  pl.BlockSpec((None, 2 * width, HEAD_DIM), lambda s, b: (s // NSA_KV_GROUPS, 0, 0)),
                  pl.BlockSpec((None, HEAD_DIM, HEAD_DIM), lambda s, b: (s // NSA_KV_GROUPS, 0, 0))],
        out_specs=pl.BlockSpec((None, None, n_rows, HEAD_DIM), lambda s, b: (s, b, 0, 0)),
        out_shape=jax.ShapeDtypeStruct((ns, bsz, n_rows, HEAD_DIM), BF16),
        compiler_params=pltpu.CompilerParams(dimension_semantics=("parallel", "parallel"),
                                             vmem_limit_bytes=VMEM_LIMIT),
        name="compress",
    )(kvc4, pos2, w1s, w2s)


def _topk_rows(score, k):
    idx = lax.broadcasted_iota(I32, score.shape, 0).astype(F32)
    work = score
    sel = jnp.zeros(score.shape, F32)
    for _ in range(k):
        mx = jnp.max(work, axis=0, keepdims=True)
        first = jnp.min(jnp.where(work == mx, idx, float(score.shape[0])), axis=0, keepdims=True)
        hit = idx == first
        sel = jnp.where(hit, 1.0, sel)
        work = jnp.where(hit, -jnp.inf, work)
    return sel


def _bias_from_keep(keep_t):
    n, q = keep_t.shape
    padded = jnp.concatenate([keep_t, jnp.zeros((LANES - n, q), F32)], axis=0) if n < LANES else keep_t
    return jnp.where(padded.T > 0.5, 0.0, NEG_BIG).astype(BF16)


def _with_ones(v):
    return jnp.concatenate([v, jnp.ones(v.shape, v.dtype)], axis=1)


def _flash_step(qa, ka, va, m, acc, mask, penalty=None):
    s = _nt_dot(qa, ka)
    if mask is not None:
        s = jnp.where(mask, s, NEG_BIG)
    if penalty is not None:
        s = s + penalty
    m_new = jnp.maximum(m, jnp.max(s, axis=1, keepdims=True))
    alpha = jnp.exp(m - m_new)
    p = jnp.exp(s - m_new).astype(BF16)
    acc = alpha * acc + jnp.dot(p, va, preferred_element_type=F32)
    return m_new, acc


def _flash_out(acc):
    return acc[:, :HEAD_DIM] * (1.0 / acc[:, HEAD_DIM:])


def _cast_specs(casts, n_steps, step_of):
    specs, shapes = [], []
    for w, axis in casts:
        size = w.shape[axis] // n_steps
        assert w.shape[axis] % n_steps == 0 and size % (BF16_SUBLANES if axis == 0 else LANES) == 0
        if axis == 0:
            specs.append(pl.BlockSpec((size, w.shape[1]), lambda *g: (step_of(*g), 0)))
        else:
            specs.append(pl.BlockSpec((w.shape[0], size), lambda *g: (0, step_of(*g))))
        shapes.append(jax.ShapeDtypeStruct(w.shape, BF16))
    return specs, shapes


def _cast_slabs(src_refs, dst_refs):
    for src, dst in zip(src_refs, dst_refs):
        dst[...] = src[...].astype(BF16)


def _nsa_kernel(q_ref, kc_ref, vc_ref, ks_ref, vs_ref, kw_ref, vw_ref, gl_ref,
                selmap_ref, et_ref, *rest, tq, seq, n_cast):
    cast_in, (o_ref, *cast_out) = rest[:n_cast], rest[n_cast:]
    _cast_slabs(cast_in, cast_out)
    qi = pl.program_id(2)
    t0 = qi * tq
    rows = NSA_HPG * tq
    n_cmp_pad = kc_ref.shape[0]
    n_sel = seq // SEL_LEN

    q = q_ref[...]
    qs = jnp.concatenate([q[:, h * LANES:(h + 1) * LANES] for h in range(NSA_HPG)], axis=0)

    s = _nt_dot(qs, kc_ref[...])
    tt = t0 + (lax.broadcasted_iota(I32, (rows, n_cmp_pad), 0) & (tq - 1))
    cend = lax.broadcasted_iota(I32, (rows, n_cmp_pad), 1) * CMP_STRIDE + (CMP_LEN - 1)
    s = jnp.where(cend <= tt, s, -jnp.inf)
    mx = jnp.max(s, axis=1, keepdims=True)
    mx = jnp.where(mx == -jnp.inf, 0.0, mx)
    e = jnp.exp(s - mx)
    den = jnp.maximum(jnp.sum(e, axis=1, keepdims=True), 1e-30)
    p_c = e * (1.0 / den)
    o_c = jnp.dot(p_c.astype(BF16), vc_ref[...], preferred_element_type=F32)

    p_sum = p_c[0:tq]
    for h in range(1, NSA_HPG):
        p_sum = p_sum + p_c[h * tq:(h + 1) * tq]
    hi, lo = _split_bf16(p_sum)
    smap_t = selmap_ref[...]
    imp_t = _nt_dot(smap_t, hi) + _nt_dot(smap_t, lo)
    blk = lax.broadcasted_iota(I32, (n_sel, tq), 0)
    cb = (t0 + lax.broadcasted_iota(I32, (n_sel, tq), 1)) // SEL_LEN
    causal = blk <= cb
    forced = (blk == 0) | (blk == cb) | (blk == cb - 1)
    n_forced = 3
    free = jnp.where(causal, jnp.where(forced, 0.0, 1.0), 0.0)
    picked = _topk_rows(jnp.where(free > 0.5, imp_t, -jnp.inf), max(min(SEL_TOPK, n_sel) - n_forced, 0))
    keep_t = jnp.where(forced, 1.0, picked * free)
    selb = _bias_from_keep(keep_t)
    qa = jnp.concatenate([qs, jnp.concatenate([selb] * NSA_HPG, axis=0)], axis=1)

    def kv_chunk(j):
        start = pl.multiple_of(j * tq, tq)
        ka = jnp.concatenate([ks_ref[pl.ds(start, tq), :], et_ref[pl.ds(start, tq), :]], axis=1)
        return ka, _with_ones(vs_ref[pl.ds(start, tq), :])

    qrow = lax.broadcasted_iota(I32, (rows, tq), 0) & (tq - 1)
    kcol = lax.broadcasted_iota(I32, (rows, tq), 1)
    m0 = jnp.full((rows, 1), NEG_BIG, F32)
    a0 = jnp.zeros((rows, 2 * HEAD_DIM), F32)
    carry = _flash_step(qa, *kv_chunk(qi), m0, a0, kcol <= qrow)
    odd = jnp.where(qi % 2 == 1, 0.0, NEG_BIG)
    carry = _flash_step(qa, *kv_chunk(jnp.maximum(qi - 1, 0)), *carry, None, odd)

    def body(i, c):
        c = _flash_step(qa, *kv_chunk(2 * i), *c, None)
        return _flash_step(qa, *kv_chunk(2 * i + 1), *c, None)

    _, a_s = lax.fori_loop(0, qi // 2, body, carry)
    o_s = _flash_out(a_s)

    n_back = WINDOW // tq
    scores, values = [], []
    for back in range(n_back + 1):
        start = pl.multiple_of(jnp.maximum(qi - back, 0) * tq, tq)
        s_b = _nt_dot(qs, kw_ref[pl.ds(start, tq), :])
        if back == 0:
            s_b = jnp.where(kcol <= qrow, s_b, NEG_BIG)
        else:
            if back == n_back:
                s_b = jnp.where(kcol > qrow, s_b, NEG_BIG)
            s_b = s_b + jnp.where(qi >= back, 0.0, NEG_BIG)
        scores.append(s_b)
        values.append(_with_ones(vw_ref[pl.ds(start, tq), :]))
    mw = scores[0].max(axis=1, keepdims=True)
    for s_b in scores[1:]:
        mw = jnp.maximum(mw, s_b.max(axis=1, keepdims=True))
    a_w = None
    for s_b, v_b in zip(scores, values):
        t = jnp.dot(jnp.exp(s_b - mw).astype(BF16), v_b, preferred_element_type=F32)
        a_w = t if a_w is None else a_w + t
    o_w = _flash_out(a_w)

    gl = gl_ref[...].astype(F32)
    sig = 1.0 / (1.0 + jnp.exp(-gl))
    for h in range(NSA_HPG):
        r = slice(h * tq, (h + 1) * tq)
        o = (sig[:, 3 * h:3 * h + 1] * o_c[r] + sig[:, 3 * h + 1:3 * h + 2] * o_s[r]
             + sig[:, 3 * h + 2:3 * h + 3] * o_w[r])
        o_ref[:, h * LANES:(h + 1) * LANES] = o.astype(BF16)


def _nsa(qkv, kcv, selmap, et_sel, bsz, seq, casts):
    m = qkv.shape[0]
    tq = NSA_TQ
    assert seq % tq == 0 and WINDOW % tq == 0 and tq % SEL_LEN == 0
    nq = seq // tq
    n_cmp_pad = kcv.shape[2]
    gw = NSA_HPG * LANES
    col = lambda c0: (lambda b, g, i: (b, c0 + g))
    full = lambda c0: pl.BlockSpec((seq, LANES), col(c0))
    cast_specs, cast_shapes = _cast_specs(casts, bsz * NSA_KV_GROUPS * nq,
                                          lambda b, g, i: (b * NSA_KV_GROUPS + g) * nq + i)
    outs = pl.pallas_call(
        functools.partial(_nsa_kernel, tq=tq, seq=seq, n_cast=len(casts)),
        grid=(bsz, NSA_KV_GROUPS, nq),
        in_specs=[pl.BlockSpec((tq, gw), lambda b, g, i: (b * nq + i, g)),
                  pl.BlockSpec((None, None, n_cmp_pad, LANES), lambda b, g, i: (g, b, 0, 0)),
                  pl.BlockSpec((None, None, n_cmp_pad, LANES),
                               lambda b, g, i: (NSA_KV_GROUPS + g, b, 0, 0)),
                  full(KS0), full(VS0), full(KW0), full(VW0),
                  pl.BlockSpec((tq, LANES), lambda b, g, i: (b * nq + i, GT0 + g)),
                  pl.BlockSpec(selmap.shape, lambda b, g, i: (0, 0)),
                  pl.BlockSpec(et_sel.shape, lambda b, g, i: (0, 0))] + cast_specs,
        out_specs=[pl.BlockSpec((tq, gw), lambda b, g, i: (b * nq + i, g))] + cast_specs,
        out_shape=[jax.ShapeDtypeStruct((m, NSA_HEADS * HEAD_DIM), BF16)] + cast_shapes,
        compiler_params=pltpu.CompilerParams(
            dimension_semantics=("parallel", "parallel", "arbitrary"),
            vmem_limit_bytes=VMEM_LIMIT),
        name="nsa_attn",
    )(qkv, kcv, kcv, qkv, qkv, qkv, qkv, qkv, selmap, et_sel, *[w for w, _ in casts])
    return outs[0], outs[1:]


def _moba_kernel(q_ref, k_ref, v_ref, avg_ref, et_ref, *rest, seq, hb, n_cast):
    cast_in, (o_ref, *cast_out), (kmh_ref, kml_ref) = rest[:n_cast], rest[n_cast:2 * n_cast + 1], rest[2 * n_cast + 1:]
    _cast_slabs(cast_in, cast_out)
    qi = pl.program_id(2)
    blk_len = MOBA_BLOCK
    nb = seq // blk_len
    nb_rows = avg_ref.shape[0]
    head = lambda h: slice(h * LANES, (h + 1) * LANES)

    @pl.when(qi == 0)
    def _():
        for h in range(hb):
            km = jnp.dot(avg_ref[...], k_ref[:, head(h)], preferred_element_type=F32)
            kmh_ref[h], kml_ref[h] = _split_bf16(km)

    blk = lax.broadcasted_iota(I32, (nb_rows, blk_len), 0)
    past = jnp.where(blk < qi, 1.0, 0.0)
    own = jnp.where(blk == qi, 1.0, 0.0)
    qas = []
    for h in range(hb):
        q = q_ref[:, head(h)]
        gate_t = _nt_dot(kmh_ref[h], q) + _nt_dot(kml_ref[h], q)
        picked = _topk_rows(jnp.where(past > 0.5, gate_t, -jnp.inf), min(MOBA_TOPK, nb - 1))
        qas.append(jnp.concatenate([q, _bias_from_keep(picked * past + own)], axis=1))

    def kv_chunk(j, h):
        rows = pl.ds(pl.multiple_of(j * blk_len, blk_len), blk_len)
        ka = jnp.concatenate([k_ref[rows, head(h)], et_ref[rows, :]], axis=1)
        return ka, _with_ones(v_ref[rows, head(h)])

    row = lax.broadcasted_iota(I32, (blk_len, blk_len), 0)
    colk = lax.broadcasted_iota(I32, (blk_len, blk_len), 1)
    m0 = jnp.full((blk_len, 1), NEG_BIG, F32)
    a0 = jnp.zeros((blk_len, 2 * HEAD_DIM), F32)
    def all_heads(j, c, mask=None, penalty=None):
        new = []
        for h in range(hb):
            new += _flash_step(qas[h], *kv_chunk(j, h), c[2 * h], c[2 * h + 1], mask, penalty)
        return tuple(new)

    carry = all_heads(qi, (m0, a0) * hb, colk <= row)
    carry = all_heads(jnp.maximum(qi - 1, 0), carry, None, jnp.where(qi % 2 == 1, 0.0, NEG_BIG))
    final = lax.fori_loop(0, qi // 2, lambda i, c: all_heads(2 * i + 1, all_heads(2 * i, c)), carry)
    for h in range(hb):
        o_ref[:, head(h)] = _flash_out(final[2 * h + 1]).astype(BF16)


def _moba(qkv, avg, et_blk, bsz, seq, casts):
    m = qkv.shape[0]
    nq = seq // MOBA_BLOCK
    hb = MOBA_HEADS_PER_STEP
    assert MOBA_HEADS % hb == 0 and QM0 % hb == 0 and KM0 % hb == 0 and VM0 % hb == 0
    hw = hb * LANES
    ng = MOBA_HEADS // hb
    cast_specs, cast_shapes = _cast_specs(casts, bsz * ng * nq, lambda b, h, i: (b * ng + h) * nq + i)
    whole = lambda c0: pl.BlockSpec((seq, hw), lambda b, h, i: (b, c0 // hb + h),
                                    pipeline_mode=pl.Buffered(1))
    outs = pl.pallas_call(
        functools.partial(_moba_kernel, seq=seq, hb=hb, n_cast=len(casts)),
        grid=(bsz, ng, nq),
        in_specs=[pl.BlockSpec((MOBA_BLOCK, hw), lambda b, h, i: (b * nq + i, QM0 // hb + h)),
                  whole(KM0), whole(VM0),
                  pl.BlockSpec(avg.shape, lambda b, h, i: (0, 0)),
                  pl.BlockSpec(et_blk.shape, lambda b, h, i: (0, 0))] + cast_specs,
        out_specs=[pl.BlockSpec((MOBA_BLOCK, hw), lambda b, h, i: (b * nq + i, h))] + cast_specs,
        out_shape=[jax.ShapeDtypeStruct((m, MOBA_HEADS * HEAD_DIM), BF16)] + cast_shapes,
        scratch_shapes=[pltpu.VMEM((hb, avg.shape[0], HEAD_DIM), BF16),
                        pltpu.VMEM((hb, avg.shape[0], HEAD_DIM), BF16)],
        compiler_params=pltpu.CompilerParams(
            dimension_semantics=("parallel", "parallel", "arbitrary"),
            vmem_limit_bytes=VMEM_LIMIT),
        name="moba_attn",
    )(qkv, qkv, qkv, avg, et_blk, *[w for w, _ in casts])
    return outs[0], outs[1:]


def _outproj_kernel(on_ref, om_ref, w_ref, x_ref, g_ref, gate_ref, o_ref):
    kn = on_ref.shape[1]
    acc = (jnp.dot(on_ref[...], w_ref[:kn, :], preferred_element_type=F32)
           + jnp.dot(om_ref[...], w_ref[kn:, :], preferred_element_type=F32))
    o_ref[...] = x_ref[...] + gate_ref[...] * _rms_norm(acc, g_ref[...])


def _outproj(o_n, o_m, w_o, x2, g_post, gate, seq):
    m, d = x2.shape
    tm = min(512, seq)
    tiles_per_b = seq // tm
    return pl.pallas_call(
        _outproj_kernel,
        grid=(m // tm,),
        in_specs=[pl.BlockSpec((tm, o_n.shape[1]), lambda i: (i, 0)),
                  pl.BlockSpec((tm, o_m.shape[1]), lambda i: (i, 0)),
                  pl.BlockSpec(w_o.shape, lambda i: (0, 0)),
                  pl.BlockSpec((tm, d), lambda i: (i, 0)),
                  pl.BlockSpec((1, d), lambda i: (0, 0)),
                  pl.BlockSpec((None, 1, d), lambda i: (i // tiles_per_b, 0, 0))],
        out_specs=pl.BlockSpec((tm, d), lambda i: (i, 0)),
        out_shape=jax.ShapeDtypeStruct((m, d), F32),
        compiler_params=pltpu.CompilerParams(dimension_semantics=("parallel",),
                                             vmem_limit_bytes=VMEM_LIMIT),
        name="outproj",
    )(o_n, o_m, w_o, x2, g_post, gate)


def _ffn_kernel(x_ref, gpre_ref, sc_ref, sh_ref, wu_ref, wd_ref, gpost_ref, gate_ref,
                o_ref, h_ref):
    k = pl.program_id(1)
    last = pl.num_programs(1) - 1
    row_blocks = [slice(r, r + FFN_ROW_BLOCK) for r in range(0, x_ref.shape[0], FFN_ROW_BLOCK)]

    def mlp(h):
        u = jnp.dot(h, wu_ref[...], preferred_element_type=F32)
        u = jnp.square(jnp.maximum(u, 0.0)).astype(BF16)
        return jnp.dot(u, wd_ref[...], preferred_element_type=F32)

    @pl.when(k == 0)
    def _():
        for rows in row_blocks:
            h = _rms_norm(x_ref[rows], gpre_ref[...]) * (1.0 + sc_ref[...]) + sh_ref[...]
            h_ref[rows] = h.astype(BF16)
            o_ref[rows] = mlp(h_ref[rows])

    @pl.when(jnp.logical_and(k > 0, k < last))
    def _():
        o_ref[...] += mlp(h_ref[...])

    @pl.when(k == last)
    def _():
        for rows in row_blocks:
            f = o_ref[rows] + mlp(h_ref[rows])
            o_ref[rows] = x_ref[rows] + gate_ref[...] * _rms_norm(f, gpost_ref[...])


def _ffn(x2, g_pre, scale, shift, w_up, w_down, g_post, gate, seq):
    m, d = x2.shape
    dff = w_up.shape[1]
    tm = min(1024, seq)
    ck = 512
    assert dff // ck >= 2 and tm % FFN_ROW_BLOCK == 0
    tiles_per_b = seq // tm
    vec = pl.BlockSpec((1, d), lambda i, k: (0, 0))
    per_b = pl.BlockSpec((None, 1, d), lambda i, k: (i // tiles_per_b, 0, 0))
    return pl.pallas_call(
        _ffn_kernel,
        grid=(m // tm, dff // ck),
        in_specs=[pl.BlockSpec((tm, d), lambda i, k: (i, 0)), vec, per_b, per_b,
                  pl.BlockSpec((d, ck), lambda i, k: (0, k)),
                  pl.BlockSpec((ck, d), lambda i, k: (k, 0)),
                  vec, per_b],
        out_specs=pl.BlockSpec((tm, d), lambda i, k: (i, 0)),
        out_shape=jax.ShapeDtypeStruct((m, d), F32),
        scratch_shapes=[pltpu.VMEM((tm, d), BF16)],
        compiler_params=pltpu.CompilerParams(dimension_semantics=("parallel", "arbitrary"),
                                             vmem_limit_bytes=VMEM_LIMIT),
        name="ffn",
    )(x2, g_pre, scale, shift, w_up, w_down, g_post, gate)


def _split_w_in(w_in):
    d = w_in.shape[0]
    n_head = sum(1 for src, _, _ in _INPROJ_TILES if src == "head") * INPROJ_TN
    pad = lambda n: jnp.zeros((d, n), BF16)
    per_group = 3 * NSA_HPG
    pieces = [w_in[:, :HEAD_COLS].astype(BF16)]
    for g in range(NSA_KV_GROUPS):
        lo = HEAD_COLS + g * per_group
        pieces += [w_in[:, lo:lo + per_group].astype(BF16), pad(LANES - per_group)]
    pieces.append(pad(n_head - HEAD_COLS - NSA_KV_GROUPS * LANES))
    return jnp.concatenate(pieces, axis=1), w_in[:, HEAD_COLS + GATE_COLS:].astype(BF16)


def _constants(seq):
    n_cmp_pad = seq // CMP_STRIDE
    n_cmp = (seq - CMP_LEN) // CMP_STRIDE + 1
    n_sel = seq // SEL_LEN
    cs = np.arange(n_cmp) * CMP_STRIDE
    ss = np.arange(n_sel) * SEL_LEN
    overlap = np.clip(np.minimum(cs[:, None] + CMP_LEN, ss[None, :] + SEL_LEN)
                      - np.maximum(cs[:, None], ss[None, :]), 0, None)
    selmap_t = np.zeros((n_sel, n_cmp_pad), np.float32)
    selmap_t[:, :n_cmp] = (overlap / CMP_LEN).T
    keys = np.arange(seq)
    et_sel = (keys[:, None] // SEL_LEN == np.arange(LANES)[None, :]).astype(np.float32)
    et_blk = (keys[:, None] // MOBA_BLOCK == np.arange(LANES)[None, :]).astype(np.float32)
    nb_rows = -(-(seq // MOBA_BLOCK) // BF16_SUBLANES) * BF16_SUBLANES
    avg = et_blk.T[:nb_rows] / MOBA_BLOCK
    to = lambda a: jnp.asarray(a, BF16)
    return to(selmap_t), to(et_sel), to(et_blk), to(avg)


def _rope_tables(seq):
    pos = jnp.arange(seq, dtype=F32)
    inv = ROPE_THETA ** (-jnp.arange(0, HEAD_DIM, 2, dtype=F32) / HEAD_DIM)
    ang = pos[:, None] * inv[None, :]
    cos, sin = jnp.cos(ang), jnp.sin(ang)
    return jnp.concatenate([cos, cos], axis=1), jnp.concatenate([-sin, sin], axis=1)


def kernel(x, c, w_ada, b_ada, pre_norm_mix, post_norm_mix, w_in, cmp_k_pos, cmp_k_w1, cmp_k_w2,
           cmp_v_pos, cmp_v_w1, cmp_v_w2, w_o, pre_norm_ffn, post_norm_ffn, w_up, w_down):
    bsz, seq, d = x.shape
    depth = w_ada.shape[0]
    cos2, sin2 = _rope_tables(seq)
    selmap, et_sel, et_blk, avg = _constants(seq)
    x2 = x.reshape(bsz * seq, d)
    for l in range(depth):
        mod = _adaln(c, w_ada[l], b_ada[l])
        shift_m, scale_m, gate_m, shift_f, scale_f, gate_f = [
            a.reshape(bsz, 1, d) for a in jnp.split(mod, 6, axis=-1)]
        row = lambda a: a.reshape(1, d)

        qkv, kvc = _inproj(x2, row(pre_norm_mix[l]), scale_m, shift_m, *_split_w_in(w_in[l]),
                           cos2, sin2, seq)
        kvc4 = kvc.reshape(2 * NSA_KV_GROUPS, bsz, seq // CMP_STRIDE, CMP_STRIDE * HEAD_DIM)
        pos2 = jnp.stack([cmp_k_pos[l].reshape(1, -1), cmp_v_pos[l].reshape(1, -1)])
        kcv = _compress(kvc4, pos2, jnp.stack([cmp_k_w1[l], cmp_v_w1[l]]),
                        jnp.stack([cmp_k_w2[l], cmp_v_w2[l]]))
        o_n, (w_o_bf,) = _nsa(qkv, kcv, selmap, et_sel, bsz, seq, [(w_o[l], 0)])
        o_m, (w_up_bf, w_down_bf) = _moba(qkv, avg, et_blk, bsz, seq, [(w_up[l], 1), (w_down[l], 0)])
        x2 = _outproj(o_n, o_m, w_o_bf, x2, row(post_norm_mix[l]), gate_m, seq)
        x2 = _ffn(x2, row(pre_norm_ffn[l]), scale_f, shift_f, w_up_bf, w_down_bf,
                  row(post_norm_ffn[l]), gate_f, seq)
    return x2.reshape(bsz, seq, d)
```

```python
import functools

import numpy as np
import jax
import jax.numpy as jnp
from jax import lax
from jax.experimental import pallas as pl
from jax.experimental.pallas import tpu as pltpu

F32 = jnp.float32
BF16 = jnp.bfloat16
I32 = jnp.int32

HEAD_DIM = 128
NSA_HEADS = 8
NSA_KV_GROUPS = 2
NSA_HPG = NSA_HEADS // NSA_KV_GROUPS
MOBA_HEADS = 8
CMP_LEN = 32
CMP_STRIDE = 16
SEL_LEN = 64
SEL_TOPK = 16
WINDOW = 512
MOBA_BLOCK = 256
MOBA_TOPK = 3
ROPE_THETA = 10000.0
EPS = 1e-6

LANES = 128
BF16_SUBLANES = 16
MXU_COLS = 256
FFN_ROW_BLOCK = 256
NSA_TQ = 256
MOBA_HEADS_PER_STEP = 8
NEG_BIG = -1e30
VMEM_LIMIT = 56 * 1024 * 1024

QN0 = 0
QM0 = 8
KM0 = 16
VM0 = 24
KC0 = 32
VC0 = 34
KS0 = 36
VS0 = 38
KW0 = 40
VW0 = 42
GT0 = 44
N_BLOCKS = 48
INPROJ_TN = 1024
HEAD_COLS = 2560
GATE_COLS = 3 * NSA_HEADS
IN_SLAB = 256
ROPE_Q, ROPE_K, PLAIN, GATES, ZERO = range(5)
_INPROJ_TILES = (
    ("head", 0, (ROPE_Q,) * 4),
    ("tail", 0, (ROPE_Q,) * 4),
    ("tail", 1, (ROPE_K,) * 4),
    ("tail", 2, (PLAIN,) * 4),
    ("head", 1, (ROPE_K, PLAIN, ROPE_K, PLAIN)),
    ("head", 2, (ROPE_K, PLAIN, GATES, ZERO)),
)


def _nt_dot(a, b):
    return lax.dot_general(a, b, (((1,), (1,)), ((), ())), preferred_element_type=F32)


def _rms_norm(x, g):
    ms = jnp.mean(x * x, axis=-1, keepdims=True)
    return x * lax.rsqrt(ms + EPS) * g


def _split_bf16(a):
    hi = a.astype(BF16)
    lo = (a - hi.astype(F32)).astype(BF16)
    return hi, lo


def _adaln_kernel(ct_ref, w_ref, b_ref, wh_ref, wa_ref, wb_ref, o_ref, head_ref, tail_ref):
    ct = ct_ref[...]
    a = ct * (1.0 / (1.0 + jnp.exp(-ct)))
    w = w_ref[...]
    for b in range(o_ref.shape[0]):
        o_ref[b] = jnp.sum(w * a[:, b:b + 1], axis=0, keepdims=True) + b_ref[...]

    t = pl.program_id(0)
    lane = lax.broadcasted_iota(I32, (1, LANES), 1)
    shifted = [pltpu.roll(col, LANES - GATE_COLS, 1)
               for col in (wa_ref[:, :LANES], wa_ref[:, LANES:], wb_ref[:, :LANES])]
    for k in range(IN_SLAB // LANES):
        tail_ref[:, k * LANES:(k + 1) * LANES] = jnp.where(
            lane < LANES - GATE_COLS, shifted[k], shifted[k + 1]).astype(BF16)

    n_plain = HEAD_COLS // IN_SLAB

    @pl.when(t < n_plain)
    def _():
        head_ref[...] = wh_ref[...].astype(BF16)

    @pl.when(t == n_plain)
    def _():
        per_group = 3 * NSA_HPG
        g = wh_ref[:, :LANES]
        for grp in range(NSA_KV_GROUPS):
            col = g if grp == 0 else pltpu.roll(g, LANES - grp * per_group, 1)
            head_ref[:, grp * LANES:(grp + 1) * LANES] = jnp.where(lane < per_group, col, 0.0).astype(BF16)

    @pl.when(t > n_plain)
    def _():
        head_ref[...] = jnp.zeros(head_ref.shape, BF16)


def _adaln(c, w, b, w_in):
    bsz, d = c.shape
    n = w.shape[1]
    tn = 1024
    steps = n // tn
    n_src = sum(1 for src, _, _ in _INPROJ_TILES if src == "head") * INPROJ_TN
    n_plain = HEAD_COLS // IN_SLAB
    assert bsz <= LANES and steps * IN_SLAB == n_src and HEAD_COLS % IN_SLAB == 0 and n_plain < steps
    assert IN_SLAB == NSA_KV_GROUPS * LANES and w_in.shape[1] == HEAD_COLS + GATE_COLS + n_src
    ct = jnp.zeros((d, LANES), F32).at[:, :bsz].set(c.T)
    slab = lambda f: pl.BlockSpec((d, IN_SLAB), lambda j: (0, f(j)))
    out, w_head, w_tail = pl.pallas_call(
        _adaln_kernel,
        grid=(steps,),
        in_specs=[pl.BlockSpec((d, LANES), lambda j: (0, 0)),
                  pl.BlockSpec((d, tn), lambda j: (0, j)),
                  pl.BlockSpec((1, tn), lambda j: (0, j)),
                  slab(lambda j: jnp.minimum(j, n_plain)),
                  slab(lambda j: n_plain + j),
                  slab(lambda j: n_plain + j + 1)],
        out_specs=[pl.BlockSpec((bsz, 1, tn), lambda j: (0, 0, j)), slab(lambda j: j), slab(lambda j: j)],
        out_shape=[jax.ShapeDtypeStruct((bsz, 1, n), F32),
                   jax.ShapeDtypeStruct((d, n_src), BF16), jax.ShapeDtypeStruct((d, n_src), BF16)],
        compiler_params=pltpu.CompilerParams(dimension_semantics=("parallel",),
                                             vmem_limit_bytes=VMEM_LIMIT),
        name="adaln",
    )(ct, w, b.reshape(1, n), w_in, w_in, w_in)
    return out.reshape(bsz, n), w_head, w_tail


def _inproj_kernel(x_ref, g_ref, sc_ref, sh_ref, wh_ref, wt_ref, cos_ref, sin_ref,
                   o_ref, kvc_ref, h_ref):
    j = pl.program_id(1)

    @pl.when(j == 0)
    def _():
        h = _rms_norm(x_ref[...], g_ref[...]) * (1.0 + sc_ref[...]) + sh_ref[...]
        h_ref[...] = h.astype(BF16)

    def tile(w_ref, kinds):
        for ci, kind in enumerate(kinds):
            c0 = ci * MXU_COLS
            if kind == ZERO:
                o_ref[:, c0:c0 + MXU_COLS] = jnp.zeros((o_ref.shape[0], MXU_COLS), BF16)
                continue
            acc = jnp.dot(h_ref[...], w_ref[:, c0:c0 + MXU_COLS], preferred_element_type=F32)
            for c in range(c0, c0 + MXU_COLS, LANES):
                ch = acc[:, c - c0:c - c0 + LANES]
                if kind in (ROPE_Q, ROPE_K):
                    ch = ch * cos_ref[...] + pltpu.roll(ch, HEAD_DIM // 2, 1) * sin_ref[...]
                if kind == ROPE_Q:
                    ch = ch * HEAD_DIM ** -0.5
                o_ref[:, c:c + LANES] = ch.astype(BF16)

    for jj, (src, _, kinds) in enumerate(_INPROJ_TILES):
        pl.when(j == jj)(functools.partial(tile, wh_ref if src == "head" else wt_ref, kinds))

    jkv, ok = divmod(KC0 * LANES, INPROJ_TN)
    ov = VC0 * LANES - jkv * INPROJ_TN
    assert 0 < ov < INPROJ_TN

    @pl.when(j == jkv)
    def _():
        for g in range(NSA_KV_GROUPS):
            kvc_ref[g] = o_ref[:, ok + g * LANES: ok + (g + 1) * LANES]
            kvc_ref[NSA_KV_GROUPS + g] = o_ref[:, ov + g * LANES: ov + (g + 1) * LANES]


def _step_lookup(j, table):
    out = jnp.int32(table[0])
    for k in range(1, len(table)):
        out = jnp.where(j >= k, table[k], out)
    return out


def _inproj(x2, g, scale, shift, w_head, w_tail, cos2, sin2, seq):
    m, d = x2.shape
    tm = min(1024, seq)
    tn = INPROJ_TN
    tiles_per_b = seq // tm
    blocks = {"head": [], "tail": []}
    for src, blk, _ in _INPROJ_TILES:
        for name, lst in blocks.items():
            lst.append(blk if src == name else (lst[-1] if lst else None))
    for lst in blocks.values():
        first = next(b for b in lst if b is not None)
        lst[:] = [first if b is None else b for b in lst]
    return pl.pallas_call(
        _inproj_kernel,
        grid=(m // tm, len(_INPROJ_TILES)),
        in_specs=[pl.BlockSpec((tm, d), lambda i, j: (i, 0)),
                  pl.BlockSpec((1, d), lambda i, j: (0, 0)),
                  pl.BlockSpec((None, 1, d), lambda i, j: (i // tiles_per_b, 0, 0)),
                  pl.BlockSpec((None, 1, d), lambda i, j: (i // tiles_per_b, 0, 0)),
                  pl.BlockSpec((d, tn), lambda i, j: (0, _step_lookup(j, blocks["head"]))),
                  pl.BlockSpec((d, tn), lambda i, j: (0, _step_lookup(j, blocks["tail"]))),
                  pl.BlockSpec((tm, LANES), lambda i, j: (i % tiles_per_b, 0)),
                  pl.BlockSpec((tm, LANES), lambda i, j: (i % tiles_per_b, 0))],
        out_specs=[pl.BlockSpec((tm, tn), lambda i, j: (i, j)),
                   pl.BlockSpec((2 * NSA_KV_GROUPS, tm, LANES), lambda i, j: (0, i, 0))],
        out_shape=[jax.ShapeDtypeStruct((m, N_BLOCKS * LANES), BF16),
                   jax.ShapeDtypeStruct((2 * NSA_KV_GROUPS, m, LANES), BF16)],
        scratch_shapes=[pltpu.VMEM((tm, d), BF16)],
        compiler_params=pltpu.CompilerParams(dimension_semantics=("parallel", "arbitrary"),
                                             vmem_limit_bytes=VMEM_LIMIT),
        name="inproj",
    )(x2, g, scale, shift, w_head, w_tail, cos2, sin2)


def _compress_kernel(x_ref, pos_ref, w1_ref, w2_ref, o_ref):
    half = CMP_STRIDE * HEAD_DIM
    x = x_ref[...].astype(F32)
    pos = pos_ref[...]
    w1 = w1_ref[...].astype(BF16)
    a = jnp.dot((x + pos[:, :half]).astype(BF16), w1[:half], preferred_element_type=F32)
    b = jnp.dot((x + pos[:, half:]).astype(BF16), w1[half:], preferred_element_type=F32)
    n_rows = x.shape[0]
    h1 = a + pltpu.roll(b, n_rows - 1, 0)
    h = h1 * (1.0 / (1.0 + jnp.exp(-h1)))
    o_ref[...] = jnp.dot(h.astype(BF16), w2_ref[...].astype(BF16),
                         preferred_element_type=F32).astype(BF16)


def _compress(kvc4, pos2, w1s, w2s):
    ns, bsz, n_rows, width = kvc4.shape
    return pl.pallas_call(
        _compress_kernel,
        grid=(ns, bsz),
        in_specs=[pl.BlockSpec((None, None, n_rows, width), lambda s, b: (s, b, 0, 0)),
                  pl.BlockSpec((None, 1, 2 * width), lambda s, b: (s // NSA_KV_GROUPS, 0, 0)),
                  pl.BlockSpec((None, 2 * width, HEAD_DIM), lambda s, b: (s // NSA_KV_GROUPS, 0, 0)),
                  pl.BlockSpec((None, HEAD_DIM, HEAD_DIM), lambda s, b: (s // NSA_KV_GROUPS, 0, 0))],
        out_specs=pl.BlockSpec((None, None, n_rows, HEAD_DIM), lambda s, b: (s, b, 0, 0)),
        out_shape=jax.ShapeDtypeStruct((ns, bsz, n_rows, HEAD_DIM), BF16),
        compiler_params=pltpu.CompilerParams(dimension_semantics=("parallel", "parallel"),
                                             vmem_limit_bytes=VMEM_LIMIT),
        name="compress",
    )(kvc4, pos2, w1s, w2s)


def _topk_rows(score, k):
    idx = lax.broadcasted_iota(I32, score.shape, 0).astype(F32)
    work = score
    sel = jnp.zeros(score.shape, F32)
    for _ in range(k):
        mx = jnp.max(work, axis=0, keepdims=True)
        first = jnp.min(jnp.where(work == mx, idx, float(score.shape[0])), axis=0, keepdims=True)
        hit = idx == first
        sel = jnp.where(hit, 1.0, sel)
        work = jnp.where(hit, -jnp.inf, work)
    return sel


def _bias_from_keep(keep_t):
    n, q = keep_t.shape
    padded = jnp.concatenate([keep_t, jnp.zeros((LANES - n, q), F32)], axis=0) if n < LANES else keep_t
    return jnp.where(padded.T > 0.5, 0.0, NEG_BIG).astype(BF16)


def _with_ones(v):
    return jnp.concatenate([v, jnp.ones(v.shape, v.dtype)], axis=1)


def _flash_step(qa, ka, va, m, acc, mask, penalty=None):
    s = _nt_dot(qa, ka)
    if mask is not None:
        s = jnp.where(mask, s, NEG_BIG)
    if penalty is not None:
        s = s + penalty
    m_new = jnp.maximum(m, jnp.max(s, axis=1, keepdims=True))
    alpha = jnp.exp(m - m_new)
    p = jnp.exp(s - m_new).astype(BF16)
    acc = alpha * acc + jnp.dot(p, va, preferred_element_type=F32)
    return m_new, acc


def _flash_out(acc):
    return acc[:, :HEAD_DIM] * (1.0 / acc[:, HEAD_DIM:])


def _cast_specs(casts, n_steps, step_of):
    specs, shapes = [], []
    for w, axis in casts:
        size = w.shape[axis] // n_steps
        assert w.shape[axis] % n_steps == 0 and size % (BF16_SUBLANES if axis == 0 else LANES) == 0
        if axis == 0:
            specs.append(pl.BlockSpec((size, w.shape[1]), lambda *g: (step_of(*g), 0)))
        else:
            specs.append(pl.BlockSpec((w.shape[0], size), lambda *g: (0, step_of(*g))))
        shapes.append(jax.ShapeDtypeStruct(w.shape, BF16))
    return specs, shapes


def _cast_slabs(src_refs, dst_refs):
    for src, dst in zip(src_refs, dst_refs):
        dst[...] = src[...].astype(BF16)


def _nsa_kernel(q_ref, kc_ref, vc_ref, ks_ref, vs_ref, kw_ref, vw_ref, gl_ref,
                selmap_ref, et_ref, *rest, tq, seq, n_cast):
    cast_in, (o_ref, *cast_out) = rest[:n_cast], rest[n_cast:]
    _cast_slabs(cast_in, cast_out)
    qi = pl.program_id(2)
    t0 = qi * tq
    rows = NSA_HPG * tq
    n_cmp_pad = kc_ref.shape[0]
    n_sel = seq // SEL_LEN

    q = q_ref[...]
    qs = jnp.concatenate([q[:, h * LANES:(h + 1) * LANES] for h in range(NSA_HPG)], axis=0)

    s = _nt_dot(qs, kc_ref[...])
    tt = t0 + (lax.broadcasted_iota(I32, (rows, n_cmp_pad), 0) & (tq - 1))
    cend = lax.broadcasted_iota(I32, (rows, n_cmp_pad), 1) * CMP_STRIDE + (CMP_LEN - 1)
    s = jnp.where(cend <= tt, s, -jnp.inf)
    mx = jnp.max(s, axis=1, keepdims=True)
    mx = jnp.where(mx == -jnp.inf, 0.0, mx)
    e = jnp.exp(s - mx)
    den = jnp.maximum(jnp.sum(e, axis=1, keepdims=True), 1e-30)
    p_c = e * (1.0 / den)
    o_c = jnp.dot(p_c.astype(BF16), vc_ref[...], preferred_element_type=F32)

    p_sum = p_c[0:tq]
    for h in range(1, NSA_HPG):
        p_sum = p_sum + p_c[h * tq:(h + 1) * tq]
    hi, lo = _split_bf16(p_sum)
    smap_t = selmap_ref[...]
    imp_t = _nt_dot(smap_t, hi) + _nt_dot(smap_t, lo)
    blk = lax.broadcasted_iota(I32, (n_sel, tq), 0)
    cb = (t0 + lax.broadcasted_iota(I32, (n_sel, tq), 1)) // SEL_LEN
    causal = blk <= cb
    forced = (blk == 0) | (blk == cb) | (blk == cb - 1)
    n_forced = 3
    free = jnp.where(causal, jnp.where(forced, 0.0, 1.0), 0.0)
    picked = _topk_rows(jnp.where(free > 0.5, imp_t, -jnp.inf), max(min(SEL_TOPK, n_sel) - n_forced, 0))
    keep_t = jnp.where(forced, 1.0, picked * free)
    selb = _bias_from_keep(keep_t)
    qa = jnp.concatenate([qs, jnp.concatenate([selb] * NSA_HPG, axis=0)], axis=1)

    def kv_chunk(j):
        start = pl.multiple_of(j * tq, tq)
        ka = jnp.concatenate([ks_ref[pl.ds(start, tq), :], et_ref[pl.ds(start, tq), :]], axis=1)
        return ka, _with_ones(vs_ref[pl.ds(start, tq), :])

    qrow = lax.broadcasted_iota(I32, (rows, tq), 0) & (tq - 1)
    kcol = lax.broadcasted_iota(I32, (rows, tq), 1)
    m0 = jnp.full((rows, 1), NEG_BIG, F32)
    a0 = jnp.zeros((rows, 2 * HEAD_DIM), F32)
    carry = _flash_step(qa, *kv_chunk(qi), m0, a0, kcol <= qrow)
    odd = jnp.where(qi % 2 == 1, 0.0, NEG_BIG)
    carry = _flash_step(qa, *kv_chunk(jnp.maximum(qi - 1, 0)), *carry, None, odd)

    def body(i, c):
        c = _flash_step(qa, *kv_chunk(2 * i), *c, None)
        return _flash_step(qa, *kv_chunk(2 * i + 1), *c, None)

    _, a_s = lax.fori_loop(0, qi // 2, body, carry)
    o_s = _flash_out(a_s)

    n_back = WINDOW // tq
    scores, values = [], []
    for back in range(n_back + 1):
        start = pl.multiple_of(jnp.maximum(qi - back, 0) * tq, tq)
        s_b = _nt_dot(qs, kw_ref[pl.ds(start, tq), :])
        if back == 0:
            s_b = jnp.where(kcol <= qrow, s_b, NEG_BIG)
        else:
            if back == n_back:
                s_b = jnp.where(kcol > qrow, s_b, NEG_BIG)
            s_b = s_b + jnp.where(qi >= back, 0.0, NEG_BIG)
        scores.append(s_b)
        values.append(_with_ones(vw_ref[pl.ds(start, tq), :]))
    mw = scores[0].max(axis=1, keepdims=True)
    for s_b in scores[1:]:
        mw = jnp.maximum(mw, s_b.max(axis=1, keepdims=True))
    a_w = None
    for s_b, v_b in zip(scores, values):
        t = jnp.dot(jnp.exp(s_b - mw).astype(BF16), v_b, preferred_element_type=F32)
        a_w = t if a_w is None else a_w + t
    o_w = _flash_out(a_w)

    gl = gl_ref[...].astype(F32)
    sig = 1.0 / (1.0 + jnp.exp(-gl))
    for h in range(NSA_HPG):
        r = slice(h * tq, (h + 1) * tq)
        o = (sig[:, 3 * h:3 * h + 1] * o_c[r] + sig[:, 3 * h + 1:3 * h + 2] * o_s[r]
             + sig[:, 3 * h + 2:3 * h + 3] * o_w[r])
        o_ref[:, h * LANES:(h + 1) * LANES] = o.astype(BF16)


def _nsa(qkv, kcv, selmap, et_sel, bsz, seq, casts):
    m = qkv.shape[0]
    tq = NSA_TQ
    assert seq % tq == 0 and WINDOW % tq == 0 and tq % SEL_LEN == 0
    nq = seq // tq
    n_cmp_pad = kcv.shape[2]
    gw = NSA_HPG * LANES
    col = lambda c0: (lambda b, g, i: (b, c0 + g))
    full = lambda c0: pl.BlockSpec((seq, LANES), col(c0))
    cast_specs, cast_shapes = _cast_specs(casts, bsz * NSA_KV_GROUPS * nq,
                                          lambda b, g, i: (b * NSA_KV_GROUPS + g) * nq + i)
    outs = pl.pallas_call(
        functools.partial(_nsa_kernel, tq=tq, seq=seq, n_cast=len(casts)),
        grid=(bsz, NSA_KV_GROUPS, nq),
        in_specs=[pl.BlockSpec((tq, gw), lambda b, g, i: (b * nq + i, g)),
                  pl.BlockSpec((None, None, n_cmp_pad, LANES), lambda b, g, i: (g, b, 0, 0)),
                  pl.BlockSpec((None, None, n_cmp_pad, LANES),
                               lambda b, g, i: (NSA_KV_GROUPS + g, b, 0, 0)),
                  full(KS0), full(VS0), full(KW0), full(VW0),
                  pl.BlockSpec((tq, LANES), lambda b, g, i: (b * nq + i, GT0 + g)),
                  pl.BlockSpec(selmap.shape, lambda b, g, i: (0, 0)),
                  pl.BlockSpec(et_sel.shape, lambda b, g, i: (0, 0))] + cast_specs,
        out_specs=[pl.BlockSpec((tq, gw), lambda b, g, i: (b * nq + i, g))] + cast_specs,
        out_shape=[jax.ShapeDtypeStruct((m, NSA_HEADS * HEAD_DIM), BF16)] + cast_shapes,
        compiler_params=pltpu.CompilerParams(
            dimension_semantics=("parallel", "parallel", "arbitrary"),
            vmem_limit_bytes=VMEM_LIMIT),
        name="nsa_attn",
    )(qkv, kcv, kcv, qkv, qkv, qkv, qkv, qkv, selmap, et_sel, *[w for w, _ in casts])
    return outs[0], outs[1:]


def _moba_kernel(q_ref, k_ref, v_ref, avg_ref, et_ref, *rest, seq, hb, n_cast):
    cast_in, (o_ref, *cast_out), (kmh_ref, kml_ref) = rest[:n_cast], rest[n_cast:2 * n_cast + 1], rest[2 * n_cast + 1:]
    _cast_slabs(cast_in, cast_out)
    qi = pl.program_id(2)
    blk_len = MOBA_BLOCK
    nb = seq // blk_len
    nb_rows = avg_ref.shape[0]
    head = lambda h: slice(h * LANES, (h + 1) * LANES)

    @pl.when(qi == 0)
    def _():
        for h in range(hb):
            km = jnp.dot(avg_ref[...], k_ref[:, head(h)], preferred_element_type=F32)
            kmh_ref[h], kml_ref[h] = _split_bf16(km)

    blk = lax.broadcasted_iota(I32, (nb_rows, blk_len), 0)
    past = jnp.where(blk < qi, 1.0, 0.0)
    own = jnp.where(blk == qi, 1.0, 0.0)
    qas = []
    for h in range(hb):
        q = q_ref[:, head(h)]
        gate_t = _nt_dot(kmh_ref[h], q) + _nt_dot(kml_ref[h], q)
        picked = _topk_rows(jnp.where(past > 0.5, gate_t, -jnp.inf), min(MOBA_TOPK, nb - 1))
        qas.append(jnp.concatenate([q, _bias_from_keep(picked * past + own)], axis=1))

    def kv_chunk(j, h):
        rows = pl.ds(pl.multiple_of(j * blk_len, blk_len), blk_len)
        ka = jnp.concatenate([k_ref[rows, head(h)], et_ref[rows, :]], axis=1)
        return ka, _with_ones(v_ref[rows, head(h)])

    row = lax.broadcasted_iota(I32, (blk_len, blk_len), 0)
    colk = lax.broadcasted_iota(I32, (blk_len, blk_len), 1)
    m0 = jnp.full((blk_len, 1), NEG_BIG, F32)
    a0 = jnp.zeros((blk_len, 2 * HEAD_DIM), F32)
    def all_heads(j, c, mask=None, penalty=None):
        new = []
        for h in range(hb):
            new += _flash_step(qas[h], *kv_chunk(j, h), c[2 * h], c[2 * h + 1], mask, penalty)
        return tuple(new)

    carry = all_heads(qi, (m0, a0) * hb, colk <= row)
    carry = all_heads(jnp.maximum(qi - 1, 0), carry, None, jnp.where(qi % 2 == 1, 0.0, NEG_BIG))
    final = lax.fori_loop(0, qi // 2, lambda i, c: all_heads(2 * i + 1, all_heads(2 * i, c)), carry)
    for h in range(hb):
        o_ref[:, head(h)] = _flash_out(final[2 * h + 1]).astype(BF16)


def _moba(qkv, avg, et_blk, bsz, seq, casts):
    m = qkv.shape[0]
    nq = seq // MOBA_BLOCK
    hb = MOBA_HEADS_PER_STEP
    assert MOBA_HEADS % hb == 0 and QM0 % hb == 0 and KM0 % hb == 0 and VM0 % hb == 0
    hw = hb * LANES
    ng = MOBA_HEADS // hb
    cast_specs, cast_shapes = _cast_specs(casts, bsz * ng * nq, lambda b, h, i: (b * ng + h) * nq + i)
    whole = lambda c0: pl.BlockSpec((seq, hw), lambda b, h, i: (b, c0 // hb + h),
                                    pipeline_mode=pl.Buffered(1))
    outs = pl.pallas_call(
        functools.partial(_moba_kernel, seq=seq, hb=hb, n_cast=len(casts)),
        grid=(bsz, ng, nq),
        in_specs=[pl.BlockSpec((MOBA_BLOCK, hw), lambda b, h, i: (b * nq + i, QM0 // hb + h)),
                  whole(KM0), whole(VM0),
                  pl.BlockSpec(avg.shape, lambda b, h, i: (0, 0)),
                  pl.BlockSpec(et_blk.shape, lambda b, h, i: (0, 0))] + cast_specs,
        out_specs=[pl.BlockSpec((MOBA_BLOCK, hw), lambda b, h, i: (b * nq + i, h))] + cast_specs,
        out_shape=[jax.ShapeDtypeStruct((m, MOBA_HEADS * HEAD_DIM), BF16)] + cast_shapes,
        scratch_shapes=[pltpu.VMEM((hb, avg.shape[0], HEAD_DIM), BF16),
                        pltpu.VMEM((hb, avg.shape[0], HEAD_DIM), BF16)],
        compiler_params=pltpu.CompilerParams(
            dimension_semantics=("parallel", "parallel", "arbitrary"),
            vmem_limit_bytes=VMEM_LIMIT),
        name="moba_attn",
    )(qkv, qkv, qkv, avg, et_blk, *[w for w, _ in casts])
    return outs[0], outs[1:]


def _outproj_kernel(on_ref, om_ref, w_ref, x_ref, g_ref, gate_ref, o_ref):
    kn = on_ref.shape[1]
    acc = (jnp.dot(on_ref[...], w_ref[:kn, :], preferred_element_type=F32)
           + jnp.dot(om_ref[...], w_ref[kn:, :], preferred_element_type=F32))
    o_ref[...] = x_ref[...] + gate_ref[...] * _rms_norm(acc, g_ref[...])


def _outproj(o_n, o_m, w_o, x2, g_post, gate, seq):
    m, d = x2.shape
    tm = min(512, seq)
    tiles_per_b = seq // tm
    return pl.pallas_call(
        _outproj_kernel,
        grid=(m // tm,),
        in_specs=[pl.BlockSpec((tm, o_n.shape[1]), lambda i: (i, 0)),
                  pl.BlockSpec((tm, o_m.shape[1]), lambda i: (i, 0)),
                  pl.BlockSpec(w_o.shape, lambda i: (0, 0)),
                  pl.BlockSpec((tm, d), lambda i: (i, 0)),
                  pl.BlockSpec((1, d), lambda i: (0, 0)),
                  pl.BlockSpec((None, 1, d), lambda i: (i // tiles_per_b, 0, 0))],
        out_specs=pl.BlockSpec((tm, d), lambda i: (i, 0)),
        out_shape=jax.ShapeDtypeStruct((m, d), F32),
        compiler_params=pltpu.CompilerParams(dimension_semantics=("parallel",),
                                             vmem_limit_bytes=VMEM_LIMIT),
        name="outproj",
    )(o_n, o_m, w_o, x2, g_post, gate)


def _ffn_kernel(x_ref, gpre_ref, sc_ref, sh_ref, wu_ref, wd_ref, gpost_ref, gate_ref,
                o_ref, h_ref):
    k = pl.program_id(1)
    last = pl.num_programs(1) - 1
    row_blocks = [slice(r, r + FFN_ROW_BLOCK) for r in range(0, x_ref.shape[0], FFN_ROW_BLOCK)]

    def mlp(h):
        u = jnp.dot(h, wu_ref[...], preferred_element_type=F32)
        u = jnp.square(jnp.maximum(u, 0.0)).astype(BF16)
        return jnp.dot(u, wd_ref[...], preferred_element_type=F32)

    @pl.when(k == 0)
    def _():
        for rows in row_blocks:
            h = _rms_norm(x_ref[rows], gpre_ref[...]) * (1.0 + sc_ref[...]) + sh_ref[...]
            h_ref[rows] = h.astype(BF16)
            o_ref[rows] = mlp(h_ref[rows])

    @pl.when(jnp.logical_and(k > 0, k < last))
    def _():
        o_ref[...] += mlp(h_ref[...])

    @pl.when(k == last)
    def _():
        for rows in row_blocks:
            f = o_ref[rows] + mlp(h_ref[rows])
            o_ref[rows] = x_ref[rows] + gate_ref[...] * _rms_norm(f, gpost_ref[...])


def _ffn(x2, g_pre, scale, shift, w_up, w_down, g_post, gate, seq):
    m, d = x2.shape
    dff = w_up.shape[1]
    tm = min(1024, seq)
    ck = 512
    assert dff // ck >= 2 and tm % FFN_ROW_BLOCK == 0
    tiles_per_b = seq // tm
    vec = pl.BlockSpec((1, d), lambda i, k: (0, 0))
    per_b = pl.BlockSpec((None, 1, d), lambda i, k: (i // tiles_per_b, 0, 0))
    return pl.pallas_call(
        _ffn_kernel,
        grid=(m // tm, dff // ck),
        in_specs=[pl.BlockSpec((tm, d), lambda i, k: (i, 0)), vec, per_b, per_b,
                  pl.BlockSpec((d, ck), lambda i, k: (0, k)),
                  pl.BlockSpec((ck, d), lambda i, k: (k, 0)),
                  vec, per_b],
        out_specs=pl.BlockSpec((tm, d), lambda i, k: (i, 0)),
        out_shape=jax.ShapeDtypeStruct((m, d), F32),
        scratch_shapes=[pltpu.VMEM((tm, d), BF16)],
        compiler_params=pltpu.CompilerParams(dimension_semantics=("parallel", "arbitrary"),
                                             vmem_limit_bytes=VMEM_LIMIT),
        name="ffn",
    )(x2, g_pre, scale, shift, w_up, w_down, g_post, gate)


def _constants(seq):
    n_cmp_pad = seq // CMP_STRIDE
    n_cmp = (seq - CMP_LEN) // CMP_STRIDE + 1
    n_sel = seq // SEL_LEN
    cs = np.arange(n_cmp) * CMP_STRIDE
    ss = np.arange(n_sel) * SEL_LEN
    overlap = np.clip(np.minimum(cs[:, None] + CMP_LEN, ss[None, :] + SEL_LEN)
                      - np.maximum(cs[:, None], ss[None, :]), 0, None)
    selmap_t = np.zeros((n_sel, n_cmp_pad), np.float32)
    selmap_t[:, :n_cmp] = (overlap / CMP_LEN).T
    keys = np.arange(seq)
    et_sel = (keys[:, None] // SEL_LEN == np.arange(LANES)[None, :]).astype(np.float32)
    et_blk = (keys[:, None] // MOBA_BLOCK == np.arange(LANES)[None, :]).astype(np.float32)
    nb_rows = -(-(seq // MOBA_BLOCK) // BF16_SUBLANES) * BF16_SUBLANES
    avg = et_blk.T[:nb_rows] / MOBA_BLOCK
    to = lambda a: jnp.asarray(a, BF16)
    return to(selmap_t), to(et_sel), to(et_blk), to(avg)


def _rope_tables(seq):
    pos = jnp.arange(seq, dtype=F32)
    inv = ROPE_THETA ** (-jnp.arange(0, HEAD_DIM, 2, dtype=F32) / HEAD_DIM)
    ang = pos[:, None] * inv[None, :]
    cos, sin = jnp.cos(ang), jnp.sin(ang)
    return jnp.concatenate([cos, cos], axis=1), jnp.concatenate([-sin, sin], axis=1)


def kernel(x, c, w_ada, b_ada, pre_norm_mix, post_norm_mix, w_in, cmp_k_pos, cmp_k_w1, cmp_k_w2,
           cmp_v_pos, cmp_v_w1, cmp_v_w2, w_o, pre_norm_ffn, post_norm_ffn, w_up, w_down):
    bsz, seq, d = x.shape
    depth = w_ada.shape[0]
    cos2, sin2 = _rope_tables(seq)
    selmap, et_sel, et_blk, avg = _constants(seq)
    x2 = x.reshape(bsz * seq, d)
    for l in range(depth):
        mod, w_head, w_tail = _adaln(c, w_ada[l], b_ada[l], w_in[l])
        shift_m, scale_m, gate_m, shift_f, scale_f, gate_f = [
            a.reshape(bsz, 1, d) for a in jnp.split(mod, 6, axis=-1)]
        row = lambda a: a.reshape(1, d)

        qkv, kvc = _inproj(x2, row(pre_norm_mix[l]), scale_m, shift_m, w_head, w_tail,
                           cos2, sin2, seq)
        kvc4 = kvc.reshape(2 * NSA_KV_GROUPS, bsz, seq // CMP_STRIDE, CMP_STRIDE * HEAD_DIM)
        pos2 = jnp.stack([cmp_k_pos[l].reshape(1, -1), cmp_v_pos[l].reshape(1, -1)])
        kcv = _compress(kvc4, pos2, jnp.stack([cmp_k_w1[l], cmp_v_w1[l]]),
                        jnp.stack([cmp_k_w2[l], cmp_v_w2[l]]))
        o_n, (w_o_bf,) = _nsa(qkv, kcv, selmap, et_sel, bsz, seq, [(w_o[l], 0)])
        o_m, (w_up_bf, w_down_bf) = _moba(qkv, avg, et_blk, bsz, seq, [(w_up[l], 1), (w_down[l], 0)])
        x2 = _outproj(o_n, o_m, w_o_bf, x2, row(post_norm_mix[l]), gate_m, seq)
        x2 = _ffn(x2, row(pre_norm_ffn[l]), scale_f, shift_f, w_up_bf, w_down_bf,
                  row(post_norm_ffn[l]), gate_f, seq)
    return x2.reshape(bsz, seq, d)
```

```python
import functools

import numpy as np
import jax
import jax.numpy as jnp
from jax import lax
from jax.experimental import pallas as pl
from jax.experimental.pallas import tpu as pltpu

F32 = jnp.float32
BF16 = jnp.bfloat16
I32 = jnp.int32

HEAD_DIM = 128
NSA_HEADS = 8
NSA_KV_GROUPS = 2
NSA_HPG = NSA_HEADS // NSA_KV_GROUPS
MOBA_HEADS = 8
CMP_LEN = 32
CMP_STRIDE = 16
SEL_LEN = 64
SEL_TOPK = 16
WINDOW = 512
MOBA_BLOCK = 256
MOBA_TOPK = 3
ROPE_THETA = 10000.0
EPS = 1e-6

LANES = 128
BF16_SUBLANES = 16
MXU_COLS = 256
FFN_ROW_BLOCK = 256
NSA_TQ = 256
MOBA_HEADS_PER_STEP = 8
NEG_BIG = -1e30
VMEM_LIMIT = 56 * 1024 * 1024

QN0 = 0
QM0 = 8
KM0 = 16
VM0 = 24
KC0 = 32
VC0 = 34
KS0 = 36
VS0 = 38
KW0 = 40
VW0 = 42
GT0 = 44
N_BLOCKS = 48
INPROJ_TN = 1024
HEAD_COLS = 2560
GATE_COLS = 3 * NSA_HEADS
IN_SLAB = 256
ROPE_Q, ROPE_K, PLAIN, GATES, ZERO = range(5)
_INPROJ_TILES = (
    ("head", 0, (ROPE_Q,) * 4),
    ("tail", 0, (ROPE_Q,) * 4),
    ("tail", 1, (ROPE_K,) * 4),
    ("tail", 2, (PLAIN,) * 4),
    ("head", 1, (ROPE_K, PLAIN, ROPE_K, PLAIN)),
    ("head", 2, (ROPE_K, PLAIN, GATES, ZERO)),
)


def _nt_dot(a, b):
    return lax.dot_general(a, b, (((1,), (1,)), ((), ())), preferred_element_type=F32)


def _rms_norm(x, g):
    ms = jnp.mean(x * x, axis=-1, keepdims=True)
    return x * lax.rsqrt(ms + EPS) * g


def _split_bf16(a):
    hi = a.astype(BF16)
    lo = (a - hi.astype(F32)).astype(BF16)
    return hi, lo


def _adaln_kernel(ct_ref, w_ref, b_ref, wh_ref, wa_ref, wb_ref, o_ref, head_ref, tail_ref):
    ct = ct_ref[...]
    a = ct * (1.0 / (1.0 + jnp.exp(-ct)))
    w = w_ref[...]
    for b in range(o_ref.shape[0]):
        o_ref[b] = jnp.sum(w * a[:, b:b + 1], axis=0, keepdims=True) + b_ref[...]

    t = pl.program_id(0)
    tail_ref[...] = jnp.concatenate([wa_ref[GATE_COLS:, :], wb_ref[:GATE_COLS, :]], axis=0).astype(BF16)

    n_plain = HEAD_COLS // IN_SLAB

    @pl.when(t < n_plain)
    def _():
        head_ref[...] = wh_ref[...].astype(BF16)

    @pl.when(t == n_plain)
    def _():
        per_group = 3 * NSA_HPG
        g = wh_ref[:LANES, :]
        row = lax.broadcasted_iota(I32, g.shape, 0)
        for grp in range(NSA_KV_GROUPS):
            rows = g if grp == 0 else pltpu.roll(g, LANES - grp * per_group, 0)
            head_ref[grp * LANES:(grp + 1) * LANES, :] = jnp.where(row < per_group, rows, 0.0).astype(BF16)

    @pl.when(t > n_plain)
    def _():
        head_ref[...] = jnp.zeros(head_ref.shape, BF16)


def _adaln(c, w, b, w_in_t):
    bsz, d = c.shape
    n = w.shape[1]
    tn = 1024
    steps = n // tn
    n_src = sum(1 for src, _, _ in _INPROJ_TILES if src == "head") * INPROJ_TN
    n_plain = HEAD_COLS // IN_SLAB
    assert bsz <= LANES and steps * IN_SLAB == n_src and HEAD_COLS % IN_SLAB == 0 and n_plain < steps
    assert IN_SLAB == NSA_KV_GROUPS * LANES and w_in_t.shape[0] == HEAD_COLS + GATE_COLS + n_src
    assert GATE_COLS % 8 == 0
    ct = jnp.zeros((d, LANES), F32).at[:, :bsz].set(c.T)
    slab = lambda f: pl.BlockSpec((IN_SLAB, d), lambda j: (f(j), 0))
    out, w_head, w_tail = pl.pallas_call(
        _adaln_kernel,
        grid=(steps,),
        in_specs=[pl.BlockSpec((d, LANES), lambda j: (0, 0)),
                  pl.BlockSpec((d, tn), lambda j: (0, j)),
                  pl.BlockSpec((1, tn), lambda j: (0, j)),
                  slab(lambda j: jnp.minimum(j, n_plain)),
                  slab(lambda j: n_plain + j),
                  slab(lambda j: n_plain + j + 1)],
        out_specs=[pl.BlockSpec((bsz, 1, tn), lambda j: (0, 0, j)), slab(lambda j: j), slab(lambda j: j)],
        out_shape=[jax.ShapeDtypeStruct((bsz, 1, n), F32),
                   jax.ShapeDtypeStruct((n_src, d), BF16), jax.ShapeDtypeStruct((n_src, d), BF16)],
        compiler_params=pltpu.CompilerParams(dimension_semantics=("parallel",),
                                             vmem_limit_bytes=VMEM_LIMIT),
        name="adaln",
    )(ct, w, b.reshape(1, n), w_in_t, w_in_t, w_in_t)
    return out.reshape(bsz, n), w_head, w_tail


def _inproj_kernel(x_ref, g_ref, sc_ref, sh_ref, wh_ref, wt_ref, cos_ref, sin_ref,
                   o_ref, kvc_ref, h_ref):
    j = pl.program_id(1)

    @pl.when(j == 0)
    def _():
        h = _rms_norm(x_ref[...], g_ref[...]) * (1.0 + sc_ref[...]) + sh_ref[...]
        h_ref[...] = h.astype(BF16)

    def tile(w_ref, kinds):
        for ci, kind in enumerate(kinds):
            c0 = ci * MXU_COLS
            if kind == ZERO:
                o_ref[:, c0:c0 + MXU_COLS] = jnp.zeros((o_ref.shape[0], MXU_COLS), BF16)
                continue
            acc = _nt_dot(h_ref[...], w_ref[c0:c0 + MXU_COLS, :])
            for c in range(c0, c0 + MXU_COLS, LANES):
                ch = acc[:, c - c0:c - c0 + LANES]
                if kind in (ROPE_Q, ROPE_K):
                    ch = ch * cos_ref[...] + pltpu.roll(ch, HEAD_DIM // 2, 1) * sin_ref[...]
                if kind == ROPE_Q:
                    ch = ch * HEAD_DIM ** -0.5
                o_ref[:, c:c + LANES] = ch.astype(BF16)

    for jj, (src, _, kinds) in enumerate(_INPROJ_TILES):
        pl.when(j == jj)(functools.partial(tile, wh_ref if src == "head" else wt_ref, kinds))

    jkv, ok = divmod(KC0 * LANES, INPROJ_TN)
    ov = VC0 * LANES - jkv * INPROJ_TN
    assert 0 < ov < INPROJ_TN

    @pl.when(j == jkv)
    def _():
        for g in range(NSA_KV_GROUPS):
            kvc_ref[g] = o_ref[:, ok + g * LANES: ok + (g + 1) * LANES]
            kvc_ref[NSA_KV_GROUPS + g] = o_ref[:, ov + g * LANES: ov + (g + 1) * LANES]


def _step_lookup(j, table):
    out = jnp.int32(table[0])
    for k in range(1, len(table)):
        out = jnp.where(j >= k, table[k], out)
    return out


def _inproj(x2, g, scale, shift, w_head, w_tail, cos2, sin2, seq):
    m, d = x2.shape
    tm = min(1024, seq)
    tn = INPROJ_TN
    tiles_per_b = seq // tm
    blocks = {"head": [], "tail": []}
    for src, blk, _ in _INPROJ_TILES:
        for name, lst in blocks.items():
            lst.append(blk if src == name else (lst[-1] if lst else None))
    for lst in blocks.values():
        first = next(b for b in lst if b is not None)
        lst[:] = [first if b is None else b for b in lst]
    return pl.pallas_call(
        _inproj_kernel,
        grid=(m // tm, len(_INPROJ_TILES)),
        in_specs=[pl.BlockSpec((tm, d), lambda i, j: (i, 0)),
                  pl.BlockSpec((1, d), lambda i, j: (0, 0)),
                  pl.BlockSpec((None, 1, d), lambda i, j: (i // tiles_per_b, 0, 0)),
                  pl.BlockSpec((None, 1, d), lambda i, j: (i // tiles_per_b, 0, 0)),
                  pl.BlockSpec((tn, d), lambda i, j: (_step_lookup(j, blocks["head"]), 0)),
                  pl.BlockSpec((tn, d), lambda i, j: (_step_lookup(j, blocks["tail"]), 0)),
                  pl.BlockSpec((tm, LANES), lambda i, j: (i % tiles_per_b, 0)),
                  pl.BlockSpec((tm, LANES), lambda i, j: (i % tiles_per_b, 0))],
        out_specs=[pl.BlockSpec((tm, tn), lambda i, j: (i, j)),
                   pl.BlockSpec((2 * NSA_KV_GROUPS, tm, LANES), lambda i, j: (0, i, 0))],
        out_shape=[jax.ShapeDtypeStruct((m, N_BLOCKS * LANES), BF16),
                   jax.ShapeDtypeStruct((2 * NSA_KV_GROUPS, m, LANES), BF16)],
        scratch_shapes=[pltpu.VMEM((tm, d), BF16)],
        compiler_params=pltpu.CompilerParams(dimension_semantics=("parallel", "arbitrary"),
                                             vmem_limit_bytes=VMEM_LIMIT),
        name="inproj",
    )(x2, g, scale, shift, w_head, w_tail, cos2, sin2)


def _compress_kernel(x_ref, pos_ref, w1_ref, w2_ref, o_ref):
    half = CMP_STRIDE * HEAD_DIM
    x = x_ref[...].astype(F32)
    pos = pos_ref[...]
    w1 = w1_ref[...].astype(BF16)
    a = jnp.dot((x + pos[:, :half]).astype(BF16), w1[:half], preferred_element_type=F32)
    b = jnp.dot((x + pos[:, half:]).astype(BF16), w1[half:], preferred_element_type=F32)
    n_rows = x.shape[0]
    h1 = a + pltpu.roll(b, n_rows - 1, 0)
    h = h1 * (1.0 / (1.0 + jnp.exp(-h1)))
    o_ref[...] = jnp.dot(h.astype(BF16), w2_ref[...].astype(BF16),
                         preferred_element_type=F32).astype(BF16)


def _compress(kvc4, pos2, w1s, w2s):
    ns, bsz, n_rows, width = kvc4.shape
    return pl.pallas_call(
        _compress_kernel,
        grid=(ns, bsz),
        in_specs=[pl.BlockSpec((None, None, n_rows, width), lambda s, b: (s, b, 0, 0)),
                  pl.BlockSpec((None, 1, 2 * width), lambda s, b: (s // NSA_KV_GROUPS, 0, 0)),
                  pl.BlockSpec((None, 2 * width, HEAD_DIM), lambda s, b: (s // NSA_KV_GROUPS, 0, 0)),
                  pl.BlockSpec((None, HEAD_DIM, HEAD_DIM), lambda s, b: (s // NSA_KV_GROUPS, 0, 0))],
        out_specs=pl.BlockSpec((None, None, n_rows, HEAD_DIM), lambda s, b: (s, b, 0, 0)),
        out_shape=jax.ShapeDtypeStruct((ns, bsz, n_rows, HEAD_DIM), BF16),
        compiler_params=pltpu.CompilerParams(dimension_semantics=("parallel", "parallel"),
                                             vmem_limit_bytes=VMEM_LIMIT),
        name="compress",
    )(kvc4, pos2, w1s, w2s)


def _topk_rows(score, k):
    idx = lax.broadcasted_iota(I32, score.shape, 0).astype(F32)
    work = score
    sel = jnp.zeros(score.shape, F32)
    for _ in range(k):
        mx = jnp.max(work, axis=0, keepdims=True)
        first = jnp.min(jnp.where(work == mx, idx, float(score.shape[0])), axis=0, keepdims=True)
        hit = idx == first
        sel = jnp.where(hit, 1.0, sel)
        work = jnp.where(hit, -jnp.inf, work)
    return sel


def _bias_from_keep(keep_t):
    n, q = keep_t.shape
    padded = jnp.concatenate([keep_t, jnp.zeros((LANES - n, q), F32)], axis=0) if n < LANES else keep_t
    return jnp.where(padded.T > 0.5, 0.0, NEG_BIG).astype(BF16)


def _with_ones(v):
    return jnp.concatenate([v, jnp.ones(v.shape, v.dtype)], axis=1)


def _flash_step(qa, ka, va, m, acc, mask, penalty=None):
    s = _nt_dot(qa, ka)
    if mask is not None:
        s = jnp.where(mask, s, NEG_BIG)
    if penalty is not None:
        s = s + penalty
    m_new = jnp.maximum(m, jnp.max(s, axis=1, keepdims=True))
    alpha = jnp.exp(m - m_new)
    p = jnp.exp(s - m_new).astype(BF16)
    acc = alpha * acc + jnp.dot(p, va, preferred_element_type=F32)
    return m_new, acc


def _flash_out(acc):
    return acc[:, :HEAD_DIM] * (1.0 / acc[:, HEAD_DIM:])


def _cast_specs(casts, n_steps, step_of):
    specs, shapes = [], []
    for w, axis in casts:
        size = w.shape[axis] // n_steps
        assert w.shape[axis] % n_steps == 0 and size % (BF16_SUBLANES if axis == 0 else LANES) == 0
        if axis == 0:
            specs.append(pl.BlockSpec((size, w.shape[1]), lambda *g: (step_of(*g), 0)))
        else:
            specs.append(pl.BlockSpec((w.shape[0], size), lambda *g: (0, step_of(*g))))
        shapes.append(jax.ShapeDtypeStruct(w.shape, BF16))
    return specs, shapes


def _cast_slabs(src_refs, dst_refs):
    for src, dst in zip(src_refs, dst_refs):
        dst[...] = src[...].astype(BF16)


def _nsa_kernel(q_ref, kc_ref, vc_ref, ks_ref, vs_ref, kw_ref, vw_ref, gl_ref,
                selmap_ref, et_ref, *rest, tq, seq, n_cast):
    cast_in, (o_ref, *cast_out) = rest[:n_cast], rest[n_cast:]
    _cast_slabs(cast_in, cast_out)
    qi = pl.program_id(2)
    t0 = qi * tq
    rows = NSA_HPG * tq
    n_cmp_pad = kc_ref.shape[0]
    n_sel = seq // SEL_LEN

    q = q_ref[...]
    qs = jnp.concatenate([q[:, h * LANES:(h + 1) * LANES] for h in range(NSA_HPG)], axis=0)

    s = _nt_dot(qs, kc_ref[...])
    tt = t0 + (lax.broadcasted_iota(I32, (rows, n_cmp_pad), 0) & (tq - 1))
    cend = lax.broadcasted_iota(I32, (rows, n_cmp_pad), 1) * CMP_STRIDE + (CMP_LEN - 1)
    s = jnp.where(cend <= tt, s, -jnp.inf)
    mx = jnp.max(s, axis=1, keepdims=True)
    mx = jnp.where(mx == -jnp.inf, 0.0, mx)
    e = jnp.exp(s - mx)
    den = jnp.maximum(jnp.sum(e, axis=1, keepdims=True), 1e-30)
    p_c = e * (1.0 / den)
    o_c = jnp.dot(p_c.astype(BF16), vc_ref[...], preferred_element_type=F32)

    p_sum = p_c[0:tq]
    for h in range(1, NSA_HPG):
        p_sum = p_sum + p_c[h * tq:(h + 1) * tq]
    hi, lo = _split_bf16(p_sum)
    smap_t = selmap_ref[...]
    imp_t = _nt_dot(smap_t, hi) + _nt_dot(smap_t, lo)
    blk = lax.broadcasted_iota(I32, (n_sel, tq), 0)
    cb = (t0 + lax.broadcasted_iota(I32, (n_sel, tq), 1)) // SEL_LEN
    causal = blk <= cb
    forced = (blk == 0) | (blk == cb) | (blk == cb - 1)
    n_forced = 3
    free = jnp.where(causal, jnp.where(forced, 0.0, 1.0), 0.0)
    picked = _topk_rows(jnp.where(free > 0.5, imp_t, -jnp.inf), max(min(SEL_TOPK, n_sel) - n_forced, 0))
    keep_t = jnp.where(forced, 1.0, picked * free)
    selb = _bias_from_keep(keep_t)
    qa = jnp.concatenate([qs, jnp.concatenate([selb] * NSA_HPG, axis=0)], axis=1)

    def kv_chunk(j):
        start = pl.multiple_of(j * tq, tq)
        ka = jnp.concatenate([ks_ref[pl.ds(start, tq), :], et_ref[pl.ds(start, tq), :]], axis=1)
        return ka, _with_ones(vs_ref[pl.ds(start, tq), :])

    qrow = lax.broadcasted_iota(I32, (rows, tq), 0) & (tq - 1)
    kcol = lax.broadcasted_iota(I32, (rows, tq), 1)
    m0 = jnp.full((rows, 1), NEG_BIG, F32)
    a0 = jnp.zeros((rows, 2 * HEAD_DIM), F32)
    carry = _flash_step(qa, *kv_chunk(qi), m0, a0, kcol <= qrow)
    odd = jnp.where(qi % 2 == 1, 0.0, NEG_BIG)
    carry = _flash_step(qa, *kv_chunk(jnp.maximum(qi - 1, 0)), *carry, None, odd)

    def body(i, c):
        c = _flash_step(qa, *kv_chunk(2 * i), *c, None)
        return _flash_step(qa, *kv_chunk(2 * i + 1), *c, None)

    _, a_s = lax.fori_loop(0, qi // 2, body, carry)
    o_s = _flash_out(a_s)

    n_back = WINDOW // tq
    scores, values = [], []
    for back in range(n_back + 1):
        start = pl.multiple_of(jnp.maximum(qi - back, 0) * tq, tq)
        s_b = _nt_dot(qs, kw_ref[pl.ds(start, tq), :])
        if back == 0:
            s_b = jnp.where(kcol <= qrow, s_b, NEG_BIG)
        else:
            if back == n_back:
                s_b = jnp.where(kcol > qrow, s_b, NEG_BIG)
            s_b = s_b + jnp.where(qi >= back, 0.0, NEG_BIG)
        scores.append(s_b)
        values.append(_with_ones(vw_ref[pl.ds(start, tq), :]))
    mw = scores[0].max(axis=1, keepdims=True)
    for s_b in scores[1:]:
        mw = jnp.maximum(mw, s_b.max(axis=1, keepdims=True))
    a_w = None
    for s_b, v_b in zip(scores, values):
        t = jnp.dot(jnp.exp(s_b - mw).astype(BF16), v_b, preferred_element_type=F32)
        a_w = t if a_w is None else a_w + t
    o_w = _flash_out(a_w)

    gl = gl_ref[...].astype(F32)
    sig = 1.0 / (1.0 + jnp.exp(-gl))
    for h in range(NSA_HPG):
        r = slice(h * tq, (h + 1) * tq)
        o = (sig[:, 3 * h:3 * h + 1] * o_c[r] + sig[:, 3 * h + 1:3 * h + 2] * o_s[r]
             + sig[:, 3 * h + 2:3 * h + 3] * o_w[r])
        o_ref[:, h * LANES:(h + 1) * LANES] = o.astype(BF16)


def _nsa(qkv, kcv, selmap, et_sel, bsz, seq, casts):
    m = qkv.shape[0]
    tq = NSA_TQ
    assert seq % tq == 0 and WINDOW % tq == 0 and tq % SEL_LEN == 0
    nq = seq // tq
    n_cmp_pad = kcv.shape[2]
    gw = NSA_HPG * LANES
    col = lambda c0: (lambda b, g, i: (b, c0 + g))
    full = lambda c0: pl.BlockSpec((seq, LANES), col(c0))
    cast_specs, cast_shapes = _cast_specs(casts, bsz * NSA_KV_GROUPS * nq,
                                          lambda b, g, i: (b * NSA_KV_GROUPS + g) * nq + i)
    outs = pl.pallas_call(
        functools.partial(_nsa_kernel, tq=tq, seq=seq, n_cast=len(casts)),
        grid=(bsz, NSA_KV_GROUPS, nq),
        in_specs=[pl.BlockSpec((tq, gw), lambda b, g, i: (b * nq + i, g)),
                  pl.BlockSpec((None, None, n_cmp_pad, LANES), lambda b, g, i: (g, b, 0, 0)),
                  pl.BlockSpec((None, None, n_cmp_pad, LANES),
                               lambda b, g, i: (NSA_KV_GROUPS + g, b, 0, 0)),
                  full(KS0), full(VS0), full(KW0), full(VW0),
                  pl.BlockSpec((tq, LANES), lambda b, g, i: (b * nq + i, GT0 + g)),
                  pl.BlockSpec(selmap.shape, lambda b, g, i: (0, 0)),
                  pl.BlockSpec(et_sel.shape, lambda b, g, i: (0, 0))] + cast_specs,
        out_specs=[pl.BlockSpec((tq, gw), lambda b, g, i: (b * nq + i, g))] + cast_specs,
        out_shape=[jax.ShapeDtypeStruct((m, NSA_HEADS * HEAD_DIM), BF16)] + cast_shapes,
        compiler_params=pltpu.CompilerParams(
            dimension_semantics=("parallel", "parallel", "arbitrary"),
            vmem_limit_bytes=VMEM_LIMIT),
        name="nsa_attn",
    )(qkv, kcv, kcv, qkv, qkv, qkv, qkv, qkv, selmap, et_sel, *[w for w, _ in casts])
    return outs[0], outs[1:]


def _moba_kernel(q_ref, k_ref, v_ref, avg_ref, et_ref, *rest, seq, hb, n_cast):
    cast_in, (o_ref, *cast_out), (kmh_ref, kml_ref) = rest[:n_cast], rest[n_cast:2 * n_cast + 1], rest[2 * n_cast + 1:]
    _cast_slabs(cast_in, cast_out)
    qi = pl.program_id(2)
    blk_len = MOBA_BLOCK
    nb = seq // blk_len
    nb_rows = avg_ref.shape[0]
    head = lambda h: slice(h * LANES, (h + 1) * LANES)

    @pl.when(qi == 0)
    def _():
        for h in range(hb):
            km = jnp.dot(avg_ref[...], k_ref[:, head(h)], preferred_element_type=F32)
            kmh_ref[h], kml_ref[h] = _split_bf16(km)

    blk = lax.broadcasted_iota(I32, (nb_rows, blk_len), 0)
    past = jnp.where(blk < qi, 1.0, 0.0)
    own = jnp.where(blk == qi, 1.0, 0.0)
    qas = []
    for h in range(hb):
        q = q_ref[:, head(h)]
        gate_t = _nt_dot(kmh_ref[h], q) + _nt_dot(kml_ref[h], q)
        picked = _topk_rows(jnp.where(past > 0.5, gate_t, -jnp.inf), min(MOBA_TOPK, nb - 1))
        qas.append(jnp.concatenate([q, _bias_from_keep(picked * past + own)], axis=1))

    def kv_chunk(j, h):
        rows = pl.ds(pl.multiple_of(j * blk_len, blk_len), blk_len)
        ka = jnp.concatenate([k_ref[rows, head(h)], et_ref[rows, :]], axis=1)
        return ka, _with_ones(v_ref[rows, head(h)])

    row = lax.broadcasted_iota(I32, (blk_len, blk_len), 0)
    colk = lax.broadcasted_iota(I32, (blk_len, blk_len), 1)
    m0 = jnp.full((blk_len, 1), NEG_BIG, F32)
    a0 = jnp.zeros((blk_len, 2 * HEAD_DIM), F32)
    def all_heads(j, c, mask=None, penalty=None):
        new = []
        for h in range(hb):
            new += _flash_step(qas[h], *kv_chunk(j, h), c[2 * h], c[2 * h + 1], mask, penalty)
        return tuple(new)

    carry = all_heads(qi, (m0, a0) * hb, colk <= row)
    carry = all_heads(jnp.maximum(qi - 1, 0), carry, None, jnp.where(qi % 2 == 1, 0.0, NEG_BIG))
    final = lax.fori_loop(0, qi // 2, lambda i, c: all_heads(2 * i + 1, all_heads(2 * i, c)), carry)
    for h in range(hb):
        o_ref[:, head(h)] = _flash_out(final[2 * h + 1]).astype(BF16)


def _moba(qkv, avg, et_blk, bsz, seq, casts):
    m = qkv.shape[0]
    nq = seq // MOBA_BLOCK
    hb = MOBA_HEADS_PER_STEP
    assert MOBA_HEADS % hb == 0 and QM0 % hb == 0 and KM0 % hb == 0 and VM0 % hb == 0
    hw = hb * LANES
    ng = MOBA_HEADS // hb
    cast_specs, cast_shapes = _cast_specs(casts, bsz * ng * nq, lambda b, h, i: (b * ng + h) * nq + i)
    whole = lambda c0: pl.BlockSpec((seq, hw), lambda b, h, i: (b, c0 // hb + h),
                                    pipeline_mode=pl.Buffered(1))
    outs = pl.pallas_call(
        functools.partial(_moba_kernel, seq=seq, hb=hb, n_cast=len(casts)),
        grid=(bsz, ng, nq),
        in_specs=[pl.BlockSpec((MOBA_BLOCK, hw), lambda b, h, i: (b * nq + i, QM0 // hb + h)),
                  whole(KM0), whole(VM0),
                  pl.BlockSpec(avg.shape, lambda b, h, i: (0, 0)),
                  pl.BlockSpec(et_blk.shape, lambda b, h, i: (0, 0))] + cast_specs,
        out_specs=[pl.BlockSpec((MOBA_BLOCK, hw), lambda b, h, i: (b * nq + i, h))] + cast_specs,
        out_shape=[jax.ShapeDtypeStruct((m, MOBA_HEADS * HEAD_DIM), BF16)] + cast_shapes,
        scratch_shapes=[pltpu.VMEM((hb, avg.shape[0], HEAD_DIM), BF16),
                        pltpu.VMEM((hb, avg.shape[0], HEAD_DIM), BF16)],
        compiler_params=pltpu.CompilerParams(
            dimension_semantics=("parallel", "parallel", "arbitrary"),
            vmem_limit_bytes=VMEM_LIMIT),
        name="moba_attn",
    )(qkv, qkv, qkv, avg, et_blk, *[w for w, _ in casts])
    return outs[0], outs[1:]


def _outproj_kernel(on_ref, om_ref, w_ref, x_ref, g_ref, gate_ref, o_ref):
    kn = on_ref.shape[1]
    acc = (jnp.dot(on_ref[...], w_ref[:kn, :], preferred_element_type=F32)
           + jnp.dot(om_ref[...], w_ref[kn:, :], preferred_element_type=F32))
    o_ref[...] = x_ref[...] + gate_ref[...] * _rms_norm(acc, g_ref[...])


def _outproj(o_n, o_m, w_o, x2, g_post, gate, seq):
    m, d = x2.shape
    tm = min(512, seq)
    tiles_per_b = seq // tm
    return pl.pallas_call(
        _outproj_kernel,
        grid=(m // tm,),
        in_specs=[pl.BlockSpec((tm, o_n.shape[1]), lambda i: (i, 0)),
                  pl.BlockSpec((tm, o_m.shape[1]), lambda i: (i, 0)),
                  pl.BlockSpec(w_o.shape, lambda i: (0, 0)),
                  pl.BlockSpec((tm, d), lambda i: (i, 0)),
                  pl.BlockSpec((1, d), lambda i: (0, 0)),
                  pl.BlockSpec((None, 1, d), lambda i: (i // tiles_per_b, 0, 0))],
        out_specs=pl.BlockSpec((tm, d), lambda i: (i, 0)),
        out_shape=jax.ShapeDtypeStruct((m, d), F32),
        compiler_params=pltpu.CompilerParams(dimension_semantics=("parallel",),
                                             vmem_limit_bytes=VMEM_LIMIT),
        name="outproj",
    )(o_n, o_m, w_o, x2, g_post, gate)


def _ffn_kernel(x_ref, gpre_ref, sc_ref, sh_ref, wu_ref, wd_ref, gpost_ref, gate_ref,
                o_ref, h_ref):
    k = pl.program_id(1)
    last = pl.num_programs(1) - 1
    row_blocks = [slice(r, r + FFN_ROW_BLOCK) for r in range(0, x_ref.shape[0], FFN_ROW_BLOCK)]

    def mlp(h):
        u = jnp.dot(h, wu_ref[...], preferred_element_type=F32)
        u = jnp.square(jnp.maximum(u, 0.0)).astype(BF16)
        return jnp.dot(u, wd_ref[...], preferred_element_type=F32)

    @pl.when(k == 0)
    def _():
        for rows in row_blocks:
            h = _rms_norm(x_ref[rows], gpre_ref[...]) * (1.0 + sc_ref[...]) + sh_ref[...]
            h_ref[rows] = h.astype(BF16)
            o_ref[rows] = mlp(h_ref[rows])

    @pl.when(jnp.logical_and(k > 0, k < last))
    def _():
        o_ref[...] += mlp(h_ref[...])

    @pl.when(k == last)
    def _():
        for rows in row_blocks:
            f = o_ref[rows] + mlp(h_ref[rows])
            o_ref[rows] = x_ref[rows] + gate_ref[...] * _rms_norm(f, gpost_ref[...])


def _ffn(x2, g_pre, scale, shift, w_up, w_down, g_post, gate, seq):
    m, d = x2.shape
    dff = w_up.shape[1]
    tm = min(1024, seq)
    ck = 512
    assert dff // ck >= 2 and tm % FFN_ROW_BLOCK == 0
    tiles_per_b = seq // tm
    vec = pl.BlockSpec((1, d), lambda i, k: (0, 0))
    per_b = pl.BlockSpec((None, 1, d), lambda i, k: (i // tiles_per_b, 0, 0))
    return pl.pallas_call(
        _ffn_kernel,
        grid=(m // tm, dff // ck),
        in_specs=[pl.BlockSpec((tm, d), lambda i, k: (i, 0)), vec, per_b, per_b,
                  pl.BlockSpec((d, ck), lambda i, k: (0, k)),
                  pl.BlockSpec((ck, d), lambda i, k: (k, 0)),
                  vec, per_b],
        out_specs=pl.BlockSpec((tm, d), lambda i, k: (i, 0)),
        out_shape=jax.ShapeDtypeStruct((m, d), F32),
        scratch_shapes=[pltpu.VMEM((tm, d), BF16)],
        compiler_params=pltpu.CompilerParams(dimension_semantics=("parallel", "arbitrary"),
                                             vmem_limit_bytes=VMEM_LIMIT),
        name="ffn",
    )(x2, g_pre, scale, shift, w_up, w_down, g_post, gate)


def _constants(seq):
    n_cmp_pad = seq // CMP_STRIDE
    n_cmp = (seq - CMP_LEN) // CMP_STRIDE + 1
    n_sel = seq // SEL_LEN
    cs = np.arange(n_cmp) * CMP_STRIDE
    ss = np.arange(n_sel) * SEL_LEN
    overlap = np.clip(np.minimum(cs[:, None] + CMP_LEN, ss[None, :] + SEL_LEN)
                      - np.maximum(cs[:, None], ss[None, :]), 0, None)
    selmap_t = np.zeros((n_sel, n_cmp_pad), np.float32)
    selmap_t[:, :n_cmp] = (overlap / CMP_LEN).T
    keys = np.arange(seq)
    et_sel = (keys[:, None] // SEL_LEN == np.arange(LANES)[None, :]).astype(np.float32)
    et_blk = (keys[:, None] // MOBA_BLOCK == np.arange(LANES)[None, :]).astype(np.float32)
    nb_rows = -(-(seq // MOBA_BLOCK) // BF16_SUBLANES) * BF16_SUBLANES
    avg = et_blk.T[:nb_rows] / MOBA_BLOCK
    to = lambda a: jnp.asarray(a, BF16)
    return to(selmap_t), to(et_sel), to(et_blk), to(avg)


def _rope_tables(seq):
    pos = jnp.arange(seq, dtype=F32)
    inv = ROPE_THETA ** (-jnp.arange(0, HEAD_DIM, 2, dtype=F32) / HEAD_DIM)
    ang = pos[:, None] * inv[None, :]
    cos, sin = jnp.cos(ang), jnp.sin(ang)
    return jnp.concatenate([cos, cos], axis=1), jnp.concatenate([-sin, sin], axis=1)


def kernel(x, c, w_ada, b_ada, pre_norm_mix, post_norm_mix, w_in, cmp_k_pos, cmp_k_w1, cmp_k_w2,
           cmp_v_pos, cmp_v_w1, cmp_v_w2, w_o, pre_norm_ffn, post_norm_ffn, w_up, w_down):
    bsz, seq, d = x.shape
    depth = w_ada.shape[0]
    cos2, sin2 = _rope_tables(seq)
    selmap, et_sel, et_blk, avg = _constants(seq)
    x2 = x.reshape(bsz * seq, d)
    for l in range(depth):
        mod, w_head, w_tail = _adaln(c, w_ada[l], b_ada[l], w_in[l].T)
        shift_m, scale_m, gate_m, shift_f, scale_f, gate_f = [
            a.reshape(bsz, 1, d) for a in jnp.split(mod, 6, axis=-1)]
        row = lambda a: a.reshape(1, d)

        qkv, kvc = _inproj(x2, row(pre_norm_mix[l]), scale_m, shift_m, w_head, w_tail,
                           cos2, sin2, seq)
        kvc4 = kvc.reshape(2 * NSA_KV_GROUPS, bsz, seq // CMP_STRIDE, CMP_STRIDE * HEAD_DIM)
        pos2 = jnp.stack([cmp_k_pos[l].reshape(1, -1), cmp_v_pos[l].reshape(1, -1)])
        kcv = _compress(kvc4, pos2, jnp.stack([cmp_k_w1[l], cmp_v_w1[l]]),
                        jnp.stack([cmp_k_w2[l], cmp_v_w2[l]]))
        o_n, (w_o_bf,) = _nsa(qkv, kcv, selmap, et_sel, bsz, seq, [(w_o[l], 0)])
        o_m, (w_up_bf, w_down_bf) = _moba(qkv, avg, et_blk, bsz, seq, [(w_up[l], 1), (w_down[l], 0)])
        x2 = _outproj(o_n, o_m, w_o_bf, x2, row(post_norm_mix[l]), gate_m, seq)
        x2 = _ffn(x2, row(pre_norm_ffn[l]), scale_f, shift_f, w_up_bf, w_down_bf,
                  row(post_norm_ffn[l]), gate_f, seq)
    return x2.reshape(bsz, seq, d)
```

```python
import functools

import numpy as np
import jax
import jax.numpy as jnp
from jax import lax
from jax.experimental import pallas as pl
from jax.experimental.pallas import tpu as pltpu

F32 = jnp.float32
BF16 = jnp.bfloat16
I32 = jnp.int32

HEAD_DIM = 128
NSA_HEADS = 8
NSA_KV_GROUPS = 2
NSA_HPG = NSA_HEADS // NSA_KV_GROUPS
MOBA_HEADS = 8
CMP_LEN = 32
CMP_STRIDE = 16
SEL_LEN = 64
SEL_TOPK = 16
WINDOW = 512
MOBA_BLOCK = 256
MOBA_TOPK = 3
ROPE_THETA = 10000.0
EPS = 1e-6

LANES = 128
BF16_SUBLANES = 16
MXU_COLS = 256
FFN_ROW_BLOCK = 256
NSA_TQ = 256
MOBA_HEADS_PER_STEP = 8
Q_SCALE = HEAD_DIM ** -0.5 * float(np.log2(np.e))
NEG_BIG = -1e30
VMEM_LIMIT = 56 * 1024 * 1024

QN0 = 0
QM0 = 8
KM0 = 16
VM0 = 24
KC0 = 32
VC0 = 34
KS0 = 36
VS0 = 38
KW0 = 40
VW0 = 42
GT0 = 44
N_BLOCKS = 48
INPROJ_TN = 1024
HEAD_COLS = 2560
GATE_COLS = 3 * NSA_HEADS
IN_SLAB = 256
ROPE_Q, ROPE_K, PLAIN, GATES, ZERO = range(5)
_INPROJ_TILES = (
    ("head", 0, (ROPE_Q,) * 4),
    ("tail", 0, (ROPE_Q,) * 4),
    ("tail", 1, (ROPE_K,) * 4),
    ("tail", 2, (PLAIN,) * 4),
    ("head", 1, (ROPE_K, PLAIN, ROPE_K, PLAIN)),
    ("head", 2, (ROPE_K, PLAIN, GATES, ZERO)),
)


def _nt_dot(a, b):
    return lax.dot_general(a, b, (((1,), (1,)), ((), ())), preferred_element_type=F32)


def _rms_norm(x, g):
    ms = jnp.mean(x * x, axis=-1, keepdims=True)
    return x * lax.rsqrt(ms + EPS) * g


def _split_bf16(a):
    hi = a.astype(BF16)
    lo = (a - hi.astype(F32)).astype(BF16)
    return hi, lo


def _adaln_kernel(ct_ref, w_ref, b_ref, wh_ref, wa_ref, wb_ref, o_ref, head_ref, tail_ref):
    ct = ct_ref[...]
    a = ct * (1.0 / (1.0 + jnp.exp(-ct)))
    w = w_ref[...]
    for b in range(o_ref.shape[0]):
        o_ref[b] = jnp.sum(w * a[:, b:b + 1], axis=0, keepdims=True) + b_ref[...]

    t = pl.program_id(0)
    tail_ref[...] = jnp.concatenate([wa_ref[GATE_COLS:, :], wb_ref[:GATE_COLS, :]], axis=0).astype(BF16)

    n_plain = HEAD_COLS // IN_SLAB

    @pl.when(t < n_plain)
    def _():
        head_ref[...] = wh_ref[...].astype(BF16)

    @pl.when(t == n_plain)
    def _():
        per_group = 3 * NSA_HPG
        g = wh_ref[:LANES, :]
        row = lax.broadcasted_iota(I32, g.shape, 0)
        for grp in range(NSA_KV_GROUPS):
            rows = g if grp == 0 else pltpu.roll(g, LANES - grp * per_group, 0)
            head_ref[grp * LANES:(grp + 1) * LANES, :] = jnp.where(row < per_group, rows, 0.0).astype(BF16)

    @pl.when(t > n_plain)
    def _():
        head_ref[...] = jnp.zeros(head_ref.shape, BF16)


def _adaln(c, w, b, w_in_t):
    bsz, d = c.shape
    n = w.shape[1]
    tn = 1024
    steps = n // tn
    n_src = sum(1 for src, _, _ in _INPROJ_TILES if src == "head") * INPROJ_TN
    n_plain = HEAD_COLS // IN_SLAB
    assert bsz <= LANES and steps * IN_SLAB == n_src and HEAD_COLS % IN_SLAB == 0 and n_plain < steps
    assert IN_SLAB == NSA_KV_GROUPS * LANES and w_in_t.shape[0] == HEAD_COLS + GATE_COLS + n_src
    assert GATE_COLS % 8 == 0
    ct = jnp.zeros((d, LANES), F32).at[:, :bsz].set(c.T)
    slab = lambda f: pl.BlockSpec((IN_SLAB, d), lambda j: (f(j), 0))
    out, w_head, w_tail = pl.pallas_call(
        _adaln_kernel,
        grid=(steps,),
        in_specs=[pl.BlockSpec((d, LANES), lambda j: (0, 0)),
                  pl.BlockSpec((d, tn), lambda j: (0, j)),
                  pl.BlockSpec((1, tn), lambda j: (0, j)),
                  slab(lambda j: jnp.minimum(j, n_plain)),
                  slab(lambda j: n_plain + j),
                  slab(lambda j: n_plain + j + 1)],
        out_specs=[pl.BlockSpec((bsz, 1, tn), lambda j: (0, 0, j)), slab(lambda j: j), slab(lambda j: j)],
        out_shape=[jax.ShapeDtypeStruct((bsz, 1, n), F32),
                   jax.ShapeDtypeStruct((n_src, d), BF16), jax.ShapeDtypeStruct((n_src, d), BF16)],
        compiler_params=pltpu.CompilerParams(dimension_semantics=("parallel",),
                                             vmem_limit_bytes=VMEM_LIMIT),
        name="adaln",
    )(ct, w, b.reshape(1, n), w_in_t, w_in_t, w_in_t)
    return out.reshape(bsz, n), w_head, w_tail


def _inproj_kernel(x_ref, g_ref, sc_ref, sh_ref, wh_ref, wt_ref, cos_ref, sin_ref,
                   o_ref, kvc_ref, h_ref):
    j = pl.program_id(1)

    @pl.when(j == 0)
    def _():
        h = _rms_norm(x_ref[...], g_ref[...]) * (1.0 + sc_ref[...]) + sh_ref[...]
        h_ref[...] = h.astype(BF16)

    def tile(w_ref, kinds):
        for ci, kind in enumerate(kinds):
            c0 = ci * MXU_COLS
            if kind == ZERO:
                o_ref[:, c0:c0 + MXU_COLS] = jnp.zeros((o_ref.shape[0], MXU_COLS), BF16)
                continue
            acc = _nt_dot(h_ref[...], w_ref[c0:c0 + MXU_COLS, :])
            for c in range(c0, c0 + MXU_COLS, LANES):
                ch = acc[:, c - c0:c - c0 + LANES]
                if kind in (ROPE_Q, ROPE_K):
                    ch = ch * cos_ref[...] + pltpu.roll(ch, HEAD_DIM // 2, 1) * sin_ref[...]
                if kind == ROPE_Q:
                    ch = ch * Q_SCALE
                o_ref[:, c:c + LANES] = ch.astype(BF16)

    for jj, (src, _, kinds) in enumerate(_INPROJ_TILES):
        pl.when(j == jj)(functools.partial(tile, wh_ref if src == "head" else wt_ref, kinds))

    jkv, ok = divmod(KC0 * LANES, INPROJ_TN)
    ov = VC0 * LANES - jkv * INPROJ_TN
    assert 0 < ov < INPROJ_TN

    @pl.when(j == jkv)
    def _():
        for g in range(NSA_KV_GROUPS):
            kvc_ref[g] = o_ref[:, ok + g * LANES: ok + (g + 1) * LANES]
            kvc_ref[NSA_KV_GROUPS + g] = o_ref[:, ov + g * LANES: ov + (g + 1) * LANES]


def _step_lookup(j, table):
    out = jnp.int32(table[0])
    for k in range(1, len(table)):
        out = jnp.where(j >= k, table[k], out)
    return out


def _inproj(x2, g, scale, shift, w_head, w_tail, cos2, sin2, seq):
    m, d = x2.shape
    tm = min(1024, seq)
    tn = INPROJ_TN
    tiles_per_b = seq // tm
    blocks = {"head": [], "tail": []}
    for src, blk, _ in _INPROJ_TILES:
        for name, lst in blocks.items():
            lst.append(blk if src == name else (lst[-1] if lst else None))
    for lst in blocks.values():
        first = next(b for b in lst if b is not None)
        lst[:] = [first if b is None else b for b in lst]
    return pl.pallas_call(
        _inproj_kernel,
        grid=(m // tm, len(_INPROJ_TILES)),
        in_specs=[pl.BlockSpec((tm, d), lambda i, j: (i, 0)),
                  pl.BlockSpec((1, d), lambda i, j: (0, 0)),
                  pl.BlockSpec((None, 1, d), lambda i, j: (i // tiles_per_b, 0, 0)),
                  pl.BlockSpec((None, 1, d), lambda i, j: (i // tiles_per_b, 0, 0)),
                  pl.BlockSpec((tn, d), lambda i, j: (_step_lookup(j, blocks["head"]), 0)),
                  pl.BlockSpec((tn, d), lambda i, j: (_step_lookup(j, blocks["tail"]), 0)),
                  pl.BlockSpec((tm, LANES), lambda i, j: (i % tiles_per_b, 0)),
                  pl.BlockSpec((tm, LANES), lambda i, j: (i % tiles_per_b, 0))],
        out_specs=[pl.BlockSpec((tm, tn), lambda i, j: (i, j)),
                   pl.BlockSpec((2 * NSA_KV_GROUPS, tm, LANES), lambda i, j: (0, i, 0))],
        out_shape=[jax.ShapeDtypeStruct((m, N_BLOCKS * LANES), BF16),
                   jax.ShapeDtypeStruct((2 * NSA_KV_GROUPS, m, LANES), BF16)],
        scratch_shapes=[pltpu.VMEM((tm, d), BF16)],
        compiler_params=pltpu.CompilerParams(dimension_semantics=("parallel", "arbitrary"),
                                             vmem_limit_bytes=VMEM_LIMIT),
        name="inproj",
    )(x2, g, scale, shift, w_head, w_tail, cos2, sin2)


def _compress_kernel(x_ref, pos_ref, w1_ref, w2_ref, o_ref):
    half = CMP_STRIDE * HEAD_DIM
    x = x_ref[...].astype(F32)
    pos = pos_ref[...]
    w1 = w1_ref[...].astype(BF16)
    a = jnp.dot((x + pos[:, :half]).astype(BF16), w1[:half], preferred_element_type=F32)
    b = jnp.dot((x + pos[:, half:]).astype(BF16), w1[half:], preferred_element_type=F32)
    n_rows = x.shape[0]
    h1 = a + pltpu.roll(b, n_rows - 1, 0)
    h = h1 * (1.0 / (1.0 + jnp.exp(-h1)))
    o_ref[...] = jnp.dot(h.astype(BF16), w2_ref[...].astype(BF16),
                         preferred_element_type=F32).astype(BF16)


def _compress(kvc4, pos2, w1s, w2s):
    ns, bsz, n_rows, width = kvc4.shape
    return pl.pallas_call(
        _compress_kernel,
        grid=(ns, bsz),
        in_specs=[pl.BlockSpec((None, None, n_rows, width), lambda s, b: (s, b, 0, 0)),
                  pl.BlockSpec((None, 1, 2 * width), lambda s, b: (s // NSA_KV_GROUPS, 0, 0)),
                  pl.BlockSpec((None, 2 * width, HEAD_DIM), lambda s, b: (s // NSA_KV_GROUPS, 0, 0)),
                  pl.BlockSpec((None, HEAD_DIM, HEAD_DIM), lambda s, b: (s // NSA_KV_GROUPS, 0, 0))],
        out_specs=pl.BlockSpec((None, None, n_rows, HEAD_DIM), lambda s, b: (s, b, 0, 0)),
        out_shape=jax.ShapeDtypeStruct((ns, bsz, n_rows, HEAD_DIM), BF16),
        compiler_params=pltpu.CompilerParams(dimension_semantics=("parallel", "parallel"),
                                             vmem_limit_bytes=VMEM_LIMIT),
        name="compress",
    )(kvc4, pos2, w1s, w2s)


def _topk_rows(score, k):
    idx = lax.broadcasted_iota(I32, score.shape, 0).astype(F32)
    work = score
    sel = jnp.zeros(score.shape, F32)
    for _ in range(k):
        mx = jnp.max(work, axis=0, keepdims=True)
        first = jnp.min(jnp.where(work == mx, idx, float(score.shape[0])), axis=0, keepdims=True)
        hit = idx == first
        sel = jnp.where(hit, 1.0, sel)
        work = jnp.where(hit, -jnp.inf, work)
    return sel


def _bias_from_keep(keep_t):
    n, q = keep_t.shape
    padded = jnp.concatenate([keep_t, jnp.zeros((LANES - n, q), F32)], axis=0) if n < LANES else keep_t
    return jnp.where(padded.T > 0.5, 0.0, NEG_BIG).astype(BF16)


def _with_ones(v):
    return jnp.concatenate([v, jnp.ones(v.shape, v.dtype)], axis=1)


def _flash_step(qa, ka, va, m, acc, mask, penalty=None):
    s = _nt_dot(qa, ka)
    if mask is not None:
        s = jnp.where(mask, s, NEG_BIG)
    if penalty is not None:
        s = s + penalty
    m_new = jnp.maximum(m, jnp.max(s, axis=1, keepdims=True))
    alpha = jnp.exp2(m - m_new)
    p = jnp.exp2(s - m_new).astype(BF16)
    acc = alpha * acc + jnp.dot(p, va, preferred_element_type=F32)
    return m_new, acc


def _flash_out(acc):
    return acc[:, :HEAD_DIM] * (1.0 / acc[:, HEAD_DIM:])


def _cast_specs(casts, n_steps, step_of):
    specs, shapes = [], []
    for w, axis in casts:
        size = w.shape[axis] // n_steps
        assert w.shape[axis] % n_steps == 0 and size % (BF16_SUBLANES if axis == 0 else LANES) == 0
        if axis == 0:
            specs.append(pl.BlockSpec((size, w.shape[1]), lambda *g: (step_of(*g), 0)))
        else:
            specs.append(pl.BlockSpec((w.shape[0], size), lambda *g: (0, step_of(*g))))
        shapes.append(jax.ShapeDtypeStruct(w.shape, BF16))
    return specs, shapes


def _cast_slabs(src_refs, dst_refs):
    for src, dst in zip(src_refs, dst_refs):
        dst[...] = src[...].astype(BF16)


def _nsa_kernel(q_ref, kc_ref, vc_ref, ks_ref, vs_ref, kw_ref, vw_ref, gl_ref,
                selmap_ref, et_ref, *rest, tq, seq, n_cast):
    cast_in, (o_ref, *cast_out) = rest[:n_cast], rest[n_cast:]
    _cast_slabs(cast_in, cast_out)
    qi = pl.program_id(2)
    t0 = qi * tq
    rows = NSA_HPG * tq
    n_cmp_pad = kc_ref.shape[0]
    n_sel = seq // SEL_LEN

    q = q_ref[...]
    qs = jnp.concatenate([q[:, h * LANES:(h + 1) * LANES] for h in range(NSA_HPG)], axis=0)
    qrow = lax.broadcasted_iota(I32, (rows, tq), 0) & (tq - 1)
    kcol = lax.broadcasted_iota(I32, (rows, tq), 1)
    m0 = jnp.full((rows, 1), NEG_BIG, F32)
    a0 = jnp.zeros((rows, 2 * HEAD_DIM), F32)

    n_back = WINDOW // tq
    win = (m0, a0)
    for back in range(n_back + 1):
        chunk = pl.ds(pl.multiple_of(jnp.maximum(qi - back, 0) * tq, tq), tq)
        mask = (kcol <= qrow) if back == 0 else (kcol > qrow) if back == n_back else None
        exists = None if back == 0 else jnp.where(qi >= back, 0.0, NEG_BIG)
        win = _flash_step(qs, kw_ref[chunk, :], _with_ones(vw_ref[chunk, :]), *win, mask, exists)
    o_w = _flash_out(win[1])

    s = _nt_dot(qs, kc_ref[...])
    tt = t0 + (lax.broadcasted_iota(I32, (rows, n_cmp_pad), 0) & (tq - 1))
    cend = lax.broadcasted_iota(I32, (rows, n_cmp_pad), 1) * CMP_STRIDE + (CMP_LEN - 1)
    s = jnp.where(cend <= tt, s, -jnp.inf)
    mx = jnp.max(s, axis=1, keepdims=True)
    mx = jnp.where(mx == -jnp.inf, 0.0, mx)
    e = jnp.exp2(s - mx)
    den = jnp.maximum(jnp.sum(e, axis=1, keepdims=True), 1e-30)
    p_c = e * (1.0 / den)
    o_c = jnp.dot(p_c.astype(BF16), vc_ref[...], preferred_element_type=F32)

    p_sum = p_c[0:tq]
    for h in range(1, NSA_HPG):
        p_sum = p_sum + p_c[h * tq:(h + 1) * tq]
    hi, lo = _split_bf16(p_sum)
    smap_t = selmap_ref[...]
    imp_t = _nt_dot(smap_t, hi) + _nt_dot(smap_t, lo)
    blk = lax.broadcasted_iota(I32, (n_sel, tq), 0)
    cb = (t0 + lax.broadcasted_iota(I32, (n_sel, tq), 1)) // SEL_LEN
    causal = blk <= cb
    forced = (blk == 0) | (blk == cb) | (blk == cb - 1)
    n_forced = 3
    free = jnp.where(causal, jnp.where(forced, 0.0, 1.0), 0.0)
    picked = _topk_rows(jnp.where(free > 0.5, imp_t, -jnp.inf), max(min(SEL_TOPK, n_sel) - n_forced, 0))
    keep_t = jnp.where(forced, 1.0, picked * free)
    selb = _bias_from_keep(keep_t)
    qa = jnp.concatenate([qs, jnp.concatenate([selb] * NSA_HPG, axis=0)], axis=1)

    def kv_chunk(j):
        start = pl.multiple_of(j * tq, tq)
        ka = jnp.concatenate([ks_ref[pl.ds(start, tq), :], et_ref[pl.ds(start, tq), :]], axis=1)
        return ka, _with_ones(vs_ref[pl.ds(start, tq), :])

    carry = _flash_step(qa, *kv_chunk(qi), m0, a0, kcol <= qrow)
    odd = jnp.where(qi % 2 == 1, 0.0, NEG_BIG)
    carry = _flash_step(qa, *kv_chunk(jnp.maximum(qi - 1, 0)), *carry, None, odd)

    def body(i, c):
        c = _flash_step(qa, *kv_chunk(2 * i), *c, None)
        return _flash_step(qa, *kv_chunk(2 * i + 1), *c, None)

    _, a_s = lax.fori_loop(0, qi // 2, body, carry)
    o_s = _flash_out(a_s)

    gl = gl_ref[...].astype(F32)
    sig = 1.0 / (1.0 + jnp.exp(-gl))
    for h in range(NSA_HPG):
        r = slice(h * tq, (h + 1) * tq)
        o = (sig[:, 3 * h:3 * h + 1] * o_c[r] + sig[:, 3 * h + 1:3 * h + 2] * o_s[r]
             + sig[:, 3 * h + 2:3 * h + 3] * o_w[r])
        o_ref[:, h * LANES:(h + 1) * LANES] = o.astype(BF16)


def _nsa(qkv, kcv, selmap, et_sel, bsz, seq, casts):
    m = qkv.shape[0]
    tq = NSA_TQ
    assert seq % tq == 0 and WINDOW % tq == 0 and tq % SEL_LEN == 0
    nq = seq // tq
    n_cmp_pad = kcv.shape[2]
    gw = NSA_HPG * LANES
    col = lambda c0: (lambda b, g, i: (b, c0 + g))
    full = lambda c0: pl.BlockSpec((seq, LANES), col(c0))
    cast_specs, cast_shapes = _cast_specs(casts, bsz * NSA_KV_GROUPS * nq,
                                          lambda b, g, i: (b * NSA_KV_GROUPS + g) * nq + i)
    outs = pl.pallas_call(
        functools.partial(_nsa_kernel, tq=tq, seq=seq, n_cast=len(casts)),
        grid=(bsz, NSA_KV_GROUPS, nq),
        in_specs=[pl.BlockSpec((tq, gw), lambda b, g, i: (b * nq + i, g)),
                  pl.BlockSpec((None, None, n_cmp_pad, LANES), lambda b, g, i: (g, b, 0, 0)),
                  pl.BlockSpec((None, None, n_cmp_pad, LANES),
                               lambda b, g, i: (NSA_KV_GROUPS + g, b, 0, 0)),
                  full(KS0), full(VS0), full(KW0), full(VW0),
                  pl.BlockSpec((tq, LANES), lambda b, g, i: (b * nq + i, GT0 + g)),
                  pl.BlockSpec(selmap.shape, lambda b, g, i: (0, 0)),
                  pl.BlockSpec(et_sel.shape, lambda b, g, i: (0, 0))] + cast_specs,
        out_specs=[pl.BlockSpec((tq, gw), lambda b, g, i: (b * nq + i, g))] + cast_specs,
        out_shape=[jax.ShapeDtypeStruct((m, NSA_HEADS * HEAD_DIM), BF16)] + cast_shapes,
        compiler_params=pltpu.CompilerParams(
            dimension_semantics=("parallel", "parallel", "arbitrary"),
            vmem_limit_bytes=VMEM_LIMIT),
        name="nsa_attn",
    )(qkv, kcv, kcv, qkv, qkv, qkv, qkv, qkv, selmap, et_sel, *[w for w, _ in casts])
    return outs[0], outs[1:]


def _moba_kernel(q_ref, k_ref, v_ref, avg_ref, et_ref, *rest, seq, hb, n_cast):
    cast_in, (o_ref, *cast_out), (kmh_ref, kml_ref) = rest[:n_cast], rest[n_cast:2 * n_cast + 1], rest[2 * n_cast + 1:]
    _cast_slabs(cast_in, cast_out)
    qi = pl.program_id(2)
    blk_len = MOBA_BLOCK
    nb = seq // blk_len
    nb_rows = avg_ref.shape[0]
    head = lambda h: slice(h * LANES, (h + 1) * LANES)

    @pl.when(qi == 0)
    def _():
        for h in range(hb):
            km = jnp.dot(avg_ref[...], k_ref[:, head(h)], preferred_element_type=F32)
            kmh_ref[h], kml_ref[h] = _split_bf16(km)

    blk = lax.broadcasted_iota(I32, (nb_rows, blk_len), 0)
    past = jnp.where(blk < qi, 1.0, 0.0)
    own = jnp.where(blk == qi, 1.0, 0.0)
    qas = []
    for h in range(hb):
        q = q_ref[:, head(h)]
        gate_t = _nt_dot(kmh_ref[h], q) + _nt_dot(kml_ref[h], q)
        picked = _topk_rows(jnp.where(past > 0.5, gate_t, -jnp.inf), min(MOBA_TOPK, nb - 1))
        qas.append(jnp.concatenate([q, _bias_from_keep(picked * past + own)], axis=1))

    def kv_chunk(j, h):
        rows = pl.ds(pl.multiple_of(j * blk_len, blk_len), blk_len)
        ka = jnp.concatenate([k_ref[rows, head(h)], et_ref[rows, :]], axis=1)
        return ka, _with_ones(v_ref[rows, head(h)])

    row = lax.broadcasted_iota(I32, (blk_len, blk_len), 0)
    colk = lax.broadcasted_iota(I32, (blk_len, blk_len), 1)
    m0 = jnp.full((blk_len, 1), NEG_BIG, F32)
    a0 = jnp.zeros((blk_len, 2 * HEAD_DIM), F32)
    def all_heads(j, c, mask=None, penalty=None):
        new = []
        for h in range(hb):
            new += _flash_step(qas[h], *kv_chunk(j, h), c[2 * h], c[2 * h + 1], mask, penalty)
        return tuple(new)

    carry = all_heads(qi, (m0, a0) * hb, colk <= row)
    carry = all_heads(jnp.maximum(qi - 1, 0), carry, None, jnp.where(qi % 2 == 1, 0.0, NEG_BIG))
    final = lax.fori_loop(0, qi // 2, lambda i, c: all_heads(2 * i + 1, all_heads(2 * i, c)), carry)
    for h in range(hb):
        o_ref[:, head(h)] = _flash_out(final[2 * h + 1]).astype(BF16)


def _moba(qkv, avg, et_blk, bsz, seq, casts):
    m = qkv.shape[0]
    nq = seq // MOBA_BLOCK
    hb = MOBA_HEADS_PER_STEP
    assert MOBA_HEADS % hb == 0 and QM0 % hb == 0 and KM0 % hb == 0 and VM0 % hb == 0
    hw = hb * LANES
    ng = MOBA_HEADS // hb
    cast_specs, cast_shapes = _cast_specs(casts, bsz * ng * nq, lambda b, h, i: (b * ng + h) * nq + i)
    whole = lambda c0: pl.BlockSpec((seq, hw), lambda b, h, i: (b, c0 // hb + h),
                                    pipeline_mode=pl.Buffered(1))
    outs = pl.pallas_call(
        functools.partial(_moba_kernel, seq=seq, hb=hb, n_cast=len(casts)),
        grid=(bsz, ng, nq),
        in_specs=[pl.BlockSpec((MOBA_BLOCK, hw), lambda b, h, i: (b * nq + i, QM0 // hb + h)),
                  whole(KM0), whole(VM0),
                  pl.BlockSpec(avg.shape, lambda b, h, i: (0, 0)),
                  pl.BlockSpec(et_blk.shape, lambda b, h, i: (0, 0))] + cast_specs,
        out_specs=[pl.BlockSpec((MOBA_BLOCK, hw), lambda b, h, i: (b * nq + i, h))] + cast_specs,
        out_shape=[jax.ShapeDtypeStruct((m, MOBA_HEADS * HEAD_DIM), BF16)] + cast_shapes,
        scratch_shapes=[pltpu.VMEM((hb, avg.shape[0], HEAD_DIM), BF16),
                        pltpu.VMEM((hb, avg.shape[0], HEAD_DIM), BF16)],
        compiler_params=pltpu.CompilerParams(
            dimension_semantics=("parallel", "parallel", "arbitrary"),
            vmem_limit_bytes=VMEM_LIMIT),
        name="moba_attn",
    )(qkv, qkv, qkv, avg, et_blk, *[w for w, _ in casts])
    return outs[0], outs[1:]


def _outproj_kernel(on_ref, om_ref, w_ref, x_ref, g_ref, gate_ref, o_ref):
    kn = on_ref.shape[1]
    acc = (jnp.dot(on_ref[...], w_ref[:kn, :], preferred_element_type=F32)
           + jnp.dot(om_ref[...], w_ref[kn:, :], preferred_element_type=F32))
    o_ref[...] = x_ref[...] + gate_ref[...] * _rms_norm(acc, g_ref[...])


def _outproj(o_n, o_m, w_o, x2, g_post, gate, seq):
    m, d = x2.shape
    tm = min(512, seq)
    tiles_per_b = seq // tm
    return pl.pallas_call(
        _outproj_kernel,
        grid=(m // tm,),
        in_specs=[pl.BlockSpec((tm, o_n.shape[1]), lambda i: (i, 0)),
                  pl.BlockSpec((tm, o_m.shape[1]), lambda i: (i, 0)),
                  pl.BlockSpec(w_o.shape, lambda i: (0, 0)),
                  pl.BlockSpec((tm, d), lambda i: (i, 0)),
                  pl.BlockSpec((1, d), lambda i: (0, 0)),
                  pl.BlockSpec((None, 1, d), lambda i: (i // tiles_per_b, 0, 0))],
        out_specs=pl.BlockSpec((tm, d), lambda i: (i, 0)),
        out_shape=jax.ShapeDtypeStruct((m, d), F32),
        compiler_params=pltpu.CompilerParams(dimension_semantics=("parallel",),
                                             vmem_limit_bytes=VMEM_LIMIT),
        name="outproj",
    )(o_n, o_m, w_o, x2, g_post, gate)


def _ffn_kernel(x_ref, gpre_ref, sc_ref, sh_ref, wu_ref, wd_ref, gpost_ref, gate_ref,
                o_ref, h_ref):
    k = pl.program_id(1)
    last = pl.num_programs(1) - 1
    row_blocks = [slice(r, r + FFN_ROW_BLOCK) for r in range(0, x_ref.shape[0], FFN_ROW_BLOCK)]

    def mlp(h):
        u = jnp.dot(h, wu_ref[...], preferred_element_type=F32)
        u = jnp.square(jnp.maximum(u, 0.0)).astype(BF16)
        return jnp.dot(u, wd_ref[...], preferred_element_type=F32)

    @pl.when(k == 0)
    def _():
        for rows in row_blocks:
            h = _rms_norm(x_ref[rows], gpre_ref[...]) * (1.0 + sc_ref[...]) + sh_ref[...]
            h_ref[rows] = h.astype(BF16)
            o_ref[rows] = mlp(h_ref[rows])

    @pl.when(jnp.logical_and(k > 0, k < last))
    def _():
        o_ref[...] += mlp(h_ref[...])

    @pl.when(k == last)
    def _():
        for rows in row_blocks:
            f = o_ref[rows] + mlp(h_ref[rows])
            o_ref[rows] = x_ref[rows] + gate_ref[...] * _rms_norm(f, gpost_ref[...])


def _ffn(x2, g_pre, scale, shift, w_up, w_down, g_post, gate, seq):
    m, d = x2.shape
    dff = w_up.shape[1]
    tm = min(1024, seq)
    ck = 512
    assert dff // ck >= 2 and tm % FFN_ROW_BLOCK == 0
    tiles_per_b = seq // tm
    vec = pl.BlockSpec((1, d), lambda i, k: (0, 0))
    per_b = pl.BlockSpec((None, 1, d), lambda i, k: (i // tiles_per_b, 0, 0))
    return pl.pallas_call(
        _ffn_kernel,
        grid=(m // tm, dff // ck),
        in_specs=[pl.BlockSpec((tm, d), lambda i, k: (i, 0)), vec, per_b, per_b,
                  pl.BlockSpec((d, ck), lambda i, k: (0, k)),
                  pl.BlockSpec((ck, d), lambda i, k: (k, 0)),
                  vec, per_b],
        out_specs=pl.BlockSpec((tm, d), lambda i, k: (i, 0)),
        out_shape=jax.ShapeDtypeStruct((m, d), F32),
        scratch_shapes=[pltpu.VMEM((tm, d), BF16)],
        compiler_params=pltpu.CompilerParams(dimension_semantics=("parallel", "arbitrary"),
                                             vmem_limit_bytes=VMEM_LIMIT),
        name="ffn",
    )(x2, g_pre, scale, shift, w_up, w_down, g_post, gate)


def _constants(seq):
    n_cmp_pad = seq // CMP_STRIDE
    n_cmp = (seq - CMP_LEN) // CMP_STRIDE + 1
    n_sel = seq // SEL_LEN
    cs = np.arange(n_cmp) * CMP_STRIDE
    ss = np.arange(n_sel) * SEL_LEN
    overlap = np.clip(np.minimum(cs[:, None] + CMP_LEN, ss[None, :] + SEL_LEN)
                      - np.maximum(cs[:, None], ss[None, :]), 0, None)
    selmap_t = np.zeros((n_sel, n_cmp_pad), np.float32)
    selmap_t[:, :n_cmp] = (overlap / CMP_LEN).T
    keys = np.arange(seq)
    et_sel = (keys[:, None] // SEL_LEN == np.arange(LANES)[None, :]).astype(np.float32)
    et_blk = (keys[:, None] // MOBA_BLOCK == np.arange(LANES)[None, :]).astype(np.float32)
    nb_rows = -(-(seq // MOBA_BLOCK) // BF16_SUBLANES) * BF16_SUBLANES
    avg = et_blk.T[:nb_rows] / MOBA_BLOCK
    to = lambda a: jnp.asarray(a, BF16)
    return to(selmap_t), to(et_sel), to(et_blk), to(avg)


def _rope_tables(seq):
    pos = jnp.arange(seq, dtype=F32)
    inv = ROPE_THETA ** (-jnp.arange(0, HEAD_DIM, 2, dtype=F32) / HEAD_DIM)
    ang = pos[:, None] * inv[None, :]
    cos, sin = jnp.cos(ang), jnp.sin(ang)
    return jnp.concatenate([cos, cos], axis=1), jnp.concatenate([-sin, sin], axis=1)


def kernel(x, c, w_ada, b_ada, pre_norm_mix, post_norm_mix, w_in, cmp_k_pos, cmp_k_w1, cmp_k_w2,
           cmp_v_pos, cmp_v_w1, cmp_v_w2, w_o, pre_norm_ffn, post_norm_ffn, w_up, w_down):
    bsz, seq, d = x.shape
    depth = w_ada.shape[0]
    cos2, sin2 = _rope_tables(seq)
    selmap, et_sel, et_blk, avg = _constants(seq)
    x2 = x.reshape(bsz * seq, d)
    for l in range(depth):
        mod, w_head, w_tail = _adaln(c, w_ada[l], b_ada[l], w_in[l].T)
        shift_m, scale_m, gate_m, shift_f, scale_f, gate_f = [
            a.reshape(bsz, 1, d) for a in jnp.split(mod, 6, axis=-1)]
        row = lambda a: a.reshape(1, d)

        qkv, kvc = _inproj(x2, row(pre_norm_mix[l]), scale_m, shift_m, w_head, w_tail,
                           cos2, sin2, seq)
        kvc4 = kvc.reshape(2 * NSA_KV_GROUPS, bsz, seq // CMP_STRIDE, CMP_STRIDE * HEAD_DIM)
        pos2 = jnp.stack([cmp_k_pos[l].reshape(1, -1), cmp_v_pos[l].reshape(1, -1)])
        kcv = _compress(kvc4, pos2, jnp.stack([cmp_k_w1[l], cmp_v_w1[l]]),
                        jnp.stack([cmp_k_w2[l], cmp_v_w2[l]]))
        o_n, (w_o_bf,) = _nsa(qkv, kcv, selmap, et_sel, bsz, seq, [(w_o[l], 0)])
        o_m, (w_up_bf, w_down_bf) = _moba(qkv, avg, et_blk, bsz, seq, [(w_up[l], 1), (w_down[l], 0)])
        x2 = _outproj(o_n, o_m, w_o_bf, x2, row(post_norm_mix[l]), gate_m, seq)
        x2 = _ffn(x2, row(pre_norm_ffn[l]), scale_f, shift_f, w_up_bf, w_down_bf,
                  row(post_norm_ffn[l]), gate_f, seq)
    return x2.reshape(bsz, seq, d)
```

```python
import functools

import numpy as np
import jax
import jax.numpy as jnp
from jax import lax
from jax.experimental import pallas as pl
from jax.experimental.pallas import tpu as pltpu

F32 = jnp.float32
BF16 = jnp.bfloat16
I32 = jnp.int32

HEAD_DIM = 128
NSA_HEADS = 8
NSA_KV_GROUPS = 2
NSA_HPG = NSA_HEADS // NSA_KV_GROUPS
MOBA_HEADS = 8
CMP_LEN = 32
CMP_STRIDE = 16
SEL_LEN = 64
SEL_TOPK = 16
WINDOW = 512
MOBA_BLOCK = 256
MOBA_TOPK = 3
ROPE_THETA = 10000.0
EPS = 1e-6

LANES = 128
BF16_SUBLANES = 16
MXU_COLS = 256
FFN_ROW_BLOCK = 256
NSA_TQ = 256
MOBA_HEADS_PER_STEP = 8
Q_SCALE = HEAD_DIM ** -0.5 * float(np.log2(np.e))
NEG_BIG = -1e30
VMEM_LIMIT = 56 * 1024 * 1024

QN0 = 0
QM0 = 8
KM0 = 16
VM0 = 24
KC0 = 32
VC0 = 34
KS0 = 36
VS0 = 38
KW0 = 40
VW0 = 42
GT0 = 44
N_BLOCKS = 48
INPROJ_TN = 1024
HEAD_COLS = 2560
GATE_COLS = 3 * NSA_HEADS
IN_SLAB = 256
ROPE_Q, ROPE_K, PLAIN, GATES, ZERO = range(5)
_INPROJ_TILES = (
    ("head", 0, (ROPE_Q,) * 4),
    ("tail", 0, (ROPE_Q,) * 4),
    ("tail", 1, (ROPE_K,) * 4),
    ("tail", 2, (PLAIN,) * 4),
    ("head", 1, (ROPE_K, PLAIN, ROPE_K, PLAIN)),
    ("head", 2, (ROPE_K, PLAIN, GATES, ZERO)),
)


def _nt_dot(a, b):
    return lax.dot_general(a, b, (((1,), (1,)), ((), ())), preferred_element_type=F32)


def _rms_norm(x, g):
    ms = jnp.mean(x * x, axis=-1, keepdims=True)
    return x * lax.rsqrt(ms + EPS) * g


def _split_bf16(a):
    hi = a.astype(BF16)
    lo = (a - hi.astype(F32)).astype(BF16)
    return hi, lo


def _adaln_kernel(ct_ref, w_ref, b_ref, wh_ref, wa_ref, wb_ref, o_ref, head_ref, tail_ref):
    ct = ct_ref[...]
    a = ct * (1.0 / (1.0 + jnp.exp(-ct)))
    w = w_ref[...]
    for b in range(o_ref.shape[0]):
        o_ref[b] = jnp.sum(w * a[:, b:b + 1], axis=0, keepdims=True) + b_ref[...]

    t = pl.program_id(0)
    tail_ref[...] = jnp.concatenate([wa_ref[GATE_COLS:, :], wb_ref[:GATE_COLS, :]], axis=0).astype(BF16)

    n_plain = HEAD_COLS // IN_SLAB

    @pl.when(t < n_plain)
    def _():
        head_ref[...] = wh_ref[...].astype(BF16)

    @pl.when(t == n_plain)
    def _():
        per_group = 3 * NSA_HPG
        g = wh_ref[:LANES, :]
        row = lax.broadcasted_iota(I32, g.shape, 0)
        for grp in range(NSA_KV_GROUPS):
            rows = g if grp == 0 else pltpu.roll(g, LANES - grp * per_group, 0)
            head_ref[grp * LANES:(grp + 1) * LANES, :] = jnp.where(row < per_group, rows, 0.0).astype(BF16)

    @pl.when(t > n_plain)
    def _():
        head_ref[...] = jnp.zeros(head_ref.shape, BF16)


def _adaln(c, w, b, w_in_t):
    bsz, d = c.shape
    n = w.shape[1]
    tn = 1024
    steps = n // tn
    n_src = sum(1 for src, _, _ in _INPROJ_TILES if src == "head") * INPROJ_TN
    n_plain = HEAD_COLS // IN_SLAB
    assert bsz <= LANES and steps * IN_SLAB == n_src and HEAD_COLS % IN_SLAB == 0 and n_plain < steps
    assert IN_SLAB == NSA_KV_GROUPS * LANES and w_in_t.shape[0] == HEAD_COLS + GATE_COLS + n_src
    assert GATE_COLS % 8 == 0
    ct = jnp.zeros((d, LANES), F32).at[:, :bsz].set(c.T)
    slab = lambda f: pl.BlockSpec((IN_SLAB, d), lambda j: (f(j), 0))
    out, w_head, w_tail = pl.pallas_call(
        _adaln_kernel,
        grid=(steps,),
        in_specs=[pl.BlockSpec((d, LANES), lambda j: (0, 0)),
                  pl.BlockSpec((d, tn), lambda j: (0, j)),
                  pl.BlockSpec((1, tn), lambda j: (0, j)),
                  slab(lambda j: jnp.minimum(j, n_plain)),
                  slab(lambda j: n_plain + j),
                  slab(lambda j: n_plain + j + 1)],
        out_specs=[pl.BlockSpec((bsz, 1, tn), lambda j: (0, 0, j)), slab(lambda j: j), slab(lambda j: j)],
        out_shape=[jax.ShapeDtypeStruct((bsz, 1, n), F32),
                   jax.ShapeDtypeStruct((n_src, d), BF16), jax.ShapeDtypeStruct((n_src, d), BF16)],
        compiler_params=pltpu.CompilerParams(dimension_semantics=("parallel",),
                                             vmem_limit_bytes=VMEM_LIMIT),
        name="adaln",
    )(ct, w, b.reshape(1, n), w_in_t, w_in_t, w_in_t)
    return out.reshape(bsz, n), w_head, w_tail


def _inproj_kernel(x_ref, g_ref, sc_ref, sh_ref, wh_ref, wt_ref, cos_ref, sin_ref,
                   o_ref, kvc_ref, h_ref):
    j = pl.program_id(1)

    @pl.when(j == 0)
    def _():
        h = _rms_norm(x_ref[...], g_ref[...]) * (1.0 + sc_ref[...]) + sh_ref[...]
        h_ref[...] = h.astype(BF16)

    def tile(w_ref, kinds):
        for ci, kind in enumerate(kinds):
            c0 = ci * MXU_COLS
            if kind == ZERO:
                o_ref[:, c0:c0 + MXU_COLS] = jnp.zeros((o_ref.shape[0], MXU_COLS), BF16)
                continue
            acc = _nt_dot(h_ref[...], w_ref[c0:c0 + MXU_COLS, :])
            for c in range(c0, c0 + MXU_COLS, LANES):
                ch = acc[:, c - c0:c - c0 + LANES]
                if kind in (ROPE_Q, ROPE_K):
                    ch = ch * cos_ref[...] + pltpu.roll(ch, HEAD_DIM // 2, 1) * sin_ref[...]
                if kind == ROPE_Q:
                    ch = ch * Q_SCALE
                o_ref[:, c:c + LANES] = ch.astype(BF16)

    for jj, (src, _, kinds) in enumerate(_INPROJ_TILES):
        pl.when(j == jj)(functools.partial(tile, wh_ref if src == "head" else wt_ref, kinds))

    jkv, ok = divmod(KC0 * LANES, INPROJ_TN)
    ov = VC0 * LANES - jkv * INPROJ_TN
    assert 0 < ov < INPROJ_TN

    @pl.when(j == jkv)
    def _():
        for g in range(NSA_KV_GROUPS):
            kvc_ref[g] = o_ref[:, ok + g * LANES: ok + (g + 1) * LANES]
            kvc_ref[NSA_KV_GROUPS + g] = o_ref[:, ov + g * LANES: ov + (g + 1) * LANES]


def _step_lookup(j, table):
    out = jnp.int32(table[0])
    for k in range(1, len(table)):
        out = jnp.where(j >= k, table[k], out)
    return out


def _inproj(x2, g, scale, shift, w_head, w_tail, cos2, sin2, seq):
    m, d = x2.shape
    tm = min(1024, seq)
    tn = INPROJ_TN
    tiles_per_b = seq // tm
    blocks = {"head": [], "tail": []}
    for src, blk, _ in _INPROJ_TILES:
        for name, lst in blocks.items():
            lst.append(blk if src == name else (lst[-1] if lst else None))
    for lst in blocks.values():
        first = next(b for b in lst if b is not None)
        lst[:] = [first if b is None else b for b in lst]
    return pl.pallas_call(
        _inproj_kernel,
        grid=(m // tm, len(_INPROJ_TILES)),
        in_specs=[pl.BlockSpec((tm, d), lambda i, j: (i, 0)),
                  pl.BlockSpec((1, d), lambda i, j: (0, 0)),
                  pl.BlockSpec((None, 1, d), lambda i, j: (i // tiles_per_b, 0, 0)),
                  pl.BlockSpec((None, 1, d), lambda i, j: (i // tiles_per_b, 0, 0)),
                  pl.BlockSpec((tn, d), lambda i, j: (_step_lookup(j, blocks["head"]), 0)),
                  pl.BlockSpec((tn, d), lambda i, j: (_step_lookup(j, blocks["tail"]), 0)),
                  pl.BlockSpec((tm, LANES), lambda i, j: (i % tiles_per_b, 0)),
                  pl.BlockSpec((tm, LANES), lambda i, j: (i % tiles_per_b, 0))],
        out_specs=[pl.BlockSpec((tm, tn), lambda i, j: (i, j)),
                   pl.BlockSpec((2 * NSA_KV_GROUPS, tm, LANES), lambda i, j: (0, i, 0))],
        out_shape=[jax.ShapeDtypeStruct((m, N_BLOCKS * LANES), BF16),
                   jax.ShapeDtypeStruct((2 * NSA_KV_GROUPS, m, LANES), BF16)],
        scratch_shapes=[pltpu.VMEM((tm, d), BF16)],
        compiler_params=pltpu.CompilerParams(dimension_semantics=("parallel", "arbitrary"),
                                             vmem_limit_bytes=VMEM_LIMIT),
        name="inproj",
    )(x2, g, scale, shift, w_head, w_tail, cos2, sin2)


def _compress_kernel(x_ref, pos_ref, w1_ref, w2_ref, o_ref):
    half = CMP_STRIDE * HEAD_DIM
    x = x_ref[...].astype(F32)
    pos = pos_ref[...]
    w1 = w1_ref[...].astype(BF16)
    a = jnp.dot((x + pos[:, :half]).astype(BF16), w1[:half], preferred_element_type=F32)
    b = jnp.dot((x + pos[:, half:]).astype(BF16), w1[half:], preferred_element_type=F32)
    n_rows = x.shape[0]
    h1 = a + pltpu.roll(b, n_rows - 1, 0)
    h = h1 * (1.0 / (1.0 + jnp.exp(-h1)))
    o_ref[...] = jnp.dot(h.astype(BF16), w2_ref[...].astype(BF16),
                         preferred_element_type=F32).astype(BF16)


def _compress(kvc4, pos2, w1s, w2s):
    ns, bsz, n_rows, width = kvc4.shape
    return pl.pallas_call(
        _compress_kernel,
        grid=(ns, bsz),
        in_specs=[pl.BlockSpec((None, None, n_rows, width), lambda s, b: (s, b, 0, 0)),
                  pl.BlockSpec((None, 1, 2 * width), lambda s, b: (s // NSA_KV_GROUPS, 0, 0)),
                  pl.BlockSpec((None, 2 * width, HEAD_DIM), lambda s, b: (s // NSA_KV_GROUPS, 0, 0)),
                  pl.BlockSpec((None, HEAD_DIM, HEAD_DIM), lambda s, b: (s // NSA_KV_GROUPS, 0, 0))],
        out_specs=pl.BlockSpec((None, None, n_rows, HEAD_DIM), lambda s, b: (s, b, 0, 0)),
        out_shape=jax.ShapeDtypeStruct((ns, bsz, n_rows, HEAD_DIM), BF16),
        compiler_params=pltpu.CompilerParams(dimension_semantics=("parallel", "parallel"),
                                             vmem_limit_bytes=VMEM_LIMIT),
        name="compress",
    )(kvc4, pos2, w1s, w2s)


def _topk_rows(score, k):
    idx = lax.broadcasted_iota(I32, score.shape, 0).astype(F32)
    work = score
    sel = jnp.zeros(score.shape, F32)
    for _ in range(k):
        mx = jnp.max(work, axis=0, keepdims=True)
        first = jnp.min(jnp.where(work == mx, idx, float(score.shape[0])), axis=0, keepdims=True)
        hit = idx == first
        sel = jnp.where(hit, 1.0, sel)
        work = jnp.where(hit, -jnp.inf, work)
    return sel


def _bias_from_keep(keep_t):
    n, q = keep_t.shape
    padded = jnp.concatenate([keep_t, jnp.zeros((LANES - n, q), F32)], axis=0) if n < LANES else keep_t
    return jnp.where(padded.T > 0.5, 0.0, NEG_BIG).astype(BF16)


def _with_ones(v):
    return jnp.concatenate([v, jnp.ones(v.shape, v.dtype)], axis=1)


def _flash_step(qa, ka, va, m, acc, mask, penalty=None):
    s = _nt_dot(qa, ka)
    if mask is not None:
        s = jnp.where(mask, s, NEG_BIG)
    if penalty is not None:
        s = s + penalty
    m_new = jnp.maximum(m, jnp.max(s, axis=1, keepdims=True))
    alpha = jnp.exp2(m - m_new)
    p = jnp.exp2(s - m_new).astype(BF16)
    acc = alpha * acc + jnp.dot(p, va, preferred_element_type=F32)
    return m_new, acc


def _flash_out(acc):
    return acc[:, :HEAD_DIM] * (1.0 / acc[:, HEAD_DIM:])


def _cast_specs(casts, n_steps, step_of):
    specs, shapes = [], []
    for w, axis in casts:
        size = w.shape[axis] // n_steps
        assert w.shape[axis] % n_steps == 0 and size % (BF16_SUBLANES if axis == 0 else LANES) == 0
        if axis == 0:
            specs.append(pl.BlockSpec((size, w.shape[1]), lambda *g: (step_of(*g), 0)))
        else:
            specs.append(pl.BlockSpec((w.shape[0], size), lambda *g: (0, step_of(*g))))
        shapes.append(jax.ShapeDtypeStruct(w.shape, BF16))
    return specs, shapes


def _cast_slabs(src_refs, dst_refs):
    for src, dst in zip(src_refs, dst_refs):
        dst[...] = src[...].astype(BF16)


def _nsa_kernel(q_ref, kc_ref, vc_ref, ks_ref, vs_ref, kw_ref, vw_ref, gl_ref,
                selmap_ref, et_ref, *rest, tq, seq, n_cast):
    cast_in, (o_ref, *cast_out) = rest[:n_cast], rest[n_cast:]
    _cast_slabs(cast_in, cast_out)
    qi = pl.program_id(2)
    t0 = qi * tq
    rows = NSA_HPG * tq
    n_cmp_pad = kc_ref.shape[0]
    n_sel = seq // SEL_LEN

    q = q_ref[...]
    qs = jnp.concatenate([q[:, h * LANES:(h + 1) * LANES] for h in range(NSA_HPG)], axis=0)
    qrow = lax.broadcasted_iota(I32, (rows, tq), 0) & (tq - 1)
    kcol = lax.broadcasted_iota(I32, (rows, tq), 1)
    m0 = jnp.full((rows, 1), NEG_BIG, F32)
    a0 = jnp.zeros((rows, 2 * HEAD_DIM), F32)

    s = _nt_dot(qs, kc_ref[...])
    tt = t0 + (lax.broadcasted_iota(I32, (rows, n_cmp_pad), 0) & (tq - 1))
    cend = lax.broadcasted_iota(I32, (rows, n_cmp_pad), 1) * CMP_STRIDE + (CMP_LEN - 1)
    s = jnp.where(cend <= tt, s, -jnp.inf)
    mx = jnp.max(s, axis=1, keepdims=True)
    mx = jnp.where(mx == -jnp.inf, 0.0, mx)
    e = jnp.exp2(s - mx)
    den = jnp.maximum(jnp.sum(e, axis=1, keepdims=True), 1e-30)
    p_c = e * (1.0 / den)
    o_c = jnp.dot(p_c.astype(BF16), vc_ref[...], preferred_element_type=F32)

    p_sum = p_c[0:tq]
    for h in range(1, NSA_HPG):
        p_sum = p_sum + p_c[h * tq:(h + 1) * tq]
    hi, lo = _split_bf16(p_sum)
    smap_t = selmap_ref[...]
    imp_t = _nt_dot(smap_t, hi) + _nt_dot(smap_t, lo)

    n_back = WINDOW // tq
    win = (m0, a0)
    for back in range(n_back + 1):
        chunk = pl.ds(pl.multiple_of(jnp.maximum(qi - back, 0) * tq, tq), tq)
        mask = (kcol <= qrow) if back == 0 else (kcol > qrow) if back == n_back else None
        exists = None if back == 0 else jnp.where(qi >= back, 0.0, NEG_BIG)
        win = _flash_step(qs, kw_ref[chunk, :], _with_ones(vw_ref[chunk, :]), *win, mask, exists)
    o_w = _flash_out(win[1])

    blk = lax.broadcasted_iota(I32, (n_sel, tq), 0)
    cb = (t0 + lax.broadcasted_iota(I32, (n_sel, tq), 1)) // SEL_LEN
    causal = blk <= cb
    forced = (blk == 0) | (blk == cb) | (blk == cb - 1)
    n_forced = 3
    free = jnp.where(causal, jnp.where(forced, 0.0, 1.0), 0.0)
    picked = _topk_rows(jnp.where(free > 0.5, imp_t, -jnp.inf), max(min(SEL_TOPK, n_sel) - n_forced, 0))
    keep_t = jnp.where(forced, 1.0, picked * free)
    selb = _bias_from_keep(keep_t)
    qa = jnp.concatenate([qs, jnp.concatenate([selb] * NSA_HPG, axis=0)], axis=1)

    def kv_chunk(j):
        start = pl.multiple_of(j * tq, tq)
        ka = jnp.concatenate([ks_ref[pl.ds(start, tq), :], et_ref[pl.ds(start, tq), :]], axis=1)
        return ka, _with_ones(vs_ref[pl.ds(start, tq), :])

    carry = _flash_step(qa, *kv_chunk(qi), m0, a0, kcol <= qrow)
    odd = jnp.where(qi % 2 == 1, 0.0, NEG_BIG)
    carry = _flash_step(qa, *kv_chunk(jnp.maximum(qi - 1, 0)), *carry, None, odd)

    def body(i, c):
        c = _flash_step(qa, *kv_chunk(2 * i), *c, None)
        return _flash_step(qa, *kv_chunk(2 * i + 1), *c, None)

    _, a_s = lax.fori_loop(0, qi // 2, body, carry)
    o_s = _flash_out(a_s)

    gl = gl_ref[...].astype(F32)
    sig = 1.0 / (1.0 + jnp.exp(-gl))
    for h in range(NSA_HPG):
        r = slice(h * tq, (h + 1) * tq)
        o = (sig[:, 3 * h:3 * h + 1] * o_c[r] + sig[:, 3 * h + 1:3 * h + 2] * o_s[r]
             + sig[:, 3 * h + 2:3 * h + 3] * o_w[r])
        o_ref[:, h * LANES:(h + 1) * LANES] = o.astype(BF16)


def _nsa(qkv, kcv, selmap, et_sel, bsz, seq, casts):
    m = qkv.shape[0]
    tq = NSA_TQ
    assert seq % tq == 0 and WINDOW % tq == 0 and tq % SEL_LEN == 0
    nq = seq // tq
    n_cmp_pad = kcv.shape[2]
    gw = NSA_HPG * LANES
    col = lambda c0: (lambda b, g, i: (b, c0 + g))
    full = lambda c0: pl.BlockSpec((seq, LANES), col(c0))
    cast_specs, cast_shapes = _cast_specs(casts, bsz * NSA_KV_GROUPS * nq,
                                          lambda b, g, i: (b * NSA_KV_GROUPS + g) * nq + i)
    outs = pl.pallas_call(
        functools.partial(_nsa_kernel, tq=tq, seq=seq, n_cast=len(casts)),
        grid=(bsz, NSA_KV_GROUPS, nq),
        in_specs=[pl.BlockSpec((tq, gw), lambda b, g, i: (b * nq + i, g)),
                  pl.BlockSpec((None, None, n_cmp_pad, LANES), lambda b, g, i: (g, b, 0, 0)),
                  pl.BlockSpec((None, None, n_cmp_pad, LANES),
                               lambda b, g, i: (NSA_KV_GROUPS + g, b, 0, 0)),
                  full(KS0), full(VS0), full(KW0), full(VW0),
                  pl.BlockSpec((tq, LANES), lambda b, g, i: (b * nq + i, GT0 + g)),
                  pl.BlockSpec(selmap.shape, lambda b, g, i: (0, 0)),
                  pl.BlockSpec(et_sel.shape, lambda b, g, i: (0, 0))] + cast_specs,
        out_specs=[pl.BlockSpec((tq, gw), lambda b, g, i: (b * nq + i, g))] + cast_specs,
        out_shape=[jax.ShapeDtypeStruct((m, NSA_HEADS * HEAD_DIM), BF16)] + cast_shapes,
        compiler_params=pltpu.CompilerParams(
            dimension_semantics=("parallel", "parallel", "arbitrary"),
            vmem_limit_bytes=VMEM_LIMIT),
        name="nsa_attn",
    )(qkv, kcv, kcv, qkv, qkv, qkv, qkv, qkv, selmap, et_sel, *[w for w, _ in casts])
    return outs[0], outs[1:]


def _moba_kernel(q_ref, k_ref, v_ref, avg_ref, et_ref, *rest, seq, hb, n_cast):
    cast_in, (o_ref, *cast_out), (kmh_ref, kml_ref) = rest[:n_cast], rest[n_cast:2 * n_cast + 1], rest[2 * n_cast + 1:]
    _cast_slabs(cast_in, cast_out)
    qi = pl.program_id(2)
    blk_len = MOBA_BLOCK
    nb = seq // blk_len
    nb_rows = avg_ref.shape[0]
    head = lambda h: slice(h * LANES, (h + 1) * LANES)

    @pl.when(qi == 0)
    def _():
        for h in range(hb):
            km = jnp.dot(avg_ref[...], k_ref[:, head(h)], preferred_element_type=F32)
            kmh_ref[h], kml_ref[h] = _split_bf16(km)

    blk = lax.broadcasted_iota(I32, (nb_rows, blk_len), 0)
    past = jnp.where(blk < qi, 1.0, 0.0)
    own = jnp.where(blk == qi, 1.0, 0.0)
    qas = []
    for h in range(hb):
        q = q_ref[:, head(h)]
        gate_t = _nt_dot(kmh_ref[h], q) + _nt_dot(kml_ref[h], q)
        picked = _topk_rows(jnp.where(past > 0.5, gate_t, -jnp.inf), min(MOBA_TOPK, nb - 1))
        qas.append(jnp.concatenate([q, _bias_from_keep(picked * past + own)], axis=1))

    def kv_chunk(j, h):
        rows = pl.ds(pl.multiple_of(j * blk_len, blk_len), blk_len)
        ka = jnp.concatenate([k_ref[rows, head(h)], et_ref[rows, :]], axis=1)
        return ka, _with_ones(v_ref[rows, head(h)])

    row = lax.broadcasted_iota(I32, (blk_len, blk_len), 0)
    colk = lax.broadcasted_iota(I32, (blk_len, blk_len), 1)
    m0 = jnp.full((blk_len, 1), NEG_BIG, F32)
    a0 = jnp.zeros((blk_len, 2 * HEAD_DIM), F32)
    def all_heads(j, c, mask=None, penalty=None):
        new = []
        for h in range(hb):
            new += _flash_step(qas[h], *kv_chunk(j, h), c[2 * h], c[2 * h + 1], mask, penalty)
        return tuple(new)

    carry = all_heads(qi, (m0, a0) * hb, colk <= row)
    carry = all_heads(jnp.maximum(qi - 1, 0), carry, None, jnp.where(qi % 2 == 1, 0.0, NEG_BIG))
    final = lax.fori_loop(0, qi // 2, lambda i, c: all_heads(2 * i + 1, all_heads(2 * i, c)), carry)
    for h in range(hb):
        o_ref[:, head(h)] = _flash_out(final[2 * h + 1]).astype(BF16)


def _moba(qkv, avg, et_blk, bsz, seq, casts):
    m = qkv.shape[0]
    nq = seq // MOBA_BLOCK
    hb = MOBA_HEADS_PER_STEP
    assert MOBA_HEADS % hb == 0 and QM0 % hb == 0 and KM0 % hb == 0 and VM0 % hb == 0
    hw = hb * LANES
    ng = MOBA_HEADS // hb
    cast_specs, cast_shapes = _cast_specs(casts, bsz * ng * nq, lambda b, h, i: (b * ng + h) * nq + i)
    whole = lambda c0: pl.BlockSpec((seq, hw), lambda b, h, i: (b, c0 // hb + h),
                                    pipeline_mode=pl.Buffered(1))
    outs = pl.pallas_call(
        functools.partial(_moba_kernel, seq=seq, hb=hb, n_cast=len(casts)),
        grid=(bsz, ng, nq),
        in_specs=[pl.BlockSpec((MOBA_BLOCK, hw), lambda b, h, i: (b * nq + i, QM0 // hb + h)),
                  whole(KM0), whole(VM0),
                  pl.BlockSpec(avg.shape, lambda b, h, i: (0, 0)),
                  pl.BlockSpec(et_blk.shape, lambda b, h, i: (0, 0))] + cast_specs,
        out_specs=[pl.BlockSpec((MOBA_BLOCK, hw), lambda b, h, i: (b * nq + i, h))] + cast_specs,
        out_shape=[jax.ShapeDtypeStruct((m, MOBA_HEADS * HEAD_DIM), BF16)] + cast_shapes,
        scratch_shapes=[pltpu.VMEM((hb, avg.shape[0], HEAD_DIM), BF16),
                        pltpu.VMEM((hb, avg.shape[0], HEAD_DIM), BF16)],
        compiler_params=pltpu.CompilerParams(
            dimension_semantics=("parallel", "parallel", "arbitrary"),
            vmem_limit_bytes=VMEM_LIMIT),
        name="moba_attn",
    )(qkv, qkv, qkv, avg, et_blk, *[w for w, _ in casts])
    return outs[0], outs[1:]


def _outproj_kernel(on_ref, om_ref, w_ref, x_ref, g_ref, gate_ref, o_ref):
    kn = on_ref.shape[1]
    acc = (jnp.dot(on_ref[...], w_ref[:kn, :], preferred_element_type=F32)
           + jnp.dot(om_ref[...], w_ref[kn:, :], preferred_element_type=F32))
    o_ref[...] = x_ref[...] + gate_ref[...] * _rms_norm(acc, g_ref[...])


def _outproj(o_n, o_m, w_o, x2, g_post, gate, seq):
    m, d = x2.shape
    tm = min(512, seq)
    tiles_per_b = seq // tm
    return pl.pallas_call(
        _outproj_kernel,
        grid=(m // tm,),
        in_specs=[pl.BlockSpec((tm, o_n.shape[1]), lambda i: (i, 0)),
                  pl.BlockSpec((tm, o_m.shape[1]), lambda i: (i, 0)),
                  pl.BlockSpec(w_o.shape, lambda i: (0, 0)),
                  pl.BlockSpec((tm, d), lambda i: (i, 0)),
                  pl.BlockSpec((1, d), lambda i: (0, 0)),
                  pl.BlockSpec((None, 1, d), lambda i: (i // tiles_per_b, 0, 0))],
        out_specs=pl.BlockSpec((tm, d), lambda i: (i, 0)),
        out_shape=jax.ShapeDtypeStruct((m, d), F32),
        compiler_params=pltpu.CompilerParams(dimension_semantics=("parallel",),
                                             vmem_limit_bytes=VMEM_LIMIT),
        name="outproj",
    )(o_n, o_m, w_o, x2, g_post, gate)


def _ffn_kernel(x_ref, gpre_ref, sc_ref, sh_ref, wu_ref, wd_ref, gpost_ref, gate_ref,
                o_ref, h_ref):
    k = pl.program_id(1)
    last = pl.num_programs(1) - 1
    row_blocks = [slice(r, r + FFN_ROW_BLOCK) for r in range(0, x_ref.shape[0], FFN_ROW_BLOCK)]

    def mlp(h):
        u = jnp.dot(h, wu_ref[...], preferred_element_type=F32)
        u = jnp.square(jnp.maximum(u, 0.0)).astype(BF16)
        return jnp.dot(u, wd_ref[...], preferred_element_type=F32)

    @pl.when(k == 0)
    def _():
        for rows in row_blocks:
            h = _rms_norm(x_ref[rows], gpre_ref[...]) * (1.0 + sc_ref[...]) + sh_ref[...]
            h_ref[rows] = h.astype(BF16)
            o_ref[rows] = mlp(h_ref[rows])

    @pl.when(jnp.logical_and(k > 0, k < last))
    def _():
        o_ref[...] += mlp(h_ref[...])

    @pl.when(k == last)
    def _():
        for rows in row_blocks:
            f = o_ref[rows] + mlp(h_ref[rows])
            o_ref[rows] = x_ref[rows] + gate_ref[...] * _rms_norm(f, gpost_ref[...])


def _ffn(x2, g_pre, scale, shift, w_up, w_down, g_post, gate, seq):
    m, d = x2.shape
    dff = w_up.shape[1]
    tm = min(1024, seq)
    ck = 512
    assert dff // ck >= 2 and tm % FFN_ROW_BLOCK == 0
    tiles_per_b = seq // tm
    vec = pl.BlockSpec((1, d), lambda i, k: (0, 0))
    per_b = pl.BlockSpec((None, 1, d), lambda i, k: (i // tiles_per_b, 0, 0))
    return pl.pallas_call(
        _ffn_kernel,
        grid=(m // tm, dff // ck),
        in_specs=[pl.BlockSpec((tm, d), lambda i, k: (i, 0)), vec, per_b, per_b,
                  pl.BlockSpec((d, ck), lambda i, k: (0, k)),
                  pl.BlockSpec((ck, d), lambda i, k: (k, 0)),
                  vec, per_b],
        out_specs=pl.BlockSpec((tm, d), lambda i, k: (i, 0)),
        out_shape=jax.ShapeDtypeStruct((m, d), F32),
        scratch_shapes=[pltpu.VMEM((tm, d), BF16)],
        compiler_params=pltpu.CompilerParams(dimension_semantics=("parallel", "arbitrary"),
                                             vmem_limit_bytes=VMEM_LIMIT),
        name="ffn",
    )(x2, g_pre, scale, shift, w_up, w_down, g_post, gate)


def _constants(seq):
    n_cmp_pad = seq // CMP_STRIDE
    n_cmp = (seq - CMP_LEN) // CMP_STRIDE + 1
    n_sel = seq // SEL_LEN
    cs = np.arange(n_cmp) * CMP_STRIDE
    ss = np.arange(n_sel) * SEL_LEN
    overlap = np.clip(np.minimum(cs[:, None] + CMP_LEN, ss[None, :] + SEL_LEN)
                      - np.maximum(cs[:, None], ss[None, :]), 0, None)
    selmap_t = np.zeros((n_sel, n_cmp_pad), np.float32)
    selmap_t[:, :n_cmp] = (overlap / CMP_LEN).T
    keys = np.arange(seq)
    et_sel = (keys[:, None] // SEL_LEN == np.arange(LANES)[None, :]).astype(np.float32)
    et_blk = (keys[:, None] // MOBA_BLOCK == np.arange(LANES)[None, :]).astype(np.float32)
    nb_rows = -(-(seq // MOBA_BLOCK) // BF16_SUBLANES) * BF16_SUBLANES
    avg = et_blk.T[:nb_rows] / MOBA_BLOCK
    to = lambda a: jnp.asarray(a, BF16)
    return to(selmap_t), to(et_sel), to(et_blk), to(avg)


def _rope_tables(seq):
    pos = jnp.arange(seq, dtype=F32)
    inv = ROPE_THETA ** (-jnp.arange(0, HEAD_DIM, 2, dtype=F32) / HEAD_DIM)
    ang = pos[:, None] * inv[None, :]
    cos, sin = jnp.cos(ang), jnp.sin(ang)
    return jnp.concatenate([cos, cos], axis=1), jnp.concatenate([-sin, sin], axis=1)


def kernel(x, c, w_ada, b_ada, pre_norm_mix, post_norm_mix, w_in, cmp_k_pos, cmp_k_w1, cmp_k_w2,
           cmp_v_pos, cmp_v_w1, cmp_v_w2, w_o, pre_norm_ffn, post_norm_ffn, w_up, w_down):
    bsz, seq, d = x.shape
    depth = w_ada.shape[0]
    cos2, sin2 = _rope_tables(seq)
    selmap, et_sel, et_blk, avg = _constants(seq)
    x2 = x.reshape(bsz * seq, d)
    for l in range(depth):
        mod, w_head, w_tail = _adaln(c, w_ada[l], b_ada[l], w_in[l].T)
        shift_m, scale_m, gate_m, shift_f, scale_f, gate_f = [
            a.reshape(bsz, 1, d) for a in jnp.split(mod, 6, axis=-1)]
        row = lambda a: a.reshape(1, d)

        qkv, kvc = _inproj(x2, row(pre_norm_mix[l]), scale_m, shift_m, w_head, w_tail,
                           cos2, sin2, seq)
        kvc4 = kvc.reshape(2 * NSA_KV_GROUPS, bsz, seq // CMP_STRIDE, CMP_STRIDE * HEAD_DIM)
        pos2 = jnp.stack([cmp_k_pos[l].reshape(1, -1), cmp_v_pos[l].reshape(1, -1)])
        kcv = _compress(kvc4, pos2, jnp.stack([cmp_k_w1[l], cmp_v_w1[l]]),
                        jnp.stack([cmp_k_w2[l], cmp_v_w2[l]]))
        o_n, (w_o_bf,) = _nsa(qkv, kcv, selmap, et_sel, bsz, seq, [(w_o[l], 0)])
        o_m, (w_up_bf, w_down_bf) = _moba(qkv, avg, et_blk, bsz, seq, [(w_up[l], 1), (w_down[l], 0)])
        x2 = _outproj(o_n, o_m, w_o_bf, x2, row(post_norm_mix[l]), gate_m, seq)
        x2 = _ffn(x2, row(pre_norm_ffn[l]), scale_f, shift_f, w_up_bf, w_down_bf,
                  row(post_norm_ffn[l]), gate_f, seq)
    return x2.reshape(bsz, seq, d)
```

```python
import functools

import numpy as np
import jax
import jax.numpy as jnp
from jax import lax
from jax.experimental import pallas as pl
from jax.experimental.pallas import tpu as pltpu

F32 = jnp.float32
BF16 = jnp.bfloat16
I32 = jnp.int32

HEAD_DIM = 128
NSA_HEADS = 8
NSA_KV_GROUPS = 2
NSA_HPG = NSA_HEADS // NSA_KV_GROUPS
MOBA_HEADS = 8
CMP_LEN = 32
CMP_STRIDE = 16
SEL_LEN = 64
SEL_TOPK = 16
WINDOW = 512
MOBA_BLOCK = 256
MOBA_TOPK = 3
ROPE_THETA = 10000.0
EPS = 1e-6

LANES = 128
BF16_SUBLANES = 16
MXU_COLS = 256
FFN_ROW_BLOCK = 256
NSA_TQ = 256
NSA_GROUPS_PER_STEP = 2
MOBA_HEADS_PER_STEP = 8
Q_SCALE = HEAD_DIM ** -0.5 * float(np.log2(np.e))
NEG_BIG = -1e30
VMEM_LIMIT = 56 * 1024 * 1024

QN0 = 0
QM0 = 8
KM0 = 16
VM0 = 24
KC0 = 32
VC0 = 34
KS0 = 36
VS0 = 38
KW0 = 40
VW0 = 42
GT0 = 44
N_BLOCKS = 48
INPROJ_TN = 1024
HEAD_COLS = 2560
GATE_COLS = 3 * NSA_HEADS
IN_SLAB = 256
ROPE_Q, ROPE_K, PLAIN, GATES, ZERO = range(5)
_INPROJ_TILES = (
    ("head", 0, (ROPE_Q,) * 4),
    ("tail", 0, (ROPE_Q,) * 4),
    ("tail", 1, (ROPE_K,) * 4),
    ("tail", 2, (PLAIN,) * 4),
    ("head", 1, (ROPE_K, PLAIN, ROPE_K, PLAIN)),
    ("head", 2, (ROPE_K, PLAIN, GATES, ZERO)),
)


def _nt_dot(a, b):
    return lax.dot_general(a, b, (((1,), (1,)), ((), ())), preferred_element_type=F32)


def _rms_norm(x, g):
    ms = jnp.mean(x * x, axis=-1, keepdims=True)
    return x * lax.rsqrt(ms + EPS) * g


def _split_bf16(a):
    hi = a.astype(BF16)
    lo = (a - hi.astype(F32)).astype(BF16)
    return hi, lo


def _adaln_kernel(ct_ref, w_ref, b_ref, wh_ref, wa_ref, wb_ref, o_ref, head_ref, tail_ref):
    ct = ct_ref[...]
    a = ct * (1.0 / (1.0 + jnp.exp(-ct)))
    w = w_ref[...]
    for b in range(o_ref.shape[0]):
        o_ref[b] = jnp.sum(w * a[:, b:b + 1], axis=0, keepdims=True) + b_ref[...]

    t = pl.program_id(0)
    tail_ref[...] = jnp.concatenate([wa_ref[GATE_COLS:, :], wb_ref[:GATE_COLS, :]], axis=0).astype(BF16)

    n_plain = HEAD_COLS // IN_SLAB

    @pl.when(t < n_plain)
    def _():
        head_ref[...] = wh_ref[...].astype(BF16)

    @pl.when(t == n_plain)
    def _():
        per_group = 3 * NSA_HPG
        g = wh_ref[:LANES, :]
        row = lax.broadcasted_iota(I32, g.shape, 0)
        for grp in range(NSA_KV_GROUPS):
            rows = g if grp == 0 else pltpu.roll(g, LANES - grp * per_group, 0)
            head_ref[grp * LANES:(grp + 1) * LANES, :] = jnp.where(row < per_group, rows, 0.0).astype(BF16)

    @pl.when(t > n_plain)
    def _():
        head_ref[...] = jnp.zeros(head_ref.shape, BF16)


def _adaln(c, w, b, w_in_t):
    bsz, d = c.shape
    n = w.shape[1]
    tn = 1024
    steps = n // tn
    n_src = sum(1 for src, _, _ in _INPROJ_TILES if src == "head") * INPROJ_TN
    n_plain = HEAD_COLS // IN_SLAB
    assert bsz <= LANES and steps * IN_SLAB == n_src and HEAD_COLS % IN_SLAB == 0 and n_plain < steps
    assert IN_SLAB == NSA_KV_GROUPS * LANES and w_in_t.shape[0] == HEAD_COLS + GATE_COLS + n_src
    assert GATE_COLS % 8 == 0
    ct = jnp.zeros((d, LANES), F32).at[:, :bsz].set(c.T)
    slab = lambda f: pl.BlockSpec((IN_SLAB, d), lambda j: (f(j), 0))
    out, w_head, w_tail = pl.pallas_call(
        _adaln_kernel,
        grid=(steps,),
        in_specs=[pl.BlockSpec((d, LANES), lambda j: (0, 0)),
                  pl.BlockSpec((d, tn), lambda j: (0, j)),
                  pl.BlockSpec((1, tn), lambda j: (0, j)),
                  slab(lambda j: jnp.minimum(j, n_plain)),
                  slab(lambda j: n_plain + j),
                  slab(lambda j: n_plain + j + 1)],
        out_specs=[pl.BlockSpec((bsz, 1, tn), lambda j: (0, 0, j)), slab(lambda j: j), slab(lambda j: j)],
        out_shape=[jax.ShapeDtypeStruct((bsz, 1, n), F32),
                   jax.ShapeDtypeStruct((n_src, d), BF16), jax.ShapeDtypeStruct((n_src, d), BF16)],
        compiler_params=pltpu.CompilerParams(dimension_semantics=("parallel",),
                                             vmem_limit_bytes=VMEM_LIMIT),
        name="adaln",
    )(ct, w, b.reshape(1, n), w_in_t, w_in_t, w_in_t)
    return out.reshape(bsz, n), w_head, w_tail


def _inproj_kernel(x_ref, g_ref, sc_ref, sh_ref, wh_ref, wt_ref, cos_ref, sin_ref,
                   o_ref, kvc_ref, h_ref):
    j = pl.program_id(1)

    @pl.when(j == 0)
    def _():
        h = _rms_norm(x_ref[...], g_ref[...]) * (1.0 + sc_ref[...]) + sh_ref[...]
        h_ref[...] = h.astype(BF16)

    def tile(w_ref, kinds):
        for ci, kind in enumerate(kinds):
            c0 = ci * MXU_COLS
            if kind == ZERO:
                o_ref[:, c0:c0 + MXU_COLS] = jnp.zeros((o_ref.shape[0], MXU_COLS), BF16)
                continue
            acc = _nt_dot(h_ref[...], w_ref[c0:c0 + MXU_COLS, :])
            for c in range(c0, c0 + MXU_COLS, LANES):
                ch = acc[:, c - c0:c - c0 + LANES]
                if kind in (ROPE_Q, ROPE_K):
                    ch = ch * cos_ref[...] + pltpu.roll(ch, HEAD_DIM // 2, 1) * sin_ref[...]
                if kind == ROPE_Q:
                    ch = ch * Q_SCALE
                o_ref[:, c:c + LANES] = ch.astype(BF16)

    for jj, (src, _, kinds) in enumerate(_INPROJ_TILES):
        pl.when(j == jj)(functools.partial(tile, wh_ref if src == "head" else wt_ref, kinds))

    jkv, ok = divmod(KC0 * LANES, INPROJ_TN)
    ov = VC0 * LANES - jkv * INPROJ_TN
    assert 0 < ov < INPROJ_TN

    @pl.when(j == jkv)
    def _():
        for g in range(NSA_KV_GROUPS):
            kvc_ref[g] = o_ref[:, ok + g * LANES: ok + (g + 1) * LANES]
            kvc_ref[NSA_KV_GROUPS + g] = o_ref[:, ov + g * LANES: ov + (g + 1) * LANES]


def _step_lookup(j, table):
    out = jnp.int32(table[0])
    for k in range(1, len(table)):
        out = jnp.where(j >= k, table[k], out)
    return out


def _inproj(x2, g, scale, shift, w_head, w_tail, cos2, sin2, seq):
    m, d = x2.shape
    tm = min(1024, seq)
    tn = INPROJ_TN
    tiles_per_b = seq // tm
    blocks = {"head": [], "tail": []}
    for src, blk, _ in _INPROJ_TILES:
        for name, lst in blocks.items():
            lst.append(blk if src == name else (lst[-1] if lst else None))
    for lst in blocks.values():
        first = next(b for b in lst if b is not None)
        lst[:] = [first if b is None else b for b in lst]
    return pl.pallas_call(
        _inproj_kernel,
        grid=(m // tm, len(_INPROJ_TILES)),
        in_specs=[pl.BlockSpec((tm, d), lambda i, j: (i, 0)),
                  pl.BlockSpec((1, d), lambda i, j: (0, 0)),
                  pl.BlockSpec((None, 1, d), lambda i, j: (i // tiles_per_b, 0, 0)),
                  pl.BlockSpec((None, 1, d), lambda i, j: (i // tiles_per_b, 0, 0)),
                  pl.BlockSpec((tn, d), lambda i, j: (_step_lookup(j, blocks["head"]), 0)),
                  pl.BlockSpec((tn, d), lambda i, j: (_step_lookup(j, blocks["tail"]), 0)),
                  pl.BlockSpec((tm, LANES), lambda i, j: (i % tiles_per_b, 0)),
                  pl.BlockSpec((tm, LANES), lambda i, j: (i % tiles_per_b, 0))],
        out_specs=[pl.BlockSpec((tm, tn), lambda i, j: (i, j)),
                   pl.BlockSpec((2 * NSA_KV_GROUPS, tm, LANES), lambda i, j: (0, i, 0))],
        out_shape=[jax.ShapeDtypeStruct((m, N_BLOCKS * LANES), BF16),
                   jax.ShapeDtypeStruct((2 * NSA_KV_GROUPS, m, LANES), BF16)],
        scratch_shapes=[pltpu.VMEM((tm, d), BF16)],
        compiler_params=pltpu.CompilerParams(dimension_semantics=("parallel", "arbitrary"),
                                             vmem_limit_bytes=VMEM_LIMIT),
        name="inproj",
    )(x2, g, scale, shift, w_head, w_tail, cos2, sin2)


def _compress_kernel(x_ref, pos_ref, w1_ref, w2_ref, o_ref):
    half = CMP_STRIDE * HEAD_DIM
    x = x_ref[...].astype(F32)
    pos = pos_ref[...]
    w1 = w1_ref[...].astype(BF16)
    a = jnp.dot((x + pos[:, :half]).astype(BF16), w1[:half], preferred_element_type=F32)
    b = jnp.dot((x + pos[:, half:]).astype(BF16), w1[half:], preferred_element_type=F32)
    n_rows = x.shape[0]
    h1 = a + pltpu.roll(b, n_rows - 1, 0)
    h = h1 * (1.0 / (1.0 + jnp.exp(-h1)))
    o_ref[...] = jnp.dot(h.astype(BF16), w2_ref[...].astype(BF16),
                         preferred_element_type=F32).astype(BF16)


def _compress(kvc4, pos2, w1s, w2s):
    ns, bsz, n_rows, width = kvc4.shape
    return pl.pallas_call(
        _compress_kernel,
        grid=(ns, bsz),
        in_specs=[pl.BlockSpec((None, None, n_rows, width), lambda s, b: (s, b, 0, 0)),
                  pl.BlockSpec((None, 1, 2 * width), lambda s, b: (s // NSA_KV_GROUPS, 0, 0)),
                  pl.BlockSpec((None, 2 * width, HEAD_DIM), lambda s, b: (s // NSA_KV_GROUPS, 0, 0)),
                  pl.BlockSpec((None, HEAD_DIM, HEAD_DIM), lambda s, b: (s // NSA_KV_GROUPS, 0, 0))],
        out_specs=pl.BlockSpec((None, None, n_rows, HEAD_DIM), lambda s, b: (s, b, 0, 0)),
        out_shape=jax.ShapeDtypeStruct((ns, bsz, n_rows, HEAD_DIM), BF16),
        compiler_params=pltpu.CompilerParams(dimension_semantics=("parallel", "parallel"),
                                             vmem_limit_bytes=VMEM_LIMIT),
        name="compress",
    )(kvc4, pos2, w1s, w2s)


def _topk_rows(score, k):
    idx = lax.broadcasted_iota(I32, score.shape, 0).astype(F32)
    work = score
    sel = jnp.zeros(score.shape, F32)
    for _ in range(k):
        mx = jnp.max(work, axis=0, keepdims=True)
        first = jnp.min(jnp.where(work == mx, idx, float(score.shape[0])), axis=0, keepdims=True)
        hit = idx == first
        sel = jnp.where(hit, 1.0, sel)
        work = jnp.where(hit, -jnp.inf, work)
    return sel


def _bias_from_keep(keep_t):
    n, q = keep_t.shape
    padded = jnp.concatenate([keep_t, jnp.zeros((LANES - n, q), F32)], axis=0) if n < LANES else keep_t
    return jnp.where(padded.T > 0.5, 0.0, NEG_BIG).astype(BF16)


def _with_ones(v):
    return jnp.concatenate([v, jnp.ones(v.shape, v.dtype)], axis=1)


def _flash_step(qa, ka, va, m, acc, mask, penalty=None):
    s = _nt_dot(qa, ka)
    if mask is not None:
        s = jnp.where(mask, s, NEG_BIG)
    if penalty is not None:
        s = s + penalty
    m_new = jnp.maximum(m, jnp.max(s, axis=1, keepdims=True))
    alpha = jnp.exp2(m - m_new)
    p = jnp.exp2(s - m_new).astype(BF16)
    acc = alpha * acc + jnp.dot(p, va, preferred_element_type=F32)
    return m_new, acc


def _flash_out(acc):
    return acc[:, :HEAD_DIM] * (1.0 / acc[:, HEAD_DIM:])


def _cast_specs(casts, n_steps, step_of):
    specs, shapes = [], []
    for w, axis in casts:
        size = w.shape[axis] // n_steps
        assert w.shape[axis] % n_steps == 0 and size % (BF16_SUBLANES if axis == 0 else LANES) == 0
        if axis == 0:
            specs.append(pl.BlockSpec((size, w.shape[1]), lambda *g: (step_of(*g), 0)))
        else:
            specs.append(pl.BlockSpec((w.shape[0], size), lambda *g: (0, step_of(*g))))
        shapes.append(jax.ShapeDtypeStruct(w.shape, BF16))
    return specs, shapes


def _cast_slabs(src_refs, dst_refs):
    for src, dst in zip(src_refs, dst_refs):
        dst[...] = src[...].astype(BF16)


def _nsa_kernel(q_ref, kc_ref, vc_ref, ks_ref, vs_ref, kw_ref, vw_ref, gl_ref,
                selmap_ref, et_ref, *rest, tq, seq, ng, n_cast):
    cast_in, (o_ref, *cast_out) = rest[:n_cast], rest[n_cast:]
    _cast_slabs(cast_in, cast_out)
    qi = pl.program_id(2)
    t0 = qi * tq
    rows = NSA_HPG * tq
    n_cmp_pad = kc_ref.shape[1]
    n_sel = seq // SEL_LEN
    gw = NSA_HPG * LANES
    grp = lambda g: slice(g * LANES, (g + 1) * LANES)

    qrow = lax.broadcasted_iota(I32, (rows, tq), 0) & (tq - 1)
    kcol = lax.broadcasted_iota(I32, (rows, tq), 1)
    m0 = jnp.full((rows, 1), NEG_BIG, F32)
    a0 = jnp.zeros((rows, 2 * HEAD_DIM), F32)
    tt = t0 + (lax.broadcasted_iota(I32, (rows, n_cmp_pad), 0) & (tq - 1))
    cend = lax.broadcasted_iota(I32, (rows, n_cmp_pad), 1) * CMP_STRIDE + (CMP_LEN - 1)
    blk = lax.broadcasted_iota(I32, (n_sel, tq), 0)
    cb = (t0 + lax.broadcasted_iota(I32, (n_sel, tq), 1)) // SEL_LEN
    causal = blk <= cb
    forced = (blk == 0) | (blk == cb) | (blk == cb - 1)
    free = jnp.where(causal, jnp.where(forced, 0.0, 1.0), 0.0)
    n_forced = 3
    n_back = WINDOW // tq
    chunk = lambda j: pl.ds(pl.multiple_of(j * tq, tq), tq)

    def kv_chunk(g, j):
        ka = jnp.concatenate([ks_ref[chunk(j), grp(g)], et_ref[chunk(j), :]], axis=1)
        return ka, _with_ones(vs_ref[chunk(j), grp(g)])

    def front(g):
        q = q_ref[:, g * gw:(g + 1) * gw]
        qs = jnp.concatenate([q[:, h * LANES:(h + 1) * LANES] for h in range(NSA_HPG)], axis=0)

        s = jnp.where(cend <= tt, _nt_dot(qs, kc_ref[g]), -jnp.inf)
        mx = jnp.max(s, axis=1, keepdims=True)
        mx = jnp.where(mx == -jnp.inf, 0.0, mx)
        e = jnp.exp2(s - mx)
        den = jnp.maximum(jnp.sum(e, axis=1, keepdims=True), 1e-30)
        p_c = e * (1.0 / den)
        o_c = jnp.dot(p_c.astype(BF16), vc_ref[g], preferred_element_type=F32)

        p_sum = p_c[0:tq]
        for h in range(1, NSA_HPG):
            p_sum = p_sum + p_c[h * tq:(h + 1) * tq]
        hi, lo = _split_bf16(p_sum)
        smap_t = selmap_ref[...]
        imp_t = _nt_dot(smap_t, hi) + _nt_dot(smap_t, lo)

        win = (m0, a0)
        for back in range(n_back + 1):
            rows_b = chunk(jnp.maximum(qi - back, 0))
            mask = (kcol <= qrow) if back == 0 else (kcol > qrow) if back == n_back else None
            exists = None if back == 0 else jnp.where(qi >= back, 0.0, NEG_BIG)
            win = _flash_step(qs, kw_ref[rows_b, grp(g)], _with_ones(vw_ref[rows_b, grp(g)]),
                              *win, mask, exists)
        o_w = _flash_out(win[1])

        picked = _topk_rows(jnp.where(free > 0.5, imp_t, -jnp.inf),
                            max(min(SEL_TOPK, n_sel) - n_forced, 0))
        selb = _bias_from_keep(jnp.where(forced, 1.0, picked * free))
        qa = jnp.concatenate([qs, jnp.concatenate([selb] * NSA_HPG, axis=0)], axis=1)

        carry = _flash_step(qa, *kv_chunk(g, qi), m0, a0, kcol <= qrow)
        odd = jnp.where(qi % 2 == 1, 0.0, NEG_BIG)
        carry = _flash_step(qa, *kv_chunk(g, jnp.maximum(qi - 1, 0)), *carry, None, odd)
        return qa, o_c, o_w, carry

    fronts = [front(g) for g in range(ng)]

    def body(i, c):
        new = []
        for g in range(ng):
            cg = _flash_step(fronts[g][0], *kv_chunk(g, 2 * i), c[2 * g], c[2 * g + 1], None)
            new += _flash_step(fronts[g][0], *kv_chunk(g, 2 * i + 1), *cg, None)
        return tuple(new)

    final = lax.fori_loop(0, qi // 2, body, tuple(x for f in fronts for x in f[3]))

    for g in range(ng):
        _, o_c, o_w, _ = fronts[g]
        o_s = _flash_out(final[2 * g + 1])
        sig = 1.0 / (1.0 + jnp.exp(-gl_ref[:, grp(g)].astype(F32)))
        for h in range(NSA_HPG):
            r = slice(h * tq, (h + 1) * tq)
            o = (sig[:, 3 * h:3 * h + 1] * o_c[r] + sig[:, 3 * h + 1:3 * h + 2] * o_s[r]
                 + sig[:, 3 * h + 2:3 * h + 3] * o_w[r])
            o_ref[:, g * gw + h * LANES:g * gw + (h + 1) * LANES] = o.astype(BF16)


def _nsa(qkv, kcv, selmap, et_sel, bsz, seq, casts):
    m = qkv.shape[0]
    tq = NSA_TQ
    ng = NSA_GROUPS_PER_STEP
    assert seq % tq == 0 and WINDOW % tq == 0 and tq % SEL_LEN == 0 and NSA_KV_GROUPS % ng == 0
    assert all(c0 % ng == 0 for c0 in (KS0, VS0, KW0, VW0, GT0))
    nq = seq // tq
    ngs = NSA_KV_GROUPS // ng
    n_cmp_pad = kcv.shape[2]
    gw = ng * NSA_HPG * LANES
    full = lambda c0: pl.BlockSpec((seq, ng * LANES), lambda b, g, i: (b, c0 // ng + g))
    cast_specs, cast_shapes = _cast_specs(casts, bsz * ngs * nq, lambda b, g, i: (b * ngs + g) * nq + i)
    outs = pl.pallas_call(
        functools.partial(_nsa_kernel, tq=tq, seq=seq, ng=ng, n_cast=len(casts)),
        grid=(bsz, ngs, nq),
        in_specs=[pl.BlockSpec((tq, gw), lambda b, g, i: (b * nq + i, g)),
                  pl.BlockSpec((ng, None, n_cmp_pad, LANES), lambda b, g, i: (g, b, 0, 0)),
                  pl.BlockSpec((ng, None, n_cmp_pad, LANES), lambda b, g, i: (ngs + g, b, 0, 0)),
                  full(KS0), full(VS0), full(KW0), full(VW0),
                  pl.BlockSpec((tq, ng * LANES), lambda b, g, i: (b * nq + i, GT0 // ng + g)),
                  pl.BlockSpec(selmap.shape, lambda b, g, i: (0, 0)),
                  pl.BlockSpec(et_sel.shape, lambda b, g, i: (0, 0))] + cast_specs,
        out_specs=[pl.BlockSpec((tq, gw), lambda b, g, i: (b * nq + i, g))] + cast_specs,
        out_shape=[jax.ShapeDtypeStruct((m, NSA_HEADS * HEAD_DIM), BF16)] + cast_shapes,
        compiler_params=pltpu.CompilerParams(
            dimension_semantics=("parallel", "parallel", "arbitrary"),
            vmem_limit_bytes=VMEM_LIMIT),
        name="nsa_attn",
    )(qkv, kcv, kcv, qkv, qkv, qkv, qkv, qkv, selmap, et_sel, *[w for w, _ in casts])
    return outs[0], outs[1:]


def _moba_kernel(q_ref, k_ref, v_ref, avg_ref, et_ref, *rest, seq, hb, n_cast):
    cast_in, (o_ref, *cast_out), (kmh_ref, kml_ref) = rest[:n_cast], rest[n_cast:2 * n_cast + 1], rest[2 * n_cast + 1:]
    _cast_slabs(cast_in, cast_out)
    qi = pl.program_id(2)
    blk_len = MOBA_BLOCK
    nb = seq // blk_len
    nb_rows = avg_ref.shape[0]
    head = lambda h: slice(h * LANES, (h + 1) * LANES)

    @pl.when(qi == 0)
    def _():
        for h in range(hb):
            km = jnp.dot(avg_ref[...], k_ref[:, head(h)], preferred_element_type=F32)
            kmh_ref[h], kml_ref[h] = _split_bf16(km)

    blk = lax.broadcasted_iota(I32, (nb_rows, blk_len), 0)
    past = jnp.where(blk < qi, 1.0, 0.0)
    own = jnp.where(blk == qi, 1.0, 0.0)
    qas = []
    for h in range(hb):
        q = q_ref[:, head(h)]
        gate_t = _nt_dot(kmh_ref[h], q) + _nt_dot(kml_ref[h], q)
        picked = _topk_rows(jnp.where(past > 0.5, gate_t, -jnp.inf), min(MOBA_TOPK, nb - 1))
        qas.append(jnp.concatenate([q, _bias_from_keep(picked * past + own)], axis=1))

    def kv_chunk(j, h):
        rows = pl.ds(pl.multiple_of(j * blk_len, blk_len), blk_len)
        ka = jnp.concatenate([k_ref[rows, head(h)], et_ref[rows, :]], axis=1)
        return ka, _with_ones(v_ref[rows, head(h)])

    row = lax.broadcasted_iota(I32, (blk_len, blk_len), 0)
    colk = lax.broadcasted_iota(I32, (blk_len, blk_len), 1)
    m0 = jnp.full((blk_len, 1), NEG_BIG, F32)
    a0 = jnp.zeros((blk_len, 2 * HEAD_DIM), F32)
    def all_heads(j, c, mask=None, penalty=None):
        new = []
        for h in range(hb):
            new += _flash_step(qas[h], *kv_chunk(j, h), c[2 * h], c[2 * h + 1], mask, penalty)
        return tuple(new)

    carry = all_heads(qi, (m0, a0) * hb, colk <= row)
    carry = all_heads(jnp.maximum(qi - 1, 0), carry, None, jnp.where(qi % 2 == 1, 0.0, NEG_BIG))
    final = lax.fori_loop(0, qi // 2, lambda i, c: all_heads(2 * i + 1, all_heads(2 * i, c)), carry)
    for h in range(hb):
        o_ref[:, head(h)] = _flash_out(final[2 * h + 1]).astype(BF16)


def _moba(qkv, avg, et_blk, bsz, seq, casts):
    m = qkv.shape[0]
    nq = seq // MOBA_BLOCK
    hb = MOBA_HEADS_PER_STEP
    assert MOBA_HEADS % hb == 0 and QM0 % hb == 0 and KM0 % hb == 0 and VM0 % hb == 0
    hw = hb * LANES
    ng = MOBA_HEADS // hb
    cast_specs, cast_shapes = _cast_specs(casts, bsz * ng * nq, lambda b, h, i: (b * ng + h) * nq + i)
    whole = lambda c0: pl.BlockSpec((seq, hw), lambda b, h, i: (b, c0 // hb + h),
                                    pipeline_mode=pl.Buffered(1))
    outs = pl.pallas_call(
        functools.partial(_moba_kernel, seq=seq, hb=hb, n_cast=len(casts)),
        grid=(bsz, ng, nq),
        in_specs=[pl.BlockSpec((MOBA_BLOCK, hw), lambda b, h, i: (b * nq + i, QM0 // hb + h)),
                  whole(KM0), whole(VM0),
                  pl.BlockSpec(avg.shape, lambda b, h, i: (0, 0)),
                  pl.BlockSpec(et_blk.shape, lambda b, h, i: (0, 0))] + cast_specs,
        out_specs=[pl.BlockSpec((MOBA_BLOCK, hw), lambda b, h, i: (b * nq + i, h))] + cast_specs,
        out_shape=[jax.ShapeDtypeStruct((m, MOBA_HEADS * HEAD_DIM), BF16)] + cast_shapes,
        scratch_shapes=[pltpu.VMEM((hb, avg.shape[0], HEAD_DIM), BF16),
                        pltpu.VMEM((hb, avg.shape[0], HEAD_DIM), BF16)],
        compiler_params=pltpu.CompilerParams(
            dimension_semantics=("parallel", "parallel", "arbitrary"),
            vmem_limit_bytes=VMEM_LIMIT),
        name="moba_attn",
    )(qkv, qkv, qkv, avg, et_blk, *[w for w, _ in casts])
    return outs[0], outs[1:]


def _outproj_kernel(on_ref, om_ref, w_ref, x_ref, g_ref, gate_ref, o_ref):
    kn = on_ref.shape[1]
    acc = (jnp.dot(on_ref[...], w_ref[:kn, :], preferred_element_type=F32)
           + jnp.dot(om_ref[...], w_ref[kn:, :], preferred_element_type=F32))
    o_ref[...] = x_ref[...] + gate_ref[...] * _rms_norm(acc, g_ref[...])


def _outproj(o_n, o_m, w_o, x2, g_post, gate, seq):
    m, d = x2.shape
    tm = min(512, seq)
    tiles_per_b = seq // tm
    return pl.pallas_call(
        _outproj_kernel,
        grid=(m // tm,),
        in_specs=[pl.BlockSpec((tm, o_n.shape[1]), lambda i: (i, 0)),
                  pl.BlockSpec((tm, o_m.shape[1]), lambda i: (i, 0)),
                  pl.BlockSpec(w_o.shape, lambda i: (0, 0)),
                  pl.BlockSpec((tm, d), lambda i: (i, 0)),
                  pl.BlockSpec((1, d), lambda i: (0, 0)),
                  pl.BlockSpec((None, 1, d), lambda i: (i // tiles_per_b, 0, 0))],
        out_specs=pl.BlockSpec((tm, d), lambda i: (i, 0)),
        out_shape=jax.ShapeDtypeStruct((m, d), F32),
        compiler_params=pltpu.CompilerParams(dimension_semantics=("parallel",),
                                             vmem_limit_bytes=VMEM_LIMIT),
        name="outproj",
    )(o_n, o_m, w_o, x2, g_post, gate)


def _ffn_kernel(x_ref, gpre_ref, sc_ref, sh_ref, wu_ref, wd_ref, gpost_ref, gate_ref,
                o_ref, h_ref):
    k = pl.program_id(1)
    last = pl.num_programs(1) - 1
    row_blocks = [slice(r, r + FFN_ROW_BLOCK) for r in range(0, x_ref.shape[0], FFN_ROW_BLOCK)]

    def mlp(h):
        u = jnp.dot(h, wu_ref[...], preferred_element_type=F32)
        u = jnp.square(jnp.maximum(u, 0.0)).astype(BF16)
        return jnp.dot(u, wd_ref[...], preferred_element_type=F32)

    @pl.when(k == 0)
    def _():
        for rows in row_blocks:
            h = _rms_norm(x_ref[rows], gpre_ref[...]) * (1.0 + sc_ref[...]) + sh_ref[...]
            h_ref[rows] = h.astype(BF16)
            o_ref[rows] = mlp(h_ref[rows])

    @pl.when(jnp.logical_and(k > 0, k < last))
    def _():
        o_ref[...] += mlp(h_ref[...])

    @pl.when(k == last)
    def _():
        for rows in row_blocks:
            f = o_ref[rows] + mlp(h_ref[rows])
            o_ref[rows] = x_ref[rows] + gate_ref[...] * _rms_norm(f, gpost_ref[...])


def _ffn(x2, g_pre, scale, shift, w_up, w_down, g_post, gate, seq):
    m, d = x2.shape
    dff = w_up.shape[1]
    tm = min(1024, seq)
    ck = 512
    assert dff // ck >= 2 and tm % FFN_ROW_BLOCK == 0
    tiles_per_b = seq // tm
    vec = pl.BlockSpec((1, d), lambda i, k: (0, 0))
    per_b = pl.BlockSpec((None, 1, d), lambda i, k: (i // tiles_per_b, 0, 0))
    return pl.pallas_call(
        _ffn_kernel,
        grid=(m // tm, dff // ck),
        in_specs=[pl.BlockSpec((tm, d), lambda i, k: (i, 0)), vec, per_b, per_b,
                  pl.BlockSpec((d, ck), lambda i, k: (0, k)),
                  pl.BlockSpec((ck, d), lambda i, k: (k, 0)),
                  vec, per_b],
        out_specs=pl.BlockSpec((tm, d), lambda i, k: (i, 0)),
        out_shape=jax.ShapeDtypeStruct((m, d), F32),
        scratch_shapes=[pltpu.VMEM((tm, d), BF16)],
        compiler_params=pltpu.CompilerParams(dimension_semantics=("parallel", "arbitrary"),
                                             vmem_limit_bytes=VMEM_LIMIT),
        name="ffn",
    )(x2, g_pre, scale, shift, w_up, w_down, g_post, gate)


def _constants(seq):
    n_cmp_pad = seq // CMP_STRIDE
    n_cmp = (seq - CMP_LEN) // CMP_STRIDE + 1
    n_sel = seq // SEL_LEN
    cs = np.arange(n_cmp) * CMP_STRIDE
    ss = np.arange(n_sel) * SEL_LEN
    overlap = np.clip(np.minimum(cs[:, None] + CMP_LEN, ss[None, :] + SEL_LEN)
                      - np.maximum(cs[:, None], ss[None, :]), 0, None)
    selmap_t = np.zeros((n_sel, n_cmp_pad), np.float32)
    selmap_t[:, :n_cmp] = (overlap / CMP_LEN).T
    keys = np.arange(seq)
    et_sel = (keys[:, None] // SEL_LEN == np.arange(LANES)[None, :]).astype(np.float32)
    et_blk = (keys[:, None] // MOBA_BLOCK == np.arange(LANES)[None, :]).astype(np.float32)
    nb_rows = -(-(seq // MOBA_BLOCK) // BF16_SUBLANES) * BF16_SUBLANES
    avg = et_blk.T[:nb_rows] / MOBA_BLOCK
    to = lambda a: jnp.asarray(a, BF16)
    return to(selmap_t), to(et_sel), to(et_blk), to(avg)


def _rope_tables(seq):
    pos = jnp.arange(seq, dtype=F32)
    inv = ROPE_THETA ** (-jnp.arange(0, HEAD_DIM, 2, dtype=F32) / HEAD_DIM)
    ang = pos[:, None] * inv[None, :]
    cos, sin = jnp.cos(ang), jnp.sin(ang)
    return jnp.concatenate([cos, cos], axis=1), jnp.concatenate([-sin, sin], axis=1)


def kernel(x, c, w_ada, b_ada, pre_norm_mix, post_norm_mix, w_in, cmp_k_pos, cmp_k_w1, cmp_k_w2,
           cmp_v_pos, cmp_v_w1, cmp_v_w2, w_o, pre_norm_ffn, post_norm_ffn, w_up, w_down):
    bsz, seq, d = x.shape
    depth = w_ada.shape[0]
    cos2, sin2 = _rope_tables(seq)
    selmap, et_sel, et_blk, avg = _constants(seq)
    x2 = x.reshape(bsz * seq, d)
    for l in range(depth):
        mod, w_head, w_tail = _adaln(c, w_ada[l], b_ada[l], w_in[l].T)
        shift_m, scale_m, gate_m, shift_f, scale_f, gate_f = [
            a.reshape(bsz, 1, d) for a in jnp.split(mod, 6, axis=-1)]
        row = lambda a: a.reshape(1, d)

        qkv, kvc = _inproj(x2, row(pre_norm_mix[l]), scale_m, shift_m, w_head, w_tail,
                           cos2, sin2, seq)
        kvc4 = kvc.reshape(2 * NSA_KV_GROUPS, bsz, seq // CMP_STRIDE, CMP_STRIDE * HEAD_DIM)
        pos2 = jnp.stack([cmp_k_pos[l].reshape(1, -1), cmp_v_pos[l].reshape(1, -1)])
        kcv = _compress(kvc4, pos2, jnp.stack([cmp_k_w1[l], cmp_v_w1[l]]),
                        jnp.stack([cmp_k_w2[l], cmp_v_w2[l]]))
        o_n, (w_o_bf,) = _nsa(qkv, kcv, selmap, et_sel, bsz, seq, [(w_o[l], 0)])
        o_m, (w_up_bf, w_down_bf) = _moba(qkv, avg, et_blk, bsz, seq, [(w_up[l], 1), (w_down[l], 0)])
        x2 = _outproj(o_n, o_m, w_o_bf, x2, row(post_norm_mix[l]), gate_m, seq)
        x2 = _ffn(x2, row(pre_norm_ffn[l]), scale_f, shift_f, w_up_bf, w_down_bf,
                  row(post_norm_ffn[l]), gate_f, seq)
    return x2.reshape(bsz, seq, d)
```

```python
import functools

import numpy as np
import jax
import jax.numpy as jnp
from jax import lax
from jax.experimental import pallas as pl
from jax.experimental.pallas import tpu as pltpu

F32 = jnp.float32
BF16 = jnp.bfloat16
I32 = jnp.int32

HEAD_DIM = 128
NSA_HEADS = 8
NSA_KV_GROUPS = 2
NSA_HPG = NSA_HEADS // NSA_KV_GROUPS
MOBA_HEADS = 8
CMP_LEN = 32
CMP_STRIDE = 16
SEL_LEN = 64
SEL_TOPK = 16
WINDOW = 512
MOBA_BLOCK = 256
MOBA_TOPK = 3
ROPE_THETA = 10000.0
EPS = 1e-6

LANES = 128
BF16_SUBLANES = 16
MXU_COLS = 256
FFN_ROW_BLOCK = 256
NSA_TQ = 256
NSA_GROUPS_PER_STEP = 2
MOBA_HEADS_PER_STEP = 8
Q_SCALE = HEAD_DIM ** -0.5 * float(np.log2(np.e))
NEG_BIG = -1e30
VMEM_LIMIT = 56 * 1024 * 1024

QN0 = 0
QM0 = 8
KM0 = 16
VM0 = 24
KC0 = 32
VC0 = 34
KS0 = 36
VS0 = 38
KW0 = 40
VW0 = 42
GT0 = 44
N_BLOCKS = 48
INPROJ_TN = 1024
HEAD_COLS = 2560
GATE_COLS = 3 * NSA_HEADS
IN_SLAB = 256
ROPE_Q, ROPE_K, PLAIN, GATES, ZERO = range(5)
_INPROJ_TILES = (
    ("head", 0, (ROPE_Q,) * 4),
    ("tail", 0, (ROPE_Q,) * 4),
    ("tail", 1, (ROPE_K,) * 4),
    ("tail", 2, (PLAIN,) * 4),
    ("head", 1, (ROPE_K, PLAIN, ROPE_K, PLAIN)),
    ("head", 2, (ROPE_K, PLAIN, GATES, ZERO)),
)


def _nt_dot(a, b):
    return lax.dot_general(a, b, (((1,), (1,)), ((), ())), preferred_element_type=F32)


def _rms_norm(x, g):
    ms = jnp.mean(x * x, axis=-1, keepdims=True)
    return x * lax.rsqrt(ms + EPS) * g


def _split_bf16(a):
    hi = a.astype(BF16)
    lo = (a - hi.astype(F32)).astype(BF16)
    return hi, lo


def _adaln_kernel(ct_ref, w_ref, b_ref, wh_ref, wa_ref, wb_ref, o_ref, head_ref, tail_ref):
    ct = ct_ref[...]
    a = ct * (1.0 / (1.0 + jnp.exp(-ct)))
    w = w_ref[...]
    for b in range(o_ref.shape[0]):
        o_ref[b] = jnp.sum(w * a[:, b:b + 1], axis=0, keepdims=True) + b_ref[...]

    t = pl.program_id(0)
    tail_ref[...] = jnp.concatenate([wa_ref[GATE_COLS:, :], wb_ref[:GATE_COLS, :]], axis=0).astype(BF16)

    n_plain = HEAD_COLS // IN_SLAB

    @pl.when(t < n_plain)
    def _():
        head_ref[...] = wh_ref[...].astype(BF16)

    @pl.when(t == n_plain)
    def _():
        per_group = 3 * NSA_HPG
        g = wh_ref[:LANES, :]
        row = lax.broadcasted_iota(I32, g.shape, 0)
        for grp in range(NSA_KV_GROUPS):
            rows = g if grp == 0 else pltpu.roll(g, LANES - grp * per_group, 0)
            head_ref[grp * LANES:(grp + 1) * LANES, :] = jnp.where(row < per_group, rows, 0.0).astype(BF16)

    @pl.when(t > n_plain)
    def _():
        head_ref[...] = jnp.zeros(head_ref.shape, BF16)


def _adaln(c, w, b, w_in_t):
    bsz, d = c.shape
    n = w.shape[1]
    tn = 1024
    steps = n // tn
    n_src = sum(1 for src, _, _ in _INPROJ_TILES if src == "head") * INPROJ_TN
    n_plain = HEAD_COLS // IN_SLAB
    assert bsz <= LANES and steps * IN_SLAB == n_src and HEAD_COLS % IN_SLAB == 0 and n_plain < steps
    assert IN_SLAB == NSA_KV_GROUPS * LANES and w_in_t.shape[0] == HEAD_COLS + GATE_COLS + n_src
    assert GATE_COLS % 8 == 0
    ct = jnp.zeros((d, LANES), F32).at[:, :bsz].set(c.T)
    slab = lambda f: pl.BlockSpec((IN_SLAB, d), lambda j: (f(j), 0))
    out, w_head, w_tail = pl.pallas_call(
        _adaln_kernel,
        grid=(steps,),
        in_specs=[pl.BlockSpec((d, LANES), lambda j: (0, 0)),
                  pl.BlockSpec((d, tn), lambda j: (0, j)),
                  pl.BlockSpec((1, tn), lambda j: (0, j)),
                  slab(lambda j: jnp.minimum(j, n_plain)),
                  slab(lambda j: n_plain + j),
                  slab(lambda j: n_plain + j + 1)],
        out_specs=[pl.BlockSpec((bsz, 1, tn), lambda j: (0, 0, j)), slab(lambda j: j), slab(lambda j: j)],
        out_shape=[jax.ShapeDtypeStruct((bsz, 1, n), F32),
                   jax.ShapeDtypeStruct((n_src, d), BF16), jax.ShapeDtypeStruct((n_src, d), BF16)],
        compiler_params=pltpu.CompilerParams(dimension_semantics=("parallel",),
                                             vmem_limit_bytes=VMEM_LIMIT),
        name="adaln",
    )(ct, w, b.reshape(1, n), w_in_t, w_in_t, w_in_t)
    return out.reshape(bsz, n), w_head, w_tail


def _inproj_kernel(x_ref, g_ref, sc_ref, sh_ref, wh_ref, wt_ref, cos_ref, sin_ref,
                   o_ref, kvc_ref, h_ref):
    j = pl.program_id(1)

    @pl.when(j == 0)
    def _():
        h = _rms_norm(x_ref[...], g_ref[...]) * (1.0 + sc_ref[...]) + sh_ref[...]
        h_ref[...] = h.astype(BF16)

    def tile(w_ref, kinds):
        for ci, kind in enumerate(kinds):
            c0 = ci * MXU_COLS
            if kind == ZERO:
                o_ref[:, c0:c0 + MXU_COLS] = jnp.zeros((o_ref.shape[0], MXU_COLS), BF16)
                continue
            acc = _nt_dot(h_ref[...], w_ref[c0:c0 + MXU_COLS, :])
            for c in range(c0, c0 + MXU_COLS, LANES):
                ch = acc[:, c - c0:c - c0 + LANES]
                if kind in (ROPE_Q, ROPE_K):
                    ch = ch * cos_ref[...] + pltpu.roll(ch, HEAD_DIM // 2, 1) * sin_ref[...]
                if kind == ROPE_Q:
                    ch = ch * Q_SCALE
                o_ref[:, c:c + LANES] = ch.astype(BF16)

    for jj, (src, _, kinds) in enumerate(_INPROJ_TILES):
        pl.when(j == jj)(functools.partial(tile, wh_ref if src == "head" else wt_ref, kinds))

    jkv, ok = divmod(KC0 * LANES, INPROJ_TN)
    ov = VC0 * LANES - jkv * INPROJ_TN
    assert 0 < ov < INPROJ_TN

    @pl.when(j == jkv)
    def _():
        for g in range(NSA_KV_GROUPS):
            kvc_ref[g] = o_ref[:, ok + g * LANES: ok + (g + 1) * LANES]
            kvc_ref[NSA_KV_GROUPS + g] = o_ref[:, ov + g * LANES: ov + (g + 1) * LANES]


def _step_lookup(j, table):
    out = jnp.int32(table[0])
    for k in range(1, len(table)):
        out = jnp.where(j >= k, table[k], out)
    return out


def _inproj(x2, g, scale, shift, w_head, w_tail, cos2, sin2, seq):
    m, d = x2.shape
    tm = min(1024, seq)
    tn = INPROJ_TN
    tiles_per_b = seq // tm
    blocks = {"head": [], "tail": []}
    for src, blk, _ in _INPROJ_TILES:
        for name, lst in blocks.items():
            lst.append(blk if src == name else (lst[-1] if lst else None))
    for lst in blocks.values():
        first = next(b for b in lst if b is not None)
        lst[:] = [first if b is None else b for b in lst]
    return pl.pallas_call(
        _inproj_kernel,
        grid=(m // tm, len(_INPROJ_TILES)),
        in_specs=[pl.BlockSpec((tm, d), lambda i, j: (i, 0)),
                  pl.BlockSpec((1, d), lambda i, j: (0, 0)),
                  pl.BlockSpec((None, 1, d), lambda i, j: (i // tiles_per_b, 0, 0)),
                  pl.BlockSpec((None, 1, d), lambda i, j: (i // tiles_per_b, 0, 0)),
                  pl.BlockSpec((tn, d), lambda i, j: (_step_lookup(j, blocks["head"]), 0)),
                  pl.BlockSpec((tn, d), lambda i, j: (_step_lookup(j, blocks["tail"]), 0)),
                  pl.BlockSpec((tm, LANES), lambda i, j: (i % tiles_per_b, 0)),
                  pl.BlockSpec((tm, LANES), lambda i, j: (i % tiles_per_b, 0))],
        out_specs=[pl.BlockSpec((tm, tn), lambda i, j: (i, j)),
                   pl.BlockSpec((2 * NSA_KV_GROUPS, tm, LANES), lambda i, j: (0, i, 0))],
        out_shape=[jax.ShapeDtypeStruct((m, N_BLOCKS * LANES), BF16),
                   jax.ShapeDtypeStruct((2 * NSA_KV_GROUPS, m, LANES), BF16)],
        scratch_shapes=[pltpu.VMEM((tm, d), BF16)],
        compiler_params=pltpu.CompilerParams(dimension_semantics=("parallel", "arbitrary"),
                                             vmem_limit_bytes=VMEM_LIMIT),
        name="inproj",
    )(x2, g, scale, shift, w_head, w_tail, cos2, sin2)


def _compress_kernel(x_ref, pos_ref, w1_ref, w2_ref, o_ref):
    half = CMP_STRIDE * HEAD_DIM
    x = x_ref[...].astype(F32)
    pos = pos_ref[...]
    w1 = w1_ref[...].astype(BF16)
    a = jnp.dot((x + pos[:, :half]).astype(BF16), w1[:half], preferred_element_type=F32)
    b = jnp.dot((x + pos[:, half:]).astype(BF16), w1[half:], preferred_element_type=F32)
    n_rows = x.shape[0]
    h1 = a + pltpu.roll(b, n_rows - 1, 0)
    h = h1 * (1.0 / (1.0 + jnp.exp(-h1)))
    o_ref[...] = jnp.dot(h.astype(BF16), w2_ref[...].astype(BF16),
                         preferred_element_type=F32).astype(BF16)


def _compress(kvc4, pos2, w1s, w2s):
    ns, bsz, n_rows, width = kvc4.shape
    return pl.pallas_call(
        _compress_kernel,
        grid=(ns, bsz),
        in_specs=[pl.BlockSpec((None, None, n_rows, width), lambda s, b: (s, b, 0, 0)),
                  pl.BlockSpec((None, 1, 2 * width), lambda s, b: (s // NSA_KV_GROUPS, 0, 0)),
                  pl.BlockSpec((None, 2 * width, HEAD_DIM), lambda s, b: (s // NSA_KV_GROUPS, 0, 0)),
                  pl.BlockSpec((None, HEAD_DIM, HEAD_DIM), lambda s, b: (s // NSA_KV_GROUPS, 0, 0))],
        out_specs=pl.BlockSpec((None, None, n_rows, HEAD_DIM), lambda s, b: (s, b, 0, 0)),
        out_shape=jax.ShapeDtypeStruct((ns, bsz, n_rows, HEAD_DIM), BF16),
        compiler_params=pltpu.CompilerParams(dimension_semantics=("parallel", "parallel"),
                                             vmem_limit_bytes=VMEM_LIMIT),
        name="compress",
    )(kvc4, pos2, w1s, w2s)


def _topk_rows(score, k):
    idx = lax.broadcasted_iota(I32, score.shape, 0).astype(F32)
    work = score
    sel = jnp.zeros(score.shape, F32)
    for _ in range(k):
        mx = jnp.max(work, axis=0, keepdims=True)
        first = jnp.min(jnp.where(work == mx, idx, float(score.shape[0])), axis=0, keepdims=True)
        hit = idx == first
        sel = jnp.where(hit, 1.0, sel)
        work = jnp.where(hit, -jnp.inf, work)
    return sel


def _bias_from_keep(keep_t):
    n, q = keep_t.shape
    padded = jnp.concatenate([keep_t, jnp.zeros((LANES - n, q), F32)], axis=0) if n < LANES else keep_t
    return jnp.where(padded.T > 0.5, 0.0, NEG_BIG).astype(BF16)


def _with_ones(v):
    return jnp.concatenate([v, jnp.ones(v.shape, v.dtype)], axis=1)


def _flash_step(qa, ka, va, m, acc, mask, penalty=None):
    s = _nt_dot(qa, ka)
    if mask is not None:
        s = jnp.where(mask, s, NEG_BIG)
    if penalty is not None:
        s = s + penalty
    m_new = jnp.maximum(m, jnp.max(s, axis=1, keepdims=True))
    alpha = jnp.exp2(m - m_new)
    p = jnp.exp2(s - m_new).astype(BF16)
    acc = alpha * acc + jnp.dot(p, va, preferred_element_type=F32)
    return m_new, acc


def _flash_out(acc):
    return acc[:, :HEAD_DIM] * (1.0 / acc[:, HEAD_DIM:])


def _cast_specs(casts, n_steps, step_of):
    specs, shapes = [], []
    for w, axis in casts:
        size = w.shape[axis] // n_steps
        assert w.shape[axis] % n_steps == 0 and size % (BF16_SUBLANES if axis == 0 else LANES) == 0
        if axis == 0:
            specs.append(pl.BlockSpec((size, w.shape[1]), lambda *g: (step_of(*g), 0)))
        else:
            specs.append(pl.BlockSpec((w.shape[0], size), lambda *g: (0, step_of(*g))))
        shapes.append(jax.ShapeDtypeStruct(w.shape, BF16))
    return specs, shapes


def _cast_slabs(src_refs, dst_refs):
    for src, dst in zip(src_refs, dst_refs):
        dst[...] = src[...].astype(BF16)


def _nsa_kernel(q_ref, kc_ref, vc_ref, ks_ref, vs_ref, kw_ref, vw_ref, gl_ref,
                selmap_ref, et_ref, *rest, tq, seq, ng, n_cast):
    cast_in, (o_ref, *cast_out) = rest[:n_cast], rest[n_cast:]
    _cast_slabs(cast_in, cast_out)
    tp = pl.program_id(2)
    rows = NSA_HPG * tq
    n_cmp_pad = kc_ref.shape[1]
    n_sel = seq // SEL_LEN
    gw = NSA_HPG * LANES
    grp = lambda g: slice(g * LANES, (g + 1) * LANES)

    qrow = lax.broadcasted_iota(I32, (rows, tq), 0) & (tq - 1)
    kcol = lax.broadcasted_iota(I32, (rows, tq), 1)
    m0 = jnp.full((rows, 1), NEG_BIG, F32)
    a0 = jnp.zeros((rows, 2 * HEAD_DIM), F32)
    trow = lax.broadcasted_iota(I32, (rows, n_cmp_pad), 0) & (tq - 1)
    cend = lax.broadcasted_iota(I32, (rows, n_cmp_pad), 1) * CMP_STRIDE + (CMP_LEN - 1)
    blk = lax.broadcasted_iota(I32, (n_sel, tq), 0)
    tcol = lax.broadcasted_iota(I32, (n_sel, tq), 1)
    n_forced = 3
    n_back = WINDOW // tq
    chunk = lambda j: pl.ds(pl.multiple_of(j * tq, tq), tq)

    def kv_chunk(g, j):
        ka = jnp.concatenate([ks_ref[chunk(j), grp(g)], et_ref[chunk(j), :]], axis=1)
        return ka, _with_ones(vs_ref[chunk(j), grp(g)])

    def front(g, par):
        qi = 2 * tp + par
        t0 = qi * tq
        tile = slice(par * tq, (par + 1) * tq)
        q = q_ref[tile, g * gw:(g + 1) * gw]
        qs = jnp.concatenate([q[:, h * LANES:(h + 1) * LANES] for h in range(NSA_HPG)], axis=0)
        cb = (t0 + tcol) // SEL_LEN
        forced = (blk == 0) | (blk == cb) | (blk == cb - 1)
        free = jnp.where(blk <= cb, jnp.where(forced, 0.0, 1.0), 0.0)

        s = jnp.where(cend <= t0 + trow, _nt_dot(qs, kc_ref[g]), -jnp.inf)
        mx = jnp.max(s, axis=1, keepdims=True)
        mx = jnp.where(mx == -jnp.inf, 0.0, mx)
        e = jnp.exp2(s - mx)
        den = jnp.maximum(jnp.sum(e, axis=1, keepdims=True), 1e-30)
        p_c = e * (1.0 / den)
        o_c = jnp.dot(p_c.astype(BF16), vc_ref[g], preferred_element_type=F32)

        p_sum = p_c[0:tq]
        for h in range(1, NSA_HPG):
            p_sum = p_sum + p_c[h * tq:(h + 1) * tq]
        hi, lo = _split_bf16(p_sum)
        smap_t = selmap_ref[...]
        imp_t = _nt_dot(smap_t, hi) + _nt_dot(smap_t, lo)

        win = (m0, a0)
        for back in range(n_back + 1):
            rows_b = chunk(jnp.maximum(qi - back, 0))
            mask = (kcol <= qrow) if back == 0 else (kcol > qrow) if back == n_back else None
            exists = None if back == 0 else jnp.where(qi >= back, 0.0, NEG_BIG)
            win = _flash_step(qs, kw_ref[rows_b, grp(g)], _with_ones(vw_ref[rows_b, grp(g)]),
                              *win, mask, exists)
        o_w = _flash_out(win[1])

        picked = _topk_rows(jnp.where(free > 0.5, imp_t, -jnp.inf),
                            max(min(SEL_TOPK, n_sel) - n_forced, 0))
        selb = _bias_from_keep(jnp.where(forced, 1.0, picked * free))
        qa = jnp.concatenate([qs, jnp.concatenate([selb] * NSA_HPG, axis=0)], axis=1)

        carry = _flash_step(qa, *kv_chunk(g, qi), m0, a0, kcol <= qrow)
        if par == 1:
            carry = _flash_step(qa, *kv_chunk(g, qi - 1), *carry, None)
        return qa, o_c, o_w, carry, tile

    fronts = [front(g, par) for g in range(ng) for par in range(2)]

    def body(i, c):
        new = []
        for k, f in enumerate(fronts):
            g = k // 2
            ck = _flash_step(f[0], *kv_chunk(g, 2 * i), c[2 * k], c[2 * k + 1], None)
            new += _flash_step(f[0], *kv_chunk(g, 2 * i + 1), *ck, None)
        return tuple(new)

    final = lax.fori_loop(0, tp, body, tuple(x for f in fronts for x in f[3]))

    for k, (_, o_c, o_w, _, tile) in enumerate(fronts):
        g = k // 2
        o_s = _flash_out(final[2 * k + 1])
        sig = 1.0 / (1.0 + jnp.exp(-gl_ref[tile, grp(g)].astype(F32)))
        for h in range(NSA_HPG):
            r = slice(h * tq, (h + 1) * tq)
            o = (sig[:, 3 * h:3 * h + 1] * o_c[r] + sig[:, 3 * h + 1:3 * h + 2] * o_s[r]
                 + sig[:, 3 * h + 2:3 * h + 3] * o_w[r])
            o_ref[tile, g * gw + h * LANES:g * gw + (h + 1) * LANES] = o.astype(BF16)


def _nsa(qkv, kcv, selmap, et_sel, bsz, seq, casts):
    m = qkv.shape[0]
    tq = NSA_TQ
    ng = NSA_GROUPS_PER_STEP
    assert seq % tq == 0 and WINDOW % tq == 0 and tq % SEL_LEN == 0 and NSA_KV_GROUPS % ng == 0
    assert all(c0 % ng == 0 for c0 in (KS0, VS0, KW0, VW0, GT0))
    assert seq % (2 * tq) == 0
    nq = seq // (2 * tq)
    ngs = NSA_KV_GROUPS // ng
    n_cmp_pad = kcv.shape[2]
    gw = ng * NSA_HPG * LANES
    full = lambda c0: pl.BlockSpec((seq, ng * LANES), lambda b, g, i: (b, c0 // ng + g))
    cast_specs, cast_shapes = _cast_specs(casts, bsz * ngs * nq, lambda b, g, i: (b * ngs + g) * nq + i)
    outs = pl.pallas_call(
        functools.partial(_nsa_kernel, tq=tq, seq=seq, ng=ng, n_cast=len(casts)),
        grid=(bsz, ngs, nq),
        in_specs=[pl.BlockSpec((2 * tq, gw), lambda b, g, i: (b * nq + i, g)),
                  pl.BlockSpec((ng, None, n_cmp_pad, LANES), lambda b, g, i: (g, b, 0, 0)),
                  pl.BlockSpec((ng, None, n_cmp_pad, LANES), lambda b, g, i: (ngs + g, b, 0, 0)),
                  full(KS0), full(VS0), full(KW0), full(VW0),
                  pl.BlockSpec((2 * tq, ng * LANES), lambda b, g, i: (b * nq + i, GT0 // ng + g)),
                  pl.BlockSpec(selmap.shape, lambda b, g, i: (0, 0)),
                  pl.BlockSpec(et_sel.shape, lambda b, g, i: (0, 0))] + cast_specs,
        out_specs=[pl.BlockSpec((2 * tq, gw), lambda b, g, i: (b * nq + i, g))] + cast_specs,
        out_shape=[jax.ShapeDtypeStruct((m, NSA_HEADS * HEAD_DIM), BF16)] + cast_shapes,
        compiler_params=pltpu.CompilerParams(
            dimension_semantics=("parallel", "parallel", "arbitrary"),
            vmem_limit_bytes=VMEM_LIMIT),
        name="nsa_attn",
    )(qkv, kcv, kcv, qkv, qkv, qkv, qkv, qkv, selmap, et_sel, *[w for w, _ in casts])
    return outs[0], outs[1:]


def _moba_kernel(q_ref, k_ref, v_ref, avg_ref, et_ref, *rest, seq, hb, n_cast):
    cast_in, (o_ref, *cast_out), (kmh_ref, kml_ref) = rest[:n_cast], rest[n_cast:2 * n_cast + 1], rest[2 * n_cast + 1:]
    _cast_slabs(cast_in, cast_out)
    qi = pl.program_id(2)
    blk_len = MOBA_BLOCK
    nb = seq // blk_len
    nb_rows = avg_ref.shape[0]
    head = lambda h: slice(h * LANES, (h + 1) * LANES)

    @pl.when(qi == 0)
    def _():
        for h in range(hb):
            km = jnp.dot(avg_ref[...], k_ref[:, head(h)], preferred_element_type=F32)
            kmh_ref[h], kml_ref[h] = _split_bf16(km)

    blk = lax.broadcasted_iota(I32, (nb_rows, blk_len), 0)
    past = jnp.where(blk < qi, 1.0, 0.0)
    own = jnp.where(blk == qi, 1.0, 0.0)
    qas = []
    for h in range(hb):
        q = q_ref[:, head(h)]
        gate_t = _nt_dot(kmh_ref[h], q) + _nt_dot(kml_ref[h], q)
        picked = _topk_rows(jnp.where(past > 0.5, gate_t, -jnp.inf), min(MOBA_TOPK, nb - 1))
        qas.append(jnp.concatenate([q, _bias_from_keep(picked * past + own)], axis=1))

    def kv_chunk(j, h):
        rows = pl.ds(pl.multiple_of(j * blk_len, blk_len), blk_len)
        ka = jnp.concatenate([k_ref[rows, head(h)], et_ref[rows, :]], axis=1)
        return ka, _with_ones(v_ref[rows, head(h)])

    row = lax.broadcasted_iota(I32, (blk_len, blk_len), 0)
    colk = lax.broadcasted_iota(I32, (blk_len, blk_len), 1)
    m0 = jnp.full((blk_len, 1), NEG_BIG, F32)
    a0 = jnp.zeros((blk_len, 2 * HEAD_DIM), F32)
    def all_heads(j, c, mask=None, penalty=None):
        new = []
        for h in range(hb):
            new += _flash_step(qas[h], *kv_chunk(j, h), c[2 * h], c[2 * h + 1], mask, penalty)
        return tuple(new)

    carry = all_heads(qi, (m0, a0) * hb, colk <= row)
    carry = all_heads(jnp.maximum(qi - 1, 0), carry, None, jnp.where(qi % 2 == 1, 0.0, NEG_BIG))
    final = lax.fori_loop(0, qi // 2, lambda i, c: all_heads(2 * i + 1, all_heads(2 * i, c)), carry)
    for h in range(hb):
        o_ref[:, head(h)] = _flash_out(final[2 * h + 1]).astype(BF16)


def _moba(qkv, avg, et_blk, bsz, seq, casts):
    m = qkv.shape[0]
    nq = seq // MOBA_BLOCK
    hb = MOBA_HEADS_PER_STEP
    assert MOBA_HEADS % hb == 0 and QM0 % hb == 0 and KM0 % hb == 0 and VM0 % hb == 0
    hw = hb * LANES
    ng = MOBA_HEADS // hb
    cast_specs, cast_shapes = _cast_specs(casts, bsz * ng * nq, lambda b, h, i: (b * ng + h) * nq + i)
    whole = lambda c0: pl.BlockSpec((seq, hw), lambda b, h, i: (b, c0 // hb + h),
                                    pipeline_mode=pl.Buffered(1))
    outs = pl.pallas_call(
        functools.partial(_moba_kernel, seq=seq, hb=hb, n_cast=len(casts)),
        grid=(bsz, ng, nq),
        in_specs=[pl.BlockSpec((MOBA_BLOCK, hw), lambda b, h, i: (b * nq + i, QM0 // hb + h)),
                  whole(KM0), whole(VM0),
                  pl.BlockSpec(avg.shape, lambda b, h, i: (0, 0)),
                  pl.BlockSpec(et_blk.shape, lambda b, h, i: (0, 0))] + cast_specs,
        out_specs=[pl.BlockSpec((MOBA_BLOCK, hw), lambda b, h, i: (b * nq + i, h))] + cast_specs,
        out_shape=[jax.ShapeDtypeStruct((m, MOBA_HEADS * HEAD_DIM), BF16)] + cast_shapes,
        scratch_shapes=[pltpu.VMEM((hb, avg.shape[0], HEAD_DIM), BF16),
                        pltpu.VMEM((hb, avg.shape[0], HEAD_DIM), BF16)],
        compiler_params=pltpu.CompilerParams(
            dimension_semantics=("parallel", "parallel", "arbitrary"),
            vmem_limit_bytes=VMEM_LIMIT),
        name="moba_attn",
    )(qkv, qkv, qkv, avg, et_blk, *[w for w, _ in casts])
    return outs[0], outs[1:]


def _outproj_kernel(on_ref, om_ref, w_ref, x_ref, g_ref, gate_ref, o_ref):
    kn = on_ref.shape[1]
    acc = (jnp.dot(on_ref[...], w_ref[:kn, :], preferred_element_type=F32)
           + jnp.dot(om_ref[...], w_ref[kn:, :], preferred_element_type=F32))
    o_ref[...] = x_ref[...] + gate_ref[...] * _rms_norm(acc, g_ref[...])


def _outproj(o_n, o_m, w_o, x2, g_post, gate, seq):
    m, d = x2.shape
    tm = min(512, seq)
    tiles_per_b = seq // tm
    return pl.pallas_call(
        _outproj_kernel,
        grid=(m // tm,),
        in_specs=[pl.BlockSpec((tm, o_n.shape[1]), lambda i: (i, 0)),
                  pl.BlockSpec((tm, o_m.shape[1]), lambda i: (i, 0)),
                  pl.BlockSpec(w_o.shape, lambda i: (0, 0)),
                  pl.BlockSpec((tm, d), lambda i: (i, 0)),
                  pl.BlockSpec((1, d), lambda i: (0, 0)),
                  pl.BlockSpec((None, 1, d), lambda i: (i // tiles_per_b, 0, 0))],
        out_specs=pl.BlockSpec((tm, d), lambda i: (i, 0)),
        out_shape=jax.ShapeDtypeStruct((m, d), F32),
        compiler_params=pltpu.CompilerParams(dimension_semantics=("parallel",),
                                             vmem_limit_bytes=VMEM_LIMIT),
        name="outproj",
    )(o_n, o_m, w_o, x2, g_post, gate)


def _ffn_kernel(x_ref, gpre_ref, sc_ref, sh_ref, wu_ref, wd_ref, gpost_ref, gate_ref,
                o_ref, h_ref):
    k = pl.program_id(1)
    last = pl.num_programs(1) - 1
    row_blocks = [slice(r, r + FFN_ROW_BLOCK) for r in range(0, x_ref.shape[0], FFN_ROW_BLOCK)]

    def mlp(h):
        u = jnp.dot(h, wu_ref[...], preferred_element_type=F32)
        u = jnp.square(jnp.maximum(u, 0.0)).astype(BF16)
        return jnp.dot(u, wd_ref[...], preferred_element_type=F32)

    @pl.when(k == 0)
    def _():
        for rows in row_blocks:
            h = _rms_norm(x_ref[rows], gpre_ref[...]) * (1.0 + sc_ref[...]) + sh_ref[...]
            h_ref[rows] = h.astype(BF16)
            o_ref[rows] = mlp(h_ref[rows])

    @pl.when(jnp.logical_and(k > 0, k < last))
    def _():
        o_ref[...] += mlp(h_ref[...])

    @pl.when(k == last)
    def _():
        for rows in row_blocks:
            f = o_ref[rows] + mlp(h_ref[rows])
            o_ref[rows] = x_ref[rows] + gate_ref[...] * _rms_norm(f, gpost_ref[...])


def _ffn(x2, g_pre, scale, shift, w_up, w_down, g_post, gate, seq):
    m, d = x2.shape
    dff = w_up.shape[1]
    tm = min(1024, seq)
    ck = 512
    assert dff // ck >= 2 and tm % FFN_ROW_BLOCK == 0
    tiles_per_b = seq // tm
    vec = pl.BlockSpec((1, d), lambda i, k: (0, 0))
    per_b = pl.BlockSpec((None, 1, d), lambda i, k: (i // tiles_per_b, 0, 0))
    return pl.pallas_call(
        _ffn_kernel,
        grid=(m // tm, dff // ck),
        in_specs=[pl.BlockSpec((tm, d), lambda i, k: (i, 0)), vec, per_b, per_b,
                  pl.BlockSpec((d, ck), lambda i, k: (0, k)),
                  pl.BlockSpec((ck, d), lambda i, k: (k, 0)),
                  vec, per_b],
        out_specs=pl.BlockSpec((tm, d), lambda i, k: (i, 0)),
        out_shape=jax.ShapeDtypeStruct((m, d), F32),
        scratch_shapes=[pltpu.VMEM((tm, d), BF16)],
        compiler_params=pltpu.CompilerParams(dimension_semantics=("parallel", "arbitrary"),
                                             vmem_limit_bytes=VMEM_LIMIT),
        name="ffn",
    )(x2, g_pre, scale, shift, w_up, w_down, g_post, gate)


def _constants(seq):
    n_cmp_pad = seq // CMP_STRIDE
    n_cmp = (seq - CMP_LEN) // CMP_STRIDE + 1
    n_sel = seq // SEL_LEN
    cs = np.arange(n_cmp) * CMP_STRIDE
    ss = np.arange(n_sel) * SEL_LEN
    overlap = np.clip(np.minimum(cs[:, None] + CMP_LEN, ss[None, :] + SEL_LEN)
                      - np.maximum(cs[:, None], ss[None, :]), 0, None)
    selmap_t = np.zeros((n_sel, n_cmp_pad), np.float32)
    selmap_t[:, :n_cmp] = (overlap / CMP_LEN).T
    keys = np.arange(seq)
    et_sel = (keys[:, None] // SEL_LEN == np.arange(LANES)[None, :]).astype(np.float32)
    et_blk = (keys[:, None] // MOBA_BLOCK == np.arange(LANES)[None, :]).astype(np.float32)
    nb_rows = -(-(seq // MOBA_BLOCK) // BF16_SUBLANES) * BF16_SUBLANES
    avg = et_blk.T[:nb_rows] / MOBA_BLOCK
    to = lambda a: jnp.asarray(a, BF16)
    return to(selmap_t), to(et_sel), to(et_blk), to(avg)


def _rope_tables(seq):
    pos = jnp.arange(seq, dtype=F32)
    inv = ROPE_THETA ** (-jnp.arange(0, HEAD_DIM, 2, dtype=F32) / HEAD_DIM)
    ang = pos[:, None] * inv[None, :]
    cos, sin = jnp.cos(ang), jnp.sin(ang)
    return jnp.concatenate([cos, cos], axis=1), jnp.concatenate([-sin, sin], axis=1)


def kernel(x, c, w_ada, b_ada, pre_norm_mix, post_norm_mix, w_in, cmp_k_pos, cmp_k_w1, cmp_k_w2,
           cmp_v_pos, cmp_v_w1, cmp_v_w2, w_o, pre_norm_ffn, post_norm_ffn, w_up, w_down):
    bsz, seq, d = x.shape
    depth = w_ada.shape[0]
    cos2, sin2 = _rope_tables(seq)
    selmap, et_sel, et_blk, avg = _constants(seq)
    x2 = x.reshape(bsz * seq, d)
    for l in range(depth):
        mod, w_head, w_tail = _adaln(c, w_ada[l], b_ada[l], w_in[l].T)
        shift_m, scale_m, gate_m, shift_f, scale_f, gate_f = [
            a.reshape(bsz, 1, d) for a in jnp.split(mod, 6, axis=-1)]
        row = lambda a: a.reshape(1, d)

        qkv, kvc = _inproj(x2, row(pre_norm_mix[l]), scale_m, shift_m, w_head, w_tail,
                           cos2, sin2, seq)
        kvc4 = kvc.reshape(2 * NSA_KV_GROUPS, bsz, seq // CMP_STRIDE, CMP_STRIDE * HEAD_DIM)
        pos2 = jnp.stack([cmp_k_pos[l].reshape(1, -1), cmp_v_pos[l].reshape(1, -1)])
        kcv = _compress(kvc4, pos2, jnp.stack([cmp_k_w1[l], cmp_v_w1[l]]),
                        jnp.stack([cmp_k_w2[l], cmp_v_w2[l]]))
        o_n, (w_o_bf,) = _nsa(qkv, kcv, selmap, et_sel, bsz, seq, [(w_o[l], 0)])
        o_m, (w_up_bf, w_down_bf) = _moba(qkv, avg, et_blk, bsz, seq, [(w_up[l], 1), (w_down[l], 0)])
        x2 = _outproj(o_n, o_m, w_o_bf, x2, row(post_norm_mix[l]), gate_m, seq)
        x2 = _ffn(x2, row(pre_norm_ffn[l]), scale_f, shift_f, w_up_bf, w_down_bf,
                  row(post_norm_ffn[l]), gate_f, seq)
    return x2.reshape(bsz, seq, d)
```

```python
import functools

import numpy as np
import jax
import jax.numpy as jnp
from jax import lax
from jax.experimental import pallas as pl
from jax.experimental.pallas import tpu as pltpu

F32 = jnp.float32
BF16 = jnp.bfloat16
I32 = jnp.int32

HEAD_DIM = 128
NSA_HEADS = 8
NSA_KV_GROUPS = 2
NSA_HPG = NSA_HEADS // NSA_KV_GROUPS
MOBA_HEADS = 8
CMP_LEN = 32
CMP_STRIDE = 16
SEL_LEN = 64
SEL_TOPK = 16
WINDOW = 512
MOBA_BLOCK = 256
MOBA_TOPK = 3
ROPE_THETA = 10000.0
EPS = 1e-6

LANES = 128
BF16_SUBLANES = 16
MXU_COLS = 256
FFN_ROW_BLOCK = 256
NSA_TQ = 256
NSA_GROUPS_PER_STEP = 2
MOBA_HEADS_PER_STEP = 4
Q_SCALE = HEAD_DIM ** -0.5 * float(np.log2(np.e))
NEG_BIG = -1e30
VMEM_LIMIT = 56 * 1024 * 1024

QN0 = 0
QM0 = 8
KM0 = 16
VM0 = 24
KC0 = 32
VC0 = 34
KS0 = 36
VS0 = 38
KW0 = 40
VW0 = 42
GT0 = 44
N_BLOCKS = 48
INPROJ_TN = 1024
HEAD_COLS = 2560
GATE_COLS = 3 * NSA_HEADS
IN_SLAB = 256
ROPE_Q, ROPE_K, PLAIN, GATES, ZERO = range(5)
_INPROJ_TILES = (
    ("head", 0, (ROPE_Q,) * 4),
    ("tail", 0, (ROPE_Q,) * 4),
    ("tail", 1, (ROPE_K,) * 4),
    ("tail", 2, (PLAIN,) * 4),
    ("head", 1, (ROPE_K, PLAIN, ROPE_K, PLAIN)),
    ("head", 2, (ROPE_K, PLAIN, GATES, ZERO)),
)


def _nt_dot(a, b):
    return lax.dot_general(a, b, (((1,), (1,)), ((), ())), preferred_element_type=F32)


def _rms_norm(x, g):
    ms = jnp.mean(x * x, axis=-1, keepdims=True)
    return x * lax.rsqrt(ms + EPS) * g


def _split_bf16(a):
    hi = a.astype(BF16)
    lo = (a - hi.astype(F32)).astype(BF16)
    return hi, lo


def _adaln_kernel(ct_ref, w_ref, b_ref, wh_ref, wa_ref, wb_ref, o_ref, head_ref, tail_ref):
    ct = ct_ref[...]
    a = ct * (1.0 / (1.0 + jnp.exp(-ct)))
    w = w_ref[...]
    for b in range(o_ref.shape[0]):
        o_ref[b] = jnp.sum(w * a[:, b:b + 1], axis=0, keepdims=True) + b_ref[...]

    t = pl.program_id(0)
    tail_ref[...] = jnp.concatenate([wa_ref[GATE_COLS:, :], wb_ref[:GATE_COLS, :]], axis=0).astype(BF16)

    n_plain = HEAD_COLS // IN_SLAB

    @pl.when(t < n_plain)
    def _():
        head_ref[...] = wh_ref[...].astype(BF16)

    @pl.when(t == n_plain)
    def _():
        per_group = 3 * NSA_HPG
        g = wh_ref[:LANES, :]
        row = lax.broadcasted_iota(I32, g.shape, 0)
        for grp in range(NSA_KV_GROUPS):
            rows = g if grp == 0 else pltpu.roll(g, LANES - grp * per_group, 0)
            head_ref[grp * LANES:(grp + 1) * LANES, :] = jnp.where(row < per_group, rows, 0.0).astype(BF16)

    @pl.when(t > n_plain)
    def _():
        head_ref[...] = jnp.zeros(head_ref.shape, BF16)


def _adaln(c, w, b, w_in_t):
    bsz, d = c.shape
    n = w.shape[1]
    tn = 1024
    steps = n // tn
    n_src = sum(1 for src, _, _ in _INPROJ_TILES if src == "head") * INPROJ_TN
    n_plain = HEAD_COLS // IN_SLAB
    assert bsz <= LANES and steps * IN_SLAB == n_src and HEAD_COLS % IN_SLAB == 0 and n_plain < steps
    assert IN_SLAB == NSA_KV_GROUPS * LANES and w_in_t.shape[0] == HEAD_COLS + GATE_COLS + n_src
    assert GATE_COLS % 8 == 0
    ct = jnp.zeros((d, LANES), F32).at[:, :bsz].set(c.T)
    slab = lambda f: pl.BlockSpec((IN_SLAB, d), lambda j: (f(j), 0))
    out, w_head, w_tail = pl.pallas_call(
        _adaln_kernel,
        grid=(steps,),
        in_specs=[pl.BlockSpec((d, LANES), lambda j: (0, 0)),
                  pl.BlockSpec((d, tn), lambda j: (0, j)),
                  pl.BlockSpec((1, tn), lambda j: (0, j)),
                  slab(lambda j: jnp.minimum(j, n_plain)),
                  slab(lambda j: n_plain + j),
                  slab(lambda j: n_plain + j + 1)],
        out_specs=[pl.BlockSpec((bsz, 1, tn), lambda j: (0, 0, j)), slab(lambda j: j), slab(lambda j: j)],
        out_shape=[jax.ShapeDtypeStruct((bsz, 1, n), F32),
                   jax.ShapeDtypeStruct((n_src, d), BF16), jax.ShapeDtypeStruct((n_src, d), BF16)],
        compiler_params=pltpu.CompilerParams(dimension_semantics=("parallel",),
                                             vmem_limit_bytes=VMEM_LIMIT),
        name="adaln",
    )(ct, w, b.reshape(1, n), w_in_t, w_in_t, w_in_t)
    return out.reshape(bsz, n), w_head, w_tail


def _inproj_kernel(x_ref, g_ref, sc_ref, sh_ref, wh_ref, wt_ref, cos_ref, sin_ref,
                   o_ref, kvc_ref, h_ref):
    j = pl.program_id(1)

    @pl.when(j == 0)
    def _():
        h = _rms_norm(x_ref[...], g_ref[...]) * (1.0 + sc_ref[...]) + sh_ref[...]
        h_ref[...] = h.astype(BF16)

    def tile(w_ref, kinds):
        for ci, kind in enumerate(kinds):
            c0 = ci * MXU_COLS
            if kind == ZERO:
                o_ref[:, c0:c0 + MXU_COLS] = jnp.zeros((o_ref.shape[0], MXU_COLS), BF16)
                continue
            acc = _nt_dot(h_ref[...], w_ref[c0:c0 + MXU_COLS, :])
            for c in range(c0, c0 + MXU_COLS, LANES):
                ch = acc[:, c - c0:c - c0 + LANES]
                if kind in (ROPE_Q, ROPE_K):
                    ch = ch * cos_ref[...] + pltpu.roll(ch, HEAD_DIM // 2, 1) * sin_ref[...]
                if kind == ROPE_Q:
                    ch = ch * Q_SCALE
                o_ref[:, c:c + LANES] = ch.astype(BF16)

    for jj, (src, _, kinds) in enumerate(_INPROJ_TILES):
        pl.when(j == jj)(functools.partial(tile, wh_ref if src == "head" else wt_ref, kinds))

    jkv, ok = divmod(KC0 * LANES, INPROJ_TN)
    ov = VC0 * LANES - jkv * INPROJ_TN
    assert 0 < ov < INPROJ_TN

    @pl.when(j == jkv)
    def _():
        for g in range(NSA_KV_GROUPS):
            kvc_ref[g] = o_ref[:, ok + g * LANES: ok + (g + 1) * LANES]
            kvc_ref[NSA_KV_GROUPS + g] = o_ref[:, ov + g * LANES: ov + (g + 1) * LANES]


def _step_lookup(j, table):
    out = jnp.int32(table[0])
    for k in range(1, len(table)):
        out = jnp.where(j >= k, table[k], out)
    return out


def _inproj(x2, g, scale, shift, w_head, w_tail, cos2, sin2, seq):
    m, d = x2.shape
    tm = min(1024, seq)
    tn = INPROJ_TN
    tiles_per_b = seq // tm
    blocks = {"head": [], "tail": []}
    for src, blk, _ in _INPROJ_TILES:
        for name, lst in blocks.items():
            lst.append(blk if src == name else (lst[-1] if lst else None))
    for lst in blocks.values():
        first = next(b for b in lst if b is not None)
        lst[:] = [first if b is None else b for b in lst]
    return pl.pallas_call(
        _inproj_kernel,
        grid=(m // tm, len(_INPROJ_TILES)),
        in_specs=[pl.BlockSpec((tm, d), lambda i, j: (i, 0)),
                  pl.BlockSpec((1, d), lambda i, j: (0, 0)),
                  pl.BlockSpec((None, 1, d), lambda i, j: (i // tiles_per_b, 0, 0)),
                  pl.BlockSpec((None, 1, d), lambda i, j: (i // tiles_per_b, 0, 0)),
                  pl.BlockSpec((tn, d), lambda i, j: (_step_lookup(j, blocks["head"]), 0)),
                  pl.BlockSpec((tn, d), lambda i, j: (_step_lookup(j, blocks["tail"]), 0)),
                  pl.BlockSpec((tm, LANES), lambda i, j: (i % tiles_per_b, 0)),
                  pl.BlockSpec((tm, LANES), lambda i, j: (i % tiles_per_b, 0))],
        out_specs=[pl.BlockSpec((tm, tn), lambda i, j: (i, j)),
                   pl.BlockSpec((2 * NSA_KV_GROUPS, tm, LANES), lambda i, j: (0, i, 0))],
        out_shape=[jax.ShapeDtypeStruct((m, N_BLOCKS * LANES), BF16),
                   jax.ShapeDtypeStruct((2 * NSA_KV_GROUPS, m, LANES), BF16)],
        scratch_shapes=[pltpu.VMEM((tm, d), BF16)],
        compiler_params=pltpu.CompilerParams(dimension_semantics=("parallel", "arbitrary"),
                                             vmem_limit_bytes=VMEM_LIMIT),
        name="inproj",
    )(x2, g, scale, shift, w_head, w_tail, cos2, sin2)


def _compress_kernel(x_ref, pos_ref, w1_ref, w2_ref, o_ref):
    half = CMP_STRIDE * HEAD_DIM
    x = x_ref[...].astype(F32)
    pos = pos_ref[...]
    w1 = w1_ref[...].astype(BF16)
    a = jnp.dot((x + pos[:, :half]).astype(BF16), w1[:half], preferred_element_type=F32)
    b = jnp.dot((x + pos[:, half:]).astype(BF16), w1[half:], preferred_element_type=F32)
    n_rows = x.shape[0]
    h1 = a + pltpu.roll(b, n_rows - 1, 0)
    h = h1 * (1.0 / (1.0 + jnp.exp(-h1)))
    o_ref[...] = jnp.dot(h.astype(BF16), w2_ref[...].astype(BF16),
                         preferred_element_type=F32).astype(BF16)


def _compress(kvc4, pos2, w1s, w2s):
    ns, bsz, n_rows, width = kvc4.shape
    return pl.pallas_call(
        _compress_kernel,
        grid=(ns, bsz),
        in_specs=[pl.BlockSpec((None, None, n_rows, width), lambda s, b: (s, b, 0, 0)),
                  pl.BlockSpec((None, 1, 2 * width), lambda s, b: (s // NSA_KV_GROUPS, 0, 0)),
                  pl.BlockSpec((None, 2 * width, HEAD_DIM), lambda s, b: (s // NSA_KV_GROUPS, 0, 0)),
                  pl.BlockSpec((None, HEAD_DIM, HEAD_DIM), lambda s, b: (s // NSA_KV_GROUPS, 0, 0))],
        out_specs=pl.BlockSpec((None, None, n_rows, HEAD_DIM), lambda s, b: (s, b, 0, 0)),
        out_shape=jax.ShapeDtypeStruct((ns, bsz, n_rows, HEAD_DIM), BF16),
        compiler_params=pltpu.CompilerParams(dimension_semantics=("parallel", "parallel"),
                                             vmem_limit_bytes=VMEM_LIMIT),
        name="compress",
    )(kvc4, pos2, w1s, w2s)


def _topk_rows(score, k):
    idx = lax.broadcasted_iota(I32, score.shape, 0).astype(F32)
    work = score
    sel = jnp.zeros(score.shape, F32)
    for _ in range(k):
        mx = jnp.max(work, axis=0, keepdims=True)
        first = jnp.min(jnp.where(work == mx, idx, float(score.shape[0])), axis=0, keepdims=True)
        hit = idx == first
        sel = jnp.where(hit, 1.0, sel)
        work = jnp.where(hit, -jnp.inf, work)
    return sel


def _bias_from_keep(keep_t):
    n, q = keep_t.shape
    padded = jnp.concatenate([keep_t, jnp.zeros((LANES - n, q), F32)], axis=0) if n < LANES else keep_t
    return jnp.where(padded.T > 0.5, 0.0, NEG_BIG).astype(BF16)


def _with_ones(v):
    return jnp.concatenate([v, jnp.ones(v.shape, v.dtype)], axis=1)


def _flash_step(qa, ka, va, m, acc, mask, penalty=None):
    s = _nt_dot(qa, ka)
    if mask is not None:
        s = jnp.where(mask, s, NEG_BIG)
    if penalty is not None:
        s = s + penalty
    m_new = jnp.maximum(m, jnp.max(s, axis=1, keepdims=True))
    alpha = jnp.exp2(m - m_new)
    p = jnp.exp2(s - m_new).astype(BF16)
    acc = alpha * acc + jnp.dot(p, va, preferred_element_type=F32)
    return m_new, acc


def _flash_out(acc):
    return acc[:, :HEAD_DIM] * (1.0 / acc[:, HEAD_DIM:])


def _cast_specs(casts, n_steps, step_of):
    specs, shapes = [], []
    for w, axis in casts:
        size = w.shape[axis] // n_steps
        assert w.shape[axis] % n_steps == 0 and size % (BF16_SUBLANES if axis == 0 else LANES) == 0
        if axis == 0:
            specs.append(pl.BlockSpec((size, w.shape[1]), lambda *g: (step_of(*g), 0)))
        else:
            specs.append(pl.BlockSpec((w.shape[0], size), lambda *g: (0, step_of(*g))))
        shapes.append(jax.ShapeDtypeStruct(w.shape, BF16))
    return specs, shapes


def _cast_slabs(src_refs, dst_refs):
    for src, dst in zip(src_refs, dst_refs):
        dst[...] = src[...].astype(BF16)


def _nsa_kernel(q_ref, kc_ref, vc_ref, ks_ref, vs_ref, kw_ref, vw_ref, gl_ref,
                selmap_ref, et_ref, *rest, tq, seq, ng, n_cast):
    cast_in, (o_ref, *cast_out) = rest[:n_cast], rest[n_cast:]
    _cast_slabs(cast_in, cast_out)
    tp = pl.program_id(2)
    rows = NSA_HPG * tq
    n_cmp_pad = kc_ref.shape[1]
    n_sel = seq // SEL_LEN
    gw = NSA_HPG * LANES
    grp = lambda g: slice(g * LANES, (g + 1) * LANES)

    qrow = lax.broadcasted_iota(I32, (rows, tq), 0) & (tq - 1)
    kcol = lax.broadcasted_iota(I32, (rows, tq), 1)
    m0 = jnp.full((rows, 1), NEG_BIG, F32)
    a0 = jnp.zeros((rows, 2 * HEAD_DIM), F32)
    trow = lax.broadcasted_iota(I32, (rows, n_cmp_pad), 0) & (tq - 1)
    cend = lax.broadcasted_iota(I32, (rows, n_cmp_pad), 1) * CMP_STRIDE + (CMP_LEN - 1)
    blk = lax.broadcasted_iota(I32, (n_sel, tq), 0)
    tcol = lax.broadcasted_iota(I32, (n_sel, tq), 1)
    n_forced = 3
    n_back = WINDOW // tq
    chunk = lambda j: pl.ds(pl.multiple_of(j * tq, tq), tq)

    def kv_chunk(g, j):
        ka = jnp.concatenate([ks_ref[chunk(j), grp(g)], et_ref[chunk(j), :]], axis=1)
        return ka, _with_ones(vs_ref[chunk(j), grp(g)])

    def front(g, par):
        qi = 2 * tp + par
        t0 = qi * tq
        tile = slice(par * tq, (par + 1) * tq)
        q = q_ref[tile, g * gw:(g + 1) * gw]
        qs = jnp.concatenate([q[:, h * LANES:(h + 1) * LANES] for h in range(NSA_HPG)], axis=0)
        cb = (t0 + tcol) // SEL_LEN
        forced = (blk == 0) | (blk == cb) | (blk == cb - 1)
        free = jnp.where(blk <= cb, jnp.where(forced, 0.0, 1.0), 0.0)

        s = jnp.where(cend <= t0 + trow, _nt_dot(qs, kc_ref[g]), -jnp.inf)
        mx = jnp.max(s, axis=1, keepdims=True)
        mx = jnp.where(mx == -jnp.inf, 0.0, mx)
        e = jnp.exp2(s - mx)
        den = jnp.maximum(jnp.sum(e, axis=1, keepdims=True), 1e-30)
        p_c = e * (1.0 / den)
        o_c = jnp.dot(p_c.astype(BF16), vc_ref[g], preferred_element_type=F32)

        p_sum = p_c[0:tq]
        for h in range(1, NSA_HPG):
            p_sum = p_sum + p_c[h * tq:(h + 1) * tq]
        hi, lo = _split_bf16(p_sum)
        smap_t = selmap_ref[...]
        imp_t = _nt_dot(smap_t, hi) + _nt_dot(smap_t, lo)

        win = (m0, a0)
        for back in range(n_back + 1):
            rows_b = chunk(jnp.maximum(qi - back, 0))
            mask = (kcol <= qrow) if back == 0 else (kcol > qrow) if back == n_back else None
            exists = None if back == 0 else jnp.where(qi >= back, 0.0, NEG_BIG)
            win = _flash_step(qs, kw_ref[rows_b, grp(g)], _with_ones(vw_ref[rows_b, grp(g)]),
                              *win, mask, exists)
        o_w = _flash_out(win[1])

        picked = _topk_rows(jnp.where(free > 0.5, imp_t, -jnp.inf),
                            max(min(SEL_TOPK, n_sel) - n_forced, 0))
        selb = _bias_from_keep(jnp.where(forced, 1.0, picked * free))
        qa = jnp.concatenate([qs, jnp.concatenate([selb] * NSA_HPG, axis=0)], axis=1)

        carry = _flash_step(qa, *kv_chunk(g, qi), m0, a0, kcol <= qrow)
        if par == 1:
            carry = _flash_step(qa, *kv_chunk(g, qi - 1), *carry, None)
        return qa, o_c, o_w, carry, tile

    fronts = [front(g, par) for g in range(ng) for par in range(2)]

    def body(i, c):
        new = []
        for k, f in enumerate(fronts):
            g = k // 2
            ck = _flash_step(f[0], *kv_chunk(g, 2 * i), c[2 * k], c[2 * k + 1], None)
            new += _flash_step(f[0], *kv_chunk(g, 2 * i + 1), *ck, None)
        return tuple(new)

    final = lax.fori_loop(0, tp, body, tuple(x for f in fronts for x in f[3]))

    for k, (_, o_c, o_w, _, tile) in enumerate(fronts):
        g = k // 2
        o_s = _flash_out(final[2 * k + 1])
        sig = 1.0 / (1.0 + jnp.exp(-gl_ref[tile, grp(g)].astype(F32)))
        for h in range(NSA_HPG):
            r = slice(h * tq, (h + 1) * tq)
            o = (sig[:, 3 * h:3 * h + 1] * o_c[r] + sig[:, 3 * h + 1:3 * h + 2] * o_s[r]
                 + sig[:, 3 * h + 2:3 * h + 3] * o_w[r])
            o_ref[tile, g * gw + h * LANES:g * gw + (h + 1) * LANES] = o.astype(BF16)


def _nsa(qkv, kcv, selmap, et_sel, bsz, seq, casts):
    m = qkv.shape[0]
    tq = NSA_TQ
    ng = NSA_GROUPS_PER_STEP
    assert seq % tq == 0 and WINDOW % tq == 0 and tq % SEL_LEN == 0 and NSA_KV_GROUPS % ng == 0
    assert all(c0 % ng == 0 for c0 in (KS0, VS0, KW0, VW0, GT0))
    assert seq % (2 * tq) == 0
    nq = seq // (2 * tq)
    ngs = NSA_KV_GROUPS // ng
    n_cmp_pad = kcv.shape[2]
    gw = ng * NSA_HPG * LANES
    full = lambda c0: pl.BlockSpec((seq, ng * LANES), lambda b, g, i: (b, c0 // ng + g))
    cast_specs, cast_shapes = _cast_specs(casts, bsz * ngs * nq, lambda b, g, i: (b * ngs + g) * nq + i)
    outs = pl.pallas_call(
        functools.partial(_nsa_kernel, tq=tq, seq=seq, ng=ng, n_cast=len(casts)),
        grid=(bsz, ngs, nq),
        in_specs=[pl.BlockSpec((2 * tq, gw), lambda b, g, i: (b * nq + i, g)),
                  pl.BlockSpec((ng, None, n_cmp_pad, LANES), lambda b, g, i: (g, b, 0, 0)),
                  pl.BlockSpec((ng, None, n_cmp_pad, LANES), lambda b, g, i: (ngs + g, b, 0, 0)),
                  full(KS0), full(VS0), full(KW0), full(VW0),
                  pl.BlockSpec((2 * tq, ng * LANES), lambda b, g, i: (b * nq + i, GT0 // ng + g)),
                  pl.BlockSpec(selmap.shape, lambda b, g, i: (0, 0)),
                  pl.BlockSpec(et_sel.shape, lambda b, g, i: (0, 0))] + cast_specs,
        out_specs=[pl.BlockSpec((2 * tq, gw), lambda b, g, i: (b * nq + i, g))] + cast_specs,
        out_shape=[jax.ShapeDtypeStruct((m, NSA_HEADS * HEAD_DIM), BF16)] + cast_shapes,
        compiler_params=pltpu.CompilerParams(
            dimension_semantics=("parallel", "parallel", "arbitrary"),
            vmem_limit_bytes=VMEM_LIMIT),
        name="nsa_attn",
    )(qkv, kcv, kcv, qkv, qkv, qkv, qkv, qkv, selmap, et_sel, *[w for w, _ in casts])
    return outs[0], outs[1:]


def _moba_kernel(q_ref, k_ref, v_ref, avg_ref, et_ref, *rest, seq, hb, n_cast):
    cast_in, (o_ref, *cast_out), (kmh_ref, kml_ref) = rest[:n_cast], rest[n_cast:2 * n_cast + 1], rest[2 * n_cast + 1:]
    _cast_slabs(cast_in, cast_out)
    tp = pl.program_id(2)
    blk_len = MOBA_BLOCK
    nb = seq // blk_len
    nb_rows = avg_ref.shape[0]
    head = lambda h: slice(h * LANES, (h + 1) * LANES)

    @pl.when(tp == 0)
    def _():
        for h in range(hb):
            km = jnp.dot(avg_ref[...], k_ref[:, head(h)], preferred_element_type=F32)
            kmh_ref[h], kml_ref[h] = _split_bf16(km)

    def kv_chunk(j, h):
        rows = pl.ds(pl.multiple_of(j * blk_len, blk_len), blk_len)
        ka = jnp.concatenate([k_ref[rows, head(h)], et_ref[rows, :]], axis=1)
        return ka, _with_ones(v_ref[rows, head(h)])

    blk = lax.broadcasted_iota(I32, (nb_rows, blk_len), 0)
    row = lax.broadcasted_iota(I32, (blk_len, blk_len), 0)
    colk = lax.broadcasted_iota(I32, (blk_len, blk_len), 1)
    m0 = jnp.full((blk_len, 1), NEG_BIG, F32)
    a0 = jnp.zeros((blk_len, 2 * HEAD_DIM), F32)

    def front(h, par):
        qi = 2 * tp + par
        past = jnp.where(blk < qi, 1.0, 0.0)
        own = jnp.where(blk == qi, 1.0, 0.0)
        q = q_ref[par * blk_len:(par + 1) * blk_len, head(h)]
        gate_t = _nt_dot(kmh_ref[h], q) + _nt_dot(kml_ref[h], q)
        picked = _topk_rows(jnp.where(past > 0.5, gate_t, -jnp.inf), min(MOBA_TOPK, nb - 1))
        qa = jnp.concatenate([q, _bias_from_keep(picked * past + own)], axis=1)
        m, acc = _flash_step(qa, *kv_chunk(qi, h), m0, a0, colk <= row)
        if par == 1:
            m, acc = _flash_step(qa, *kv_chunk(qi - 1, h), m, acc, None)
        return qa, m, acc

    streams = [(h, par) for h in range(hb) for par in range(2)]
    fronts = [front(h, par) for h, par in streams]

    def all_streams(j, c):
        new = []
        for k, (h, _) in enumerate(streams):
            new += _flash_step(fronts[k][0], *kv_chunk(j, h), c[2 * k], c[2 * k + 1], None)
        return tuple(new)

    final = lax.fori_loop(0, tp, lambda i, c: all_streams(2 * i + 1, all_streams(2 * i, c)),
                          tuple(x for f in fronts for x in f[1:]))
    for k, (h, par) in enumerate(streams):
        o_ref[par * blk_len:(par + 1) * blk_len, head(h)] = _flash_out(final[2 * k + 1]).astype(BF16)


def _moba(qkv, avg, et_blk, bsz, seq, casts):
    m = qkv.shape[0]
    assert seq % (2 * MOBA_BLOCK) == 0
    nq = seq // (2 * MOBA_BLOCK)
    hb = MOBA_HEADS_PER_STEP
    assert MOBA_HEADS % hb == 0 and QM0 % hb == 0 and KM0 % hb == 0 and VM0 % hb == 0
    hw = hb * LANES
    ng = MOBA_HEADS // hb
    cast_specs, cast_shapes = _cast_specs(casts, bsz * ng * nq, lambda b, h, i: (b * ng + h) * nq + i)
    whole = lambda c0: pl.BlockSpec((seq, hw), lambda b, h, i: (b, c0 // hb + h),
                                    pipeline_mode=pl.Buffered(1))
    outs = pl.pallas_call(
        functools.partial(_moba_kernel, seq=seq, hb=hb, n_cast=len(casts)),
        grid=(bsz, ng, nq),
        in_specs=[pl.BlockSpec((2 * MOBA_BLOCK, hw), lambda b, h, i: (b * nq + i, QM0 // hb + h)),
                  whole(KM0), whole(VM0),
                  pl.BlockSpec(avg.shape, lambda b, h, i: (0, 0)),
                  pl.BlockSpec(et_blk.shape, lambda b, h, i: (0, 0))] + cast_specs,
        out_specs=[pl.BlockSpec((2 * MOBA_BLOCK, hw), lambda b, h, i: (b * nq + i, h))] + cast_specs,
        out_shape=[jax.ShapeDtypeStruct((m, MOBA_HEADS * HEAD_DIM), BF16)] + cast_shapes,
        scratch_shapes=[pltpu.VMEM((hb, avg.shape[0], HEAD_DIM), BF16),
                        pltpu.VMEM((hb, avg.shape[0], HEAD_DIM), BF16)],
        compiler_params=pltpu.CompilerParams(
            dimension_semantics=("parallel", "parallel", "arbitrary"),
            vmem_limit_bytes=VMEM_LIMIT),
        name="moba_attn",
    )(qkv, qkv, qkv, avg, et_blk, *[w for w, _ in casts])
    return outs[0], outs[1:]


def _outproj_kernel(on_ref, om_ref, w_ref, x_ref, g_ref, gate_ref, o_ref):
    kn = on_ref.shape[1]
    acc = (jnp.dot(on_ref[...], w_ref[:kn, :], preferred_element_type=F32)
           + jnp.dot(om_ref[...], w_ref[kn:, :], preferred_element_type=F32))
    o_ref[...] = x_ref[...] + gate_ref[...] * _rms_norm(acc, g_ref[...])


def _outproj(o_n, o_m, w_o, x2, g_post, gate, seq):
    m, d = x2.shape
    tm = min(512, seq)
    tiles_per_b = seq // tm
    return pl.pallas_call(
        _outproj_kernel,
        grid=(m // tm,),
        in_specs=[pl.BlockSpec((tm, o_n.shape[1]), lambda i: (i, 0)),
                  pl.BlockSpec((tm, o_m.shape[1]), lambda i: (i, 0)),
                  pl.BlockSpec(w_o.shape, lambda i: (0, 0)),
                  pl.BlockSpec((tm, d), lambda i: (i, 0)),
                  pl.BlockSpec((1, d), lambda i: (0, 0)),
                  pl.BlockSpec((None, 1, d), lambda i: (i // tiles_per_b, 0, 0))],
        out_specs=pl.BlockSpec((tm, d), lambda i: (i, 0)),
        out_shape=jax.ShapeDtypeStruct((m, d), F32),
        compiler_params=pltpu.CompilerParams(dimension_semantics=("parallel",),
                                             vmem_limit_bytes=VMEM_LIMIT),
        name="outproj",
    )(o_n, o_m, w_o, x2, g_post, gate)


def _ffn_kernel(x_ref, gpre_ref, sc_ref, sh_ref, wu_ref, wd_ref, gpost_ref, gate_ref,
                o_ref, h_ref):
    k = pl.program_id(1)
    last = pl.num_programs(1) - 1
    row_blocks = [slice(r, r + FFN_ROW_BLOCK) for r in range(0, x_ref.shape[0], FFN_ROW_BLOCK)]

    def mlp(h):
        u = jnp.dot(h, wu_ref[...], preferred_element_type=F32)
        u = jnp.square(jnp.maximum(u, 0.0)).astype(BF16)
        return jnp.dot(u, wd_ref[...], preferred_element_type=F32)

    @pl.when(k == 0)
    def _():
        for rows in row_blocks:
            h = _rms_norm(x_ref[rows], gpre_ref[...]) * (1.0 + sc_ref[...]) + sh_ref[...]
            h_ref[rows] = h.astype(BF16)
            o_ref[rows] = mlp(h_ref[rows])

    @pl.when(jnp.logical_and(k > 0, k < last))
    def _():
        o_ref[...] += mlp(h_ref[...])

    @pl.when(k == last)
    def _():
        for rows in row_blocks:
            f = o_ref[rows] + mlp(h_ref[rows])
            o_ref[rows] = x_ref[rows] + gate_ref[...] * _rms_norm(f, gpost_ref[...])


def _ffn(x2, g_pre, scale, shift, w_up, w_down, g_post, gate, seq):
    m, d = x2.shape
    dff = w_up.shape[1]
    tm = min(1024, seq)
    ck = 512
    assert dff // ck >= 2 and tm % FFN_ROW_BLOCK == 0
    tiles_per_b = seq // tm
    vec = pl.BlockSpec((1, d), lambda i, k: (0, 0))
    per_b = pl.BlockSpec((None, 1, d), lambda i, k: (i // tiles_per_b, 0, 0))
    return pl.pallas_call(
        _ffn_kernel,
        grid=(m // tm, dff // ck),
        in_specs=[pl.BlockSpec((tm, d), lambda i, k: (i, 0)), vec, per_b, per_b,
                  pl.BlockSpec((d, ck), lambda i, k: (0, k)),
                  pl.BlockSpec((ck, d), lambda i, k: (k, 0)),
                  vec, per_b],
        out_specs=pl.BlockSpec((tm, d), lambda i, k: (i, 0)),
        out_shape=jax.ShapeDtypeStruct((m, d), F32),
        scratch_shapes=[pltpu.VMEM((tm, d), BF16)],
        compiler_params=pltpu.CompilerParams(dimension_semantics=("parallel", "arbitrary"),
                                             vmem_limit_bytes=VMEM_LIMIT),
        name="ffn",
    )(x2, g_pre, scale, shift, w_up, w_down, g_post, gate)


def _constants(seq):
    n_cmp_pad = seq // CMP_STRIDE
    n_cmp = (seq - CMP_LEN) // CMP_STRIDE + 1
    n_sel = seq // SEL_LEN
    cs = np.arange(n_cmp) * CMP_STRIDE
    ss = np.arange(n_sel) * SEL_LEN
    overlap = np.clip(np.minimum(cs[:, None] + CMP_LEN, ss[None, :] + SEL_LEN)
                      - np.maximum(cs[:, None], ss[None, :]), 0, None)
    selmap_t = np.zeros((n_sel, n_cmp_pad), np.float32)
    selmap_t[:, :n_cmp] = (overlap / CMP_LEN).T
    keys = np.arange(seq)
    et_sel = (keys[:, None] // SEL_LEN == np.arange(LANES)[None, :]).astype(np.float32)
    et_blk = (keys[:, None] // MOBA_BLOCK == np.arange(LANES)[None, :]).astype(np.float32)
    nb_rows = -(-(seq // MOBA_BLOCK) // BF16_SUBLANES) * BF16_SUBLANES
    avg = et_blk.T[:nb_rows] / MOBA_BLOCK
    to = lambda a: jnp.asarray(a, BF16)
    return to(selmap_t), to(et_sel), to(et_blk), to(avg)


def _rope_tables(seq):
    pos = jnp.arange(seq, dtype=F32)
    inv = ROPE_THETA ** (-jnp.arange(0, HEAD_DIM, 2, dtype=F32) / HEAD_DIM)
    ang = pos[:, None] * inv[None, :]
    cos, sin = jnp.cos(ang), jnp.sin(ang)
    return jnp.concatenate([cos, cos], axis=1), jnp.concatenate([-sin, sin], axis=1)


def kernel(x, c, w_ada, b_ada, pre_norm_mix, post_norm_mix, w_in, cmp_k_pos, cmp_k_w1, cmp_k_w2,
           cmp_v_pos, cmp_v_w1, cmp_v_w2, w_o, pre_norm_ffn, post_norm_ffn, w_up, w_down):
    bsz, seq, d = x.shape
    depth = w_ada.shape[0]
    cos2, sin2 = _rope_tables(seq)
    selmap, et_sel, et_blk, avg = _constants(seq)
    x2 = x.reshape(bsz * seq, d)
    for l in range(depth):
        mod, w_head, w_tail = _adaln(c, w_ada[l], b_ada[l], w_in[l].T)
        shift_m, scale_m, gate_m, shift_f, scale_f, gate_f = [
            a.reshape(bsz, 1, d) for a in jnp.split(mod, 6, axis=-1)]
        row = lambda a: a.reshape(1, d)

        qkv, kvc = _inproj(x2, row(pre_norm_mix[l]), scale_m, shift_m, w_head, w_tail,
                           cos2, sin2, seq)
        kvc4 = kvc.reshape(2 * NSA_KV_GROUPS, bsz, seq // CMP_STRIDE, CMP_STRIDE * HEAD_DIM)
        pos2 = jnp.stack([cmp_k_pos[l].reshape(1, -1), cmp_v_pos[l].reshape(1, -1)])
        kcv = _compress(kvc4, pos2, jnp.stack([cmp_k_w1[l], cmp_v_w1[l]]),
                        jnp.stack([cmp_k_w2[l], cmp_v_w2[l]]))
        o_n, (w_o_bf,) = _nsa(qkv, kcv, selmap, et_sel, bsz, seq, [(w_o[l], 0)])
        o_m, (w_up_bf, w_down_bf) = _moba(qkv, avg, et_blk, bsz, seq, [(w_up[l], 1), (w_down[l], 0)])
        x2 = _outproj(o_n, o_m, w_o_bf, x2, row(post_norm_mix[l]), gate_m, seq)
        x2 = _ffn(x2, row(pre_norm_ffn[l]), scale_f, shift_f, w_up_bf, w_down_bf,
                  row(post_norm_ffn[l]), gate_f, seq)
    return x2.reshape(bsz, seq, d)
```

```python
import functools

import numpy as np
import jax
import jax.numpy as jnp
from jax import lax
from jax.experimental import pallas as pl
from jax.experimental.pallas import tpu as pltpu

F32 = jnp.float32
BF16 = jnp.bfloat16
I32 = jnp.int32

HEAD_DIM = 128
NSA_HEADS = 8
NSA_KV_GROUPS = 2
NSA_HPG = NSA_HEADS // NSA_KV_GROUPS
MOBA_HEADS = 8
CMP_LEN = 32
CMP_STRIDE = 16
SEL_LEN = 64
SEL_TOPK = 16
WINDOW = 512
MOBA_BLOCK = 256
MOBA_TOPK = 3
ROPE_THETA = 10000.0
EPS = 1e-6

LANES = 128
BF16_SUBLANES = 16
MXU_COLS = 256
FFN_ROW_BLOCK = 256
NSA_TQ = 256
NSA_GROUPS_PER_STEP = 2
MOBA_HEADS_PER_STEP = 4
Q_SCALE = HEAD_DIM ** -0.5 * float(np.log2(np.e))
NEG_BIG = -1e30
VMEM_LIMIT = 56 * 1024 * 1024

QN0 = 0
QM0 = 8
KM0 = 16
VM0 = 24
KC0 = 32
VC0 = 34
KS0 = 36
VS0 = 38
KW0 = 40
VW0 = 42
GT0 = 44
N_BLOCKS = 48
INPROJ_TN = 1024
HEAD_COLS = 2560
GATE_COLS = 3 * NSA_HEADS
IN_SLAB = 256
ROPE_Q, ROPE_K, PLAIN, GATES, ZERO = range(5)
_INPROJ_TILES = (
    ("head", 0, (ROPE_Q,) * 4),
    ("tail", 0, (ROPE_Q,) * 4),
    ("tail", 1, (ROPE_K,) * 4),
    ("tail", 2, (PLAIN,) * 4),
    ("head", 1, (ROPE_K, PLAIN, ROPE_K, PLAIN)),
    ("head", 2, (ROPE_K, PLAIN, GATES, ZERO)),
)


def _nt_dot(a, b):
    return lax.dot_general(a, b, (((1,), (1,)), ((), ())), preferred_element_type=F32)


def _rms_norm(x, g):
    ms = jnp.mean(x * x, axis=-1, keepdims=True)
    return x * lax.rsqrt(ms + EPS) * g


def _split_bf16(a):
    hi = a.astype(BF16)
    lo = (a - hi.astype(F32)).astype(BF16)
    return hi, lo


def _adaln_kernel(ct_ref, w_ref, b_ref, wh_ref, wa_ref, wb_ref, o_ref, head_ref, tail_ref):
    ct = ct_ref[...]
    a = ct * (1.0 / (1.0 + jnp.exp(-ct)))
    w = w_ref[...]
    for b in range(o_ref.shape[0]):
        o_ref[b] = jnp.sum(w * a[:, b:b + 1], axis=0, keepdims=True) + b_ref[...]

    t = pl.program_id(0)
    tail_ref[...] = jnp.concatenate([wa_ref[GATE_COLS:, :], wb_ref[:GATE_COLS, :]], axis=0).astype(BF16)

    n_plain = HEAD_COLS // IN_SLAB

    @pl.when(t < n_plain)
    def _():
        head_ref[...] = wh_ref[...].astype(BF16)

    @pl.when(t == n_plain)
    def _():
        per_group = 3 * NSA_HPG
        g = wh_ref[:LANES, :]
        row = lax.broadcasted_iota(I32, g.shape, 0)
        for grp in range(NSA_KV_GROUPS):
            rows = g if grp == 0 else pltpu.roll(g, LANES - grp * per_group, 0)
            head_ref[grp * LANES:(grp + 1) * LANES, :] = jnp.where(row < per_group, rows, 0.0).astype(BF16)

    @pl.when(t > n_plain)
    def _():
        head_ref[...] = jnp.zeros(head_ref.shape, BF16)


def _adaln(c, w, b, w_in_t):
    bsz, d = c.shape
    n = w.shape[1]
    tn = 1024
    steps = n // tn
    n_src = sum(1 for src, _, _ in _INPROJ_TILES if src == "head") * INPROJ_TN
    n_plain = HEAD_COLS // IN_SLAB
    assert bsz <= LANES and steps * IN_SLAB == n_src and HEAD_COLS % IN_SLAB == 0 and n_plain < steps
    assert IN_SLAB == NSA_KV_GROUPS * LANES and w_in_t.shape[0] == HEAD_COLS + GATE_COLS + n_src
    assert GATE_COLS % 8 == 0
    ct = jnp.zeros((d, LANES), F32).at[:, :bsz].set(c.T)
    slab = lambda f: pl.BlockSpec((IN_SLAB, d), lambda j: (f(j), 0))
    out, w_head, w_tail = pl.pallas_call(
        _adaln_kernel,
        grid=(steps,),
        in_specs=[pl.BlockSpec((d, LANES), lambda j: (0, 0)),
                  pl.BlockSpec((d, tn), lambda j: (0, j)),
                  pl.BlockSpec((1, tn), lambda j: (0, j)),
                  slab(lambda j: jnp.minimum(j, n_plain)),
                  slab(lambda j: n_plain + j),
                  slab(lambda j: n_plain + j + 1)],
        out_specs=[pl.BlockSpec((bsz, 1, tn), lambda j: (0, 0, j)), slab(lambda j: j), slab(lambda j: j)],
        out_shape=[jax.ShapeDtypeStruct((bsz, 1, n), F32),
                   jax.ShapeDtypeStruct((n_src, d), BF16), jax.ShapeDtypeStruct((n_src, d), BF16)],
        compiler_params=pltpu.CompilerParams(dimension_semantics=("parallel",),
                                             vmem_limit_bytes=VMEM_LIMIT),
        name="adaln",
    )(ct, w, b.reshape(1, n), w_in_t, w_in_t, w_in_t)
    return out.reshape(bsz, n), w_head, w_tail


def _inproj_kernel(x_ref, g_ref, sc_ref, sh_ref, wh_ref, wt_ref, cos_ref, sin_ref,
                   o_ref, kvc_ref, h_ref):
    j = pl.program_id(1)

    @pl.when(j == 0)
    def _():
        h = _rms_norm(x_ref[...], g_ref[...]) * (1.0 + sc_ref[...]) + sh_ref[...]
        h_ref[...] = h.astype(BF16)

    def tile(w_ref, kinds):
        for ci, kind in enumerate(kinds):
            c0 = ci * MXU_COLS
            if kind == ZERO:
                o_ref[:, c0:c0 + MXU_COLS] = jnp.zeros((o_ref.shape[0], MXU_COLS), BF16)
                continue
            acc = _nt_dot(h_ref[...], w_ref[c0:c0 + MXU_COLS, :])
            for c in range(c0, c0 + MXU_COLS, LANES):
                ch = acc[:, c - c0:c - c0 + LANES]
                if kind in (ROPE_Q, ROPE_K):
                    ch = ch * cos_ref[...] + pltpu.roll(ch, HEAD_DIM // 2, 1) * sin_ref[...]
                if kind == ROPE_Q:
                    ch = ch * Q_SCALE
                o_ref[:, c:c + LANES] = ch.astype(BF16)

    for jj, (src, _, kinds) in enumerate(_INPROJ_TILES):
        pl.when(j == jj)(functools.partial(tile, wh_ref if src == "head" else wt_ref, kinds))

    jkv, ok = divmod(KC0 * LANES, INPROJ_TN)
    ov = VC0 * LANES - jkv * INPROJ_TN
    assert 0 < ov < INPROJ_TN

    @pl.when(j == jkv)
    def _():
        for g in range(NSA_KV_GROUPS):
            kvc_ref[g] = o_ref[:, ok + g * LANES: ok + (g + 1) * LANES]
            kvc_ref[NSA_KV_GROUPS + g] = o_ref[:, ov + g * LANES: ov + (g + 1) * LANES]


def _step_lookup(j, table):
    out = jnp.int32(table[0])
    for k in range(1, len(table)):
        out = jnp.where(j >= k, table[k], out)
    return out


def _inproj(x2, g, scale, shift, w_head, w_tail, cos2, sin2, seq):
    m, d = x2.shape
    tm = min(1024, seq)
    tn = INPROJ_TN
    tiles_per_b = seq // tm
    blocks = {"head": [], "tail": []}
    for src, blk, _ in _INPROJ_TILES:
        for name, lst in blocks.items():
            lst.append(blk if src == name else (lst[-1] if lst else None))
    for lst in blocks.values():
        first = next(b for b in lst if b is not None)
        lst[:] = [first if b is None else b for b in lst]
    return pl.pallas_call(
        _inproj_kernel,
        grid=(m // tm, len(_INPROJ_TILES)),
        in_specs=[pl.BlockSpec((tm, d), lambda i, j: (i, 0)),
                  pl.BlockSpec((1, d), lambda i, j: (0, 0)),
                  pl.BlockSpec((None, 1, d), lambda i, j: (i // tiles_per_b, 0, 0)),
                  pl.BlockSpec((None, 1, d), lambda i, j: (i // tiles_per_b, 0, 0)),
                  pl.BlockSpec((tn, d), lambda i, j: (_step_lookup(j, blocks["head"]), 0)),
                  pl.BlockSpec((tn, d), lambda i, j: (_step_lookup(j, blocks["tail"]), 0)),
                  pl.BlockSpec((tm, LANES), lambda i, j: (i % tiles_per_b, 0)),
                  pl.BlockSpec((tm, LANES), lambda i, j: (i % tiles_per_b, 0))],
        out_specs=[pl.BlockSpec((tm, tn), lambda i, j: (i, j)),
                   pl.BlockSpec((2 * NSA_KV_GROUPS, tm, LANES), lambda i, j: (0, i, 0))],
        out_shape=[jax.ShapeDtypeStruct((m, N_BLOCKS * LANES), BF16),
                   jax.ShapeDtypeStruct((2 * NSA_KV_GROUPS, m, LANES), BF16)],
        scratch_shapes=[pltpu.VMEM((tm, d), BF16)],
        compiler_params=pltpu.CompilerParams(dimension_semantics=("parallel", "arbitrary"),
                                             vmem_limit_bytes=VMEM_LIMIT),
        name="inproj",
    )(x2, g, scale, shift, w_head, w_tail, cos2, sin2)


def _compress_kernel(x_ref, pos_ref, w1_ref, w2_ref, o_ref):
    half = CMP_STRIDE * HEAD_DIM
    x = x_ref[...].astype(F32)
    pos = pos_ref[...]
    w1 = w1_ref[...].astype(BF16)
    a = jnp.dot((x + pos[:, :half]).astype(BF16), w1[:half], preferred_element_type=F32)
    b = jnp.dot((x + pos[:, half:]).astype(BF16), w1[half:], preferred_element_type=F32)
    n_rows = x.shape[0]
    h1 = a + pltpu.roll(b, n_rows - 1, 0)
    h = h1 * (1.0 / (1.0 + jnp.exp(-h1)))
    o_ref[...] = jnp.dot(h.astype(BF16), w2_ref[...].astype(BF16),
                         preferred_element_type=F32).astype(BF16)


def _compress(kvc4, pos2, w1s, w2s):
    ns, bsz, n_rows, width = kvc4.shape
    return pl.pallas_call(
        _compress_kernel,
        grid=(ns, bsz),
        in_specs=[pl.BlockSpec((None, None, n_rows, width), lambda s, b: (s, b, 0, 0)),
                  pl.BlockSpec((None, 1, 2 * width), lambda s, b: (s // NSA_KV_GROUPS, 0, 0)),
                  pl.BlockSpec((None, 2 * width, HEAD_DIM), lambda s, b: (s // NSA_KV_GROUPS, 0, 0)),
                  pl.BlockSpec((None, HEAD_DIM, HEAD_DIM), lambda s, b: (s // NSA_KV_GROUPS, 0, 0))],
        out_specs=pl.BlockSpec((None, None, n_rows, HEAD_DIM), lambda s, b: (s, b, 0, 0)),
        out_shape=jax.ShapeDtypeStruct((ns, bsz, n_rows, HEAD_DIM), BF16),
        compiler_params=pltpu.CompilerParams(dimension_semantics=("parallel", "parallel"),
                                             vmem_limit_bytes=VMEM_LIMIT),
        name="compress",
    )(kvc4, pos2, w1s, w2s)


def _topk_rows(score, k):
    idx = lax.broadcasted_iota(I32, score.shape, 0).astype(F32)
    work = score
    sel = jnp.zeros(score.shape, F32)
    for _ in range(k):
        mx = jnp.max(work, axis=0, keepdims=True)
        first = jnp.min(jnp.where(work == mx, idx, float(score.shape[0])), axis=0, keepdims=True)
        hit = idx == first
        sel = jnp.where(hit, 1.0, sel)
        work = jnp.where(hit, -jnp.inf, work)
    return sel


def _bias_from_keep(keep_t):
    n, q = keep_t.shape
    padded = jnp.concatenate([keep_t, jnp.zeros((LANES - n, q), F32)], axis=0) if n < LANES else keep_t
    return jnp.where(padded.T > 0.5, 0.0, NEG_BIG).astype(BF16)


def _with_ones(v):
    return jnp.concatenate([v, jnp.ones(v.shape, v.dtype)], axis=1)


def _flash_step(qa, ka, va, m, acc, mask, penalty=None):
    s = _nt_dot(qa, ka)
    if mask is not None:
        s = jnp.where(mask, s, NEG_BIG)
    if penalty is not None:
        s = s + penalty
    m_new = jnp.maximum(m, jnp.max(s, axis=1, keepdims=True))
    alpha = jnp.exp2(m - m_new)
    p = jnp.exp2(s - m_new).astype(BF16)
    acc = alpha * acc + jnp.dot(p, va, preferred_element_type=F32)
    return m_new, acc


def _flash_out(acc):
    return acc[:, :HEAD_DIM] * (1.0 / acc[:, HEAD_DIM:])


def _cast_specs(casts, n_steps, step_of):
    specs, shapes = [], []
    for w, axis in casts:
        size = w.shape[axis] // n_steps
        assert w.shape[axis] % n_steps == 0 and size % (BF16_SUBLANES if axis == 0 else LANES) == 0
        if axis == 0:
            specs.append(pl.BlockSpec((size, w.shape[1]), lambda *g: (step_of(*g), 0)))
        else:
            specs.append(pl.BlockSpec((w.shape[0], size), lambda *g: (0, step_of(*g))))
        shapes.append(jax.ShapeDtypeStruct(w.shape, BF16))
    return specs, shapes


def _cast_slabs(src_refs, dst_refs):
    for src, dst in zip(src_refs, dst_refs):
        dst[...] = src[...].astype(BF16)


def _nsa_kernel(q_ref, kc_ref, vc_ref, ks_ref, vs_ref, kw_ref, vw_ref, gl_ref,
                selmap_ref, et_ref, *rest, tq, seq, ng, n_cast):
    cast_in, (o_ref, *cast_out) = rest[:n_cast], rest[n_cast:]
    _cast_slabs(cast_in, cast_out)
    tp = pl.program_id(2)
    rows = NSA_HPG * tq
    n_cmp_pad = kc_ref.shape[1]
    n_sel = seq // SEL_LEN
    gw = NSA_HPG * LANES
    grp = lambda g: slice(g * LANES, (g + 1) * LANES)

    qrow = lax.broadcasted_iota(I32, (rows, tq), 0) & (tq - 1)
    kcol = lax.broadcasted_iota(I32, (rows, tq), 1)
    m0 = jnp.full((rows, 1), NEG_BIG, F32)
    a0 = jnp.zeros((rows, 2 * HEAD_DIM), F32)
    trow = lax.broadcasted_iota(I32, (rows, n_cmp_pad), 0) & (tq - 1)
    cend = lax.broadcasted_iota(I32, (rows, n_cmp_pad), 1) * CMP_STRIDE + (CMP_LEN - 1)
    blk = lax.broadcasted_iota(I32, (n_sel, tq), 0)
    tcol = lax.broadcasted_iota(I32, (n_sel, tq), 1)
    n_forced = 3
    n_back = WINDOW // tq
    chunk = lambda j: pl.ds(pl.multiple_of(j * tq, tq), tq)

    def kv_chunk(g, j):
        ka = jnp.concatenate([ks_ref[chunk(j), grp(g)], et_ref[chunk(j), :]], axis=1)
        return ka, _with_ones(vs_ref[chunk(j), grp(g)])

    def front(g, par):
        qi = 2 * tp + par
        t0 = qi * tq
        tile = slice(par * tq, (par + 1) * tq)
        q = q_ref[tile, g * gw:(g + 1) * gw]
        qs = jnp.concatenate([q[:, h * LANES:(h + 1) * LANES] for h in range(NSA_HPG)], axis=0)
        cb = (t0 + tcol) // SEL_LEN
        forced = (blk == 0) | (blk == cb) | (blk == cb - 1)
        free = jnp.where(blk <= cb, jnp.where(forced, 0.0, 1.0), 0.0)

        s = jnp.where(cend <= t0 + trow, _nt_dot(qs, kc_ref[g]), -jnp.inf)
        mx = jnp.max(s, axis=1, keepdims=True)
        mx = jnp.where(mx == -jnp.inf, 0.0, mx)
        e = jnp.exp2(s - mx)
        den = jnp.maximum(jnp.sum(e, axis=1, keepdims=True), 1e-30)
        p_c = e * (1.0 / den)
        o_c = jnp.dot(p_c.astype(BF16), vc_ref[g], preferred_element_type=F32)

        p_sum = p_c[0:tq]
        for h in range(1, NSA_HPG):
            p_sum = p_sum + p_c[h * tq:(h + 1) * tq]
        hi, lo = _split_bf16(p_sum)
        smap_t = selmap_ref[...]
        imp_t = _nt_dot(smap_t, hi) + _nt_dot(smap_t, lo)

        win = (m0, a0)
        for back in range(n_back + 1):
            rows_b = chunk(jnp.maximum(qi - back, 0))
            mask = (kcol <= qrow) if back == 0 else (kcol > qrow) if back == n_back else None
            exists = None if back == 0 else jnp.where(qi >= back, 0.0, NEG_BIG)
            win = _flash_step(qs, kw_ref[rows_b, grp(g)], _with_ones(vw_ref[rows_b, grp(g)]),
                              *win, mask, exists)
        o_w = _flash_out(win[1])

        picked = _topk_rows(jnp.where(free > 0.5, imp_t, -jnp.inf),
                            max(min(SEL_TOPK, n_sel) - n_forced, 0))
        selb = _bias_from_keep(jnp.where(forced, 1.0, picked * free))
        qa = jnp.concatenate([qs, jnp.concatenate([selb] * NSA_HPG, axis=0)], axis=1)

        carry = _flash_step(qa, *kv_chunk(g, qi), m0, a0, kcol <= qrow)
        if par == 1:
            carry = _flash_step(qa, *kv_chunk(g, qi - 1), *carry, None)
        return qa, o_c, o_w, carry, tile

    fronts = [front(g, par) for g in range(ng) for par in range(2)]

    def body(i, c):
        new = []
        for k, f in enumerate(fronts):
            g = k // 2
            ck = _flash_step(f[0], *kv_chunk(g, 2 * i), c[2 * k], c[2 * k + 1], None)
            new += _flash_step(f[0], *kv_chunk(g, 2 * i + 1), *ck, None)
        return tuple(new)

    final = lax.fori_loop(0, tp, body, tuple(x for f in fronts for x in f[3]))

    for k, (_, o_c, o_w, _, tile) in enumerate(fronts):
        g = k // 2
        o_s = _flash_out(final[2 * k + 1])
        sig = 1.0 / (1.0 + jnp.exp(-gl_ref[tile, grp(g)].astype(F32)))
        for h in range(NSA_HPG):
            r = slice(h * tq, (h + 1) * tq)
            o = (sig[:, 3 * h:3 * h + 1] * o_c[r] + sig[:, 3 * h + 1:3 * h + 2] * o_s[r]
                 + sig[:, 3 * h + 2:3 * h + 3] * o_w[r])
            o_ref[tile, g * gw + h * LANES:g * gw + (h + 1) * LANES] = o.astype(BF16)


def _nsa(qkv, kcv, selmap, et_sel, bsz, seq, casts):
    m = qkv.shape[0]
    tq = NSA_TQ
    ng = NSA_GROUPS_PER_STEP
    assert seq % tq == 0 and WINDOW % tq == 0 and tq % SEL_LEN == 0 and NSA_KV_GROUPS % ng == 0
    assert all(c0 % ng == 0 for c0 in (KS0, VS0, KW0, VW0, GT0))
    assert seq % (2 * tq) == 0
    nq = seq // (2 * tq)
    ngs = NSA_KV_GROUPS // ng
    n_cmp_pad = kcv.shape[2]
    gw = ng * NSA_HPG * LANES
    full = lambda c0: pl.BlockSpec((seq, ng * LANES), lambda b, g, i: (b, c0 // ng + g))
    cast_specs, cast_shapes = _cast_specs(casts, bsz * ngs * nq, lambda b, g, i: (b * ngs + g) * nq + i)
    outs = pl.pallas_call(
        functools.partial(_nsa_kernel, tq=tq, seq=seq, ng=ng, n_cast=len(casts)),
        grid=(bsz, ngs, nq),
        in_specs=[pl.BlockSpec((2 * tq, gw), lambda b, g, i: (b * nq + i, g)),
                  pl.BlockSpec((ng, None, n_cmp_pad, LANES), lambda b, g, i: (g, b, 0, 0)),
                  pl.BlockSpec((ng, None, n_cmp_pad, LANES), lambda b, g, i: (ngs + g, b, 0, 0)),
                  full(KS0), full(VS0), full(KW0), full(VW0),
                  pl.BlockSpec((2 * tq, ng * LANES), lambda b, g, i: (b * nq + i, GT0 // ng + g)),
                  pl.BlockSpec(selmap.shape, lambda b, g, i: (0, 0)),
                  pl.BlockSpec(et_sel.shape, lambda b, g, i: (0, 0))] + cast_specs,
        out_specs=[pl.BlockSpec((2 * tq, gw), lambda b, g, i: (b * nq + i, g))] + cast_specs,
        out_shape=[jax.ShapeDtypeStruct((m, NSA_HEADS * HEAD_DIM), BF16)] + cast_shapes,
        compiler_params=pltpu.CompilerParams(
            dimension_semantics=("parallel", "parallel", "arbitrary"),
            vmem_limit_bytes=VMEM_LIMIT),
        name="nsa_attn",
    )(qkv, kcv, kcv, qkv, qkv, qkv, qkv, qkv, selmap, et_sel, *[w for w, _ in casts])
    return outs[0], outs[1:]


def _moba_kernel(q_ref, k_ref, v_ref, avg_ref, et_ref, *rest, seq, hb, n_cast):
    cast_in, (o_ref, *cast_out), (kmh_ref, kml_ref) = rest[:n_cast], rest[n_cast:2 * n_cast + 1], rest[2 * n_cast + 1:]
    _cast_slabs(cast_in, cast_out)
    tp = pl.program_id(2)
    blk_len = MOBA_BLOCK
    rows = 2 * blk_len
    nb = seq // blk_len
    nb_rows = avg_ref.shape[0]
    head = lambda h: slice(h * LANES, (h + 1) * LANES)

    @pl.when(tp == 0)
    def _():
        for h in range(hb):
            km = jnp.dot(avg_ref[...], k_ref[:, head(h)], preferred_element_type=F32)
            kmh_ref[h], kml_ref[h] = _split_bf16(km)

    blk = lax.broadcasted_iota(I32, (nb_rows, rows), 0)
    qblk = 2 * tp + lax.broadcasted_iota(I32, (nb_rows, rows), 1) // blk_len
    past = jnp.where(blk < qblk, 1.0, 0.0)
    own = jnp.where(blk == qblk, 1.0, 0.0)
    qas = []
    for h in range(hb):
        q = q_ref[:, head(h)]
        gate_t = _nt_dot(kmh_ref[h], q) + _nt_dot(kml_ref[h], q)
        picked = _topk_rows(jnp.where(past > 0.5, gate_t, -jnp.inf), min(MOBA_TOPK, nb - 1))
        qas.append(jnp.concatenate([q, _bias_from_keep(picked * past + own)], axis=1))

    def kv_chunk(j, h):
        keys = pl.ds(pl.multiple_of(j * blk_len, blk_len), blk_len)
        ka = jnp.concatenate([k_ref[keys, head(h)], et_ref[keys, :]], axis=1)
        return ka, _with_ones(v_ref[keys, head(h)])

    row = lax.broadcasted_iota(I32, (rows, blk_len), 0)
    colk = lax.broadcasted_iota(I32, (rows, blk_len), 1)
    m0 = jnp.full((rows, 1), NEG_BIG, F32)
    a0 = jnp.zeros((rows, 2 * HEAD_DIM), F32)

    def all_heads(j, c, mask=None):
        new = []
        for h in range(hb):
            new += _flash_step(qas[h], *kv_chunk(j, h), c[2 * h], c[2 * h + 1], mask)
        return tuple(new)

    carry = all_heads(2 * tp, (m0, a0) * hb, (row >= blk_len) | (colk <= row))
    carry = all_heads(2 * tp + 1, carry, colk <= row - blk_len)
    final = lax.fori_loop(0, tp, lambda i, c: all_heads(2 * i + 1, all_heads(2 * i, c)), carry)
    for h in range(hb):
        o_ref[:, head(h)] = _flash_out(final[2 * h + 1]).astype(BF16)


def _moba(qkv, avg, et_blk, bsz, seq, casts):
    m = qkv.shape[0]
    assert seq % (2 * MOBA_BLOCK) == 0
    nq = seq // (2 * MOBA_BLOCK)
    hb = MOBA_HEADS_PER_STEP
    assert MOBA_HEADS % hb == 0 and QM0 % hb == 0 and KM0 % hb == 0 and VM0 % hb == 0
    hw = hb * LANES
    ng = MOBA_HEADS // hb
    cast_specs, cast_shapes = _cast_specs(casts, bsz * ng * nq, lambda b, h, i: (b * ng + h) * nq + i)
    whole = lambda c0: pl.BlockSpec((seq, hw), lambda b, h, i: (b, c0 // hb + h),
                                    pipeline_mode=pl.Buffered(1))
    outs = pl.pallas_call(
        functools.partial(_moba_kernel, seq=seq, hb=hb, n_cast=len(casts)),
        grid=(bsz, ng, nq),
        in_specs=[pl.BlockSpec((2 * MOBA_BLOCK, hw), lambda b, h, i: (b * nq + i, QM0 // hb + h)),
                  whole(KM0), whole(VM0),
                  pl.BlockSpec(avg.shape, lambda b, h, i: (0, 0)),
                  pl.BlockSpec(et_blk.shape, lambda b, h, i: (0, 0))] + cast_specs,
        out_specs=[pl.BlockSpec((2 * MOBA_BLOCK, hw), lambda b, h, i: (b * nq + i, h))] + cast_specs,
        out_shape=[jax.ShapeDtypeStruct((m, MOBA_HEADS * HEAD_DIM), BF16)] + cast_shapes,
        scratch_shapes=[pltpu.VMEM((hb, avg.shape[0], HEAD_DIM), BF16),
                        pltpu.VMEM((hb, avg.shape[0], HEAD_DIM), BF16)],
        compiler_params=pltpu.CompilerParams(
            dimension_semantics=("parallel", "parallel", "arbitrary"),
            vmem_limit_bytes=VMEM_LIMIT),
        name="moba_attn",
    )(qkv, qkv, qkv, avg, et_blk, *[w for w, _ in casts])
    return outs[0], outs[1:]


def _outproj_kernel(on_ref, om_ref, w_ref, x_ref, g_ref, gate_ref, o_ref):
    kn = on_ref.shape[1]
    acc = (jnp.dot(on_ref[...], w_ref[:kn, :], preferred_element_type=F32)
           + jnp.dot(om_ref[...], w_ref[kn:, :], preferred_element_type=F32))
    o_ref[...] = x_ref[...] + gate_ref[...] * _rms_norm(acc, g_ref[...])


def _outproj(o_n, o_m, w_o, x2, g_post, gate, seq):
    m, d = x2.shape
    tm = min(512, seq)
    tiles_per_b = seq // tm
    return pl.pallas_call(
        _outproj_kernel,
        grid=(m // tm,),
        in_specs=[pl.BlockSpec((tm, o_n.shape[1]), lambda i: (i, 0)),
                  pl.BlockSpec((tm, o_m.shape[1]), lambda i: (i, 0)),
                  pl.BlockSpec(w_o.shape, lambda i: (0, 0)),
                  pl.BlockSpec((tm, d), lambda i: (i, 0)),
                  pl.BlockSpec((1, d), lambda i: (0, 0)),
                  pl.BlockSpec((None, 1, d), lambda i: (i // tiles_per_b, 0, 0))],
        out_specs=pl.BlockSpec((tm, d), lambda i: (i, 0)),
        out_shape=jax.ShapeDtypeStruct((m, d), F32),
        compiler_params=pltpu.CompilerParams(dimension_semantics=("parallel",),
                                             vmem_limit_bytes=VMEM_LIMIT),
        name="outproj",
    )(o_n, o_m, w_o, x2, g_post, gate)


def _ffn_kernel(x_ref, gpre_ref, sc_ref, sh_ref, wu_ref, wd_ref, gpost_ref, gate_ref,
                o_ref, h_ref):
    k = pl.program_id(1)
    last = pl.num_programs(1) - 1
    row_blocks = [slice(r, r + FFN_ROW_BLOCK) for r in range(0, x_ref.shape[0], FFN_ROW_BLOCK)]

    def mlp(h):
        u = jnp.dot(h, wu_ref[...], preferred_element_type=F32)
        u = jnp.square(jnp.maximum(u, 0.0)).astype(BF16)
        return jnp.dot(u, wd_ref[...], preferred_element_type=F32)

    @pl.when(k == 0)
    def _():
        for rows in row_blocks:
            h = _rms_norm(x_ref[rows], gpre_ref[...]) * (1.0 + sc_ref[...]) + sh_ref[...]
            h_ref[rows] = h.astype(BF16)
            o_ref[rows] = mlp(h_ref[rows])

    @pl.when(jnp.logical_and(k > 0, k < last))
    def _():
        o_ref[...] += mlp(h_ref[...])

    @pl.when(k == last)
    def _():
        for rows in row_blocks:
            f = o_ref[rows] + mlp(h_ref[rows])
            o_ref[rows] = x_ref[rows] + gate_ref[...] * _rms_norm(f, gpost_ref[...])


def _ffn(x2, g_pre, scale, shift, w_up, w_down, g_post, gate, seq):
    m, d = x2.shape
    dff = w_up.shape[1]
    tm = min(1024, seq)
    ck = 512
    assert dff // ck >= 2 and tm % FFN_ROW_BLOCK == 0
    tiles_per_b = seq // tm
    vec = pl.BlockSpec((1, d), lambda i, k: (0, 0))
    per_b = pl.BlockSpec((None, 1, d), lambda i, k: (i // tiles_per_b, 0, 0))
    return pl.pallas_call(
        _ffn_kernel,
        grid=(m // tm, dff // ck),
        in_specs=[pl.BlockSpec((tm, d), lambda i, k: (i, 0)), vec, per_b, per_b,
                  pl.BlockSpec((d, ck), lambda i, k: (0, k)),
                  pl.BlockSpec((ck, d), lambda i, k: (k, 0)),
                  vec, per_b],
        out_specs=pl.BlockSpec((tm, d), lambda i, k: (i, 0)),
        out_shape=jax.ShapeDtypeStruct((m, d), F32),
        scratch_shapes=[pltpu.VMEM((tm, d), BF16)],
        compiler_params=pltpu.CompilerParams(dimension_semantics=("parallel", "arbitrary"),
                                             vmem_limit_bytes=VMEM_LIMIT),
        name="ffn",
    )(x2, g_pre, scale, shift, w_up, w_down, g_post, gate)


def _constants(seq):
    n_cmp_pad = seq // CMP_STRIDE
    n_cmp = (seq - CMP_LEN) // CMP_STRIDE + 1
    n_sel = seq // SEL_LEN
    cs = np.arange(n_cmp) * CMP_STRIDE
    ss = np.arange(n_sel) * SEL_LEN
    overlap = np.clip(np.minimum(cs[:, None] + CMP_LEN, ss[None, :] + SEL_LEN)
                      - np.maximum(cs[:, None], ss[None, :]), 0, None)
    selmap_t = np.zeros((n_sel, n_cmp_pad), np.float32)
    selmap_t[:, :n_cmp] = (overlap / CMP_LEN).T
    keys = np.arange(seq)
    et_sel = (keys[:, None] // SEL_LEN == np.arange(LANES)[None, :]).astype(np.float32)
    et_blk = (keys[:, None] // MOBA_BLOCK == np.arange(LANES)[None, :]).astype(np.float32)
    nb_rows = -(-(seq // MOBA_BLOCK) // BF16_SUBLANES) * BF16_SUBLANES
    avg = et_blk.T[:nb_rows] / MOBA_BLOCK
    to = lambda a: jnp.asarray(a, BF16)
    return to(selmap_t), to(et_sel), to(et_blk), to(avg)


def _rope_tables(seq):
    pos = jnp.arange(seq, dtype=F32)
    inv = ROPE_THETA ** (-jnp.arange(0, HEAD_DIM, 2, dtype=F32) / HEAD_DIM)
    ang = pos[:, None] * inv[None, :]
    cos, sin = jnp.cos(ang), jnp.sin(ang)
    return jnp.concatenate([cos, cos], axis=1), jnp.concatenate([-sin, sin], axis=1)


def kernel(x, c, w_ada, b_ada, pre_norm_mix, post_norm_mix, w_in, cmp_k_pos, cmp_k_w1, cmp_k_w2,
           cmp_v_pos, cmp_v_w1, cmp_v_w2, w_o, pre_norm_ffn, post_norm_ffn, w_up, w_down):
    bsz, seq, d = x.shape
    depth = w_ada.shape[0]
    cos2, sin2 = _rope_tables(seq)
    selmap, et_sel, et_blk, avg = _constants(seq)
    x2 = x.reshape(bsz * seq, d)
    for l in range(depth):
        mod, w_head, w_tail = _adaln(c, w_ada[l], b_ada[l], w_in[l].T)
        shift_m, scale_m, gate_m, shift_f, scale_f, gate_f = [
            a.reshape(bsz, 1, d) for a in jnp.split(mod, 6, axis=-1)]
        row = lambda a: a.reshape(1, d)

        qkv, kvc = _inproj(x2, row(pre_norm_mix[l]), scale_m, shift_m, w_head, w_tail,
                           cos2, sin2, seq)
        kvc4 = kvc.reshape(2 * NSA_KV_GROUPS, bsz, seq // CMP_STRIDE, CMP_STRIDE * HEAD_DIM)
        pos2 = jnp.stack([cmp_k_pos[l].reshape(1, -1), cmp_v_pos[l].reshape(1, -1)])
        kcv = _compress(kvc4, pos2, jnp.stack([cmp_k_w1[l], cmp_v_w1[l]]),
                        jnp.stack([cmp_k_w2[l], cmp_v_w2[l]]))
        o_n, (w_o_bf,) = _nsa(qkv, kcv, selmap, et_sel, bsz, seq, [(w_o[l], 0)])
        o_m, (w_up_bf, w_down_bf) = _moba(qkv, avg, et_blk, bsz, seq, [(w_up[l], 1), (w_down[l], 0)])
        x2 = _outproj(o_n, o_m, w_o_bf, x2, row(post_norm_mix[l]), gate_m, seq)
        x2 = _ffn(x2, row(pre_norm_ffn[l]), scale_f, shift_f, w_up_bf, w_down_bf,
                  row(post_norm_ffn[l]), gate_f, seq)
    return x2.reshape(bsz, seq, d)
```

```python
import functools

import numpy as np
import jax
import jax.numpy as jnp
from jax import lax
from jax.experimental import pallas as pl
from jax.experimental.pallas import tpu as pltpu

F32 = jnp.float32
BF16 = jnp.bfloat16
I32 = jnp.int32

HEAD_DIM = 128
NSA_HEADS = 8
NSA_KV_GROUPS = 2
NSA_HPG = NSA_HEADS // NSA_KV_GROUPS
MOBA_HEADS = 8
CMP_LEN = 32
CMP_STRIDE = 16
SEL_LEN = 64
SEL_TOPK = 16
WINDOW = 512
MOBA_BLOCK = 256
MOBA_TOPK = 3
ROPE_THETA = 10000.0
EPS = 1e-6

LANES = 128
BF16_SUBLANES = 16
MXU_COLS = 256
FFN_ROW_BLOCK = 256
NSA_TQ = 256
NSA_GROUPS_PER_STEP = 2
MOBA_HEADS_PER_STEP = 8
Q_SCALE = HEAD_DIM ** -0.5 * float(np.log2(np.e))
NEG_BIG = -1e30
VMEM_LIMIT = 56 * 1024 * 1024

QN0 = 0
QM0 = 8
KM0 = 16
VM0 = 24
KC0 = 32
VC0 = 34
KS0 = 36
VS0 = 38
KW0 = 40
VW0 = 42
GT0 = 44
N_BLOCKS = 48
INPROJ_TN = 1024
HEAD_COLS = 2560
GATE_COLS = 3 * NSA_HEADS
IN_SLAB = 256
ROPE_Q, ROPE_K, PLAIN, GATES, ZERO = range(5)
_INPROJ_TILES = (
    ("head", 0, (ROPE_Q,) * 4),
    ("tail", 0, (ROPE_Q,) * 4),
    ("tail", 1, (ROPE_K,) * 4),
    ("tail", 2, (PLAIN,) * 4),
    ("head", 1, (ROPE_K, PLAIN, ROPE_K, PLAIN)),
    ("head", 2, (ROPE_K, PLAIN, GATES, ZERO)),
)


def _nt_dot(a, b):
    return lax.dot_general(a, b, (((1,), (1,)), ((), ())), preferred_element_type=F32)


def _rms_norm(x, g):
    ms = jnp.mean(x * x, axis=-1, keepdims=True)
    return x * lax.rsqrt(ms + EPS) * g


def _split_bf16(a):
    hi = a.astype(BF16)
    lo = (a - hi.astype(F32)).astype(BF16)
    return hi, lo


def _adaln_kernel(ct_ref, w_ref, b_ref, wh_ref, wa_ref, wb_ref, o_ref, head_ref, tail_ref):
    ct = ct_ref[...]
    a = ct * (1.0 / (1.0 + jnp.exp(-ct)))
    w = w_ref[...]
    for b in range(o_ref.shape[0]):
        o_ref[b] = jnp.sum(w * a[:, b:b + 1], axis=0, keepdims=True) + b_ref[...]

    t = pl.program_id(0)
    tail_ref[...] = jnp.concatenate([wa_ref[GATE_COLS:, :], wb_ref[:GATE_COLS, :]], axis=0).astype(BF16)

    n_plain = HEAD_COLS // IN_SLAB

    @pl.when(t < n_plain)
    def _():
        head_ref[...] = wh_ref[...].astype(BF16)

    @pl.when(t == n_plain)
    def _():
        per_group = 3 * NSA_HPG
        g = wh_ref[:LANES, :]
        row = lax.broadcasted_iota(I32, g.shape, 0)
        for grp in range(NSA_KV_GROUPS):
            rows = g if grp == 0 else pltpu.roll(g, LANES - grp * per_group, 0)
            head_ref[grp * LANES:(grp + 1) * LANES, :] = jnp.where(row < per_group, rows, 0.0).astype(BF16)

    @pl.when(t > n_plain)
    def _():
        head_ref[...] = jnp.zeros(head_ref.shape, BF16)


def _adaln(c, w, b, w_in_t):
    bsz, d = c.shape
    n = w.shape[1]
    tn = 1024
    steps = n // tn
    n_src = sum(1 for src, _, _ in _INPROJ_TILES if src == "head") * INPROJ_TN
    n_plain = HEAD_COLS // IN_SLAB
    assert bsz <= LANES and steps * IN_SLAB == n_src and HEAD_COLS % IN_SLAB == 0 and n_plain < steps
    assert IN_SLAB == NSA_KV_GROUPS * LANES and w_in_t.shape[0] == HEAD_COLS + GATE_COLS + n_src
    assert GATE_COLS % 8 == 0
    ct = jnp.zeros((d, LANES), F32).at[:, :bsz].set(c.T)
    slab = lambda f: pl.BlockSpec((IN_SLAB, d), lambda j: (f(j), 0))
    out, w_head, w_tail = pl.pallas_call(
        _adaln_kernel,
        grid=(steps,),
        in_specs=[pl.BlockSpec((d, LANES), lambda j: (0, 0)),
                  pl.BlockSpec((d, tn), lambda j: (0, j)),
                  pl.BlockSpec((1, tn), lambda j: (0, j)),
                  slab(lambda j: jnp.minimum(j, n_plain)),
                  slab(lambda j: n_plain + j),
                  slab(lambda j: n_plain + j + 1)],
        out_specs=[pl.BlockSpec((bsz, 1, tn), lambda j: (0, 0, j)), slab(lambda j: j), slab(lambda j: j)],
        out_shape=[jax.ShapeDtypeStruct((bsz, 1, n), F32),
                   jax.ShapeDtypeStruct((n_src, d), BF16), jax.ShapeDtypeStruct((n_src, d), BF16)],
        compiler_params=pltpu.CompilerParams(dimension_semantics=("parallel",),
                                             vmem_limit_bytes=VMEM_LIMIT),
        name="adaln",
    )(ct, w, b.reshape(1, n), w_in_t, w_in_t, w_in_t)
    return out.reshape(bsz, n), w_head, w_tail


def _inproj_kernel(x_ref, g_ref, sc_ref, sh_ref, wh_ref, wt_ref, cos_ref, sin_ref,
                   o_ref, kvc_ref, h_ref):
    j = pl.program_id(1)

    @pl.when(j == 0)
    def _():
        h = _rms_norm(x_ref[...], g_ref[...]) * (1.0 + sc_ref[...]) + sh_ref[...]
        h_ref[...] = h.astype(BF16)

    def tile(w_ref, kinds):
        for ci, kind in enumerate(kinds):
            c0 = ci * MXU_COLS
            if kind == ZERO:
                o_ref[:, c0:c0 + MXU_COLS] = jnp.zeros((o_ref.shape[0], MXU_COLS), BF16)
                continue
            acc = _nt_dot(h_ref[...], w_ref[c0:c0 + MXU_COLS, :])
            for c in range(c0, c0 + MXU_COLS, LANES):
                ch = acc[:, c - c0:c - c0 + LANES]
                if kind in (ROPE_Q, ROPE_K):
                    ch = ch * cos_ref[...] + pltpu.roll(ch, HEAD_DIM // 2, 1) * sin_ref[...]
                if kind == ROPE_Q:
                    ch = ch * Q_SCALE
                o_ref[:, c:c + LANES] = ch.astype(BF16)

    for jj, (src, _, kinds) in enumerate(_INPROJ_TILES):
        pl.when(j == jj)(functools.partial(tile, wh_ref if src == "head" else wt_ref, kinds))

    jkv, ok = divmod(KC0 * LANES, INPROJ_TN)
    ov = VC0 * LANES - jkv * INPROJ_TN
    assert 0 < ov < INPROJ_TN

    @pl.when(j == jkv)
    def _():
        for g in range(NSA_KV_GROUPS):
            kvc_ref[g] = o_ref[:, ok + g * LANES: ok + (g + 1) * LANES]
            kvc_ref[NSA_KV_GROUPS + g] = o_ref[:, ov + g * LANES: ov + (g + 1) * LANES]


def _step_lookup(j, table):
    out = jnp.int32(table[0])
    for k in range(1, len(table)):
        out = jnp.where(j >= k, table[k], out)
    return out


def _inproj(x2, g, scale, shift, w_head, w_tail, cos2, sin2, seq):
    m, d = x2.shape
    tm = min(1024, seq)
    tn = INPROJ_TN
    tiles_per_b = seq // tm
    blocks = {"head": [], "tail": []}
    for src, blk, _ in _INPROJ_TILES:
        for name, lst in blocks.items():
            lst.append(blk if src == name else (lst[-1] if lst else None))
    for lst in blocks.values():
        first = next(b for b in lst if b is not None)
        lst[:] = [first if b is None else b for b in lst]
    return pl.pallas_call(
        _inproj_kernel,
        grid=(m // tm, len(_INPROJ_TILES)),
        in_specs=[pl.BlockSpec((tm, d), lambda i, j: (i, 0)),
                  pl.BlockSpec((1, d), lambda i, j: (0, 0)),
                  pl.BlockSpec((None, 1, d), lambda i, j: (i // tiles_per_b, 0, 0)),
                  pl.BlockSpec((None, 1, d), lambda i, j: (i // tiles_per_b, 0, 0)),
                  pl.BlockSpec((tn, d), lambda i, j: (_step_lookup(j, blocks["head"]), 0)),
                  pl.BlockSpec((tn, d), lambda i, j: (_step_lookup(j, blocks["tail"]), 0)),
                  pl.BlockSpec((tm, LANES), lambda i, j: (i % tiles_per_b, 0)),
                  pl.BlockSpec((tm, LANES), lambda i, j: (i % tiles_per_b, 0))],
        out_specs=[pl.BlockSpec((tm, tn), lambda i, j: (i, j)),
                   pl.BlockSpec((2 * NSA_KV_GROUPS, tm, LANES), lambda i, j: (0, i, 0))],
        out_shape=[jax.ShapeDtypeStruct((m, N_BLOCKS * LANES), BF16),
                   jax.ShapeDtypeStruct((2 * NSA_KV_GROUPS, m, LANES), BF16)],
        scratch_shapes=[pltpu.VMEM((tm, d), BF16)],
        compiler_params=pltpu.CompilerParams(dimension_semantics=("parallel", "arbitrary"),
                                             vmem_limit_bytes=VMEM_LIMIT),
        name="inproj",
    )(x2, g, scale, shift, w_head, w_tail, cos2, sin2)


def _compress_kernel(x_ref, pos_ref, w1_ref, w2_ref, o_ref):
    half = CMP_STRIDE * HEAD_DIM
    x = x_ref[...].astype(F32)
    pos = pos_ref[...]
    w1 = w1_ref[...].astype(BF16)
    a = jnp.dot((x + pos[:, :half]).astype(BF16), w1[:half], preferred_element_type=F32)
    b = jnp.dot((x + pos[:, half:]).astype(BF16), w1[half:], preferred_element_type=F32)
    n_rows = x.shape[0]
    h1 = a + pltpu.roll(b, n_rows - 1, 0)
    h = h1 * (1.0 / (1.0 + jnp.exp(-h1)))
    o_ref[...] = jnp.dot(h.astype(BF16), w2_ref[...].astype(BF16),
                         preferred_element_type=F32).astype(BF16)


def _compress(kvc4, pos2, w1s, w2s):
    ns, bsz, n_rows, width = kvc4.shape
    return pl.pallas_call(
        _compress_kernel,
        grid=(ns, bsz),
        in_specs=[pl.BlockSpec((None, None, n_rows, width), lambda s, b: (s, b, 0, 0)),
                  pl.BlockSpec((None, 1, 2 * width), lambda s, b: (s // NSA_KV_GROUPS, 0, 0)),
                  pl.BlockSpec((None, 2 * width, HEAD_DIM), lambda s, b: (s // NSA_KV_GROUPS, 0, 0)),
                  pl.BlockSpec((None, HEAD_DIM, HEAD_DIM), lambda s, b: (s // NSA_KV_GROUPS, 0, 0))],
        out_specs=pl.BlockSpec((None, None, n_rows, HEAD_DIM), lambda s, b: (s, b, 0, 0)),
        out_shape=jax.ShapeDtypeStruct((ns, bsz, n_rows, HEAD_DIM), BF16),
        compiler_params=pltpu.CompilerParams(dimension_semantics=("parallel", "parallel"),
                                             vmem_limit_bytes=VMEM_LIMIT),
        name="compress",
    )(kvc4, pos2, w1s, w2s)


def _topk_rows(score, k):
    idx = lax.broadcasted_iota(I32, score.shape, 0).astype(F32)
    work = score
    sel = jnp.zeros(score.shape, F32)
    for _ in range(k):
        mx = jnp.max(work, axis=0, keepdims=True)
        first = jnp.min(jnp.where(work == mx, idx, float(score.shape[0])), axis=0, keepdims=True)
        hit = idx == first
        sel = jnp.where(hit, 1.0, sel)
        work = jnp.where(hit, -jnp.inf, work)
    return sel


def _bias_from_keep(keep_t):
    n, q = keep_t.shape
    padded = jnp.concatenate([keep_t, jnp.zeros((LANES - n, q), F32)], axis=0) if n < LANES else keep_t
    return jnp.where(padded.T > 0.5, 0.0, NEG_BIG).astype(BF16)


def _with_ones(v):
    return jnp.concatenate([v, jnp.ones(v.shape, v.dtype)], axis=1)


def _flash_step(qa, ka, va, m, acc, mask, penalty=None):
    s = _nt_dot(qa, ka)
    if mask is not None:
        s = jnp.where(mask, s, NEG_BIG)
    if penalty is not None:
        s = s + penalty
    m_new = jnp.maximum(m, jnp.max(s, axis=1, keepdims=True))
    alpha = jnp.exp2(m - m_new)
    p = jnp.exp2(s - m_new).astype(BF16)
    acc = alpha * acc + jnp.dot(p, va, preferred_element_type=F32)
    return m_new, acc


def _flash_out(acc):
    return acc[:, :HEAD_DIM] * (1.0 / acc[:, HEAD_DIM:])


def _cast_specs(casts, n_steps, step_of):
    specs, shapes = [], []
    for w, axis in casts:
        size = w.shape[axis] // n_steps
        assert w.shape[axis] % n_steps == 0 and size % (BF16_SUBLANES if axis == 0 else LANES) == 0
        if axis == 0:
            specs.append(pl.BlockSpec((size, w.shape[1]), lambda *g: (step_of(*g), 0)))
        else:
            specs.append(pl.BlockSpec((w.shape[0], size), lambda *g: (0, step_of(*g))))
        shapes.append(jax.ShapeDtypeStruct(w.shape, BF16))
    return specs, shapes


def _cast_slabs(src_refs, dst_refs):
    for src, dst in zip(src_refs, dst_refs):
        dst[...] = src[...].astype(BF16)


def _nsa_kernel(q_ref, kc_ref, vc_ref, ks_ref, vs_ref, kw_ref, vw_ref, gl_ref,
                selmap_ref, et_ref, *rest, tq, seq, ng, n_cast):
    cast_in, (o_ref, *cast_out) = rest[:n_cast], rest[n_cast:]
    _cast_slabs(cast_in, cast_out)
    tp = pl.program_id(2)
    rows = NSA_HPG * tq
    n_cmp_pad = kc_ref.shape[1]
    n_sel = seq // SEL_LEN
    gw = NSA_HPG * LANES
    grp = lambda g: slice(g * LANES, (g + 1) * LANES)

    qrow = lax.broadcasted_iota(I32, (rows, tq), 0) & (tq - 1)
    kcol = lax.broadcasted_iota(I32, (rows, tq), 1)
    m0 = jnp.full((rows, 1), NEG_BIG, F32)
    a0 = jnp.zeros((rows, 2 * HEAD_DIM), F32)
    trow = lax.broadcasted_iota(I32, (rows, n_cmp_pad), 0) & (tq - 1)
    cend = lax.broadcasted_iota(I32, (rows, n_cmp_pad), 1) * CMP_STRIDE + (CMP_LEN - 1)
    blk = lax.broadcasted_iota(I32, (n_sel, tq), 0)
    tcol = lax.broadcasted_iota(I32, (n_sel, tq), 1)
    n_forced = 3
    n_back = WINDOW // tq
    chunk = lambda j: pl.ds(pl.multiple_of(j * tq, tq), tq)

    def kv_chunk(g, j):
        ka = jnp.concatenate([ks_ref[chunk(j), grp(g)], et_ref[chunk(j), :]], axis=1)
        return ka, _with_ones(vs_ref[chunk(j), grp(g)])

    def front(g, par):
        qi = 2 * tp + par
        t0 = qi * tq
        tile = slice(par * tq, (par + 1) * tq)
        q = q_ref[tile, g * gw:(g + 1) * gw]
        qs = jnp.concatenate([q[:, h * LANES:(h + 1) * LANES] for h in range(NSA_HPG)], axis=0)
        cb = (t0 + tcol) // SEL_LEN
        forced = (blk == 0) | (blk == cb) | (blk == cb - 1)
        free = jnp.where(blk <= cb, jnp.where(forced, 0.0, 1.0), 0.0)

        s = jnp.where(cend <= t0 + trow, _nt_dot(qs, kc_ref[g]), -jnp.inf)
        mx = jnp.max(s, axis=1, keepdims=True)
        mx = jnp.where(mx == -jnp.inf, 0.0, mx)
        e = jnp.exp2(s - mx)
        den = jnp.maximum(jnp.sum(e, axis=1, keepdims=True), 1e-30)
        p_c = e * (1.0 / den)
        o_c = jnp.dot(p_c.astype(BF16), vc_ref[g], preferred_element_type=F32)

        p_sum = p_c[0:tq]
        for h in range(1, NSA_HPG):
            p_sum = p_sum + p_c[h * tq:(h + 1) * tq]
        hi, lo = _split_bf16(p_sum)
        smap_t = selmap_ref[...]
        imp_t = _nt_dot(smap_t, hi) + _nt_dot(smap_t, lo)

        win = (m0, a0)
        for back in range(n_back + 1):
            rows_b = chunk(jnp.maximum(qi - back, 0))
            mask = (kcol <= qrow) if back == 0 else (kcol > qrow) if back == n_back else None
            exists = None if back == 0 else jnp.where(qi >= back, 0.0, NEG_BIG)
            win = _flash_step(qs, kw_ref[rows_b, grp(g)], _with_ones(vw_ref[rows_b, grp(g)]),
                              *win, mask, exists)
        o_w = _flash_out(win[1])

        picked = _topk_rows(jnp.where(free > 0.5, imp_t, -jnp.inf),
                            max(min(SEL_TOPK, n_sel) - n_forced, 0))
        selb = _bias_from_keep(jnp.where(forced, 1.0, picked * free))
        qa = jnp.concatenate([qs, jnp.concatenate([selb] * NSA_HPG, axis=0)], axis=1)

        carry = _flash_step(qa, *kv_chunk(g, qi), m0, a0, kcol <= qrow)
        if par == 1:
            carry = _flash_step(qa, *kv_chunk(g, qi - 1), *carry, None)
        return qa, o_c, o_w, carry, tile

    fronts = [front(g, par) for g in range(ng) for par in range(2)]

    def body(i, c):
        new = []
        for k, f in enumerate(fronts):
            g = k // 2
            ck = _flash_step(f[0], *kv_chunk(g, 2 * i), c[2 * k], c[2 * k + 1], None)
            new += _flash_step(f[0], *kv_chunk(g, 2 * i + 1), *ck, None)
        return tuple(new)

    final = lax.fori_loop(0, tp, body, tuple(x for f in fronts for x in f[3]))

    for k, (_, o_c, o_w, _, tile) in enumerate(fronts):
        g = k // 2
        o_s = _flash_out(final[2 * k + 1])
        sig = 1.0 / (1.0 + jnp.exp(-gl_ref[tile, grp(g)].astype(F32)))
        for h in range(NSA_HPG):
            r = slice(h * tq, (h + 1) * tq)
            o = (sig[:, 3 * h:3 * h + 1] * o_c[r] + sig[:, 3 * h + 1:3 * h + 2] * o_s[r]
                 + sig[:, 3 * h + 2:3 * h + 3] * o_w[r])
            o_ref[tile, g * gw + h * LANES:g * gw + (h + 1) * LANES] = o.astype(BF16)


def _nsa(qkv, kcv, selmap, et_sel, bsz, seq, casts):
    m = qkv.shape[0]
    tq = NSA_TQ
    ng = NSA_GROUPS_PER_STEP
    assert seq % tq == 0 and WINDOW % tq == 0 and tq % SEL_LEN == 0 and NSA_KV_GROUPS % ng == 0
    assert all(c0 % ng == 0 for c0 in (KS0, VS0, KW0, VW0, GT0))
    assert seq % (2 * tq) == 0
    nq = seq // (2 * tq)
    ngs = NSA_KV_GROUPS // ng
    n_cmp_pad = kcv.shape[2]
    gw = ng * NSA_HPG * LANES
    full = lambda c0: pl.BlockSpec((seq, ng * LANES), lambda b, g, i: (b, c0 // ng + g))
    cast_specs, cast_shapes = _cast_specs(casts, bsz * ngs * nq, lambda b, g, i: (b * ngs + g) * nq + i)
    outs = pl.pallas_call(
        functools.partial(_nsa_kernel, tq=tq, seq=seq, ng=ng, n_cast=len(casts)),
        grid=(bsz, ngs, nq),
        in_specs=[pl.BlockSpec((2 * tq, gw), lambda b, g, i: (b * nq + i, g)),
                  pl.BlockSpec((ng, None, n_cmp_pad, LANES), lambda b, g, i: (g, b, 0, 0)),
                  pl.BlockSpec((ng, None, n_cmp_pad, LANES), lambda b, g, i: (ngs + g, b, 0, 0)),
                  full(KS0), full(VS0), full(KW0), full(VW0),
                  pl.BlockSpec((2 * tq, ng * LANES), lambda b, g, i: (b * nq + i, GT0 // ng + g)),
                  pl.BlockSpec(selmap.shape, lambda b, g, i: (0, 0)),
                  pl.BlockSpec(et_sel.shape, lambda b, g, i: (0, 0))] + cast_specs,
        out_specs=[pl.BlockSpec((2 * tq, gw), lambda b, g, i: (b * nq + i, g))] + cast_specs,
        out_shape=[jax.ShapeDtypeStruct((m, NSA_HEADS * HEAD_DIM), BF16)] + cast_shapes,
        compiler_params=pltpu.CompilerParams(
            dimension_semantics=("parallel", "parallel", "arbitrary"),
            vmem_limit_bytes=VMEM_LIMIT),
        name="nsa_attn",
    )(qkv, kcv, kcv, qkv, qkv, qkv, qkv, qkv, selmap, et_sel, *[w for w, _ in casts])
    return outs[0], outs[1:]


def _moba_kernel(q_ref, k_ref, v_ref, avg_ref, et_ref, *rest, seq, hb, n_cast):
    cast_in, (o_ref, *cast_out), (kmh_ref, kml_ref) = rest[:n_cast], rest[n_cast:2 * n_cast + 1], rest[2 * n_cast + 1:]
    _cast_slabs(cast_in, cast_out)
    tp = pl.program_id(2)
    blk_len = MOBA_BLOCK
    rows = 2 * blk_len
    nb = seq // blk_len
    nb_rows = avg_ref.shape[0]
    head = lambda h: slice(h * LANES, (h + 1) * LANES)

    @pl.when(tp == 0)
    def _():
        for h in range(hb):
            km = jnp.dot(avg_ref[...], k_ref[:, head(h)], preferred_element_type=F32)
            kmh_ref[h], kml_ref[h] = _split_bf16(km)

    blk = lax.broadcasted_iota(I32, (nb_rows, rows), 0)
    qblk = 2 * tp + lax.broadcasted_iota(I32, (nb_rows, rows), 1) // blk_len
    past = jnp.where(blk < qblk, 1.0, 0.0)
    own = jnp.where(blk == qblk, 1.0, 0.0)
    qas = []
    for h in range(hb):
        q = q_ref[:, head(h)]
        gate_t = _nt_dot(kmh_ref[h], q) + _nt_dot(kml_ref[h], q)
        picked = _topk_rows(jnp.where(past > 0.5, gate_t, -jnp.inf), min(MOBA_TOPK, nb - 1))
        qas.append(jnp.concatenate([q, _bias_from_keep(picked * past + own)], axis=1))

    def kv_chunk(j, h):
        keys = pl.ds(pl.multiple_of(j * blk_len, blk_len), blk_len)
        ka = jnp.concatenate([k_ref[keys, head(h)], et_ref[keys, :]], axis=1)
        return ka, _with_ones(v_ref[keys, head(h)])

    row = lax.broadcasted_iota(I32, (rows, blk_len), 0)
    colk = lax.broadcasted_iota(I32, (rows, blk_len), 1)
    m0 = jnp.full((rows, 1), NEG_BIG, F32)
    a0 = jnp.zeros((rows, 2 * HEAD_DIM), F32)

    def all_heads(j, c, mask=None):
        new = []
        for h in range(hb):
            new += _flash_step(qas[h], *kv_chunk(j, h), c[2 * h], c[2 * h + 1], mask)
        return tuple(new)

    carry = all_heads(2 * tp, (m0, a0) * hb, (row >= blk_len) | (colk <= row))
    carry = all_heads(2 * tp + 1, carry, colk <= row - blk_len)
    final = lax.fori_loop(0, tp, lambda i, c: all_heads(2 * i + 1, all_heads(2 * i, c)), carry)
    for h in range(hb):
        o_ref[:, head(h)] = _flash_out(final[2 * h + 1]).astype(BF16)


def _moba(qkv, avg, et_blk, bsz, seq, casts):
    m = qkv.shape[0]
    assert seq % (2 * MOBA_BLOCK) == 0
    nq = seq // (2 * MOBA_BLOCK)
    hb = MOBA_HEADS_PER_STEP
    assert MOBA_HEADS % hb == 0 and QM0 % hb == 0 and KM0 % hb == 0 and VM0 % hb == 0
    hw = hb * LANES
    ng = MOBA_HEADS // hb
    cast_specs, cast_shapes = _cast_specs(casts, bsz * ng * nq, lambda b, h, i: (b * ng + h) * nq + i)
    whole = lambda c0: pl.BlockSpec((seq, hw), lambda b, h, i: (b, c0 // hb + h),
                                    pipeline_mode=pl.Buffered(1))
    outs = pl.pallas_call(
        functools.partial(_moba_kernel, seq=seq, hb=hb, n_cast=len(casts)),
        grid=(bsz, ng, nq),
        in_specs=[pl.BlockSpec((2 * MOBA_BLOCK, hw), lambda b, h, i: (b * nq + i, QM0 // hb + h)),
                  whole(KM0), whole(VM0),
                  pl.BlockSpec(avg.shape, lambda b, h, i: (0, 0)),
                  pl.BlockSpec(et_blk.shape, lambda b, h, i: (0, 0))] + cast_specs,
        out_specs=[pl.BlockSpec((2 * MOBA_BLOCK, hw), lambda b, h, i: (b * nq + i, h))] + cast_specs,
        out_shape=[jax.ShapeDtypeStruct((m, MOBA_HEADS * HEAD_DIM), BF16)] + cast_shapes,
        scratch_shapes=[pltpu.VMEM((hb, avg.shape[0], HEAD_DIM), BF16),
                        pltpu.VMEM((hb, avg.shape[0], HEAD_DIM), BF16)],
        compiler_params=pltpu.CompilerParams(
            dimension_semantics=("parallel", "parallel", "arbitrary"),
            vmem_limit_bytes=VMEM_LIMIT),
        name="moba_attn",
    )(qkv, qkv, qkv, avg, et_blk, *[w for w, _ in casts])
    return outs[0], outs[1:]


def _outproj_kernel(on_ref, om_ref, w_ref, x_ref, g_ref, gate_ref, *rest, n_cast):
    cast_in, (o_ref, *cast_out) = rest[:n_cast], rest[n_cast:]
    _cast_slabs(cast_in, cast_out)
    kn = on_ref.shape[1]
    acc = (jnp.dot(on_ref[...], w_ref[:kn, :], preferred_element_type=F32)
           + jnp.dot(om_ref[...], w_ref[kn:, :], preferred_element_type=F32))
    o_ref[...] = x_ref[...] + gate_ref[...] * _rms_norm(acc, g_ref[...])


def _outproj(o_n, o_m, w_o, x2, g_post, gate, seq, casts):
    m, d = x2.shape
    tm = min(512, seq)
    tiles_per_b = seq // tm
    cast_specs, cast_shapes = _cast_specs(casts, m // tm, lambda i: i)
    outs = pl.pallas_call(
        functools.partial(_outproj_kernel, n_cast=len(casts)),
        grid=(m // tm,),
        in_specs=[pl.BlockSpec((tm, o_n.shape[1]), lambda i: (i, 0)),
                  pl.BlockSpec((tm, o_m.shape[1]), lambda i: (i, 0)),
                  pl.BlockSpec(w_o.shape, lambda i: (0, 0), pipeline_mode=pl.Buffered(1)),
                  pl.BlockSpec((tm, d), lambda i: (i, 0)),
                  pl.BlockSpec((1, d), lambda i: (0, 0)),
                  pl.BlockSpec((None, 1, d), lambda i: (i // tiles_per_b, 0, 0))] + cast_specs,
        out_specs=[pl.BlockSpec((tm, d), lambda i: (i, 0))] + cast_specs,
        out_shape=[jax.ShapeDtypeStruct((m, d), F32)] + cast_shapes,
        compiler_params=pltpu.CompilerParams(dimension_semantics=("parallel",),
                                             vmem_limit_bytes=VMEM_LIMIT),
        name="outproj",
    )(o_n, o_m, w_o, x2, g_post, gate, *[w for w, _ in casts])
    return outs[0], outs[1:]


def _ffn_kernel(x_ref, gpre_ref, sc_ref, sh_ref, wu_ref, wd_ref, gpost_ref, gate_ref,
                o_ref, h_ref):
    k = pl.program_id(1)
    last = pl.num_programs(1) - 1
    row_blocks = [slice(r, r + FFN_ROW_BLOCK) for r in range(0, x_ref.shape[0], FFN_ROW_BLOCK)]

    def mlp(h):
        u = jnp.dot(h, wu_ref[...], preferred_element_type=F32)
        u = jnp.square(jnp.maximum(u, 0.0)).astype(BF16)
        return jnp.dot(u, wd_ref[...], preferred_element_type=F32)

    @pl.when(k == 0)
    def _():
        for rows in row_blocks:
            h = _rms_norm(x_ref[rows], gpre_ref[...]) * (1.0 + sc_ref[...]) + sh_ref[...]
            h_ref[rows] = h.astype(BF16)
            o_ref[rows] = mlp(h_ref[rows])

    @pl.when(jnp.logical_and(k > 0, k < last))
    def _():
        o_ref[...] += mlp(h_ref[...])

    @pl.when(k == last)
    def _():
        for rows in row_blocks:
            f = o_ref[rows] + mlp(h_ref[rows])
            o_ref[rows] = x_ref[rows] + gate_ref[...] * _rms_norm(f, gpost_ref[...])


def _ffn(x2, g_pre, scale, shift, w_up, w_down, g_post, gate, seq):
    m, d = x2.shape
    dff = w_up.shape[1]
    tm = min(1024, seq)
    ck = 512
    assert dff // ck >= 2 and tm % FFN_ROW_BLOCK == 0
    tiles_per_b = seq // tm
    vec = pl.BlockSpec((1, d), lambda i, k: (0, 0))
    per_b = pl.BlockSpec((None, 1, d), lambda i, k: (i // tiles_per_b, 0, 0))
    return pl.pallas_call(
        _ffn_kernel,
        grid=(m // tm, dff // ck),
        in_specs=[pl.BlockSpec((tm, d), lambda i, k: (i, 0)), vec, per_b, per_b,
                  pl.BlockSpec((d, ck), lambda i, k: (0, k)),
                  pl.BlockSpec((ck, d), lambda i, k: (k, 0)),
                  vec, per_b],
        out_specs=pl.BlockSpec((tm, d), lambda i, k: (i, 0)),
        out_shape=jax.ShapeDtypeStruct((m, d), F32),
        scratch_shapes=[pltpu.VMEM((tm, d), BF16)],
        compiler_params=pltpu.CompilerParams(dimension_semantics=("parallel", "arbitrary"),
                                             vmem_limit_bytes=VMEM_LIMIT),
        name="ffn",
    )(x2, g_pre, scale, shift, w_up, w_down, g_post, gate)


def _constants(seq):
    n_cmp_pad = seq // CMP_STRIDE
    n_cmp = (seq - CMP_LEN) // CMP_STRIDE + 1
    n_sel = seq // SEL_LEN
    cs = np.arange(n_cmp) * CMP_STRIDE
    ss = np.arange(n_sel) * SEL_LEN
    overlap = np.clip(np.minimum(cs[:, None] + CMP_LEN, ss[None, :] + SEL_LEN)
                      - np.maximum(cs[:, None], ss[None, :]), 0, None)
    selmap_t = np.zeros((n_sel, n_cmp_pad), np.float32)
    selmap_t[:, :n_cmp] = (overlap / CMP_LEN).T
    keys = np.arange(seq)
    et_sel = (keys[:, None] // SEL_LEN == np.arange(LANES)[None, :]).astype(np.float32)
    et_blk = (keys[:, None] // MOBA_BLOCK == np.arange(LANES)[None, :]).astype(np.float32)
    nb_rows = -(-(seq // MOBA_BLOCK) // BF16_SUBLANES) * BF16_SUBLANES
    avg = et_blk.T[:nb_rows] / MOBA_BLOCK
    to = lambda a: jnp.asarray(a, BF16)
    return to(selmap_t), to(et_sel), to(et_blk), to(avg)


def _rope_tables(seq):
    pos = jnp.arange(seq, dtype=F32)
    inv = ROPE_THETA ** (-jnp.arange(0, HEAD_DIM, 2, dtype=F32) / HEAD_DIM)
    ang = pos[:, None] * inv[None, :]
    cos, sin = jnp.cos(ang), jnp.sin(ang)
    return jnp.concatenate([cos, cos], axis=1), jnp.concatenate([-sin, sin], axis=1)


def kernel(x, c, w_ada, b_ada, pre_norm_mix, post_norm_mix, w_in, cmp_k_pos, cmp_k_w1, cmp_k_w2,
           cmp_v_pos, cmp_v_w1, cmp_v_w2, w_o, pre_norm_ffn, post_norm_ffn, w_up, w_down):
    bsz, seq, d = x.shape
    depth = w_ada.shape[0]
    cos2, sin2 = _rope_tables(seq)
    selmap, et_sel, et_blk, avg = _constants(seq)
    x2 = x.reshape(bsz * seq, d)
    for l in range(depth):
        mod, w_head, w_tail = _adaln(c, w_ada[l], b_ada[l], w_in[l].T)
        shift_m, scale_m, gate_m, shift_f, scale_f, gate_f = [
            a.reshape(bsz, 1, d) for a in jnp.split(mod, 6, axis=-1)]
        row = lambda a: a.reshape(1, d)

        qkv, kvc = _inproj(x2, row(pre_norm_mix[l]), scale_m, shift_m, w_head, w_tail,
                           cos2, sin2, seq)
        kvc4 = kvc.reshape(2 * NSA_KV_GROUPS, bsz, seq // CMP_STRIDE, CMP_STRIDE * HEAD_DIM)
        pos2 = jnp.stack([cmp_k_pos[l].reshape(1, -1), cmp_v_pos[l].reshape(1, -1)])
        kcv = _compress(kvc4, pos2, jnp.stack([cmp_k_w1[l], cmp_v_w1[l]]),
                        jnp.stack([cmp_k_w2[l], cmp_v_w2[l]]))
        o_n, (w_o_bf,) = _nsa(qkv, kcv, selmap, et_sel, bsz, seq, [(w_o[l], 0)])
        o_m, (w_up_bf,) = _moba(qkv, avg, et_blk, bsz, seq, [(w_up[l], 1)])
        x2, (w_down_bf,) = _outproj(o_n, o_m, w_o_bf, x2, row(post_norm_mix[l]), gate_m, seq,
                                    [(w_down[l], 0)])
        x2 = _ffn(x2, row(pre_norm_ffn[l]), scale_f, shift_f, w_up_bf, w_down_bf,
                  row(post_norm_ffn[l]), gate_f, seq)
    return x2.reshape(bsz, seq, d)
```

```python
import functools

import numpy as np
import jax
import jax.numpy as jnp
from jax import lax
from jax.experimental import pallas as pl
from jax.experimental.pallas import tpu as pltpu

F32 = jnp.float32
BF16 = jnp.bfloat16
I32 = jnp.int32

HEAD_DIM = 128
NSA_HEADS = 8
NSA_KV_GROUPS = 2
NSA_HPG = NSA_HEADS // NSA_KV_GROUPS
MOBA_HEADS = 8
CMP_LEN = 32
CMP_STRIDE = 16
SEL_LEN = 64
SEL_TOPK = 16
WINDOW = 512
MOBA_BLOCK = 256
MOBA_TOPK = 3
ROPE_THETA = 10000.0
EPS = 1e-6

LANES = 128
BF16_SUBLANES = 16
MXU_COLS = 256
FFN_ROW_BLOCK = 256
NSA_TQ = 256
NSA_GROUPS_PER_STEP = 2
MOBA_HEADS_PER_STEP = 8
Q_SCALE = HEAD_DIM ** -0.5 * float(np.log2(np.e))
NEG_BIG = -1e30
VMEM_LIMIT = 56 * 1024 * 1024

QN0 = 0
QM0 = 8
KM0 = 16
VM0 = 24
KC0 = 32
VC0 = 34
KS0 = 36
VS0 = 38
KW0 = 40
VW0 = 42
GT0 = 44
N_BLOCKS = 48
INPROJ_TN = 1024
HEAD_COLS = 2560
GATE_COLS = 3 * NSA_HEADS
IN_SLAB = 256
ROPE_Q, ROPE_K, PLAIN, GATES, ZERO = range(5)
_INPROJ_TILES = (
    ("head", 0, (ROPE_Q,) * 4),
    ("tail", 0, (ROPE_Q,) * 4),
    ("tail", 1, (ROPE_K,) * 4),
    ("tail", 2, (PLAIN,) * 4),
    ("head", 1, (ROPE_K, PLAIN, ROPE_K, PLAIN)),
    ("head", 2, (ROPE_K, PLAIN, GATES, ZERO)),
)


def _nt_dot(a, b):
    return lax.dot_general(a, b, (((1,), (1,)), ((), ())), preferred_element_type=F32)


def _rms_norm(x, g):
    ms = jnp.mean(x * x, axis=-1, keepdims=True)
    return x * lax.rsqrt(ms + EPS) * g


def _split_bf16(a):
    hi = a.astype(BF16)
    lo = (a - hi.astype(F32)).astype(BF16)
    return hi, lo


def _adaln_kernel(ct_ref, w_ref, b_ref, wh_ref, wa_ref, wb_ref, o_ref, head_ref, tail_ref):
    ct = ct_ref[...]
    a = ct * (1.0 / (1.0 + jnp.exp(-ct)))
    w = w_ref[...]
    for b in range(o_ref.shape[0]):
        o_ref[b] = jnp.sum(w * a[:, b:b + 1], axis=0, keepdims=True) + b_ref[...]

    t = pl.program_id(0)
    tail_ref[...] = jnp.concatenate([wa_ref[GATE_COLS:, :], wb_ref[:GATE_COLS, :]], axis=0).astype(BF16)

    n_plain = HEAD_COLS // IN_SLAB

    @pl.when(t < n_plain)
    def _():
        head_ref[...] = wh_ref[...].astype(BF16)

    @pl.when(t == n_plain)
    def _():
        per_group = 3 * NSA_HPG
        g = wh_ref[:LANES, :]
        row = lax.broadcasted_iota(I32, g.shape, 0)
        for grp in range(NSA_KV_GROUPS):
            rows = g if grp == 0 else pltpu.roll(g, LANES - grp * per_group, 0)
            head_ref[grp * LANES:(grp + 1) * LANES, :] = jnp.where(row < per_group, rows, 0.0).astype(BF16)

    @pl.when(t > n_plain)
    def _():
        head_ref[...] = jnp.zeros(head_ref.shape, BF16)


def _adaln(c, w, b, w_in_t):
    bsz, d = c.shape
    n = w.shape[1]
    tn = 1024
    steps = n // tn
    n_src = sum(1 for src, _, _ in _INPROJ_TILES if src == "head") * INPROJ_TN
    n_plain = HEAD_COLS // IN_SLAB
    assert bsz <= LANES and steps * IN_SLAB == n_src and HEAD_COLS % IN_SLAB == 0 and n_plain < steps
    assert IN_SLAB == NSA_KV_GROUPS * LANES and w_in_t.shape[0] == HEAD_COLS + GATE_COLS + n_src
    assert GATE_COLS % 8 == 0
    ct = jnp.zeros((d, LANES), F32).at[:, :bsz].set(c.T)
    slab = lambda f: pl.BlockSpec((IN_SLAB, d), lambda j: (f(j), 0))
    out, w_head, w_tail = pl.pallas_call(
        _adaln_kernel,
        grid=(steps,),
        in_specs=[pl.BlockSpec((d, LANES), lambda j: (0, 0)),
                  pl.BlockSpec((d, tn), lambda j: (0, j)),
                  pl.BlockSpec((1, tn), lambda j: (0, j)),
                  slab(lambda j: jnp.minimum(j, n_plain)),
                  slab(lambda j: n_plain + j),
                  slab(lambda j: n_plain + j + 1)],
        out_specs=[pl.BlockSpec((bsz, 1, tn), lambda j: (0, 0, j)), slab(lambda j: j), slab(lambda j: j)],
        out_shape=[jax.ShapeDtypeStruct((bsz, 1, n), F32),
                   jax.ShapeDtypeStruct((n_src, d), BF16), jax.ShapeDtypeStruct((n_src, d), BF16)],
        compiler_params=pltpu.CompilerParams(dimension_semantics=("parallel",),
                                             vmem_limit_bytes=VMEM_LIMIT),
        name="adaln",
    )(ct, w, b.reshape(1, n), w_in_t, w_in_t, w_in_t)
    return out.reshape(bsz, n), w_head, w_tail


def _inproj_kernel(x_ref, g_ref, sc_ref, sh_ref, wh_ref, wt_ref, cos_ref, sin_ref,
                   o_ref, kvc_ref, h_ref, kv32_ref):
    j = pl.program_id(1)

    @pl.when(j == 0)
    def _():
        h = _rms_norm(x_ref[...], g_ref[...]) * (1.0 + sc_ref[...]) + sh_ref[...]
        h_ref[...] = h.astype(BF16)

    jkv, ok = divmod(KC0 * LANES, INPROJ_TN)
    ov = VC0 * LANES - jkv * INPROJ_TN
    assert 0 < ov < INPROJ_TN
    side = {ok + g * LANES: g for g in range(NSA_KV_GROUPS)}
    side.update({ov + g * LANES: NSA_KV_GROUPS + g for g in range(NSA_KV_GROUPS)})

    def tile(jj, w_ref, kinds):
        for ci, kind in enumerate(kinds):
            c0 = ci * MXU_COLS
            if kind == ZERO:
                o_ref[:, c0:c0 + MXU_COLS] = jnp.zeros((o_ref.shape[0], MXU_COLS), BF16)
                continue
            acc = _nt_dot(h_ref[...], w_ref[c0:c0 + MXU_COLS, :])
            for c in range(c0, c0 + MXU_COLS, LANES):
                ch = acc[:, c - c0:c - c0 + LANES]
                if kind in (ROPE_Q, ROPE_K):
                    ch = ch * cos_ref[...] + pltpu.roll(ch, HEAD_DIM // 2, 1) * sin_ref[...]
                if kind == ROPE_Q:
                    ch = ch * Q_SCALE
                o_ref[:, c:c + LANES] = ch.astype(BF16)
                if jj == jkv and c in side:
                    kv32_ref[side[c]] = ch

    for jj, (src, _, kinds) in enumerate(_INPROJ_TILES):
        pl.when(j == jj)(functools.partial(tile, jj, wh_ref if src == "head" else wt_ref, kinds))

    @pl.when(j == jkv)
    def _():
        n_rows = kv32_ref.shape[1] // CMP_STRIDE
        for s in range(2 * NSA_KV_GROUPS):
            for t in range(CMP_STRIDE):
                kvc_ref[s, :, t * LANES:(t + 1) * LANES] = kv32_ref[
                    s, pl.ds(t, n_rows, stride=CMP_STRIDE), :].astype(BF16)


def _step_lookup(j, table):
    out = jnp.int32(table[0])
    for k in range(1, len(table)):
        out = jnp.where(j >= k, table[k], out)
    return out


def _inproj(x2, g, scale, shift, w_head, w_tail, cos2, sin2, seq):
    m, d = x2.shape
    tm = min(1024, seq)
    tn = INPROJ_TN
    tiles_per_b = seq // tm
    blocks = {"head": [], "tail": []}
    for src, blk, _ in _INPROJ_TILES:
        for name, lst in blocks.items():
            lst.append(blk if src == name else (lst[-1] if lst else None))
    for lst in blocks.values():
        first = next(b for b in lst if b is not None)
        lst[:] = [first if b is None else b for b in lst]
    return pl.pallas_call(
        _inproj_kernel,
        grid=(m // tm, len(_INPROJ_TILES)),
        in_specs=[pl.BlockSpec((tm, d), lambda i, j: (i, 0)),
                  pl.BlockSpec((1, d), lambda i, j: (0, 0)),
                  pl.BlockSpec((None, 1, d), lambda i, j: (i // tiles_per_b, 0, 0)),
                  pl.BlockSpec((None, 1, d), lambda i, j: (i // tiles_per_b, 0, 0)),
                  pl.BlockSpec((tn, d), lambda i, j: (_step_lookup(j, blocks["head"]), 0)),
                  pl.BlockSpec((tn, d), lambda i, j: (_step_lookup(j, blocks["tail"]), 0)),
                  pl.BlockSpec((tm, LANES), lambda i, j: (i % tiles_per_b, 0)),
                  pl.BlockSpec((tm, LANES), lambda i, j: (i % tiles_per_b, 0))],
        out_specs=[pl.BlockSpec((tm, tn), lambda i, j: (i, j)),
                   pl.BlockSpec((2 * NSA_KV_GROUPS, tm // CMP_STRIDE, CMP_STRIDE * LANES),
                                lambda i, j: (0, i, 0))],
        out_shape=[jax.ShapeDtypeStruct((m, N_BLOCKS * LANES), BF16),
                   jax.ShapeDtypeStruct((2 * NSA_KV_GROUPS, m // CMP_STRIDE, CMP_STRIDE * LANES), BF16)],
        scratch_shapes=[pltpu.VMEM((tm, d), BF16),
                        pltpu.VMEM((2 * NSA_KV_GROUPS, tm, LANES), F32)],
        compiler_params=pltpu.CompilerParams(dimension_semantics=("parallel", "arbitrary"),
                                             vmem_limit_bytes=VMEM_LIMIT),
        name="inproj",
    )(x2, g, scale, shift, w_head, w_tail, cos2, sin2)


def _compress_kernel(x_ref, pos_ref, w1_ref, w2_ref, o_ref):
    half = CMP_STRIDE * HEAD_DIM
    x = x_ref[...].astype(F32)
    pos = pos_ref[...]
    w1 = w1_ref[...].astype(BF16)
    a = jnp.dot((x + pos[:, :half]).astype(BF16), w1[:half], preferred_element_type=F32)
    b = jnp.dot((x + pos[:, half:]).astype(BF16), w1[half:], preferred_element_type=F32)
    n_rows = x.shape[0]
    h1 = a + pltpu.roll(b, n_rows - 1, 0)
    h = h1 * (1.0 / (1.0 + jnp.exp(-h1)))
    o_ref[...] = jnp.dot(h.astype(BF16), w2_ref[...].astype(BF16),
                         preferred_element_type=F32).astype(BF16)


def _compress(kvc4, pos2, w1s, w2s):
    ns, bsz, n_rows, width = kvc4.shape
    return pl.pallas_call(
        _compress_kernel,
        grid=(ns, bsz),
        in_specs=[pl.BlockSpec((None, None, n_rows, width), lambda s, b: (s, b, 0, 0)),
                  pl.BlockSpec((None, 1, 2 * width), lambda s, b: (s // NSA_KV_GROUPS, 0, 0)),
                  pl.BlockSpec((None, 2 * width, HEAD_DIM), lambda s, b: (s // NSA_KV_GROUPS, 0, 0)),
                  pl.BlockSpec((None, HEAD_DIM, HEAD_DIM), lambda s, b: (s // NSA_KV_GROUPS, 0, 0))],
        out_specs=pl.BlockSpec((None, None, n_rows, HEAD_DIM), lambda s, b: (s, b, 0, 0)),
        out_shape=jax.ShapeDtypeStruct((ns, bsz, n_rows, HEAD_DIM), BF16),
        compiler_params=pltpu.CompilerParams(dimension_semantics=("parallel", "parallel"),
                                             vmem_limit_bytes=VMEM_LIMIT),
        name="compress",
    )(kvc4, pos2, w1s, w2s)


def _topk_rows(score, k):
    idx = lax.broadcasted_iota(I32, score.shape, 0).astype(F32)
    work = score
    sel = jnp.zeros(score.shape, F32)
    for _ in range(k):
        mx = jnp.max(work, axis=0, keepdims=True)
        first = jnp.min(jnp.where(work == mx, idx, float(score.shape[0])), axis=0, keepdims=True)
        hit = idx == first
        sel = jnp.where(hit, 1.0, sel)
        work = jnp.where(hit, -jnp.inf, work)
    return sel


def _bias_from_keep(keep_t):
    n, q = keep_t.shape
    padded = jnp.concatenate([keep_t, jnp.zeros((LANES - n, q), F32)], axis=0) if n < LANES else keep_t
    return jnp.where(padded.T > 0.5, 0.0, NEG_BIG).astype(BF16)


def _with_ones(v):
    return jnp.concatenate([v, jnp.ones(v.shape, v.dtype)], axis=1)


def _flash_step(qa, ka, va, m, acc, mask, penalty=None):
    s = _nt_dot(qa, ka)
    if mask is not None:
        s = jnp.where(mask, s, NEG_BIG)
    if penalty is not None:
        s = s + penalty
    m_new = jnp.maximum(m, jnp.max(s, axis=1, keepdims=True))
    alpha = jnp.exp2(m - m_new)
    p = jnp.exp2(s - m_new).astype(BF16)
    acc = alpha * acc + jnp.dot(p, va, preferred_element_type=F32)
    return m_new, acc


def _flash_out(acc):
    return acc[:, :HEAD_DIM] * (1.0 / acc[:, HEAD_DIM:])


def _cast_specs(casts, n_steps, step_of):
    specs, shapes = [], []
    for w, axis in casts:
        size = w.shape[axis] // n_steps
        assert w.shape[axis] % n_steps == 0 and size % (BF16_SUBLANES if axis == 0 else LANES) == 0
        if axis == 0:
            specs.append(pl.BlockSpec((size, w.shape[1]), lambda *g: (step_of(*g), 0)))
        else:
            specs.append(pl.BlockSpec((w.shape[0], size), lambda *g: (0, step_of(*g))))
        shapes.append(jax.ShapeDtypeStruct(w.shape, BF16))
    return specs, shapes


def _cast_slabs(src_refs, dst_refs):
    for src, dst in zip(src_refs, dst_refs):
        dst[...] = src[...].astype(BF16)


def _nsa_kernel(q_ref, kc_ref, vc_ref, ks_ref, vs_ref, kw_ref, vw_ref, gl_ref,
                selmap_ref, et_ref, *rest, tq, seq, ng, n_cast):
    cast_in, (o_ref, *cast_out) = rest[:n_cast], rest[n_cast:]
    _cast_slabs(cast_in, cast_out)
    tp = pl.program_id(2)
    rows = NSA_HPG * tq
    n_cmp_pad = kc_ref.shape[1]
    n_sel = seq // SEL_LEN
    gw = NSA_HPG * LANES
    grp = lambda g: slice(g * LANES, (g + 1) * LANES)

    qrow = lax.broadcasted_iota(I32, (rows, tq), 0) & (tq - 1)
    kcol = lax.broadcasted_iota(I32, (rows, tq), 1)
    m0 = jnp.full((rows, 1), NEG_BIG, F32)
    a0 = jnp.zeros((rows, 2 * HEAD_DIM), F32)
    trow = lax.broadcasted_iota(I32, (rows, n_cmp_pad), 0) & (tq - 1)
    cend = lax.broadcasted_iota(I32, (rows, n_cmp_pad), 1) * CMP_STRIDE + (CMP_LEN - 1)
    blk = lax.broadcasted_iota(I32, (n_sel, tq), 0)
    tcol = lax.broadcasted_iota(I32, (n_sel, tq), 1)
    n_forced = 3
    n_back = WINDOW // tq
    chunk = lambda j: pl.ds(pl.multiple_of(j * tq, tq), tq)

    def kv_chunk(g, j):
        ka = jnp.concatenate([ks_ref[chunk(j), grp(g)], et_ref[chunk(j), :]], axis=1)
        return ka, _with_ones(vs_ref[chunk(j), grp(g)])

    def front(g, par):
        qi = 2 * tp + par
        t0 = qi * tq
        tile = slice(par * tq, (par + 1) * tq)
        q = q_ref[tile, g * gw:(g + 1) * gw]
        qs = jnp.concatenate([q[:, h * LANES:(h + 1) * LANES] for h in range(NSA_HPG)], axis=0)
        cb = (t0 + tcol) // SEL_LEN
        forced = (blk == 0) | (blk == cb) | (blk == cb - 1)
        free = jnp.where(blk <= cb, jnp.where(forced, 0.0, 1.0), 0.0)

        s = jnp.where(cend <= t0 + trow, _nt_dot(qs, kc_ref[g]), -jnp.inf)
        mx = jnp.max(s, axis=1, keepdims=True)
        mx = jnp.where(mx == -jnp.inf, 0.0, mx)
        e = jnp.exp2(s - mx)
        den = jnp.maximum(jnp.sum(e, axis=1, keepdims=True), 1e-30)
        p_c = e * (1.0 / den)
        o_c = jnp.dot(p_c.astype(BF16), vc_ref[g], preferred_element_type=F32)

        p_sum = p_c[0:tq]
        for h in range(1, NSA_HPG):
            p_sum = p_sum + p_c[h * tq:(h + 1) * tq]
        hi, lo = _split_bf16(p_sum)
        smap_t = selmap_ref[...]
        imp_t = _nt_dot(smap_t, hi) + _nt_dot(smap_t, lo)

        win = (m0, a0)
        for back in range(n_back + 1):
            rows_b = chunk(jnp.maximum(qi - back, 0))
            mask = (kcol <= qrow) if back == 0 else (kcol > qrow) if back == n_back else None
            exists = None if back == 0 else jnp.where(qi >= back, 0.0, NEG_BIG)
            win = _flash_step(qs, kw_ref[rows_b, grp(g)], _with_ones(vw_ref[rows_b, grp(g)]),
                              *win, mask, exists)
        o_w = _flash_out(win[1])

        picked = _topk_rows(jnp.where(free > 0.5, imp_t, -jnp.inf),
                            max(min(SEL_TOPK, n_sel) - n_forced, 0))
        selb = _bias_from_keep(jnp.where(forced, 1.0, picked * free))
        qa = jnp.concatenate([qs, jnp.concatenate([selb] * NSA_HPG, axis=0)], axis=1)

        carry = _flash_step(qa, *kv_chunk(g, qi), m0, a0, kcol <= qrow)
        if par == 1:
            carry = _flash_step(qa, *kv_chunk(g, qi - 1), *carry, None)
        return qa, o_c, o_w, carry, tile

    fronts = [front(g, par) for g in range(ng) for par in range(2)]

    def body(i, c):
        new = []
        for k, f in enumerate(fronts):
            g = k // 2
            ck = _flash_step(f[0], *kv_chunk(g, 2 * i), c[2 * k], c[2 * k + 1], None)
            new += _flash_step(f[0], *kv_chunk(g, 2 * i + 1), *ck, None)
        return tuple(new)

    final = lax.fori_loop(0, tp, body, tuple(x for f in fronts for x in f[3]))

    for k, (_, o_c, o_w, _, tile) in enumerate(fronts):
        g = k // 2
        o_s = _flash_out(final[2 * k + 1])
        sig = 1.0 / (1.0 + jnp.exp(-gl_ref[tile, grp(g)].astype(F32)))
        for h in range(NSA_HPG):
            r = slice(h * tq, (h + 1) * tq)
            o = (sig[:, 3 * h:3 * h + 1] * o_c[r] + sig[:, 3 * h + 1:3 * h + 2] * o_s[r]
                 + sig[:, 3 * h + 2:3 * h + 3] * o_w[r])
            o_ref[tile, g * gw + h * LANES:g * gw + (h + 1) * LANES] = o.astype(BF16)


def _nsa(qkv, kcv, selmap, et_sel, bsz, seq, casts):
    m = qkv.shape[0]
    tq = NSA_TQ
    ng = NSA_GROUPS_PER_STEP
    assert seq % tq == 0 and WINDOW % tq == 0 and tq % SEL_LEN == 0 and NSA_KV_GROUPS % ng == 0
    assert all(c0 % ng == 0 for c0 in (KS0, VS0, KW0, VW0, GT0))
    assert seq % (2 * tq) == 0
    nq = seq // (2 * tq)
    ngs = NSA_KV_GROUPS // ng
    n_cmp_pad = kcv.shape[2]
    gw = ng * NSA_HPG * LANES
    full = lambda c0: pl.BlockSpec((seq, ng * LANES), lambda b, g, i: (b, c0 // ng + g))
    cast_specs, cast_shapes = _cast_specs(casts, bsz * ngs * nq, lambda b, g, i: (b * ngs + g) * nq + i)
    outs = pl.pallas_call(
        functools.partial(_nsa_kernel, tq=tq, seq=seq, ng=ng, n_cast=len(casts)),
        grid=(bsz, ngs, nq),
        in_specs=[pl.BlockSpec((2 * tq, gw), lambda b, g, i: (b * nq + i, g)),
                  pl.BlockSpec((ng, None, n_cmp_pad, LANES), lambda b, g, i: (g, b, 0, 0)),
                  pl.BlockSpec((ng, None, n_cmp_pad, LANES), lambda b, g, i: (ngs + g, b, 0, 0)),
                  full(KS0), full(VS0), full(KW0), full(VW0),
                  pl.BlockSpec((2 * tq, ng * LANES), lambda b, g, i: (b * nq + i, GT0 // ng + g)),
                  pl.BlockSpec(selmap.shape, lambda b, g, i: (0, 0)),
                  pl.BlockSpec(et_sel.shape, lambda b, g, i: (0, 0))] + cast_specs,
        out_specs=[pl.BlockSpec((2 * tq, gw), lambda b, g, i: (b * nq + i, g))] + cast_specs,
        out_shape=[jax.ShapeDtypeStruct((m, NSA_HEADS * HEAD_DIM), BF16)] + cast_shapes,
        compiler_params=pltpu.CompilerParams(
            dimension_semantics=("parallel", "parallel", "arbitrary"),
            vmem_limit_bytes=VMEM_LIMIT),
        name="nsa_attn",
    )(qkv, kcv, kcv, qkv, qkv, qkv, qkv, qkv, selmap, et_sel, *[w for w, _ in casts])
    return outs[0], outs[1:]


def _moba_kernel(q_ref, k_ref, v_ref, avg_ref, et_ref, *rest, seq, hb, n_cast):
    cast_in, (o_ref, *cast_out), (kmh_ref, kml_ref) = rest[:n_cast], rest[n_cast:2 * n_cast + 1], rest[2 * n_cast + 1:]
    _cast_slabs(cast_in, cast_out)
    tp = pl.program_id(2)
    blk_len = MOBA_BLOCK
    rows = 2 * blk_len
    nb = seq // blk_len
    nb_rows = avg_ref.shape[0]
    head = lambda h: slice(h * LANES, (h + 1) * LANES)

    @pl.when(tp == 0)
    def _():
        for h in range(hb):
            km = jnp.dot(avg_ref[...], k_ref[:, head(h)], preferred_element_type=F32)
            kmh_ref[h], kml_ref[h] = _split_bf16(km)

    blk = lax.broadcasted_iota(I32, (nb_rows, rows), 0)
    qblk = 2 * tp + lax.broadcasted_iota(I32, (nb_rows, rows), 1) // blk_len
    past = jnp.where(blk < qblk, 1.0, 0.0)
    own = jnp.where(blk == qblk, 1.0, 0.0)
    qas = []
    for h in range(hb):
        q = q_ref[:, head(h)]
        gate_t = _nt_dot(kmh_ref[h], q) + _nt_dot(kml_ref[h], q)
        picked = _topk_rows(jnp.where(past > 0.5, gate_t, -jnp.inf), min(MOBA_TOPK, nb - 1))
        qas.append(jnp.concatenate([q, _bias_from_keep(picked * past + own)], axis=1))

    def kv_chunk(j, h):
        keys = pl.ds(pl.multiple_of(j * blk_len, blk_len), blk_len)
        ka = jnp.concatenate([k_ref[keys, head(h)], et_ref[keys, :]], axis=1)
        return ka, _with_ones(v_ref[keys, head(h)])

    row = lax.broadcasted_iota(I32, (rows, blk_len), 0)
    colk = lax.broadcasted_iota(I32, (rows, blk_len), 1)
    m0 = jnp.full((rows, 1), NEG_BIG, F32)
    a0 = jnp.zeros((rows, 2 * HEAD_DIM), F32)

    def all_heads(j, c, mask=None):
        new = []
        for h in range(hb):
            new += _flash_step(qas[h], *kv_chunk(j, h), c[2 * h], c[2 * h + 1], mask)
        return tuple(new)

    carry = all_heads(2 * tp, (m0, a0) * hb, (row >= blk_len) | (colk <= row))
    carry = all_heads(2 * tp + 1, carry, colk <= row - blk_len)
    final = lax.fori_loop(0, tp, lambda i, c: all_heads(2 * i + 1, all_heads(2 * i, c)), carry)
    for h in range(hb):
        o_ref[:, head(h)] = _flash_out(final[2 * h + 1]).astype(BF16)


def _moba(qkv, avg, et_blk, bsz, seq, casts):
    m = qkv.shape[0]
    assert seq % (2 * MOBA_BLOCK) == 0
    nq = seq // (2 * MOBA_BLOCK)
    hb = MOBA_HEADS_PER_STEP
    assert MOBA_HEADS % hb == 0 and QM0 % hb == 0 and KM0 % hb == 0 and VM0 % hb == 0
    hw = hb * LANES
    ng = MOBA_HEADS // hb
    cast_specs, cast_shapes = _cast_specs(casts, bsz * ng * nq, lambda b, h, i: (b * ng + h) * nq + i)
    whole = lambda c0: pl.BlockSpec((seq, hw), lambda b, h, i: (b, c0 // hb + h),
                                    pipeline_mode=pl.Buffered(1))
    outs = pl.pallas_call(
        functools.partial(_moba_kernel, seq=seq, hb=hb, n_cast=len(casts)),
        grid=(bsz, ng, nq),
        in_specs=[pl.BlockSpec((2 * MOBA_BLOCK, hw), lambda b, h, i: (b * nq + i, QM0 // hb + h)),
                  whole(KM0), whole(VM0),
                  pl.BlockSpec(avg.shape, lambda b, h, i: (0, 0)),
                  pl.BlockSpec(et_blk.shape, lambda b, h, i: (0, 0))] + cast_specs,
        out_specs=[pl.BlockSpec((2 * MOBA_BLOCK, hw), lambda b, h, i: (b * nq + i, h))] + cast_specs,
        out_shape=[jax.ShapeDtypeStruct((m, MOBA_HEADS * HEAD_DIM), BF16)] + cast_shapes,
        scratch_shapes=[pltpu.VMEM((hb, avg.shape[0], HEAD_DIM), BF16),
                        pltpu.VMEM((hb, avg.shape[0], HEAD_DIM), BF16)],
        compiler_params=pltpu.CompilerParams(
            dimension_semantics=("parallel", "parallel", "arbitrary"),
            vmem_limit_bytes=VMEM_LIMIT),
        name="moba_attn",
    )(qkv, qkv, qkv, avg, et_blk, *[w for w, _ in casts])
    return outs[0], outs[1:]


def _outproj_kernel(on_ref, om_ref, w_ref, x_ref, g_ref, gate_ref, *rest, n_cast):
    cast_in, (o_ref, *cast_out) = rest[:n_cast], rest[n_cast:]
    _cast_slabs(cast_in, cast_out)
    kn = on_ref.shape[1]
    acc = (jnp.dot(on_ref[...], w_ref[:kn, :], preferred_element_type=F32)
           + jnp.dot(om_ref[...], w_ref[kn:, :], preferred_element_type=F32))
    o_ref[...] = x_ref[...] + gate_ref[...] * _rms_norm(acc, g_ref[...])


def _outproj(o_n, o_m, w_o, x2, g_post, gate, seq, casts):
    m, d = x2.shape
    tm = min(512, seq)
    tiles_per_b = seq // tm
    cast_specs, cast_shapes = _cast_specs(casts, m // tm, lambda i: i)
    outs = pl.pallas_call(
        functools.partial(_outproj_kernel, n_cast=len(casts)),
        grid=(m // tm,),
        in_specs=[pl.BlockSpec((tm, o_n.shape[1]), lambda i: (i, 0)),
                  pl.BlockSpec((tm, o_m.shape[1]), lambda i: (i, 0)),
                  pl.BlockSpec(w_o.shape, lambda i: (0, 0), pipeline_mode=pl.Buffered(1)),
                  pl.BlockSpec((tm, d), lambda i: (i, 0)),
                  pl.BlockSpec((1, d), lambda i: (0, 0)),
                  pl.BlockSpec((None, 1, d), lambda i: (i // tiles_per_b, 0, 0))] + cast_specs,
        out_specs=[pl.BlockSpec((tm, d), lambda i: (i, 0))] + cast_specs,
        out_shape=[jax.ShapeDtypeStruct((m, d), F32)] + cast_shapes,
        compiler_params=pltpu.CompilerParams(dimension_semantics=("parallel",),
                                             vmem_limit_bytes=VMEM_LIMIT),
        name="outproj",
    )(o_n, o_m, w_o, x2, g_post, gate, *[w for w, _ in casts])
    return outs[0], outs[1:]


def _ffn_kernel(x_ref, gpre_ref, sc_ref, sh_ref, wu_ref, wd_ref, gpost_ref, gate_ref,
                o_ref, h_ref):
    k = pl.program_id(1)
    last = pl.num_programs(1) - 1
    row_blocks = [slice(r, r + FFN_ROW_BLOCK) for r in range(0, x_ref.shape[0], FFN_ROW_BLOCK)]

    def mlp(h):
        u = jnp.dot(h, wu_ref[...], preferred_element_type=F32)
        u = jnp.square(jnp.maximum(u, 0.0)).astype(BF16)
        return jnp.dot(u, wd_ref[...], preferred_element_type=F32)

    @pl.when(k == 0)
    def _():
        for rows in row_blocks:
            h = _rms_norm(x_ref[rows], gpre_ref[...]) * (1.0 + sc_ref[...]) + sh_ref[...]
            h_ref[rows] = h.astype(BF16)
            o_ref[rows] = mlp(h_ref[rows])

    @pl.when(jnp.logical_and(k > 0, k < last))
    def _():
        o_ref[...] += mlp(h_ref[...])

    @pl.when(k == last)
    def _():
        for rows in row_blocks:
            f = o_ref[rows] + mlp(h_ref[rows])
            o_ref[rows] = x_ref[rows] + gate_ref[...] * _rms_norm(f, gpost_ref[...])


def _ffn(x2, g_pre, scale, shift, w_up, w_down, g_post, gate, seq):
    m, d = x2.shape
    dff = w_up.shape[1]
    tm = min(1024, seq)
    ck = 512
    assert dff // ck >= 2 and tm % FFN_ROW_BLOCK == 0
    tiles_per_b = seq // tm
    vec = pl.BlockSpec((1, d), lambda i, k: (0, 0))
    per_b = pl.BlockSpec((None, 1, d), lambda i, k: (i // tiles_per_b, 0, 0))
    return pl.pallas_call(
        _ffn_kernel,
        grid=(m // tm, dff // ck),
        in_specs=[pl.BlockSpec((tm, d), lambda i, k: (i, 0)), vec, per_b, per_b,
                  pl.BlockSpec((d, ck), lambda i, k: (0, k)),
                  pl.BlockSpec((ck, d), lambda i, k: (k, 0)),
                  vec, per_b],
        out_specs=pl.BlockSpec((tm, d), lambda i, k: (i, 0)),
        out_shape=jax.ShapeDtypeStruct((m, d), F32),
        scratch_shapes=[pltpu.VMEM((tm, d), BF16)],
        compiler_params=pltpu.CompilerParams(dimension_semantics=("parallel", "arbitrary"),
                                             vmem_limit_bytes=VMEM_LIMIT),
        name="ffn",
    )(x2, g_pre, scale, shift, w_up, w_down, g_post, gate)


def _constants(seq):
    n_cmp_pad = seq // CMP_STRIDE
    n_cmp = (seq - CMP_LEN) // CMP_STRIDE + 1
    n_sel = seq // SEL_LEN
    cs = np.arange(n_cmp) * CMP_STRIDE
    ss = np.arange(n_sel) * SEL_LEN
    overlap = np.clip(np.minimum(cs[:, None] + CMP_LEN, ss[None, :] + SEL_LEN)
                      - np.maximum(cs[:, None], ss[None, :]), 0, None)
    selmap_t = np.zeros((n_sel, n_cmp_pad), np.float32)
    selmap_t[:, :n_cmp] = (overlap / CMP_LEN).T
    keys = np.arange(seq)
    et_sel = (keys[:, None] // SEL_LEN == np.arange(LANES)[None, :]).astype(np.float32)
    et_blk = (keys[:, None] // MOBA_BLOCK == np.arange(LANES)[None, :]).astype(np.float32)
    nb_rows = -(-(seq // MOBA_BLOCK) // BF16_SUBLANES) * BF16_SUBLANES
    avg = et_blk.T[:nb_rows] / MOBA_BLOCK
    to = lambda a: jnp.asarray(a, BF16)
    return to(selmap_t), to(et_sel), to(et_blk), to(avg)


def _rope_tables(seq):
    pos = jnp.arange(seq, dtype=F32)
    inv = ROPE_THETA ** (-jnp.arange(0, HEAD_DIM, 2, dtype=F32) / HEAD_DIM)
    ang = pos[:, None] * inv[None, :]
    cos, sin = jnp.cos(ang), jnp.sin(ang)
    return jnp.concatenate([cos, cos], axis=1), jnp.concatenate([-sin, sin], axis=1)


def kernel(x, c, w_ada, b_ada, pre_norm_mix, post_norm_mix, w_in, cmp_k_pos, cmp_k_w1, cmp_k_w2,
           cmp_v_pos, cmp_v_w1, cmp_v_w2, w_o, pre_norm_ffn, post_norm_ffn, w_up, w_down):
    bsz, seq, d = x.shape
    depth = w_ada.shape[0]
    cos2, sin2 = _rope_tables(seq)
    selmap, et_sel, et_blk, avg = _constants(seq)
    x2 = x.reshape(bsz * seq, d)
    for l in range(depth):
        mod, w_head, w_tail = _adaln(c, w_ada[l], b_ada[l], w_in[l].T)
        shift_m, scale_m, gate_m, shift_f, scale_f, gate_f = [
            a.reshape(bsz, 1, d) for a in jnp.split(mod, 6, axis=-1)]
        row = lambda a: a.reshape(1, d)

        qkv, kvc = _inproj(x2, row(pre_norm_mix[l]), scale_m, shift_m, w_head, w_tail,
                           cos2, sin2, seq)
        kvc4 = kvc.reshape(2 * NSA_KV_GROUPS, bsz, seq // CMP_STRIDE, CMP_STRIDE * HEAD_DIM)
        pos2 = jnp.stack([cmp_k_pos[l].reshape(1, -1), cmp_v_pos[l].reshape(1, -1)])
        kcv = _compress(kvc4, pos2, jnp.stack([cmp_k_w1[l], cmp_v_w1[l]]),
                        jnp.stack([cmp_k_w2[l], cmp_v_w2[l]]))
        o_n, (w_o_bf,) = _nsa(qkv, kcv, selmap, et_sel, bsz, seq, [(w_o[l], 0)])
        o_m, (w_up_bf,) = _moba(qkv, avg, et_blk, bsz, seq, [(w_up[l], 1)])
        x2, (w_down_bf,) = _outproj(o_n, o_m, w_o_bf, x2, row(post_norm_mix[l]), gate_m, seq,
                                    [(w_down[l], 0)])
        x2 = _ffn(x2, row(pre_norm_ffn[l]), scale_f, shift_f, w_up_bf, w_down_bf,
                  row(post_norm_ffn[l]), gate_f, seq)
    return x2.reshape(bsz, seq, d)
```

```python
import functools

import numpy as np
import jax
import jax.numpy as jnp
from jax import lax
from jax.experimental import pallas as pl
from jax.experimental.pallas import tpu as pltpu

F32 = jnp.float32
BF16 = jnp.bfloat16
I32 = jnp.int32

HEAD_DIM = 128
NSA_HEADS = 8
NSA_KV_GROUPS = 2
NSA_HPG = NSA_HEADS // NSA_KV_GROUPS
MOBA_HEADS = 8
CMP_LEN = 32
CMP_STRIDE = 16
SEL_LEN = 64
SEL_TOPK = 16
WINDOW = 512
MOBA_BLOCK = 256
MOBA_TOPK = 3
ROPE_THETA = 10000.0
EPS = 1e-6

LANES = 128
BF16_SUBLANES = 16
MXU_COLS = 256
FFN_ROW_BLOCK = 256
NSA_TQ = 256
NSA_GROUPS_PER_STEP = 2
MOBA_HEADS_PER_STEP = 8
Q_SCALE = HEAD_DIM ** -0.5 * float(np.log2(np.e))
NEG_BIG = -1e30
VMEM_LIMIT = 56 * 1024 * 1024

QN0 = 0
QM0 = 8
KM0 = 16
VM0 = 24
KC0 = 32
VC0 = 34
KS0 = 36
VS0 = 38
KW0 = 40
VW0 = 42
GT0 = 44
N_BLOCKS = 48
INPROJ_TN = 1024
HEAD_COLS = 2560
GATE_COLS = 3 * NSA_HEADS
IN_SLAB = 256
ROPE_Q, ROPE_K, PLAIN, GATES, ZERO = range(5)
_INPROJ_TILES = (
    ("head", 0, (ROPE_Q,) * 4),
    ("tail", 0, (ROPE_Q,) * 4),
    ("tail", 1, (ROPE_K,) * 4),
    ("tail", 2, (PLAIN,) * 4),
    ("head", 1, (ROPE_K, PLAIN, ROPE_K, PLAIN)),
    ("head", 2, (ROPE_K, PLAIN, GATES, ZERO)),
)


def _nt_dot(a, b):
    return lax.dot_general(a, b, (((1,), (1,)), ((), ())), preferred_element_type=F32)


def _rms_norm(x, g):
    ms = jnp.mean(x * x, axis=-1, keepdims=True)
    return x * lax.rsqrt(ms + EPS) * g


def _split_bf16(a):
    hi = a.astype(BF16)
    lo = (a - hi.astype(F32)).astype(BF16)
    return hi, lo


def _adaln_kernel(ct_ref, w_ref, b_ref, wh_ref, wa_ref, wb_ref, o_ref, head_ref, tail_ref):
    ct = ct_ref[...]
    a = ct * (1.0 / (1.0 + jnp.exp(-ct)))
    w = w_ref[...]
    for b in range(o_ref.shape[0]):
        o_ref[b] = jnp.sum(w * a[:, b:b + 1], axis=0, keepdims=True) + b_ref[...]

    t = pl.program_id(0)
    tail_ref[...] = jnp.concatenate([wa_ref[GATE_COLS:, :], wb_ref[:GATE_COLS, :]], axis=0).astype(BF16)

    n_plain = HEAD_COLS // IN_SLAB

    @pl.when(t < n_plain)
    def _():
        head_ref[...] = wh_ref[...].astype(BF16)

    @pl.when(t == n_plain)
    def _():
        per_group = 3 * NSA_HPG
        g = wh_ref[:LANES, :]
        row = lax.broadcasted_iota(I32, g.shape, 0)
        for grp in range(NSA_KV_GROUPS):
            rows = g if grp == 0 else pltpu.roll(g, LANES - grp * per_group, 0)
            head_ref[grp * LANES:(grp + 1) * LANES, :] = jnp.where(row < per_group, rows, 0.0).astype(BF16)

    @pl.when(t > n_plain)
    def _():
        head_ref[...] = jnp.zeros(head_ref.shape, BF16)


def _adaln(c, w, b, w_in_t):
    bsz, d = c.shape
    n = w.shape[1]
    tn = 1024
    steps = n // tn
    n_src = sum(1 for src, _, _ in _INPROJ_TILES if src == "head") * INPROJ_TN
    n_plain = HEAD_COLS // IN_SLAB
    assert bsz <= LANES and steps * IN_SLAB == n_src and HEAD_COLS % IN_SLAB == 0 and n_plain < steps
    assert IN_SLAB == NSA_KV_GROUPS * LANES and w_in_t.shape[0] == HEAD_COLS + GATE_COLS + n_src
    assert GATE_COLS % 8 == 0
    ct = jnp.zeros((d, LANES), F32).at[:, :bsz].set(c.T)
    slab = lambda f: pl.BlockSpec((IN_SLAB, d), lambda j: (f(j), 0))
    out, w_head, w_tail = pl.pallas_call(
        _adaln_kernel,
        grid=(steps,),
        in_specs=[pl.BlockSpec((d, LANES), lambda j: (0, 0)),
                  pl.BlockSpec((d, tn), lambda j: (0, j)),
                  pl.BlockSpec((1, tn), lambda j: (0, j)),
                  slab(lambda j: jnp.minimum(j, n_plain)),
                  slab(lambda j: n_plain + j),
                  slab(lambda j: n_plain + j + 1)],
        out_specs=[pl.BlockSpec((bsz, 1, tn), lambda j: (0, 0, j)), slab(lambda j: j), slab(lambda j: j)],
        out_shape=[jax.ShapeDtypeStruct((bsz, 1, n), F32),
                   jax.ShapeDtypeStruct((n_src, d), BF16), jax.ShapeDtypeStruct((n_src, d), BF16)],
        compiler_params=pltpu.CompilerParams(dimension_semantics=("parallel",),
                                             vmem_limit_bytes=VMEM_LIMIT),
        name="adaln",
    )(ct, w, b.reshape(1, n), w_in_t, w_in_t, w_in_t)
    return out.reshape(bsz, n), w_head, w_tail


def _inproj_kernel(x_ref, g_ref, sc_ref, sh_ref, wh_ref, wt_ref, cos_ref, sin_ref,
                   o_ref, kvc_ref, h_ref, kv32_ref):
    j = pl.program_id(1)

    @pl.when(j == 0)
    def _():
        h = _rms_norm(x_ref[...], g_ref[...]) * (1.0 + sc_ref[...]) + sh_ref[...]
        h_ref[...] = h.astype(BF16)

    jkv, ok = divmod(KC0 * LANES, INPROJ_TN)
    ov = VC0 * LANES - jkv * INPROJ_TN
    assert 0 < ov < INPROJ_TN
    side = {ok + g * LANES: g for g in range(NSA_KV_GROUPS)}
    side.update({ov + g * LANES: NSA_KV_GROUPS + g for g in range(NSA_KV_GROUPS)})

    def tile(jj, w_ref, kinds):
        for ci, kind in enumerate(kinds):
            c0 = ci * MXU_COLS
            if kind == ZERO:
                o_ref[:, c0:c0 + MXU_COLS] = jnp.zeros((o_ref.shape[0], MXU_COLS), BF16)
                continue
            acc = _nt_dot(h_ref[...], w_ref[c0:c0 + MXU_COLS, :])
            for c in range(c0, c0 + MXU_COLS, LANES):
                ch = acc[:, c - c0:c - c0 + LANES]
                if kind in (ROPE_Q, ROPE_K):
                    ch = ch * cos_ref[...] + pltpu.roll(ch, HEAD_DIM // 2, 1) * sin_ref[...]
                if kind == ROPE_Q:
                    ch = ch * Q_SCALE
                o_ref[:, c:c + LANES] = ch.astype(BF16)
                if jj == jkv and c in side:
                    kv32_ref[side[c]] = ch

    for jj, (src, _, kinds) in enumerate(_INPROJ_TILES):
        pl.when(j == jj)(functools.partial(tile, jj, wh_ref if src == "head" else wt_ref, kinds))

    @pl.when(j == jkv)
    def _():
        n_rows = kv32_ref.shape[1] // CMP_STRIDE
        for s in range(2 * NSA_KV_GROUPS):
            for t in range(CMP_STRIDE):
                kvc_ref[s, :, t * LANES:(t + 1) * LANES] = kv32_ref[
                    s, pl.ds(t, n_rows, stride=CMP_STRIDE), :].astype(BF16)


def _step_lookup(j, table):
    out = jnp.int32(table[0])
    for k in range(1, len(table)):
        out = jnp.where(j >= k, table[k], out)
    return out


def _inproj(x2, g, scale, shift, w_head, w_tail, cos2, sin2, seq):
    m, d = x2.shape
    tm = min(1024, seq)
    tn = INPROJ_TN
    tiles_per_b = seq // tm
    blocks = {"head": [], "tail": []}
    for src, blk, _ in _INPROJ_TILES:
        for name, lst in blocks.items():
            lst.append(blk if src == name else (lst[-1] if lst else None))
    for lst in blocks.values():
        first = next(b for b in lst if b is not None)
        lst[:] = [first if b is None else b for b in lst]
    return pl.pallas_call(
        _inproj_kernel,
        grid=(m // tm, len(_INPROJ_TILES)),
        in_specs=[pl.BlockSpec((tm, d), lambda i, j: (i, 0)),
                  pl.BlockSpec((1, d), lambda i, j: (0, 0)),
                  pl.BlockSpec((None, 1, d), lambda i, j: (i // tiles_per_b, 0, 0)),
                  pl.BlockSpec((None, 1, d), lambda i, j: (i // tiles_per_b, 0, 0)),
                  pl.BlockSpec((tn, d), lambda i, j: (_step_lookup(j, blocks["head"]), 0)),
                  pl.BlockSpec((tn, d), lambda i, j: (_step_lookup(j, blocks["tail"]), 0)),
                  pl.BlockSpec((tm, LANES), lambda i, j: (i % tiles_per_b, 0)),
                  pl.BlockSpec((tm, LANES), lambda i, j: (i % tiles_per_b, 0))],
        out_specs=[pl.BlockSpec((tm, tn), lambda i, j: (i, j)),
                   pl.BlockSpec((2 * NSA_KV_GROUPS, tm // CMP_STRIDE, CMP_STRIDE * LANES),
                                lambda i, j: (0, i, 0))],
        out_shape=[jax.ShapeDtypeStruct((m, N_BLOCKS * LANES), BF16),
                   jax.ShapeDtypeStruct((2 * NSA_KV_GROUPS, m // CMP_STRIDE, CMP_STRIDE * LANES), BF16)],
        scratch_shapes=[pltpu.VMEM((tm, d), BF16),
                        pltpu.VMEM((2 * NSA_KV_GROUPS, tm, LANES), F32)],
        compiler_params=pltpu.CompilerParams(dimension_semantics=("parallel", "arbitrary"),
                                             vmem_limit_bytes=VMEM_LIMIT),
        name="inproj",
    )(x2, g, scale, shift, w_head, w_tail, cos2, sin2)


def _compress_kernel(x_ref, pos_k_ref, w1_k_ref, w2_k_ref, pos_v_ref, w1_v_ref, w2_v_ref, o_ref):
    half = CMP_STRIDE * HEAD_DIM

    def mlp(pos_ref, w1_ref, w2_ref):
        x = x_ref[...].astype(F32)
        pos = pos_ref[...]
        w1 = w1_ref[...].astype(BF16)
        a = jnp.dot((x + pos[:, :half]).astype(BF16), w1[:half], preferred_element_type=F32)
        b = jnp.dot((x + pos[:, half:]).astype(BF16), w1[half:], preferred_element_type=F32)
        h1 = a + pltpu.roll(b, x.shape[0] - 1, 0)
        h = h1 * (1.0 / (1.0 + jnp.exp(-h1)))
        o_ref[...] = jnp.dot(h.astype(BF16), w2_ref[...].astype(BF16),
                             preferred_element_type=F32).astype(BF16)

    is_key = pl.program_id(0) < NSA_KV_GROUPS
    pl.when(is_key)(functools.partial(mlp, pos_k_ref, w1_k_ref, w2_k_ref))
    pl.when(jnp.logical_not(is_key))(functools.partial(mlp, pos_v_ref, w1_v_ref, w2_v_ref))


def _compress(kvc4, k_params, v_params):
    ns, bsz, n_rows, width = kvc4.shape
    whole = lambda a: pl.BlockSpec(a.shape, lambda s, b: (0,) * a.ndim)
    return pl.pallas_call(
        _compress_kernel,
        grid=(ns, bsz),
        in_specs=[pl.BlockSpec((None, None, n_rows, width), lambda s, b: (s, b, 0, 0))]
                 + [whole(a) for a in (*k_params, *v_params)],
        out_specs=pl.BlockSpec((None, None, n_rows, HEAD_DIM), lambda s, b: (s, b, 0, 0)),
        out_shape=jax.ShapeDtypeStruct((ns, bsz, n_rows, HEAD_DIM), BF16),
        compiler_params=pltpu.CompilerParams(dimension_semantics=("parallel", "parallel"),
                                             vmem_limit_bytes=VMEM_LIMIT),
        name="compress",
    )(kvc4, *k_params, *v_params)


def _topk_rows(score, k):
    idx = lax.broadcasted_iota(I32, score.shape, 0).astype(F32)
    work = score
    sel = jnp.zeros(score.shape, F32)
    for _ in range(k):
        mx = jnp.max(work, axis=0, keepdims=True)
        first = jnp.min(jnp.where(work == mx, idx, float(score.shape[0])), axis=0, keepdims=True)
        hit = idx == first
        sel = jnp.where(hit, 1.0, sel)
        work = jnp.where(hit, -jnp.inf, work)
    return sel


def _bias_from_keep(keep_t):
    n, q = keep_t.shape
    padded = jnp.concatenate([keep_t, jnp.zeros((LANES - n, q), F32)], axis=0) if n < LANES else keep_t
    return jnp.where(padded.T > 0.5, 0.0, NEG_BIG).astype(BF16)


def _with_ones(v):
    return jnp.concatenate([v, jnp.ones(v.shape, v.dtype)], axis=1)


def _flash_step(qa, ka, va, m, acc, mask, penalty=None):
    s = _nt_dot(qa, ka)
    if mask is not None:
        s = jnp.where(mask, s, NEG_BIG)
    if penalty is not None:
        s = s + penalty
    m_new = jnp.maximum(m, jnp.max(s, axis=1, keepdims=True))
    alpha = jnp.exp2(m - m_new)
    p = jnp.exp2(s - m_new).astype(BF16)
    acc = alpha * acc + jnp.dot(p, va, preferred_element_type=F32)
    return m_new, acc


def _flash_out(acc):
    return acc[:, :HEAD_DIM] * (1.0 / acc[:, HEAD_DIM:])


def _cast_specs(casts, n_steps, step_of):
    specs, shapes = [], []
    for w, axis in casts:
        size = w.shape[axis] // n_steps
        assert w.shape[axis] % n_steps == 0 and size % (BF16_SUBLANES if axis == 0 else LANES) == 0
        if axis == 0:
            specs.append(pl.BlockSpec((size, w.shape[1]), lambda *g: (step_of(*g), 0)))
        else:
            specs.append(pl.BlockSpec((w.shape[0], size), lambda *g: (0, step_of(*g))))
        shapes.append(jax.ShapeDtypeStruct(w.shape, BF16))
    return specs, shapes


def _cast_slabs(src_refs, dst_refs):
    for src, dst in zip(src_refs, dst_refs):
        dst[...] = src[...].astype(BF16)


def _nsa_kernel(q_ref, kc_ref, vc_ref, ks_ref, vs_ref, kw_ref, vw_ref, gl_ref,
                selmap_ref, et_ref, *rest, tq, seq, ng, n_cast):
    cast_in, (o_ref, *cast_out) = rest[:n_cast], rest[n_cast:]
    _cast_slabs(cast_in, cast_out)
    tp = pl.program_id(2)
    rows = NSA_HPG * tq
    n_cmp_pad = kc_ref.shape[1]
    n_sel = seq // SEL_LEN
    gw = NSA_HPG * LANES
    grp = lambda g: slice(g * LANES, (g + 1) * LANES)

    qrow = lax.broadcasted_iota(I32, (rows, tq), 0) & (tq - 1)
    kcol = lax.broadcasted_iota(I32, (rows, tq), 1)
    m0 = jnp.full((rows, 1), NEG_BIG, F32)
    a0 = jnp.zeros((rows, 2 * HEAD_DIM), F32)
    trow = lax.broadcasted_iota(I32, (rows, n_cmp_pad), 0) & (tq - 1)
    cend = lax.broadcasted_iota(I32, (rows, n_cmp_pad), 1) * CMP_STRIDE + (CMP_LEN - 1)
    blk = lax.broadcasted_iota(I32, (n_sel, tq), 0)
    tcol = lax.broadcasted_iota(I32, (n_sel, tq), 1)
    n_forced = 3
    n_back = WINDOW // tq
    chunk = lambda j: pl.ds(pl.multiple_of(j * tq, tq), tq)

    def kv_chunk(g, j):
        ka = jnp.concatenate([ks_ref[chunk(j), grp(g)], et_ref[chunk(j), :]], axis=1)
        return ka, _with_ones(vs_ref[chunk(j), grp(g)])

    def front(g, par):
        qi = 2 * tp + par
        t0 = qi * tq
        tile = slice(par * tq, (par + 1) * tq)
        q = q_ref[tile, g * gw:(g + 1) * gw]
        qs = jnp.concatenate([q[:, h * LANES:(h + 1) * LANES] for h in range(NSA_HPG)], axis=0)
        cb = (t0 + tcol) // SEL_LEN
        forced = (blk == 0) | (blk == cb) | (blk == cb - 1)
        free = jnp.where(blk <= cb, jnp.where(forced, 0.0, 1.0), 0.0)

        s = jnp.where(cend <= t0 + trow, _nt_dot(qs, kc_ref[g]), -jnp.inf)
        mx = jnp.max(s, axis=1, keepdims=True)
        mx = jnp.where(mx == -jnp.inf, 0.0, mx)
        e = jnp.exp2(s - mx)
        den = jnp.maximum(jnp.sum(e, axis=1, keepdims=True), 1e-30)
        p_c = e * (1.0 / den)
        o_c = jnp.dot(p_c.astype(BF16), vc_ref[g], preferred_element_type=F32)

        p_sum = p_c[0:tq]
        for h in range(1, NSA_HPG):
            p_sum = p_sum + p_c[h * tq:(h + 1) * tq]
        hi, lo = _split_bf16(p_sum)
        smap_t = selmap_ref[...]
        imp_t = _nt_dot(smap_t, hi) + _nt_dot(smap_t, lo)

        win = (m0, a0)
        for back in range(n_back + 1):
            rows_b = chunk(jnp.maximum(qi - back, 0))
            mask = (kcol <= qrow) if back == 0 else (kcol > qrow) if back == n_back else None
            exists = None if back == 0 else jnp.where(qi >= back, 0.0, NEG_BIG)
            win = _flash_step(qs, kw_ref[rows_b, grp(g)], _with_ones(vw_ref[rows_b, grp(g)]),
                              *win, mask, exists)
        o_w = _flash_out(win[1])

        picked = _topk_rows(jnp.where(free > 0.5, imp_t, -jnp.inf),
                            max(min(SEL_TOPK, n_sel) - n_forced, 0))
        selb = _bias_from_keep(jnp.where(forced, 1.0, picked * free))
        qa = jnp.concatenate([qs, jnp.concatenate([selb] * NSA_HPG, axis=0)], axis=1)

        carry = _flash_step(qa, *kv_chunk(g, qi), m0, a0, kcol <= qrow)
        if par == 1:
            carry = _flash_step(qa, *kv_chunk(g, qi - 1), *carry, None)
        return qa, o_c, o_w, carry, tile

    fronts = [front(g, par) for g in range(ng) for par in range(2)]

    def body(i, c):
        new = []
        for k, f in enumerate(fronts):
            g = k // 2
            ck = _flash_step(f[0], *kv_chunk(g, 2 * i), c[2 * k], c[2 * k + 1], None)
            new += _flash_step(f[0], *kv_chunk(g, 2 * i + 1), *ck, None)
        return tuple(new)

    final = lax.fori_loop(0, tp, body, tuple(x for f in fronts for x in f[3]))

    for k, (_, o_c, o_w, _, tile) in enumerate(fronts):
        g = k // 2
        o_s = _flash_out(final[2 * k + 1])
        sig = 1.0 / (1.0 + jnp.exp(-gl_ref[tile, grp(g)].astype(F32)))
        for h in range(NSA_HPG):
            r = slice(h * tq, (h + 1) * tq)
            o = (sig[:, 3 * h:3 * h + 1] * o_c[r] + sig[:, 3 * h + 1:3 * h + 2] * o_s[r]
                 + sig[:, 3 * h + 2:3 * h + 3] * o_w[r])
            o_ref[tile, g * gw + h * LANES:g * gw + (h + 1) * LANES] = o.astype(BF16)


def _nsa(qkv, kcv, selmap, et_sel, bsz, seq, casts):
    m = qkv.shape[0]
    tq = NSA_TQ
    ng = NSA_GROUPS_PER_STEP
    assert seq % tq == 0 and WINDOW % tq == 0 and tq % SEL_LEN == 0 and NSA_KV_GROUPS % ng == 0
    assert all(c0 % ng == 0 for c0 in (KS0, VS0, KW0, VW0, GT0))
    assert seq % (2 * tq) == 0
    nq = seq // (2 * tq)
    ngs = NSA_KV_GROUPS // ng
    n_cmp_pad = kcv.shape[2]
    gw = ng * NSA_HPG * LANES
    full = lambda c0: pl.BlockSpec((seq, ng * LANES), lambda b, g, i: (b, c0 // ng + g))
    cast_specs, cast_shapes = _cast_specs(casts, bsz * ngs * nq, lambda b, g, i: (b * ngs + g) * nq + i)
    outs = pl.pallas_call(
        functools.partial(_nsa_kernel, tq=tq, seq=seq, ng=ng, n_cast=len(casts)),
        grid=(bsz, ngs, nq),
        in_specs=[pl.BlockSpec((2 * tq, gw), lambda b, g, i: (b * nq + i, g)),
                  pl.BlockSpec((ng, None, n_cmp_pad, LANES), lambda b, g, i: (g, b, 0, 0)),
                  pl.BlockSpec((ng, None, n_cmp_pad, LANES), lambda b, g, i: (ngs + g, b, 0, 0)),
                  full(KS0), full(VS0), full(KW0), full(VW0),
                  pl.BlockSpec((2 * tq, ng * LANES), lambda b, g, i: (b * nq + i, GT0 // ng + g)),
                  pl.BlockSpec(selmap.shape, lambda b, g, i: (0, 0)),
                  pl.BlockSpec(et_sel.shape, lambda b, g, i: (0, 0))] + cast_specs,
        out_specs=[pl.BlockSpec((2 * tq, gw), lambda b, g, i: (b * nq + i, g))] + cast_specs,
        out_shape=[jax.ShapeDtypeStruct((m, NSA_HEADS * HEAD_DIM), BF16)] + cast_shapes,
        compiler_params=pltpu.CompilerParams(
            dimension_semantics=("parallel", "parallel", "arbitrary"),
            vmem_limit_bytes=VMEM_LIMIT),
        name="nsa_attn",
    )(qkv, kcv, kcv, qkv, qkv, qkv, qkv, qkv, selmap, et_sel, *[w for w, _ in casts])
    return outs[0], outs[1:]


def _moba_kernel(q_ref, k_ref, v_ref, avg_ref, et_ref, *rest, seq, hb, n_cast):
    cast_in, (o_ref, *cast_out), (kmh_ref, kml_ref) = rest[:n_cast], rest[n_cast:2 * n_cast + 1], rest[2 * n_cast + 1:]
    _cast_slabs(cast_in, cast_out)
    tp = pl.program_id(2)
    blk_len = MOBA_BLOCK
    rows = 2 * blk_len
    nb = seq // blk_len
    nb_rows = avg_ref.shape[0]
    head = lambda h: slice(h * LANES, (h + 1) * LANES)

    @pl.when(tp == 0)
    def _():
        for h in range(hb):
            km = jnp.dot(avg_ref[...], k_ref[:, head(h)], preferred_element_type=F32)
            kmh_ref[h], kml_ref[h] = _split_bf16(km)

    blk = lax.broadcasted_iota(I32, (nb_rows, rows), 0)
    qblk = 2 * tp + lax.broadcasted_iota(I32, (nb_rows, rows), 1) // blk_len
    past = jnp.where(blk < qblk, 1.0, 0.0)
    own = jnp.where(blk == qblk, 1.0, 0.0)
    qas = []
    for h in range(hb):
        q = q_ref[:, head(h)]
        gate_t = _nt_dot(kmh_ref[h], q) + _nt_dot(kml_ref[h], q)
        picked = _topk_rows(jnp.where(past > 0.5, gate_t, -jnp.inf), min(MOBA_TOPK, nb - 1))
        qas.append(jnp.concatenate([q, _bias_from_keep(picked * past + own)], axis=1))

    def kv_chunk(j, h):
        keys = pl.ds(pl.multiple_of(j * blk_len, blk_len), blk_len)
        ka = jnp.concatenate([k_ref[keys, head(h)], et_ref[keys, :]], axis=1)
        return ka, _with_ones(v_ref[keys, head(h)])

    row = lax.broadcasted_iota(I32, (rows, blk_len), 0)
    colk = lax.broadcasted_iota(I32, (rows, blk_len), 1)
    m0 = jnp.full((rows, 1), NEG_BIG, F32)
    a0 = jnp.zeros((rows, 2 * HEAD_DIM), F32)

    def all_heads(j, c, mask=None):
        new = []
        for h in range(hb):
            new += _flash_step(qas[h], *kv_chunk(j, h), c[2 * h], c[2 * h + 1], mask)
        return tuple(new)

    carry = all_heads(2 * tp, (m0, a0) * hb, (row >= blk_len) | (colk <= row))
    carry = all_heads(2 * tp + 1, carry, colk <= row - blk_len)
    final = lax.fori_loop(0, tp, lambda i, c: all_heads(2 * i + 1, all_heads(2 * i, c)), carry)
    for h in range(hb):
        o_ref[:, head(h)] = _flash_out(final[2 * h + 1]).astype(BF16)


def _moba(qkv, avg, et_blk, bsz, seq, casts):
    m = qkv.shape[0]
    assert seq % (2 * MOBA_BLOCK) == 0
    nq = seq // (2 * MOBA_BLOCK)
    hb = MOBA_HEADS_PER_STEP
    assert MOBA_HEADS % hb == 0 and QM0 % hb == 0 and KM0 % hb == 0 and VM0 % hb == 0
    hw = hb * LANES
    ng = MOBA_HEADS // hb
    cast_specs, cast_shapes = _cast_specs(casts, bsz * ng * nq, lambda b, h, i: (b * ng + h) * nq + i)
    whole = lambda c0: pl.BlockSpec((seq, hw), lambda b, h, i: (b, c0 // hb + h),
                                    pipeline_mode=pl.Buffered(1))
    outs = pl.pallas_call(
        functools.partial(_moba_kernel, seq=seq, hb=hb, n_cast=len(casts)),
        grid=(bsz, ng, nq),
        in_specs=[pl.BlockSpec((2 * MOBA_BLOCK, hw), lambda b, h, i: (b * nq + i, QM0 // hb + h)),
                  whole(KM0), whole(VM0),
                  pl.BlockSpec(avg.shape, lambda b, h, i: (0, 0)),
                  pl.BlockSpec(et_blk.shape, lambda b, h, i: (0, 0))] + cast_specs,
        out_specs=[pl.BlockSpec((2 * MOBA_BLOCK, hw), lambda b, h, i: (b * nq + i, h))] + cast_specs,
        out_shape=[jax.ShapeDtypeStruct((m, MOBA_HEADS * HEAD_DIM), BF16)] + cast_shapes,
        scratch_shapes=[pltpu.VMEM((hb, avg.shape[0], HEAD_DIM), BF16),
                        pltpu.VMEM((hb, avg.shape[0], HEAD_DIM), BF16)],
        compiler_params=pltpu.CompilerParams(
            dimension_semantics=("parallel", "parallel", "arbitrary"),
            vmem_limit_bytes=VMEM_LIMIT),
        name="moba_attn",
    )(qkv, qkv, qkv, avg, et_blk, *[w for w, _ in casts])
    return outs[0], outs[1:]


def _outproj_kernel(on_ref, om_ref, w_ref, x_ref, g_ref, gate_ref, *rest, n_cast):
    cast_in, (o_ref, *cast_out) = rest[:n_cast], rest[n_cast:]
    _cast_slabs(cast_in, cast_out)
    kn = on_ref.shape[1]
    acc = (jnp.dot(on_ref[...], w_ref[:kn, :], preferred_element_type=F32)
           + jnp.dot(om_ref[...], w_ref[kn:, :], preferred_element_type=F32))
    o_ref[...] = x_ref[...] + gate_ref[...] * _rms_norm(acc, g_ref[...])


def _outproj(o_n, o_m, w_o, x2, g_post, gate, seq, casts):
    m, d = x2.shape
    tm = min(512, seq)
    tiles_per_b = seq // tm
    cast_specs, cast_shapes = _cast_specs(casts, m // tm, lambda i: i)
    outs = pl.pallas_call(
        functools.partial(_outproj_kernel, n_cast=len(casts)),
        grid=(m // tm,),
        in_specs=[pl.BlockSpec((tm, o_n.shape[1]), lambda i: (i, 0)),
                  pl.BlockSpec((tm, o_m.shape[1]), lambda i: (i, 0)),
                  pl.BlockSpec(w_o.shape, lambda i: (0, 0), pipeline_mode=pl.Buffered(1)),
                  pl.BlockSpec((tm, d), lambda i: (i, 0)),
                  pl.BlockSpec((1, d), lambda i: (0, 0)),
                  pl.BlockSpec((None, 1, d), lambda i: (i // tiles_per_b, 0, 0))] + cast_specs,
        out_specs=[pl.BlockSpec((tm, d), lambda i: (i, 0))] + cast_specs,
        out_shape=[jax.ShapeDtypeStruct((m, d), F32)] + cast_shapes,
        compiler_params=pltpu.CompilerParams(dimension_semantics=("parallel",),
                                             vmem_limit_bytes=VMEM_LIMIT),
        name="outproj",
    )(o_n, o_m, w_o, x2, g_post, gate, *[w for w, _ in casts])
    return outs[0], outs[1:]


def _ffn_kernel(x_ref, gpre_ref, sc_ref, sh_ref, wu_ref, wd_ref, gpost_ref, gate_ref,
                o_ref, h_ref):
    k = pl.program_id(1)
    last = pl.num_programs(1) - 1
    row_blocks = [slice(r, r + FFN_ROW_BLOCK) for r in range(0, x_ref.shape[0], FFN_ROW_BLOCK)]

    def mlp(h):
        u = jnp.dot(h, wu_ref[...], preferred_element_type=F32)
        u = jnp.square(jnp.maximum(u, 0.0)).astype(BF16)
        return jnp.dot(u, wd_ref[...], preferred_element_type=F32)

    @pl.when(k == 0)
    def _():
        for rows in row_blocks:
            h = _rms_norm(x_ref[rows], gpre_ref[...]) * (1.0 + sc_ref[...]) + sh_ref[...]
            h_ref[rows] = h.astype(BF16)
            o_ref[rows] = mlp(h_ref[rows])

    @pl.when(jnp.logical_and(k > 0, k < last))
    def _():
        o_ref[...] += mlp(h_ref[...])

    @pl.when(k == last)
    def _():
        for rows in row_blocks:
            f = o_ref[rows] + mlp(h_ref[rows])
            o_ref[rows] = x_ref[rows] + gate_ref[...] * _rms_norm(f, gpost_ref[...])


def _ffn(x2, g_pre, scale, shift, w_up, w_down, g_post, gate, seq):
    m, d = x2.shape
    dff = w_up.shape[1]
    tm = min(1024, seq)
    ck = 512
    assert dff // ck >= 2 and tm % FFN_ROW_BLOCK == 0
    tiles_per_b = seq // tm
    vec = pl.BlockSpec((1, d), lambda i, k: (0, 0))
    per_b = pl.BlockSpec((None, 1, d), lambda i, k: (i // tiles_per_b, 0, 0))
    return pl.pallas_call(
        _ffn_kernel,
        grid=(m // tm, dff // ck),
        in_specs=[pl.BlockSpec((tm, d), lambda i, k: (i, 0)), vec, per_b, per_b,
                  pl.BlockSpec((d, ck), lambda i, k: (0, k)),
                  pl.BlockSpec((ck, d), lambda i, k: (k, 0)),
                  vec, per_b],
        out_specs=pl.BlockSpec((tm, d), lambda i, k: (i, 0)),
        out_shape=jax.ShapeDtypeStruct((m, d), F32),
        scratch_shapes=[pltpu.VMEM((tm, d), BF16)],
        compiler_params=pltpu.CompilerParams(dimension_semantics=("parallel", "arbitrary"),
                                             vmem_limit_bytes=VMEM_LIMIT),
        name="ffn",
    )(x2, g_pre, scale, shift, w_up, w_down, g_post, gate)


def _constants(seq):
    n_cmp_pad = seq // CMP_STRIDE
    n_cmp = (seq - CMP_LEN) // CMP_STRIDE + 1
    n_sel = seq // SEL_LEN
    cs = np.arange(n_cmp) * CMP_STRIDE
    ss = np.arange(n_sel) * SEL_LEN
    overlap = np.clip(np.minimum(cs[:, None] + CMP_LEN, ss[None, :] + SEL_LEN)
                      - np.maximum(cs[:, None], ss[None, :]), 0, None)
    selmap_t = np.zeros((n_sel, n_cmp_pad), np.float32)
    selmap_t[:, :n_cmp] = (overlap / CMP_LEN).T
    keys = np.arange(seq)
    et_sel = (keys[:, None] // SEL_LEN == np.arange(LANES)[None, :]).astype(np.float32)
    et_blk = (keys[:, None] // MOBA_BLOCK == np.arange(LANES)[None, :]).astype(np.float32)
    nb_rows = -(-(seq // MOBA_BLOCK) // BF16_SUBLANES) * BF16_SUBLANES
    avg = et_blk.T[:nb_rows] / MOBA_BLOCK
    to = lambda a: jnp.asarray(a, BF16)
    return to(selmap_t), to(et_sel), to(et_blk), to(avg)


def _rope_tables(seq):
    pos = jnp.arange(seq, dtype=F32)
    inv = ROPE_THETA ** (-jnp.arange(0, HEAD_DIM, 2, dtype=F32) / HEAD_DIM)
    ang = pos[:, None] * inv[None, :]
    cos, sin = jnp.cos(ang), jnp.sin(ang)
    return jnp.concatenate([cos, cos], axis=1), jnp.concatenate([-sin, sin], axis=1)


def kernel(x, c, w_ada, b_ada, pre_norm_mix, post_norm_mix, w_in, cmp_k_pos, cmp_k_w1, cmp_k_w2,
           cmp_v_pos, cmp_v_w1, cmp_v_w2, w_o, pre_norm_ffn, post_norm_ffn, w_up, w_down):
    bsz, seq, d = x.shape
    depth = w_ada.shape[0]
    cos2, sin2 = _rope_tables(seq)
    selmap, et_sel, et_blk, avg = _constants(seq)
    x2 = x.reshape(bsz * seq, d)
    for l in range(depth):
        mod, w_head, w_tail = _adaln(c, w_ada[l], b_ada[l], w_in[l].T)
        shift_m, scale_m, gate_m, shift_f, scale_f, gate_f = [
            a.reshape(bsz, 1, d) for a in jnp.split(mod, 6, axis=-1)]
        row = lambda a: a.reshape(1, d)

        qkv, kvc = _inproj(x2, row(pre_norm_mix[l]), scale_m, shift_m, w_head, w_tail,
                           cos2, sin2, seq)
        kvc4 = kvc.reshape(2 * NSA_KV_GROUPS, bsz, seq // CMP_STRIDE, CMP_STRIDE * HEAD_DIM)
        kcv = _compress(kvc4, (cmp_k_pos[l].reshape(1, -1), cmp_k_w1[l], cmp_k_w2[l]),
                        (cmp_v_pos[l].reshape(1, -1), cmp_v_w1[l], cmp_v_w2[l]))
        o_n, (w_o_bf,) = _nsa(qkv, kcv, selmap, et_sel, bsz, seq, [(w_o[l], 0)])
        o_m, (w_up_bf,) = _moba(qkv, avg, et_blk, bsz, seq, [(w_up[l], 1)])
        x2, (w_down_bf,) = _outproj(o_n, o_m, w_o_bf, x2, row(post_norm_mix[l]), gate_m, seq,
                                    [(w_down[l], 0)])
        x2 = _ffn(x2, row(pre_norm_ffn[l]), scale_f, shift_f, w_up_bf, w_down_bf,
                  row(post_norm_ffn[l]), gate_f, seq)
    return x2.reshape(bsz, seq, d)
```
